```python
import math
import jax, jax.numpy as jnp
from jax import lax
import numpy as np

D_MODEL = 2048
BATCH = 1
SEQ = 8192
DEPTH = 1

SSM_EXPAND = 2
D_INNER = SSM_EXPAND * D_MODEL
SSM_HEAD_DIM = 64
SSM_HEADS = D_INNER // SSM_HEAD_DIM
SSM_GROUPS = 8
SSM_HEADS_PER_GROUP = SSM_HEADS // SSM_GROUPS
SSM_STATE = 128
SSM_CONV = 4
SSM_CHUNK = 256
D_XBC = D_INNER + 2 * SSM_GROUPS * SSM_STATE
ATTN_HEAD_DIM = 128
ATTN_HEADS = D_MODEL // ATTN_HEAD_DIM
D_ATTN = ATTN_HEADS * ATTN_HEAD_DIM
MOBA_BLOCK = 256
MOBA_TOPK = 3
MOBA_QCHUNK = 32
REL_BUCKETS = 32
REL_MAX_DIST = 128
D_FF = 5632
FFN_CONV = 3
EPS = 1e-6
IN_SPLITS = (D_INNER, D_XBC, SSM_HEADS, D_ATTN, D_ATTN, D_ATTN, 2 * D_MODEL)
D_IN_PROJ = sum(IN_SPLITS)

kernel_name = "hybrid_ssd_moba_convffn"


def _split_points(sizes):
    pts, acc = [], 0
    for s in sizes[:-1]:
        acc += s
        pts.append(acc)
    return pts


def rmsnorm(x, w):
    xf = x.astype(jnp.float32)
    y = xf * lax.rsqrt(jnp.mean(xf * xf, axis=-1, keepdims=True) + EPS)
    return (y * w.astype(jnp.float32)).astype(x.dtype)


def causal_dwconv(x, w, b):
    K = w.shape[0]
    S = x.shape[1]
    xp = jnp.pad(x, ((0, 0), (K - 1, 0), (0, 0)))
    y = b
    for k in range(K):
        y = y + xp[:, k:k + S] * w[k]
    return y


def rel_bucket(dist):
    n = jnp.maximum(dist, 0)
    max_exact = REL_BUCKETS // 2
    nf = jnp.maximum(n, 1).astype(jnp.float32)
    large = max_exact + (jnp.log(nf / max_exact) / math.log(REL_MAX_DIST / max_exact)
                         * (REL_BUCKETS - max_exact)).astype(jnp.int32)
    large = jnp.minimum(large, REL_BUCKETS - 1)
    return jnp.where(n < max_exact, n, large)


def ssd_mixer(z, xbc, dt_raw, conv_w, conv_b, dt_bias, a_log, d_skip, norm_w):
    Bsz, S, _ = xbc.shape
    G, R, P, N, L = SSM_GROUPS, SSM_HEADS_PER_GROUP, SSM_HEAD_DIM, SSM_STATE, SSM_CHUNK
    xbc = jax.nn.silu(causal_dwconv(xbc, conv_w, conv_b))
    xs, Bm, Cm = jnp.split(xbc, [D_INNER, D_INNER + G * N], axis=-1)
    dt = jax.nn.softplus((dt_raw + dt_bias).astype(jnp.float32))
    A = -jnp.exp(a_log.astype(jnp.float32))
    pad = (-S) % L
    Sp = S + pad
    NC = Sp // L

    def chunked(t, tail):
        t = jnp.pad(t, ((0, 0), (0, pad), (0, 0)))
        return t.reshape((Bsz, NC, L) + tail)

    x = chunked(xs, (G, R, P)).astype(jnp.float32)
    Bc = chunked(Bm, (G, N)).astype(jnp.float32)
    Cc = chunked(Cm, (G, N)).astype(jnp.float32)
    dtc = chunked(dt, (G, R))
    X = x * dtc[..., None]
    a_cs = jnp.cumsum(dtc * A.reshape(G, R), axis=2)
    causal = jnp.tril(jnp.ones((L, L), dtype=bool))[None, None, :, :, None, None]
    decay = jnp.exp(jnp.where(causal, a_cs[:, :, :, None] - a_cs[:, :, None, :], -jnp.inf))
    cb = jnp.einsum('bclgn,bcsgn->bclsg', Cc, Bc)
    y_diag = jnp.einsum('bclsg,bclsgr,bcsgrp->bclgrp', cb, decay, X)
    decay_states = jnp.exp(a_cs[:, :, -1:] - a_cs)
    states = jnp.einsum('bclgn,bclgr,bclgrp->bcgrpn', Bc, decay_states, X)
    chunk_decay = jnp.exp(a_cs[:, :, -1])

    def step(h, inp):
        dec, st = inp
        return h * dec[..., None, None] + st, h

    h0 = jnp.zeros((Bsz, G, R, P, N), jnp.float32)
    _, h_prev = lax.scan(step, h0, (jnp.moveaxis(chunk_decay, 1, 0), jnp.moveaxis(states, 1, 0)))
    h_prev = jnp.moveaxis(h_prev, 0, 1)
    y_off = jnp.einsum('bclgn,bcgrpn,bclgr->bclgrp', Cc, h_prev, jnp.exp(a_cs))
    y = y_diag + y_off + x * d_skip.astype(jnp.float32).reshape(G, R)[:, :, None]
    y = y.reshape(Bsz, Sp, D_INNER)[:, :S]
    yg = (y * jax.nn.silu(z.astype(jnp.float32))).reshape(Bsz, S, G, D_INNER // G)
    yg = yg * lax.rsqrt(jnp.mean(yg * yg, axis=-1, keepdims=True) + EPS)
    yg = yg.reshape(Bsz, S, D_INNER) * norm_w.astype(jnp.float32)
    return yg.astype(z.dtype)


def moba_attention(q, k, v, q_norm_w, k_norm_w, rel_table):
    Bsz, S, H, D = q.shape
    BS, QC = MOBA_BLOCK, MOBA_QCHUNK
    q = rmsnorm(q, q_norm_w) * (D ** -0.5)
    k = rmsnorm(k, k_norm_w)
    pad = (-S) % BS
    Sp = S + pad
    NB = Sp // BS
    topk = min(MOBA_TOPK, NB)
    kb = jnp.pad(k, ((0, 0), (0, pad), (0, 0), (0, 0))).reshape(Bsz, NB, BS, H, D).transpose(0, 3, 1, 2, 4)
    vb = jnp.pad(v, ((0, 0), (0, pad), (0, 0), (0, 0))).reshape(Bsz, NB, BS, H, D).transpose(0, 3, 1, 2, 4)
    kbar = jnp.mean(kb, axis=3)
    NQ = S // QC
    qc = jnp.moveaxis(q.transpose(0, 2, 1, 3).reshape(Bsz, H, NQ, QC, D), 2, 0)
    table_h = rel_table.T
    bidx = jnp.arange(Bsz)[:, None, None, None]
    hidx = jnp.arange(H)[None, :, None, None]
    hidx5 = hidx[..., None]

    def one_chunk(args):
        qi, ci = args
        qpos = ci * QC + jnp.arange(QC)
        blk = (ci * QC) // BS
        gate = jnp.einsum('bhqd,bhnd->bhqn', qi, kbar).astype(jnp.float32)
        gate = jnp.where(jnp.arange(NB) < blk, gate, -jnp.inf)
        _, sel = lax.top_k(gate, topk)
        valid = jnp.arange(topk) < blk
        ksel = kb[bidx, hidx, sel]
        vsel = vb[bidx, hidx, sel]
        s_sel = jnp.einsum('bhqd,bhqtkd->bhqtk', qi, ksel).astype(jnp.float32)
        kpos_sel = sel[..., None] * BS + jnp.arange(BS)
        bias_sel = table_h[hidx5, rel_bucket(qpos[:, None, None] - kpos_sel)]
        s_sel = jnp.where(valid[:, None], s_sel + bias_sel, -jnp.inf)
        kown = lax.dynamic_index_in_dim(kb, blk, axis=2, keepdims=False)
        vown = lax.dynamic_index_in_dim(vb, blk, axis=2, keepdims=False)
        s_own = jnp.einsum('bhqd,bhkd->bhqk', qi, kown).astype(jnp.float32)
        dist = qpos[:, None] - (blk * BS + jnp.arange(BS))[None, :]
        bias_own = table_h[:, rel_bucket(dist)]
        s_own = jnp.where(dist >= 0, s_own + bias_own, -jnp.inf)
        logits = jnp.concatenate([s_sel.reshape(Bsz, H, QC, topk * BS), s_own], axis=-1)
        p = jax.nn.softmax(logits, axis=-1).astype(v.dtype)
        p_sel = p[..., :topk * BS].reshape(Bsz, H, QC, topk, BS)
        p_own = p[..., topk * BS:]
        return (jnp.einsum('bhqtk,bhqtkd->bhqd', p_sel, vsel)
                + jnp.einsum('bhqk,bhkd->bhqd', p_own, vown))

    out = lax.map(one_chunk, (qc, jnp.arange(NQ)))
    out = jnp.moveaxis(out, 0, 2).reshape(Bsz, H, S, D).transpose(0, 2, 1, 3)
    return out.reshape(Bsz, S, H * D)


def setup_inputs(seed: int = 0) -> dict:
    key = jax.random.key(seed)
    ks = jax.random.split(key, 24)
    f32 = jnp.float32

    def nrm(k, shape, scale):
        return jax.random.normal(k, shape, f32) * scale

    dt0 = jnp.exp(jax.random.uniform(ks[5], (DEPTH, SSM_HEADS), f32, math.log(1e-3), math.log(1e-1)))
    return {
        "x": nrm(ks[0], (BATCH, SEQ, D_MODEL), 1.0),
        "attn_norm_w": 1.0 + nrm(ks[1], (DEPTH, D_MODEL), 0.02),
        "w_in": nrm(ks[2], (DEPTH, D_MODEL, D_IN_PROJ), D_MODEL ** -0.5),
        "b_gate": nrm(ks[3], (DEPTH, 2 * D_MODEL), 0.02),
        "ssm_conv_w": nrm(ks[4], (DEPTH, SSM_CONV, D_XBC), SSM_CONV ** -0.5),
        "ssm_conv_b": nrm(ks[6], (DEPTH, D_XBC), 0.02),
        "ssm_dt_bias": dt0 + jnp.log(-jnp.expm1(-dt0)),
        "ssm_a_log": jnp.log(jax.random.uniform(ks[7], (DEPTH, SSM_HEADS), f32, 1.0, 16.0)),
        "ssm_d": 1.0 + nrm(ks[8], (DEPTH, SSM_HEADS), 0.1),
        "ssm_norm_w": 1.0 + nrm(ks[9], (DEPTH, D_INNER), 0.02),
        "q_norm_w": 1.0 + nrm(ks[10], (DEPTH, ATTN_HEAD_DIM), 0.02),
        "k_norm_w": 1.0 + nrm(ks[11], (DEPTH, ATTN_HEAD_DIM), 0.02),
        "rel_bias": nrm(ks[12], (REL_BUCKETS, ATTN_HEADS), 0.5),
        "w_ssm_out": nrm(ks[13], (DEPTH, D_INNER, D_MODEL), D_INNER ** -0.5),
        "w_attn_out": nrm(ks[14], (DEPTH, D_ATTN, D_MODEL), D_ATTN ** -0.5),
        "w_out": nrm(ks[15], (DEPTH, D_MODEL, D_MODEL), D_MODEL ** -0.5),
        "ffn_norm_w": 1.0 + nrm(ks[16], (DEPTH, D_MODEL), 0.02),
        "w_up": nrm(ks[17], (DEPTH, D_MODEL, 2 * D_FF), D_MODEL ** -0.5),
        "ffn_conv_w": nrm(ks[18], (DEPTH, FFN_CONV, 2 * D_FF), FFN_CONV ** -0.5),
        "ffn_conv_b": nrm(ks[19], (DEPTH, 2 * D_FF), 0.02),
        "w_down": nrm(ks[20], (DEPTH, D_FF, D_MODEL), D_FF ** -0.5),
    }


def reference(x, attn_norm_w, w_in, b_gate, ssm_conv_w, ssm_conv_b, ssm_dt_bias, ssm_a_log,
              ssm_d, ssm_norm_w, q_norm_w, k_norm_w, rel_bias, w_ssm_out, w_attn_out, w_out,
              ffn_norm_w, w_up, ffn_conv_w, ffn_conv_b, w_down):
    Bsz, S, _ = x.shape
    pts = _split_points(IN_SPLITS)
    for l in range(DEPTH):
        h = rmsnorm(x, attn_norm_w[l])
        proj = h @ w_in[l]
        z, xbc, dt_raw, q, k, v, g = jnp.split(proj, pts, axis=-1)
        y_ssm = ssd_mixer(z, xbc, dt_raw, ssm_conv_w[l], ssm_conv_b[l], ssm_dt_bias[l],
                          ssm_a_log[l], ssm_d[l], ssm_norm_w[l]) @ w_ssm_out[l]
        q = q.reshape(Bsz, S, ATTN_HEADS, ATTN_HEAD_DIM)
        k = k.reshape(Bsz, S, ATTN_HEADS, ATTN_HEAD_DIM)
        v = v.reshape(Bsz, S, ATTN_HEADS, ATTN_HEAD_DIM)
        y_attn = moba_attention(q, k, v, q_norm_w[l], k_norm_w[l], rel_bias) @ w_attn_out[l]
        g_ssm, g_attn = jnp.split(jax.nn.sigmoid(g + b_gate[l]), 2, axis=-1)
        x = x + (g_ssm * y_ssm + g_attn * y_attn) @ w_out[l]
        h = rmsnorm(x, ffn_norm_w[l])
        u = causal_dwconv(h @ w_up[l], ffn_conv_w[l], ffn_conv_b[l])
        u_gate, u_up = jnp.split(u, 2, axis=-1)
        x = x + (jax.nn.silu(u_gate) * u_up) @ w_down[l]
    return x
```

```python
import functools
import math

import jax
import jax.numpy as jnp
from jax import lax
from jax.experimental import pallas as pl
from jax.experimental.pallas import tpu as pltpu

F32 = jnp.float32
BF16 = jnp.bfloat16

D_MODEL = 2048
D_INNER = 4096
SSM_HEADS = 64
SSM_HEAD_DIM = 64
SSM_GROUPS = 8
SSM_HEADS_PER_GROUP = SSM_HEADS // SSM_GROUPS
SSM_STATE = 128
SSM_CONV = 4
SSM_CHUNK = 256
GROUP_CH = D_INNER // SSM_GROUPS
D_XBC = D_INNER + 2 * SSM_GROUPS * SSM_STATE
ATTN_HEADS = 16
ATTN_HEAD_DIM = 128
D_ATTN = ATTN_HEADS * ATTN_HEAD_DIM
MOBA_BLOCK = 256
MOBA_TOPK = 3
REL_BUCKETS = 32
REL_MAX_DIST = 128
D_FF = 5632
FFN_CONV = 3
EPS = 1e-6

LANES = 128
SUBLANES = 8
D_PROJ = D_XBC + 3 * D_ATTN + 2 * D_INNER
COL_Q = D_XBC
COL_K = COL_Q + D_ATTN
COL_V = COL_K + D_ATTN
COL_Z = COL_V + D_ATTN
COL_G = COL_Z + D_INNER
VMEM_LIMIT = 56 * 1024 * 1024


def _cparams(sem):
    return pltpu.CompilerParams(dimension_semantics=sem, vmem_limit_bytes=VMEM_LIMIT)


def _split3(a):
    hi = a.astype(BF16)
    r = a - hi.astype(F32)
    mid = r.astype(BF16)
    lo = (r - mid.astype(F32)).astype(BF16)
    return hi, mid, lo


def _dot(a, b):
    return jnp.dot(a, b, preferred_element_type=F32)


def _dot_nt(a, b):
    return lax.dot_general(a, b, (((1,), (1,)), ((), ())), preferred_element_type=F32)


def _sigmoid(x):
    return 1.0 / (1.0 + jnp.exp(-x))


def _in_proj_kernel(x_ref, nw_ref, w_ref, wdt_ref, o_ref, dt_ref, h_scr):
    @pl.when(pl.program_id(1) == 0)
    def _():
        x = x_ref[...]
        ms = jnp.mean(x * x, axis=-1, keepdims=True)
        h = x * lax.rsqrt(ms + EPS) * nw_ref[...]
        h_scr[...] = h.astype(BF16)
        h_hi, h_mid, _ = _split3(h)
        w_hi, w_mid, _ = _split3(wdt_ref[...])
        dt_ref[...] = _dot(h_hi, w_hi) + _dot(h_hi, w_mid) + _dot(h_mid, w_hi)

    o_ref[...] = _dot(h_scr[...], w_ref[...]).astype(o_ref.dtype)


def _in_proj(x, norm_w, w_main, w_dt, tm=1024, tn=1024):
    S = x.shape[0]
    tm = min(tm, S)
    return pl.pallas_call(
        _in_proj_kernel,
        grid=(S // tm, D_PROJ // tn),
        in_specs=[
            pl.BlockSpec((tm, D_MODEL), lambda m, n: (m, 0)),
            pl.BlockSpec((1, D_MODEL), lambda m, n: (0, 0)),
            pl.BlockSpec((D_MODEL, tn), lambda m, n: (0, n)),
            pl.BlockSpec((D_MODEL, LANES), lambda m, n: (0, 0)),
        ],
        out_specs=[
            pl.BlockSpec((tm, tn), lambda m, n: (m, n)),
            pl.BlockSpec((tm, LANES), lambda m, n: (m, 0)),
        ],
        out_shape=[
            jax.ShapeDtypeStruct((S, D_PROJ), BF16),
            jax.ShapeDtypeStruct((S, LANES), F32),
        ],
        scratch_shapes=[pltpu.VMEM((tm, D_MODEL), BF16)],
        compiler_params=_cparams(("arbitrary", "arbitrary")),
        name="in_proj",
    )(x, norm_w, w_main, w_dt)


def _conv_silu(raw_ref, tail_ref, pad_ref, w_ref, b_ref, g, taps):
    L = raw_ref.shape[0]
    pad_ref[0:SUBLANES, :] = tail_ref[g]
    pad_ref[SUBLANES:SUBLANES + L, :] = raw_ref[...].astype(F32)
    tail_ref[g] = pad_ref[L:L + SUBLANES, :]
    acc = b_ref[...]
    for k in range(taps):
        off = SUBLANES - (taps - 1) + k
        acc = acc + w_ref[k:k + 1, :] * pad_ref[off:off + L, :]
    return acc * _sigmoid(acc)


def _ssd_kernel(x_ref, b_ref, c_ref, z_ref, dtr_ref, cwx_ref, cwb_ref, cwc_ref,
                cbx_ref, cbb_ref, cbc_ref, dtb_ref, alog_ref, dsk_ref, nw_ref,
                o_ref,
                tailx, tailb, tailc, padx, padb, padc, acs_scr, acst_scr, dtt_scr, ht_scr):
    c = pl.program_id(0)
    g = pl.program_id(1)
    L = SSM_CHUNK
    QW = 4 * SSM_HEAD_DIM

    @pl.when(c == 0)
    def _():
        tailx[g] = jnp.zeros(tailx.shape[1:], F32)
        tailb[g] = jnp.zeros(tailb.shape[1:], F32)
        tailc[g] = jnp.zeros(tailc.shape[1:], F32)
        ht_scr[g] = jnp.zeros(ht_scr.shape[1:], F32)

    @pl.when(g == 0)
    def _():
        t = dtr_ref[...] + dtb_ref[...]
        dt = jnp.maximum(t, 0.0) + jnp.log(1.0 + jnp.exp(-jnp.abs(t)))
        a = -jnp.exp(alog_ref[...])
        da = dt * a
        ri = lax.broadcasted_iota(jnp.int32, (L, L), 0)
        ci = lax.broadcasted_iota(jnp.int32, (L, L), 1)
        tri = jnp.where(ri >= ci, 1.0, 0.0).astype(BF16)
        hi, mid, lo = _split3(da)
        a_cs = _dot(tri, hi) + _dot(tri, mid) + _dot(tri, lo)
        acs_scr[...] = a_cs
        acst_scr[...] = a_cs.T
        dtt_scr[...] = dt.T

    xa = _conv_silu(x_ref, tailx, padx, cwx_ref, cbx_ref, g, SSM_CONV)
    ba = _conv_silu(b_ref, tailb, padb, cwb_ref, cbb_ref, g, SSM_CONV)
    ca = _conv_silu(c_ref, tailc, padc, cwc_ref, cbc_ref, g, SSM_CONV)

    xb = xa.astype(BF16)
    bt = ba.T
    cb = _dot(ca.astype(BF16), bt.astype(BF16))
    a_cs = acs_scr[...]
    ri = lax.broadcasted_iota(jnp.int32, (L, L), 0)
    ci = lax.broadcasted_iota(jnp.int32, (L, L), 1)
    causal = ri >= ci
    lane_h = lax.broadcasted_iota(jnp.int32, (L, LANES), 1)
    lane_q = lax.broadcasted_iota(jnp.int32, (1, QW), 1)

    ys = []
    for q in range(GROUP_CH // QW):
        xq = xb[:, q * QW:(q + 1) * QW]
        hq = ht_scr[g, :, q * QW:(q + 1) * QW]
        hqb = hq.astype(BF16)
        acc = jnp.zeros((L, QW), F32)
        st = jnp.zeros((SSM_STATE, QW), F32)
        cdq = jnp.zeros((1, QW), F32)
        for j in range(QW // SSM_HEAD_DIM):
            h = g * SSM_HEADS_PER_GROUP + q * (QW // SSM_HEAD_DIM) + j
            a_row = acst_scr[pl.ds(h, 1), :]
            dt_row = dtt_scr[pl.ds(h, 1), :]
            a_col = jnp.sum(jnp.where(lane_h == h, a_cs, 0.0), axis=1, keepdims=True)
            dmat = jnp.where(causal, a_col - a_row, -jnp.inf)
            mh = (cb * jnp.exp(dmat) * dt_row).astype(BF16)
            lm = (lane_q >= j * SSM_HEAD_DIM) & (lane_q < (j + 1) * SSM_HEAD_DIM)
            xm = jnp.where(lm, xq, jnp.zeros_like(xq))
            hm = jnp.where(lm, hqb, jnp.zeros_like(hqb))
            ce = (ca * jnp.exp(a_col)).astype(BF16)
            acc = acc + _dot(mh, xm) + _dot(ce, hm)
            a_last = a_row[:, L - 1:L]
            w_row = dt_row * jnp.exp(a_last - a_row)
            st = st + _dot((bt * w_row).astype(BF16), xm)
            cdq = jnp.where(lm, jnp.exp(a_last), cdq)
        ht_scr[g, :, q * QW:(q + 1) * QW] = hq * cdq + st
        ys.append(acc + xa[:, q * QW:(q + 1) * QW] * dsk_ref[:, q * QW:(q + 1) * QW])

    z = z_ref[...].astype(F32)
    sz = z * _sigmoid(z)
    ygs = [ys[q] * sz[:, q * QW:(q + 1) * QW] for q in range(len(ys))]
    ss = sum(jnp.sum(v * v, axis=1, keepdims=True) for v in ygs)
    scale = lax.rsqrt(ss * (1.0 / GROUP_CH) + EPS)
    for q in range(len(ys)):
        o_ref[:, q * QW:(q + 1) * QW] = (
            ygs[q] * scale * nw_ref[:, q * QW:(q + 1) * QW]).astype(o_ref.dtype)


def _ssd(proj, dt_raw, conv_w, conv_b, dt_bias, a_log, d_exp, norm_w):
    S = proj.shape[0]
    L, G, N = SSM_CHUNK, SSM_GROUPS, SSM_STATE
    xblk = D_INNER // GROUP_CH
    bcol = D_INNER // N
    ccol = bcol + G
    zcol = COL_Z // GROUP_CH
    return pl.pallas_call(
        _ssd_kernel,
        grid=(S // L, G),
        in_specs=[
            pl.BlockSpec((L, GROUP_CH), lambda c, g: (c, g)),
            pl.BlockSpec((L, N), lambda c, g: (c, bcol + g)),
            pl.BlockSpec((L, N), lambda c, g: (c, ccol + g)),
            pl.BlockSpec((L, GROUP_CH), lambda c, g: (c, zcol + g)),
            pl.BlockSpec((L, LANES), lambda c, g: (c, 0)),
            pl.BlockSpec((SSM_CONV, GROUP_CH), lambda c, g: (0, g)),
            pl.BlockSpec((SSM_CONV, N), lambda c, g: (0, bcol + g)),
            pl.BlockSpec((SSM_CONV, N), lambda c, g: (0, ccol + g)),
            pl.BlockSpec((1, GROUP_CH), lambda c, g: (0, g)),
            pl.BlockSpec((1, N), lambda c, g: (0, bcol + g)),
            pl.BlockSpec((1, N), lambda c, g: (0, ccol + g)),
            pl.BlockSpec((1, LANES), lambda c, g: (0, 0)),
            pl.BlockSpec((1, LANES), lambda c, g: (0, 0)),
            pl.BlockSpec((1, GROUP_CH), lambda c, g: (0, g)),
            pl.BlockSpec((1, GROUP_CH), lambda c, g: (0, g)),
        ],
        out_specs=pl.BlockSpec((L, GROUP_CH), lambda c, g: (c, g)),
        out_shape=jax.ShapeDtypeStruct((S, D_INNER), BF16),
        scratch_shapes=[
            pltpu.VMEM((G, SUBLANES, GROUP_CH), F32),
            pltpu.VMEM((G, SUBLANES, N), F32),
            pltpu.VMEM((G, SUBLANES, N), F32),
            pltpu.VMEM((L + SUBLANES, GROUP_CH), F32),
            pltpu.VMEM((L + SUBLANES, N), F32),
            pltpu.VMEM((L + SUBLANES, N), F32),
            pltpu.VMEM((L, LANES), F32),
            pltpu.VMEM((LANES, L), F32),
            pltpu.VMEM((LANES, L), F32),
            pltpu.VMEM((G, N, GROUP_CH), F32),
        ],
        compiler_params=_cparams(("arbitrary", "arbitrary")),
        name="ssd",
    )(proj, proj, proj, proj, dt_raw, conv_w, conv_w, conv_w, conv_b, conv_b, conv_b,
      dt_bias, a_log, d_exp, norm_w)


def _qk_prep_kernel(q_ref, k_ref, qw_ref, kw_ref, qn_ref, kn_ref, pen_ref, kbar_scr):
    S = q_ref.shape[0]
    BS = MOBA_BLOCK
    nb = S // BS
    kbar_scr[...] = jnp.zeros(kbar_scr.shape, F32)

    def kloop(b, carry):
        rows = pl.ds(pl.multiple_of(b * BS, BS), BS)
        k = k_ref[rows, :].astype(F32)
        kn = k * lax.rsqrt(jnp.mean(k * k, axis=-1, keepdims=True) + EPS) * kw_ref[...]
        kn_ref[0, rows, :] = kn.astype(BF16)
        kbar_scr[pl.ds(b, 1), :] = jnp.mean(kn, axis=0, keepdims=True)
        return carry

    lax.fori_loop(0, nb, kloop, 0)
    kb_hi, kb_mid, _ = _split3(kbar_scr[...])

    def qloop(i, carry):
        rows = pl.ds(pl.multiple_of(i * BS, BS), BS)
        q = q_ref[rows, :].astype(F32)
        qn = q * lax.rsqrt(jnp.mean(q * q, axis=-1, keepdims=True) + EPS) * qw_ref[...]
        qn = qn * (ATTN_HEAD_DIM ** -0.5)
        qn_ref[0, rows, :] = qn.astype(BF16)
        q_hi, q_mid, _ = _split3(qn)
        gate = _dot_nt(q_hi, kb_hi) + _dot_nt(q_hi, kb_mid) + _dot_nt(q_mid, kb_hi)
        lane = lax.broadcasted_iota(jnp.int32, gate.shape, 1)
        lane_f = lane.astype(F32)
        gate = jnp.where(lane < i, gate, -jnp.inf)
        pen = jnp.full(gate.shape, -jnp.inf, F32)
        for _ in range(MOBA_TOPK):
            mx = jnp.max(gate, axis=1, keepdims=True)
            cand = jnp.where((gate == mx) & (mx > -jnp.inf), lane_f, float(LANES))
            idx = jnp.min(cand, axis=1, keepdims=True)
            hit = lane_f == idx
            pen = jnp.where(hit, 0.0, pen)
            gate = jnp.where(hit, -jnp.inf, gate)
        pen_ref[0, rows, :] = pen
        return carry

    lax.fori_loop(0, nb, qloop, 0)


def _qk_prep(proj, q_norm_w, k_norm_w):
    S = proj.shape[0]
    H, D = ATTN_HEADS, ATTN_HEAD_DIM
    assert S // MOBA_BLOCK <= LANES
    return pl.pallas_call(
        _qk_prep_kernel,
        grid=(H,),
        in_specs=[
            pl.BlockSpec((S, D), lambda h: (0, COL_Q // D + h)),
            pl.BlockSpec((S, D), lambda h: (0, COL_K // D + h)),
            pl.BlockSpec((1, D), lambda h: (0, 0)),
            pl.BlockSpec((1, D), lambda h: (0, 0)),
        ],
        out_specs=[
            pl.BlockSpec((1, S, D), lambda h: (h, 0, 0)),
            pl.BlockSpec((1, S, D), lambda h: (h, 0, 0)),
            pl.BlockSpec((1, S, LANES), lambda h: (h, 0, 0)),
        ],
        out_shape=[
            jax.ShapeDtypeStruct((H, S, D), BF16),
            jax.ShapeDtypeStruct((H, S, D), BF16),
            jax.ShapeDtypeStruct((H, S, LANES), F32),
        ],
        scratch_shapes=[pltpu.VMEM((LANES, D), F32)],
        compiler_params=_cparams(("arbitrary",)),
        name="qk_prep",
    )(proj, proj, q_norm_w, k_norm_w)


def _rel_bucket(dist):
    n = jnp.maximum(dist, 0)
    max_exact = REL_BUCKETS // 2
    nf = jnp.maximum(n, 1).astype(F32)
    large = max_exact + (jnp.log(nf / max_exact) / math.log(REL_MAX_DIST / max_exact)
                         * (REL_BUCKETS - max_exact)).astype(jnp.int32)
    large = jnp.minimum(large, REL_BUCKETS - 1)
    return jnp.where(n < max_exact, n, large)


def _moba_kernel(tbl_ref, q_ref, k_ref, v_ref, pen_ref, o_ref,
                 bown_scr, bprev_scr, m_scr, l_scr, acc_scr):
    h = pl.program_id(0)
    i = pl.program_id(1)
    BS = MOBA_BLOCK

    @pl.when(i == 0)
    def _():
        ql = lax.broadcasted_iota(jnp.int32, (BS, BS), 0)
        kl = lax.broadcasted_iota(jnp.int32, (BS, BS), 1)
        d = ql - kl
        far = tbl_ref[h, REL_BUCKETS - 1]

        def lookup(bucket):
            val = jnp.zeros((BS, BS), F32)
            for b in range(REL_BUCKETS):
                val = jnp.where(bucket == b, tbl_ref[h, b], val)
            return val - far

        bown_scr[...] = jnp.where(d >= 0, lookup(_rel_bucket(d)), -jnp.inf)
        bprev_scr[...] = lookup(_rel_bucket(d + BS))

    q = q_ref[0]
    pen = pen_ref[0]
    lane = lax.broadcasted_iota(jnp.int32, pen.shape, 1)

    def scores(j):
        rows = pl.ds(pl.multiple_of(j * BS, BS), BS)
        return _dot_nt(q, k_ref[0, rows, :]), v_ref[rows, :]

    def pen_col(j):
        return jnp.sum(jnp.where(lane == j, pen, 0.0), axis=1, keepdims=True)

    def update(s, v):
        m_old = m_scr[...]
        m_new = jnp.maximum(m_old, jnp.max(s, axis=1, keepdims=True))
        alpha = jnp.exp(m_old - m_new)
        p = jnp.exp(s - m_new)
        l_scr[...] = alpha * l_scr[...] + jnp.sum(p, axis=1, keepdims=True)
        acc_scr[...] = alpha * acc_scr[...] + _dot(p.astype(BF16), v)
        m_scr[...] = m_new

    s, v = scores(i)
    s = s + bown_scr[...]
    m0 = jnp.max(s, axis=1, keepdims=True)
    p = jnp.exp(s - m0)
    m_scr[...] = m0
    l_scr[...] = jnp.sum(p, axis=1, keepdims=True)
    acc_scr[...] = _dot(p.astype(BF16), v)

    @pl.when(i >= 1)
    def _():
        s, v = scores(i - 1)
        update(s + bprev_scr[...] + pen_col(i - 1), v)

    def far_loop(j, carry):
        s, v = scores(j)
        update(s + pen_col(j), v)
        return carry

    lax.fori_loop(0, jnp.maximum(i - 1, 0), far_loop, 0)
    o_ref[...] = (acc_scr[...] / l_scr[...]).astype(o_ref.dtype)


def _moba(qn, kn, proj, pen, table_h):
    H, S, D = qn.shape
    BS = MOBA_BLOCK
    return pl.pallas_call(
        _moba_kernel,
        grid=(H, S // BS),
        in_specs=[
            pl.BlockSpec(memory_space=pltpu.SMEM),
            pl.BlockSpec((1, BS, D), lambda h, i: (h, i, 0)),
            pl.BlockSpec((1, S, D), lambda h, i: (h, 0, 0)),
            pl.BlockSpec((S, D), lambda h, i: (0, COL_V // D + h)),
            pl.BlockSpec((1, BS, LANES), lambda h, i: (h, i, 0)),
        ],
        out_specs=pl.BlockSpec((BS, D), lambda h, i: (i, h)),
        out_shape=jax.ShapeDtypeStruct((S, H * D), BF16),
        scratch_shapes=[
            pltpu.VMEM((BS, BS), F32),
            pltpu.VMEM((BS, BS), F32),
            pltpu.VMEM((BS, 1), F32),
            pltpu.VMEM((BS, 1), F32),
            pltpu.VMEM((BS, D), F32),
        ],
        compiler_params=_cparams(("arbitrary", "arbitrary")),
        name="moba",
    )(table_h, qn, kn, proj, pen)


def _mix_kernel(ys_ref, ya_ref, ws_ref, wa_ref, gs_ref, ga_ref, bs_ref, ba_ref, o_ref):
    gs = _sigmoid(gs_ref[...].astype(F32) + bs_ref[...])
    ga = _sigmoid(ga_ref[...].astype(F32) + ba_ref[...])
    o_ref[...] = (gs * _dot(ys_ref[...], ws_ref[...])
                  + ga * _dot(ya_ref[...], wa_ref[...])).astype(o_ref.dtype)


def _mix(y_ssd, y_attn, w_ssm_out, w_attn_out, proj, b_gate, tm=1024, tn=512):
    S = y_ssd.shape[0]
    tm = min(tm, S)
    gs_col = COL_G // tn
    ga_col = (COL_G + D_MODEL) // tn
    nb = D_MODEL // tn
    return pl.pallas_call(
        _mix_kernel,
        grid=(S // tm, D_MODEL // tn),
        in_specs=[
            pl.BlockSpec((tm, D_INNER), lambda m, n: (m, 0)),
            pl.BlockSpec((tm, D_ATTN), lambda m, n: (m, 0)),
            pl.BlockSpec((D_INNER, tn), lambda m, n: (0, n)),
            pl.BlockSpec((D_ATTN, tn), lambda m, n: (0, n)),
            pl.BlockSpec((tm, tn), lambda m, n: (m, gs_col + n)),
            pl.BlockSpec((tm, tn), lambda m, n: (m, ga_col + n)),
            pl.BlockSpec((1, tn), lambda m, n: (0, n)),
            pl.BlockSpec((1, tn), lambda m, n: (0, nb + n)),
        ],
        out_specs=pl.BlockSpec((tm, tn), lambda m, n: (m, n)),
        out_shape=jax.ShapeDtypeStruct((S, D_MODEL), BF16),
        compiler_params=_cparams(("arbitrary", "arbitrary")),
        name="mix",
    )(y_ssd, y_attn, w_ssm_out, w_attn_out, proj, proj, b_gate, b_gate)


def _out_norm_kernel(x_ref, mix_ref, w_ref, nw_ref, x1_ref, h2_ref):
    x1 = x_ref[...] + _dot(mix_ref[...], w_ref[...])
    x1_ref[...] = x1
    ms = jnp.mean(x1 * x1, axis=-1, keepdims=True)
    h2_ref[...] = (x1 * lax.rsqrt(ms + EPS) * nw_ref[...]).astype(h2_ref.dtype)


def _out_norm(x, mix, w_out, ffn_norm_w, tm=512):
    S = x.shape[0]
    tm = min(tm, S)
    return pl.pallas_call(
        _out_norm_kernel,
        grid=(S // tm,),
        in_specs=[
            pl.BlockSpec((tm, D_MODEL), lambda m: (m, 0)),
            pl.BlockSpec((tm, D_MODEL), lambda m: (m, 0)),
            pl.BlockSpec((D_MODEL, D_MODEL), lambda m: (0, 0)),
            pl.BlockSpec((1, D_MODEL), lambda m: (0, 0)),
        ],
        out_specs=[
            pl.BlockSpec((tm, D_MODEL), lambda m: (m, 0)),
            pl.BlockSpec((tm, D_MODEL), lambda m: (m, 0)),
        ],
        out_shape=[
            jax.ShapeDtypeStruct((S, D_MODEL), F32),
            jax.ShapeDtypeStruct((S, D_MODEL), BF16),
        ],
        compiler_params=_cparams(("arbitrary",)),
        name="out_norm",
    )(x, mix, w_out, ffn_norm_w)


def _ffn_up_kernel(h_ref, wg_ref, wu_ref, cwg_ref, cwu_ref, cbg_ref, cbu_ref, o_ref,
                   tailg, tailu, padg, padu):
    tm = h_ref.shape[0]

    @pl.when(pl.program_id(1) == 0)
    def _():
        tailg[...] = jnp.zeros(tailg.shape, F32)
        tailu[...] = jnp.zeros(tailu.shape, F32)

    def conv(w_ref, cw_ref, cb_ref, tail, pad):
        pad[0:SUBLANES, :] = tail[...]
        pad[SUBLANES:SUBLANES + tm, :] = _dot(h_ref[...], w_ref[...])
        tail[...] = pad[tm:tm + SUBLANES, :]
        acc = cb_ref[...]
        for k in range(FFN_CONV):
            off = SUBLANES - (FFN_CONV - 1) + k
            acc = acc + cw_ref[k:k + 1, :] * pad[off:off + tm, :]
        return acc

    ug = conv(wg_ref, cwg_ref, cbg_ref, tailg, padg)
    uu = conv(wu_ref, cwu_ref, cbu_ref, tailu, padu)
    o_ref[...] = (ug * _sigmoid(ug) * uu).astype(o_ref.dtype)


def _ffn_up(h2, w_up, conv_w, conv_b, tm=1024, tn=512):
    S = h2.shape[0]
    tm = min(tm, S)
    nb = D_FF // tn
    return pl.pallas_call(
        _ffn_up_kernel,
        grid=(nb, S // tm),
        in_specs=[
            pl.BlockSpec((tm, D_MODEL), lambda n, m: (m, 0)),
            pl.BlockSpec((D_MODEL, tn), lambda n, m: (0, n)),
            pl.BlockSpec((D_MODEL, tn), lambda n, m: (0, nb + n)),
            pl.BlockSpec((FFN_CONV, tn), lambda n, m: (0, n)),
            pl.BlockSpec((FFN_CONV, tn), lambda n, m: (0, nb + n)),
            pl.BlockSpec((1, tn), lambda n, m: (0, n)),
            pl.BlockSpec((1, tn), lambda n, m: (0, nb + n)),
        ],
        out_specs=pl.BlockSpec((tm, tn), lambda n, m: (m, n)),
        out_shape=jax.ShapeDtypeStruct((S, D_FF), BF16),
        scratch_shapes=[
            pltpu.VMEM((SUBLANES, tn), F32),
            pltpu.VMEM((SUBLANES, tn), F32),
            pltpu.VMEM((tm + SUBLANES, tn), F32),
            pltpu.VMEM((tm + SUBLANES, tn), F32),
        ],
        compiler_params=_cparams(("arbitrary", "arbitrary")),
        name="ffn_up",
    )(h2, w_up, w_up, conv_w, conv_w, conv_b, conv_b)


def _ffn_down_kernel(a_ref, w_ref, x1_ref, o_ref, acc_scr):
    k = pl.program_id(1)

    @pl.when(k == 0)
    def _():
        acc_scr[...] = x1_ref[...]

    acc_scr[...] += _dot(a_ref[...], w_ref[...])

    @pl.when(k == pl.num_programs(1) - 1)
    def _():
        o_ref[...] = acc_scr[...]


def _ffn_down(act, w_down, x1, tm=1024, tk=512):
    S = act.shape[0]
    tm = min(tm, S)
    return pl.pallas_call(
        _ffn_down_kernel,
        grid=(S // tm, D_FF // tk),
        in_specs=[
            pl.BlockSpec((tm, tk), lambda m, k: (m, k)),
            pl.BlockSpec((tk, D_MODEL), lambda m, k: (k, 0)),
            pl.BlockSpec((tm, D_MODEL), lambda m, k: (m, 0)),
        ],
        out_specs=pl.BlockSpec((tm, D_MODEL), lambda m, k: (m, 0)),
        out_shape=jax.ShapeDtypeStruct((S, D_MODEL), F32),
        scratch_shapes=[pltpu.VMEM((tm, D_MODEL), F32)],
        compiler_params=_cparams(("arbitrary", "arbitrary")),
        name="ffn_down",
    )(act, w_down, x1)


def _pad_lanes(v):
    return jnp.pad(v, ((0, 0), (0, LANES - v.shape[1])))


def _layer(x, attn_norm_w, w_in, b_gate, ssm_conv_w, ssm_conv_b, ssm_dt_bias, ssm_a_log, ssm_d,
           ssm_norm_w, q_norm_w, k_norm_w, rel_bias, w_ssm_out, w_attn_out, w_out, ffn_norm_w,
           w_up, ffn_conv_w, ffn_conv_b, w_down):
    o_z, o_xbc = 0, D_INNER
    o_dt = o_xbc + D_XBC
    o_q = o_dt + SSM_HEADS
    o_g = o_q + 3 * D_ATTN
    w_main = jnp.concatenate(
        [w_in[:, o_xbc:o_dt], w_in[:, o_q:o_g], w_in[:, o_z:o_xbc], w_in[:, o_g:]], axis=1).astype(BF16)
    w_dt = _pad_lanes(w_in[:, o_dt:o_q])

    proj, dt_raw = _in_proj(x, attn_norm_w[None, :], w_main, w_dt)
    y_ssd = _ssd(proj, dt_raw, ssm_conv_w, ssm_conv_b[None, :], _pad_lanes(ssm_dt_bias[None, :]),
                 _pad_lanes(ssm_a_log[None, :]), jnp.repeat(ssm_d, SSM_HEAD_DIM)[None, :],
                 ssm_norm_w[None, :])
    qn, kn, pen = _qk_prep(proj, q_norm_w[None, :], k_norm_w[None, :])
    y_attn = _moba(qn, kn, proj, pen, rel_bias.T)
    mix = _mix(y_ssd, y_attn, w_ssm_out.astype(BF16), w_attn_out.astype(BF16), proj, b_gate[None, :])
    x1, h2 = _out_norm(x, mix, w_out.astype(BF16), ffn_norm_w[None, :])
    act = _ffn_up(h2, w_up.astype(BF16), ffn_conv_w, ffn_conv_b[None, :])
    return _ffn_down(act, w_down.astype(BF16), x1)


def kernel(x, attn_norm_w, w_in, b_gate, ssm_conv_w, ssm_conv_b, ssm_dt_bias, ssm_a_log, ssm_d,
           ssm_norm_w, q_norm_w, k_norm_w, rel_bias, w_ssm_out, w_attn_out, w_out, ffn_norm_w,
           w_up, ffn_conv_w, ffn_conv_b, w_down):
    assert x.shape[0] == 1 and attn_norm_w.shape[0] == 1
    out = _layer(x[0], attn_norm_w[0], w_in[0], b_gate[0], ssm_conv_w[0], ssm_conv_b[0],
                 ssm_dt_bias[0], ssm_a_log[0], ssm_d[0], ssm_norm_w[0], q_norm_w[0], k_norm_w[0],
                 rel_bias, w_ssm_out[0], w_attn_out[0], w_out[0], ffn_norm_w[0], w_up[0],
                 ffn_conv_w[0], ffn_conv_b[0], w_down[0])
    return out[None]
```

```python
import functools
import math

import jax
import jax.numpy as jnp
from jax import lax
from jax.experimental import pallas as pl
from jax.experimental.pallas import tpu as pltpu

F32 = jnp.float32
BF16 = jnp.bfloat16

D_MODEL = 2048
D_INNER = 4096
SSM_HEADS = 64
SSM_HEAD_DIM = 64
SSM_GROUPS = 8
SSM_HEADS_PER_GROUP = SSM_HEADS // SSM_GROUPS
SSM_STATE = 128
SSM_CONV = 4
SSM_CHUNK = 256
GROUP_CH = D_INNER // SSM_GROUPS
D_XBC = D_INNER + 2 * SSM_GROUPS * SSM_STATE
ATTN_HEADS = 16
ATTN_HEAD_DIM = 128
D_ATTN = ATTN_HEADS * ATTN_HEAD_DIM
MOBA_BLOCK = 256
MOBA_TOPK = 3
REL_BUCKETS = 32
REL_MAX_DIST = 128
D_FF = 5632
FFN_CONV = 3
EPS = 1e-6
LOG2E = 1.4426950408889634

LANES = 128
SUBLANES = 8
D_PROJ = D_XBC + 3 * D_ATTN + 2 * D_INNER
COL_Q = D_XBC
COL_K = COL_Q + D_ATTN
COL_V = COL_K + D_ATTN
COL_Z = COL_V + D_ATTN
COL_G = COL_Z + D_INNER
VMEM_LIMIT = 56 * 1024 * 1024


def _cparams(sem):
    return pltpu.CompilerParams(dimension_semantics=sem, vmem_limit_bytes=VMEM_LIMIT)


def _split3(a):
    hi = a.astype(BF16)
    r = a - hi.astype(F32)
    mid = r.astype(BF16)
    lo = (r - mid.astype(F32)).astype(BF16)
    return hi, mid, lo


def _dot(a, b):
    return jnp.dot(a, b, preferred_element_type=F32)


def _dot_nt(a, b):
    return lax.dot_general(a, b, (((1,), (1,)), ((), ())), preferred_element_type=F32)


def _sigmoid(x):
    return 1.0 / (1.0 + jnp.exp(-x))


def _in_proj_kernel(x_ref, nw_ref, w_ref, wdt_ref, o_ref, dt_ref, h_scr):
    @pl.when(pl.program_id(1) == 0)
    def _():
        x = x_ref[...]
        ms = jnp.mean(x * x, axis=-1, keepdims=True)
        h = x * lax.rsqrt(ms + EPS) * nw_ref[...]
        h_scr[...] = h.astype(BF16)
        h_hi, h_mid, _ = _split3(h)
        w_hi, w_mid, _ = _split3(wdt_ref[...])
        dt_ref[...] = _dot(h_hi, w_hi) + _dot(h_hi, w_mid) + _dot(h_mid, w_hi)

    o_ref[...] = _dot(h_scr[...], w_ref[...]).astype(o_ref.dtype)


def _in_proj(x, norm_w, w_main, w_dt, tm=1024, tn=1024):
    S = x.shape[0]
    tm = min(tm, S)
    return pl.pallas_call(
        _in_proj_kernel,
        grid=(S // tm, D_PROJ // tn),
        in_specs=[
            pl.BlockSpec((tm, D_MODEL), lambda m, n: (m, 0)),
            pl.BlockSpec((1, D_MODEL), lambda m, n: (0, 0)),
            pl.BlockSpec((D_MODEL, tn), lambda m, n: (0, n)),
            pl.BlockSpec((D_MODEL, LANES), lambda m, n: (0, 0)),
        ],
        out_specs=[
            pl.BlockSpec((tm, tn), lambda m, n: (m, n)),
            pl.BlockSpec((tm, LANES), lambda m, n: (m, 0)),
        ],
        out_shape=[
            jax.ShapeDtypeStruct((S, D_PROJ), BF16),
            jax.ShapeDtypeStruct((S, LANES), F32),
        ],
        scratch_shapes=[pltpu.VMEM((tm, D_MODEL), BF16)],
        compiler_params=_cparams(("arbitrary", "arbitrary")),
        name="in_proj",
    )(x, norm_w, w_main, w_dt)


def _conv_silu(raw_ref, tail_ref, pad_ref, w_ref, b_ref, g, taps):
    L = raw_ref.shape[0]
    pad_ref[0:SUBLANES, :] = tail_ref[g]
    pad_ref[SUBLANES:SUBLANES + L, :] = raw_ref[...].astype(F32)
    tail_ref[g] = pad_ref[L:L + SUBLANES, :]
    acc = b_ref[...]
    for k in range(taps):
        off = SUBLANES - (taps - 1) + k
        acc = acc + w_ref[k:k + 1, :] * pad_ref[off:off + L, :]
    return acc * _sigmoid(acc)


def _ssd_kernel(x_ref, b_ref, c_ref, z_ref, dtr_ref, cwx_ref, cwb_ref, cwc_ref,
                cbx_ref, cbb_ref, cbc_ref, dtb_ref, alog_ref, dsk_ref, nw_ref,
                o_ref,
                tailx, tailb, tailc, padx, padb, padc, acs_scr, acst_scr, dtt_scr, ht_scr):
    c = pl.program_id(0)
    g = pl.program_id(1)
    L = SSM_CHUNK
    QW = 4 * SSM_HEAD_DIM

    @pl.when(c == 0)
    def _():
        tailx[g] = jnp.zeros(tailx.shape[1:], F32)
        tailb[g] = jnp.zeros(tailb.shape[1:], F32)
        tailc[g] = jnp.zeros(tailc.shape[1:], F32)
        ht_scr[g] = jnp.zeros(ht_scr.shape[1:], F32)

    @pl.when(g == 0)
    def _():
        t = dtr_ref[...] + dtb_ref[...]
        dt = jnp.maximum(t, 0.0) + jnp.log(1.0 + jnp.exp(-jnp.abs(t)))
        a = -jnp.exp(alog_ref[...])
        da = dt * a
        ri = lax.broadcasted_iota(jnp.int32, (L, L), 0)
        ci = lax.broadcasted_iota(jnp.int32, (L, L), 1)
        tri = jnp.where(ri >= ci, 1.0, 0.0).astype(BF16)
        hi, mid, lo = _split3(da)
        a_cs = _dot(tri, hi) + _dot(tri, mid) + _dot(tri, lo)
        acs_scr[...] = a_cs
        acst_scr[...] = a_cs.T
        dtt_scr[...] = dt.T

    xa = _conv_silu(x_ref, tailx, padx, cwx_ref, cbx_ref, g, SSM_CONV)
    ba = _conv_silu(b_ref, tailb, padb, cwb_ref, cbb_ref, g, SSM_CONV)
    ca = _conv_silu(c_ref, tailc, padc, cwc_ref, cbc_ref, g, SSM_CONV)

    xb = xa.astype(BF16)
    bt = ba.T
    cb = _dot(ca.astype(BF16), bt.astype(BF16))
    a_cs = acs_scr[...]
    ri = lax.broadcasted_iota(jnp.int32, (L, L), 0)
    ci = lax.broadcasted_iota(jnp.int32, (L, L), 1)
    causal = ri >= ci
    lane_h = lax.broadcasted_iota(jnp.int32, (L, LANES), 1)
    lane_q = lax.broadcasted_iota(jnp.int32, (1, QW), 1)

    ys = []
    for q in range(GROUP_CH // QW):
        xq = xb[:, q * QW:(q + 1) * QW]
        hq = ht_scr[g, :, q * QW:(q + 1) * QW]
        hqb = hq.astype(BF16)
        acc = jnp.zeros((L, QW), F32)
        st = jnp.zeros((SSM_STATE, QW), F32)
        cdq = jnp.zeros((1, QW), F32)
        for j in range(QW // SSM_HEAD_DIM):
            h = g * SSM_HEADS_PER_GROUP + q * (QW // SSM_HEAD_DIM) + j
            a_row = acst_scr[pl.ds(h, 1), :]
            dt_row = dtt_scr[pl.ds(h, 1), :]
            a_col = jnp.sum(jnp.where(lane_h == h, a_cs, 0.0), axis=1, keepdims=True)
            dmat = jnp.where(causal, a_col - a_row, -jnp.inf)
            mh = (cb * jnp.exp(dmat) * dt_row).astype(BF16)
            lm = (lane_q >= j * SSM_HEAD_DIM) & (lane_q < (j + 1) * SSM_HEAD_DIM)
            xm = jnp.where(lm, xq, jnp.zeros_like(xq))
            hm = jnp.where(lm, hqb, jnp.zeros_like(hqb))
            ce = (ca * jnp.exp(a_col)).astype(BF16)
            acc = acc + _dot(mh, xm) + _dot(ce, hm)
            a_last = a_row[:, L - 1:L]
            w_row = dt_row * jnp.exp(a_last - a_row)
            st = st + _dot((bt * w_row).astype(BF16), xm)
            cdq = jnp.where(lm, jnp.exp(a_last), cdq)
        ht_scr[g, :, q * QW:(q + 1) * QW] = hq * cdq + st
        ys.append(acc + xa[:, q * QW:(q + 1) * QW] * dsk_ref[:, q * QW:(q + 1) * QW])

    z = z_ref[...].astype(F32)
    sz = z * _sigmoid(z)
    ygs = [ys[q] * sz[:, q * QW:(q + 1) * QW] for q in range(len(ys))]
    ss = sum(jnp.sum(v * v, axis=1, keepdims=True) for v in ygs)
    scale = lax.rsqrt(ss * (1.0 / GROUP_CH) + EPS)
    for q in range(len(ys)):
        o_ref[:, q * QW:(q + 1) * QW] = (
            ygs[q] * scale * nw_ref[:, q * QW:(q + 1) * QW]).astype(o_ref.dtype)


def _ssd(proj, dt_raw, conv_w, conv_b, dt_bias, a_log, d_exp, norm_w):
    S = proj.shape[0]
    L, G, N = SSM_CHUNK, SSM_GROUPS, SSM_STATE
    xblk = D_INNER // GROUP_CH
    bcol = D_INNER // N
    ccol = bcol + G
    zcol = COL_Z // GROUP_CH
    return pl.pallas_call(
        _ssd_kernel,
        grid=(S // L, G),
        in_specs=[
            pl.BlockSpec((L, GROUP_CH), lambda c, g: (c, g)),
            pl.BlockSpec((L, N), lambda c, g: (c, bcol + g)),
            pl.BlockSpec((L, N), lambda c, g: (c, ccol + g)),
            pl.BlockSpec((L, GROUP_CH), lambda c, g: (c, zcol + g)),
            pl.BlockSpec((L, LANES), lambda c, g: (c, 0)),
            pl.BlockSpec((SSM_CONV, GROUP_CH), lambda c, g: (0, g)),
            pl.BlockSpec((SSM_CONV, N), lambda c, g: (0, bcol + g)),
            pl.BlockSpec((SSM_CONV, N), lambda c, g: (0, ccol + g)),
            pl.BlockSpec((1, GROUP_CH), lambda c, g: (0, g)),
            pl.BlockSpec((1, N), lambda c, g: (0, bcol + g)),
            pl.BlockSpec((1, N), lambda c, g: (0, ccol + g)),
            pl.BlockSpec((1, LANES), lambda c, g: (0, 0)),
            pl.BlockSpec((1, LANES), lambda c, g: (0, 0)),
            pl.BlockSpec((1, GROUP_CH), lambda c, g: (0, g)),
            pl.BlockSpec((1, GROUP_CH), lambda c, g: (0, g)),
        ],
        out_specs=pl.BlockSpec((L, GROUP_CH), lambda c, g: (c, g)),
        out_shape=jax.ShapeDtypeStruct((S, D_INNER), BF16),
        scratch_shapes=[
            pltpu.VMEM((G, SUBLANES, GROUP_CH), F32),
            pltpu.VMEM((G, SUBLANES, N), F32),
            pltpu.VMEM((G, SUBLANES, N), F32),
            pltpu.VMEM((L + SUBLANES, GROUP_CH), F32),
            pltpu.VMEM((L + SUBLANES, N), F32),
            pltpu.VMEM((L + SUBLANES, N), F32),
            pltpu.VMEM((L, LANES), F32),
            pltpu.VMEM((LANES, L), F32),
            pltpu.VMEM((LANES, L), F32),
            pltpu.VMEM((G, N, GROUP_CH), F32),
        ],
        compiler_params=_cparams(("arbitrary", "arbitrary")),
        name="ssd",
    )(proj, proj, proj, proj, dt_raw, conv_w, conv_w, conv_w, conv_b, conv_b, conv_b,
      dt_bias, a_log, d_exp, norm_w)


def _qk_prep_kernel(q_ref, k_ref, v_ref, qw_ref, kw_ref, qt_ref, kn_ref, vt_ref, pen_ref, kbar_scr):
    S = q_ref.shape[0]
    BS = MOBA_BLOCK
    nb = S // BS
    nbp = kbar_scr.shape[0]
    kbar_scr[...] = jnp.zeros(kbar_scr.shape, F32)

    def kloop(b, carry):
        rows = pl.ds(pl.multiple_of(b * BS, BS), BS)
        k = k_ref[rows, :].astype(F32)
        kn = k * lax.rsqrt(jnp.mean(k * k, axis=-1, keepdims=True) + EPS) * kw_ref[...]
        kn_ref[0, rows, :] = kn.astype(BF16)
        kbar_scr[pl.ds(b, 1), :] = jnp.mean(kn, axis=0, keepdims=True)
        vt_ref[0, b] = v_ref[rows, :].astype(F32).T.astype(BF16)
        return carry

    lax.fori_loop(0, nb, kloop, 0, unroll=2)
    kb_hi, kb_mid, _ = _split3(kbar_scr[...])

    def qloop(i, carry):
        rows = pl.ds(pl.multiple_of(i * BS, BS), BS)
        q = q_ref[rows, :].astype(F32)
        qn = q * lax.rsqrt(jnp.mean(q * q, axis=-1, keepdims=True) + EPS) * qw_ref[...]
        qnt = (qn * (ATTN_HEAD_DIM ** -0.5)).T
        qt_ref[0, i] = (qnt * LOG2E).astype(BF16)
        q_hi, q_mid, _ = _split3(qnt)
        gate = _dot(kb_hi, q_hi) + _dot(kb_mid, q_hi) + _dot(kb_hi, q_mid)
        blk = lax.broadcasted_iota(jnp.int32, gate.shape, 0)
        blk_f = blk.astype(F32)
        gate = jnp.where(blk < i, gate, -jnp.inf)
        pen = jnp.full(gate.shape, -jnp.inf, F32)
        for _ in range(MOBA_TOPK):
            mx = jnp.max(gate, axis=0, keepdims=True)
            cand = jnp.where((gate == mx) & (mx > -jnp.inf), blk_f, float(nbp))
            idx = jnp.min(cand, axis=0, keepdims=True)
            hit = blk_f == idx
            pen = jnp.where(hit, 0.0, pen)
            gate = jnp.where(hit, -jnp.inf, gate)
        pen_ref[0, i] = pen
        return carry

    lax.fori_loop(0, nb, qloop, 0, unroll=2)


def _qk_prep(proj, q_norm_w, k_norm_w):
    S = proj.shape[0]
    H, D, BS = ATTN_HEADS, ATTN_HEAD_DIM, MOBA_BLOCK
    nb = S // BS
    nbp = -(-nb // SUBLANES) * SUBLANES
    return pl.pallas_call(
        _qk_prep_kernel,
        grid=(H,),
        in_specs=[
            pl.BlockSpec((S, D), lambda h: (0, COL_Q // D + h)),
            pl.BlockSpec((S, D), lambda h: (0, COL_K // D + h)),
            pl.BlockSpec((S, D), lambda h: (0, COL_V // D + h)),
            pl.BlockSpec((1, D), lambda h: (0, 0)),
            pl.BlockSpec((1, D), lambda h: (0, 0)),
        ],
        out_specs=[
            pl.BlockSpec((1, nb, D, BS), lambda h: (h, 0, 0, 0)),
            pl.BlockSpec((1, S, D), lambda h: (h, 0, 0)),
            pl.BlockSpec((1, nb, D, BS), lambda h: (h, 0, 0, 0)),
            pl.BlockSpec((1, nb, nbp, BS), lambda h: (h, 0, 0, 0)),
        ],
        out_shape=[
            jax.ShapeDtypeStruct((H, nb, D, BS), BF16),
            jax.ShapeDtypeStruct((H, S, D), BF16),
            jax.ShapeDtypeStruct((H, nb, D, BS), BF16),
            jax.ShapeDtypeStruct((H, nb, nbp, BS), F32),
        ],
        scratch_shapes=[pltpu.VMEM((nbp, D), F32)],
        compiler_params=_cparams(("arbitrary",)),
        name="qk_prep",
    )(proj, proj, proj, q_norm_w, k_norm_w)


def _rel_bucket(dist):
    n = jnp.maximum(dist, 0)
    max_exact = REL_BUCKETS // 2
    nf = jnp.maximum(n, 1).astype(F32)
    large = max_exact + (jnp.log(nf / max_exact) / math.log(REL_MAX_DIST / max_exact)
                         * (REL_BUCKETS - max_exact)).astype(jnp.int32)
    large = jnp.minimum(large, REL_BUCKETS - 1)
    return jnp.where(n < max_exact, n, large)


MOBA_HEADS_PER_STEP = 2


def _moba_kernel(tbl_ref, qt_ref, k_ref, vt_ref, pen_ref, o_ref,
                 bown_scr, bprev_scr, m_scr, l_scr, acc_scr, s0_scr, s1_scr):
    hp = pl.program_id(0)
    i = pl.program_id(1)
    BS, HB = MOBA_BLOCK, MOBA_HEADS_PER_STEP
    D = ATTN_HEAD_DIM

    @pl.when(i == 0)
    def _():
        kl = lax.broadcasted_iota(jnp.int32, (BS, BS), 0)
        ql = lax.broadcasted_iota(jnp.int32, (BS, BS), 1)
        d = ql - kl
        b_own = _rel_bucket(d)
        b_prev = _rel_bucket(d + BS)
        for a in range(HB):
            h = hp * HB + a
            far = tbl_ref[h, REL_BUCKETS - 1]

            def lookup(bucket):
                val = jnp.zeros((BS, BS), F32)
                for b in range(REL_BUCKETS):
                    val = jnp.where(bucket == b, tbl_ref[h, b], val)
                return (val - far) * LOG2E

            bown_scr[a] = jnp.where(d >= 0, lookup(b_own), -jnp.inf)
            bprev_scr[a] = lookup(b_prev)

    def krows(a, j):
        return k_ref[a, pl.ds(pl.multiple_of(j * BS, BS), BS), :]

    jp = jnp.maximum(i - 1, 0)
    no_prev = jnp.where(i == 0, -jnp.inf, 0.0)
    for a in range(HB):
        qt = qt_ref[a, 0]
        s_own = _dot(krows(a, i), qt) + bown_scr[a]
        s_prev = _dot(krows(a, jp), qt) + bprev_scr[a] + (pen_ref[a, 0, pl.ds(jp, 1), :] + no_prev)
        m0 = jnp.maximum(jnp.max(s_own, axis=0, keepdims=True), jnp.max(s_prev, axis=0, keepdims=True))
        p_own = jnp.exp2(s_own - m0)
        p_prev = jnp.exp2(s_prev - m0)
        m_scr[a] = m0
        l_scr[a] = jnp.sum(p_own, axis=0, keepdims=True) + jnp.sum(p_prev, axis=0, keepdims=True)
        acc_scr[a] = (_dot(vt_ref[a, i], p_own.astype(BF16))
                      + _dot(vt_ref[a, jp], p_prev.astype(BF16)))

    n_far = jnp.maximum(i - 1, 0)
    nb = k_ref.shape[1] // BS

    def stage_scores(j, slot_scr):
        jc = jnp.minimum(j, nb - 1)
        for a in range(HB):
            slot_scr[a] = _dot(krows(a, jc), qt_ref[a, 0])

    def consume(j, slot_scr, mask_tail):
        jc = jnp.minimum(j, nb - 1)
        for a in range(HB):
            pr = pen_ref[a, 0, pl.ds(jc, 1), :]
            if mask_tail:
                pr = pr + jnp.where(j >= n_far, -jnp.inf, 0.0)
            s = slot_scr[a] + pr
            m_old = m_scr[a]
            m_new = jnp.maximum(m_old, jnp.max(s, axis=0, keepdims=True))
            alpha = jnp.exp2(m_old - m_new)
            p = jnp.exp2(s - m_new)
            l_scr[a] = alpha * l_scr[a] + jnp.sum(p, axis=0, keepdims=True)
            acc_scr[a] = alpha * acc_scr[a] + _dot(vt_ref[a, jc], p.astype(BF16))
            m_scr[a] = m_new

    stage_scores(0, s0_scr)

    def far_loop(g, carry):
        j = 2 * g
        stage_scores(j + 1, s1_scr)
        consume(j, s0_scr, False)
        stage_scores(j + 2, s0_scr)
        consume(j + 1, s1_scr, True)
        return carry

    lax.fori_loop(0, (n_far + 1) // 2, far_loop, 0)
    for a in range(HB):
        o_ref[:, a * D:(a + 1) * D] = (acc_scr[a] / l_scr[a]).T.astype(o_ref.dtype)


def _moba(qt, kn, vt, pen, table_h):
    H, nb, D, BS = qt.shape
    S = nb * BS
    nbp = pen.shape[2]
    HB = MOBA_HEADS_PER_STEP
    return pl.pallas_call(
        _moba_kernel,
        grid=(H // HB, nb),
        in_specs=[
            pl.BlockSpec(memory_space=pltpu.SMEM),
            pl.BlockSpec((HB, 1, D, BS), lambda hp, i: (hp, i, 0, 0)),
            pl.BlockSpec((HB, S, D), lambda hp, i: (hp, 0, 0)),
            pl.BlockSpec((HB, nb, D, BS), lambda hp, i: (hp, 0, 0, 0)),
            pl.BlockSpec((HB, 1, nbp, BS), lambda hp, i: (hp, i, 0, 0)),
        ],
        out_specs=pl.BlockSpec((BS, HB * D), lambda hp, i: (i, hp)),
        out_shape=jax.ShapeDtypeStruct((S, H * D), BF16),
        scratch_shapes=[
            pltpu.VMEM((HB, BS, BS), F32),
            pltpu.VMEM((HB, BS, BS), F32),
            pltpu.VMEM((HB, 1, BS), F32),
            pltpu.VMEM((HB, 1, BS), F32),
            pltpu.VMEM((HB, D, BS), F32),
            pltpu.VMEM((HB, BS, BS), F32),
            pltpu.VMEM((HB, BS, BS), F32),
        ],
        compiler_params=_cparams(("arbitrary", "arbitrary")),
        name="moba",
    )(table_h, qt, kn, vt, pen)


def _mix_kernel(ys_ref, ya_ref, ws_ref, wa_ref, gs_ref, ga_ref, bs_ref, ba_ref, o_ref):
    gs = _sigmoid(gs_ref[...].astype(F32) + bs_ref[...])
    ga = _sigmoid(ga_ref[...].astype(F32) + ba_ref[...])
    o_ref[...] = (gs * _dot(ys_ref[...], ws_ref[...])
                  + ga * _dot(ya_ref[...], wa_ref[...])).astype(o_ref.dtype)


def _mix(y_ssd, y_attn, w_ssm_out, w_attn_out, proj, b_gate, tm=1024, tn=512):
    S = y_ssd.shape[0]
    tm = min(tm, S)
    gs_col = COL_G // tn
    ga_col = (COL_G + D_MODEL) // tn
    nb = D_MODEL // tn
    return pl.pallas_call(
        _mix_kernel,
        grid=(S // tm, D_MODEL // tn),
        in_specs=[
            pl.BlockSpec((tm, D_INNER), lambda m, n: (m, 0)),
            pl.BlockSpec((tm, D_ATTN), lambda m, n: (m, 0)),
            pl.BlockSpec((D_INNER, tn), lambda m, n: (0, n)),
            pl.BlockSpec((D_ATTN, tn), lambda m, n: (0, n)),
            pl.BlockSpec((tm, tn), lambda m, n: (m, gs_col + n)),
            pl.BlockSpec((tm, tn), lambda m, n: (m, ga_col + n)),
            pl.BlockSpec((1, tn), lambda m, n: (0, n)),
            pl.BlockSpec((1, tn), lambda m, n: (0, nb + n)),
        ],
        out_specs=pl.BlockSpec((tm, tn), lambda m, n: (m, n)),
        out_shape=jax.ShapeDtypeStruct((S, D_MODEL), BF16),
        compiler_params=_cparams(("arbitrary", "arbitrary")),
        name="mix",
    )(y_ssd, y_attn, w_ssm_out, w_attn_out, proj, proj, b_gate, b_gate)


def _out_norm_kernel(x_ref, mix_ref, w_ref, nw_ref, x1_ref, h2_ref):
    x1 = x_ref[...] + _dot(mix_ref[...], w_ref[...])
    x1_ref[...] = x1
    ms = jnp.mean(x1 * x1, axis=-1, keepdims=True)
    h2_ref[...] = (x1 * lax.rsqrt(ms + EPS) * nw_ref[...]).astype(h2_ref.dtype)


def _out_norm(x, mix, w_out, ffn_norm_w, tm=512):
    S = x.shape[0]
    tm = min(tm, S)
    return pl.pallas_call(
        _out_norm_kernel,
        grid=(S // tm,),
        in_specs=[
            pl.BlockSpec((tm, D_MODEL), lambda m: (m, 0)),
            pl.BlockSpec((tm, D_MODEL), lambda m: (m, 0)),
            pl.BlockSpec((D_MODEL, D_MODEL), lambda m: (0, 0)),
            pl.BlockSpec((1, D_MODEL), lambda m: (0, 0)),
        ],
        out_specs=[
            pl.BlockSpec((tm, D_MODEL), lambda m: (m, 0)),
            pl.BlockSpec((tm, D_MODEL), lambda m: (m, 0)),
        ],
        out_shape=[
            jax.ShapeDtypeStruct((S, D_MODEL), F32),
            jax.ShapeDtypeStruct((S, D_MODEL), BF16),
        ],
        compiler_params=_cparams(("arbitrary",)),
        name="out_norm",
    )(x, mix, w_out, ffn_norm_w)


def _ffn_up_kernel(h_ref, wg_ref, wu_ref, cwg_ref, cwu_ref, cbg_ref, cbu_ref, o_ref,
                   tailg, tailu, padg, padu):
    tm = h_ref.shape[0]

    @pl.when(pl.program_id(1) == 0)
    def _():
        tailg[...] = jnp.zeros(tailg.shape, F32)
        tailu[...] = jnp.zeros(tailu.shape, F32)

    def conv(w_ref, cw_ref, cb_ref, tail, pad):
        pad[0:SUBLANES, :] = tail[...]
        pad[SUBLANES:SUBLANES + tm, :] = _dot(h_ref[...], w_ref[...])
        tail[...] = pad[tm:tm + SUBLANES, :]
        acc = cb_ref[...]
        for k in range(FFN_CONV):
            off = SUBLANES - (FFN_CONV - 1) + k
            acc = acc + cw_ref[k:k + 1, :] * pad[off:off + tm, :]
        return acc

    ug = conv(wg_ref, cwg_ref, cbg_ref, tailg, padg)
    uu = conv(wu_ref, cwu_ref, cbu_ref, tailu, padu)
    o_ref[...] = (ug * _sigmoid(ug) * uu).astype(o_ref.dtype)


def _ffn_up(h2, w_up, conv_w, conv_b, tm=1024, tn=512):
    S = h2.shape[0]
    tm = min(tm, S)
    nb = D_FF // tn
    return pl.pallas_call(
        _ffn_up_kernel,
        grid=(nb, S // tm),
        in_specs=[
            pl.BlockSpec((tm, D_MODEL), lambda n, m: (m, 0)),
            pl.BlockSpec((D_MODEL, tn), lambda n, m: (0, n)),
            pl.BlockSpec((D_MODEL, tn), lambda n, m: (0, nb + n)),
            pl.BlockSpec((FFN_CONV, tn), lambda n, m: (0, n)),
            pl.BlockSpec((FFN_CONV, tn), lambda n, m: (0, nb + n)),
            pl.BlockSpec((1, tn), lambda n, m: (0, n)),
            pl.BlockSpec((1, tn), lambda n, m: (0, nb + n)),
        ],
        out_specs=pl.BlockSpec((tm, tn), lambda n, m: (m, n)),
        out_shape=jax.ShapeDtypeStruct((S, D_FF), BF16),
        scratch_shapes=[
            pltpu.VMEM((SUBLANES, tn), F32),
            pltpu.VMEM((SUBLANES, tn), F32),
            pltpu.VMEM((tm + SUBLANES, tn), F32),
            pltpu.VMEM((tm + SUBLANES, tn), F32),
        ],
        compiler_params=_cparams(("arbitrary", "arbitrary")),
        name="ffn_up",
    )(h2, w_up, w_up, conv_w, conv_w, conv_b, conv_b)


def _ffn_down_kernel(a_ref, w_ref, x1_ref, o_ref, acc_scr):
    k = pl.program_id(1)

    @pl.when(k == 0)
    def _():
        acc_scr[...] = x1_ref[...]

    acc_scr[...] += _dot(a_ref[...], w_ref[...])

    @pl.when(k == pl.num_programs(1) - 1)
    def _():
        o_ref[...] = acc_scr[...]


def _ffn_down(act, w_down, x1, tm=1024, tk=512):
    S = act.shape[0]
    tm = min(tm, S)
    return pl.pallas_call(
        _ffn_down_kernel,
        grid=(S // tm, D_FF // tk),
        in_specs=[
            pl.BlockSpec((tm, tk), lambda m, k: (m, k)),
            pl.BlockSpec((tk, D_MODEL), lambda m, k: (k, 0)),
            pl.BlockSpec((tm, D_MODEL), lambda m, k: (m, 0)),
        ],
        out_specs=pl.BlockSpec((tm, D_MODEL), lambda m, k: (m, 0)),
        out_shape=jax.ShapeDtypeStruct((S, D_MODEL), F32),
        scratch_shapes=[pltpu.VMEM((tm, D_MODEL), F32)],
        compiler_params=_cparams(("arbitrary", "arbitrary")),
        name="ffn_down",
    )(act, w_down, x1)


def _pad_lanes(v):
    return jnp.pad(v, ((0, 0), (0, LANES - v.shape[1])))


def _layer(x, attn_norm_w, w_in, b_gate, ssm_conv_w, ssm_conv_b, ssm_dt_bias, ssm_a_log, ssm_d,
           ssm_norm_w, q_norm_w, k_norm_w, rel_bias, w_ssm_out, w_attn_out, w_out, ffn_norm_w,
           w_up, ffn_conv_w, ffn_conv_b, w_down):
    o_z, o_xbc = 0, D_INNER
    o_dt = o_xbc + D_XBC
    o_q = o_dt + SSM_HEADS
    o_g = o_q + 3 * D_ATTN
    w_main = jnp.concatenate(
        [w_in[:, o_xbc:o_dt], w_in[:, o_q:o_g], w_in[:, o_z:o_xbc], w_in[:, o_g:]], axis=1).astype(BF16)
    w_dt = _pad_lanes(w_in[:, o_dt:o_q])

    proj, dt_raw = _in_proj(x, attn_norm_w[None, :], w_main, w_dt)
    y_ssd = _ssd(proj, dt_raw, ssm_conv_w, ssm_conv_b[None, :], _pad_lanes(ssm_dt_bias[None, :]),
                 _pad_lanes(ssm_a_log[None, :]), jnp.repeat(ssm_d, SSM_HEAD_DIM)[None, :],
                 ssm_norm_w[None, :])
    qt, kn, vt, pen = _qk_prep(proj, q_norm_w[None, :], k_norm_w[None, :])
    y_attn = _moba(qt, kn, vt, pen, rel_bias.T)
    mix = _mix(y_ssd, y_attn, w_ssm_out.astype(BF16), w_attn_out.astype(BF16), proj, b_gate[None, :])
    x1, h2 = _out_norm(x, mix, w_out.astype(BF16), ffn_norm_w[None, :])
    act = _ffn_up(h2, w_up.astype(BF16), ffn_conv_w, ffn_conv_b[None, :])
    return _ffn_down(act, w_down.astype(BF16), x1)


def kernel(x, attn_norm_w, w_in, b_gate, ssm_conv_w, ssm_conv_b, ssm_dt_bias, ssm_a_log, ssm_d,
           ssm_norm_w, q_norm_w, k_norm_w, rel_bias, w_ssm_out, w_attn_out, w_out, ffn_norm_w,
           w_up, ffn_conv_w, ffn_conv_b, w_down):
    assert x.shape[0] == 1 and attn_norm_w.shape[0] == 1
    out = _layer(x[0], attn_norm_w[0], w_in[0], b_gate[0], ssm_conv_w[0], ssm_conv_b[0],
                 ssm_dt_bias[0], ssm_a_log[0], ssm_d[0], ssm_norm_w[0], q_norm_w[0], k_norm_w[0],
                 rel_bias, w_ssm_out[0], w_attn_out[0], w_out[0], ffn_norm_w[0], w_up[0],
                 ffn_conv_w[0], ffn_conv_b[0], w_down[0])
    return out[None]
```

```python
import functools
import math

import jax
import jax.numpy as jnp
from jax import lax
from jax.experimental import pallas as pl
from jax.experimental.pallas import tpu as pltpu

F32 = jnp.float32
BF16 = jnp.bfloat16

D_MODEL = 2048
D_INNER = 4096
SSM_HEADS = 64
SSM_HEAD_DIM = 64
SSM_GROUPS = 8
SSM_HEADS_PER_GROUP = SSM_HEADS // SSM_GROUPS
SSM_STATE = 128
SSM_CONV = 4
SSM_CHUNK = 256
GROUP_CH = D_INNER // SSM_GROUPS
D_XBC = D_INNER + 2 * SSM_GROUPS * SSM_STATE
ATTN_HEADS = 16
ATTN_HEAD_DIM = 128
D_ATTN = ATTN_HEADS * ATTN_HEAD_DIM
MOBA_BLOCK = 256
MOBA_TOPK = 3
REL_BUCKETS = 32
REL_MAX_DIST = 128
D_FF = 5632
FFN_CONV = 3
EPS = 1e-6
LOG2E = 1.4426950408889634

LANES = 128
SUBLANES = 8
D_PROJ = D_XBC + 3 * D_ATTN + 2 * D_INNER
COL_Q = D_XBC
COL_K = COL_Q + D_ATTN
COL_V = COL_K + D_ATTN
COL_Z = COL_V + D_ATTN
COL_G = COL_Z + D_INNER
VMEM_LIMIT = 56 * 1024 * 1024


def _cparams(sem):
    return pltpu.CompilerParams(dimension_semantics=sem, vmem_limit_bytes=VMEM_LIMIT)


def _split3(a):
    hi = a.astype(BF16)
    r = a - hi.astype(F32)
    mid = r.astype(BF16)
    lo = (r - mid.astype(F32)).astype(BF16)
    return hi, mid, lo


def _dot(a, b):
    return jnp.dot(a, b, preferred_element_type=F32)


def _dot_nt(a, b):
    return lax.dot_general(a, b, (((1,), (1,)), ((), ())), preferred_element_type=F32)


def _sigmoid(x):
    return 1.0 / (1.0 + jnp.exp(-x))


def _in_proj_kernel(x_ref, nw_ref, w_ref, wdt_ref, o_ref, dt_ref, h_scr):
    @pl.when(pl.program_id(1) == 0)
    def _():
        x = x_ref[...]
        ms = jnp.mean(x * x, axis=-1, keepdims=True)
        h = x * lax.rsqrt(ms + EPS) * nw_ref[...]
        h_scr[...] = h.astype(BF16)
        h_hi, h_mid, _ = _split3(h)
        w_hi, w_mid, _ = _split3(wdt_ref[...])
        dt_ref[...] = _dot_nt(h_hi, w_hi) + _dot_nt(h_hi, w_mid) + _dot_nt(h_mid, w_hi)

    o_ref[...] = _dot_nt(h_scr[...], w_ref[...]).astype(o_ref.dtype)


def _in_proj(x, norm_w, wt_main, wt_dt, tm=1024, tn=1024):
    S = x.shape[0]
    tm = min(tm, S)
    return pl.pallas_call(
        _in_proj_kernel,
        grid=(S // tm, D_PROJ // tn),
        in_specs=[
            pl.BlockSpec((tm, D_MODEL), lambda m, n: (m, 0)),
            pl.BlockSpec((1, D_MODEL), lambda m, n: (0, 0)),
            pl.BlockSpec((tn, D_MODEL), lambda m, n: (n, 0)),
            pl.BlockSpec((LANES, D_MODEL), lambda m, n: (0, 0)),
        ],
        out_specs=[
            pl.BlockSpec((tm, tn), lambda m, n: (m, n)),
            pl.BlockSpec((tm, LANES), lambda m, n: (m, 0)),
        ],
        out_shape=[
            jax.ShapeDtypeStruct((S, D_PROJ), BF16),
            jax.ShapeDtypeStruct((S, LANES), F32),
        ],
        scratch_shapes=[pltpu.VMEM((tm, D_MODEL), BF16)],
        compiler_params=_cparams(("arbitrary", "arbitrary")),
        name="in_proj",
    )(x, norm_w, wt_main, wt_dt)


def _conv_silu(raw_ref, tail_ref, pad_ref, w_ref, b_ref, g, taps):
    L = raw_ref.shape[0]
    pad_ref[0:SUBLANES, :] = tail_ref[g]
    pad_ref[SUBLANES:SUBLANES + L, :] = raw_ref[...].astype(F32)
    tail_ref[g] = pad_ref[L:L + SUBLANES, :]
    acc = b_ref[...]
    for k in range(taps):
        off = SUBLANES - (taps - 1) + k
        acc = acc + w_ref[k:k + 1, :] * pad_ref[off:off + L, :]
    return acc * _sigmoid(acc)


def _ssd_kernel(x_ref, b_ref, c_ref, z_ref, dtr_ref, cwx_ref, cwb_ref, cwc_ref,
                cbx_ref, cbb_ref, cbc_ref, dtb_ref, alog_ref, dsk_ref, nw_ref,
                o_ref,
                tailx, tailb, tailc, padx, padb, padc, acs_scr, acst_scr, dtt_scr, ht_scr):
    c = pl.program_id(0)
    g = pl.program_id(1)
    L = SSM_CHUNK
    QW = 4 * SSM_HEAD_DIM

    @pl.when(c == 0)
    def _():
        tailx[g] = jnp.zeros(tailx.shape[1:], F32)
        tailb[g] = jnp.zeros(tailb.shape[1:], F32)
        tailc[g] = jnp.zeros(tailc.shape[1:], F32)
        ht_scr[g] = jnp.zeros(ht_scr.shape[1:], F32)

    @pl.when(g == 0)
    def _():
        t = dtr_ref[...] + dtb_ref[...]
        dt = jnp.maximum(t, 0.0) + jnp.log(1.0 + jnp.exp(-jnp.abs(t)))
        a = -jnp.exp(alog_ref[...])
        da = dt * a
        ri = lax.broadcasted_iota(jnp.int32, (L, L), 0)
        ci = lax.broadcasted_iota(jnp.int32, (L, L), 1)
        tri = jnp.where(ri >= ci, 1.0, 0.0).astype(BF16)
        hi, mid, lo = _split3(da)
        a_cs = _dot(tri, hi) + _dot(tri, mid) + _dot(tri, lo)
        acs_scr[...] = a_cs
        acst_scr[...] = a_cs.T
        dtt_scr[...] = dt.T

    xa = _conv_silu(x_ref, tailx, padx, cwx_ref, cbx_ref, g, SSM_CONV)
    ba = _conv_silu(b_ref, tailb, padb, cwb_ref, cbb_ref, g, SSM_CONV)
    ca = _conv_silu(c_ref, tailc, padc, cwc_ref, cbc_ref, g, SSM_CONV)

    xb = xa.astype(BF16)
    bt = ba.T
    cb = _dot(ca.astype(BF16), bt.astype(BF16))
    a_cs = acs_scr[...]
    ri = lax.broadcasted_iota(jnp.int32, (L, L), 0)
    ci = lax.broadcasted_iota(jnp.int32, (L, L), 1)
    causal = ri >= ci
    lane_h = lax.broadcasted_iota(jnp.int32, (L, LANES), 1)
    lane_q = lax.broadcasted_iota(jnp.int32, (1, QW), 1)

    ys = []
    for q in range(GROUP_CH // QW):
        xq = xb[:, q * QW:(q + 1) * QW]
        hq = ht_scr[g, :, q * QW:(q + 1) * QW]
        hqb = hq.astype(BF16)
        acc = jnp.zeros((L, QW), F32)
        st = jnp.zeros((SSM_STATE, QW), F32)
        cdq = jnp.zeros((1, QW), F32)
        for j in range(QW // SSM_HEAD_DIM):
            h = g * SSM_HEADS_PER_GROUP + q * (QW // SSM_HEAD_DIM) + j
            a_row = acst_scr[pl.ds(h, 1), :]
            dt_row = dtt_scr[pl.ds(h, 1), :]
            a_col = jnp.sum(jnp.where(lane_h == h, a_cs, 0.0), axis=1, keepdims=True)
            dmat = jnp.where(causal, a_col - a_row, -jnp.inf)
            mh = (cb * jnp.exp(dmat) * dt_row).astype(BF16)
            lm = (lane_q >= j * SSM_HEAD_DIM) & (lane_q < (j + 1) * SSM_HEAD_DIM)
            xm = jnp.where(lm, xq, jnp.zeros_like(xq))
            hm = jnp.where(lm, hqb, jnp.zeros_like(hqb))
            ce = (ca * jnp.exp(a_col)).astype(BF16)
            acc = acc + _dot(mh, xm) + _dot(ce, hm)
            a_last = a_row[:, L - 1:L]
            w_row = dt_row * jnp.exp(a_last - a_row)
            st = st + _dot((bt * w_row).astype(BF16), xm)
            cdq = jnp.where(lm, jnp.exp(a_last), cdq)
        ht_scr[g, :, q * QW:(q + 1) * QW] = hq * cdq + st
        ys.append(acc + xa[:, q * QW:(q + 1) * QW] * dsk_ref[:, q * QW:(q + 1) * QW])

    z = z_ref[...].astype(F32)
    sz = z * _sigmoid(z)
    ygs = [ys[q] * sz[:, q * QW:(q + 1) * QW] for q in range(len(ys))]
    ss = sum(jnp.sum(v * v, axis=1, keepdims=True) for v in ygs)
    scale = lax.rsqrt(ss * (1.0 / GROUP_CH) + EPS)
    for q in range(len(ys)):
        o_ref[:, q * QW:(q + 1) * QW] = (
            ygs[q] * scale * nw_ref[:, q * QW:(q + 1) * QW]).astype(o_ref.dtype)


def _ssd(proj, dt_raw, conv_w, conv_b, dt_bias, a_log, d_exp, norm_w):
    S = proj.shape[0]
    L, G, N = SSM_CHUNK, SSM_GROUPS, SSM_STATE
    xblk = D_INNER // GROUP_CH
    bcol = D_INNER // N
    ccol = bcol + G
    zcol = COL_Z // GROUP_CH
    return pl.pallas_call(
        _ssd_kernel,
        grid=(S // L, G),
        in_specs=[
            pl.BlockSpec((L, GROUP_CH), lambda c, g: (c, g)),
            pl.BlockSpec((L, N), lambda c, g: (c, bcol + g)),
            pl.BlockSpec((L, N), lambda c, g: (c, ccol + g)),
            pl.BlockSpec((L, GROUP_CH), lambda c, g: (c, zcol + g)),
            pl.BlockSpec((L, LANES), lambda c, g: (c, 0)),
            pl.BlockSpec((SSM_CONV, GROUP_CH), lambda c, g: (0, g)),
            pl.BlockSpec((SSM_CONV, N), lambda c, g: (0, bcol + g)),
            pl.BlockSpec((SSM_CONV, N), lambda c, g: (0, ccol + g)),
            pl.BlockSpec((1, GROUP_CH), lambda c, g: (0, g)),
            pl.BlockSpec((1, N), lambda c, g: (0, bcol + g)),
            pl.BlockSpec((1, N), lambda c, g: (0, ccol + g)),
            pl.BlockSpec((1, LANES), lambda c, g: (0, 0)),
            pl.BlockSpec((1, LANES), lambda c, g: (0, 0)),
            pl.BlockSpec((1, GROUP_CH), lambda c, g: (0, g)),
            pl.BlockSpec((1, GROUP_CH), lambda c, g: (0, g)),
        ],
        out_specs=pl.BlockSpec((L, GROUP_CH), lambda c, g: (c, g)),
        out_shape=jax.ShapeDtypeStruct((S, D_INNER), BF16),
        scratch_shapes=[
            pltpu.VMEM((G, SUBLANES, GROUP_CH), F32),
            pltpu.VMEM((G, SUBLANES, N), F32),
            pltpu.VMEM((G, SUBLANES, N), F32),
            pltpu.VMEM((L + SUBLANES, GROUP_CH), F32),
            pltpu.VMEM((L + SUBLANES, N), F32),
            pltpu.VMEM((L + SUBLANES, N), F32),
            pltpu.VMEM((L, LANES), F32),
            pltpu.VMEM((LANES, L), F32),
            pltpu.VMEM((LANES, L), F32),
            pltpu.VMEM((G, N, GROUP_CH), F32),
        ],
        compiler_params=_cparams(("arbitrary", "arbitrary")),
        name="ssd",
    )(proj, proj, proj, proj, dt_raw, conv_w, conv_w, conv_w, conv_b, conv_b, conv_b,
      dt_bias, a_log, d_exp, norm_w)


def _qk_prep_kernel(q_ref, k_ref, v_ref, qw_ref, kw_ref, qt_ref, kn_ref, vt_ref, pen_ref, kbar_scr):
    S = q_ref.shape[0]
    BS = MOBA_BLOCK
    nb = S // BS
    nbp = kbar_scr.shape[0]
    kbar_scr[...] = jnp.zeros(kbar_scr.shape, F32)

    def kloop(b, carry):
        rows = pl.ds(pl.multiple_of(b * BS, BS), BS)
        k = k_ref[rows, :].astype(F32)
        kn = k * lax.rsqrt(jnp.mean(k * k, axis=-1, keepdims=True) + EPS) * kw_ref[...]
        kn_ref[0, rows, :] = kn.astype(BF16)
        kbar_scr[pl.ds(b, 1), :] = jnp.mean(kn, axis=0, keepdims=True)
        vt_ref[0, b, 0:ATTN_HEAD_DIM, :] = v_ref[rows, :].astype(F32).T.astype(BF16)
        vt_ref[0, b, ATTN_HEAD_DIM:, :] = jnp.ones((MOBA_VT_ROWS - ATTN_HEAD_DIM, BS), BF16)
        return carry

    lax.fori_loop(0, nb, kloop, 0, unroll=2)
    kb_hi, kb_mid, _ = _split3(kbar_scr[...])

    def qloop(i, carry):
        rows = pl.ds(pl.multiple_of(i * BS, BS), BS)
        q = q_ref[rows, :].astype(F32)
        qn = q * lax.rsqrt(jnp.mean(q * q, axis=-1, keepdims=True) + EPS) * qw_ref[...]
        qnt = (qn * (ATTN_HEAD_DIM ** -0.5)).T
        qt_ref[0, i] = (qnt * LOG2E).astype(BF16)
        q_hi, q_mid, _ = _split3(qnt)
        gate = _dot(kb_hi, q_hi) + _dot(kb_mid, q_hi) + _dot(kb_hi, q_mid)
        blk = lax.broadcasted_iota(jnp.int32, gate.shape, 0)
        blk_f = blk.astype(F32)
        gate = jnp.where(blk < i, gate, -jnp.inf)
        pen = jnp.full(gate.shape, -jnp.inf, F32)
        for _ in range(MOBA_TOPK):
            mx = jnp.max(gate, axis=0, keepdims=True)
            cand = jnp.where((gate == mx) & (mx > -jnp.inf), blk_f, float(nbp))
            idx = jnp.min(cand, axis=0, keepdims=True)
            hit = blk_f == idx
            pen = jnp.where(hit, 0.0, pen)
            gate = jnp.where(hit, -jnp.inf, gate)
        pen_ref[0, i] = pen
        return carry

    lax.fori_loop(0, nb, qloop, 0, unroll=2)


def _qk_prep(proj, q_norm_w, k_norm_w):
    S = proj.shape[0]
    H, D, BS = ATTN_HEADS, ATTN_HEAD_DIM, MOBA_BLOCK
    nb = S // BS
    nbp = -(-nb // SUBLANES) * SUBLANES
    return pl.pallas_call(
        _qk_prep_kernel,
        grid=(H,),
        in_specs=[
            pl.BlockSpec((S, D), lambda h: (0, COL_Q // D + h)),
            pl.BlockSpec((S, D), lambda h: (0, COL_K // D + h)),
            pl.BlockSpec((S, D), lambda h: (0, COL_V // D + h)),
            pl.BlockSpec((1, D), lambda h: (0, 0)),
            pl.BlockSpec((1, D), lambda h: (0, 0)),
        ],
        out_specs=[
            pl.BlockSpec((1, nb, D, BS), lambda h: (h, 0, 0, 0)),
            pl.BlockSpec((1, S, D), lambda h: (h, 0, 0)),
            pl.BlockSpec((1, nb, MOBA_VT_ROWS, BS), lambda h: (h, 0, 0, 0)),
            pl.BlockSpec((1, nb, nbp, BS), lambda h: (h, 0, 0, 0)),
        ],
        out_shape=[
            jax.ShapeDtypeStruct((H, nb, D, BS), BF16),
            jax.ShapeDtypeStruct((H, S, D), BF16),
            jax.ShapeDtypeStruct((H, nb, MOBA_VT_ROWS, BS), BF16),
            jax.ShapeDtypeStruct((H, nb, nbp, BS), F32),
        ],
        scratch_shapes=[pltpu.VMEM((nbp, D), F32)],
        compiler_params=_cparams(("arbitrary",)),
        name="qk_prep",
    )(proj, proj, proj, q_norm_w, k_norm_w)


def _rel_bucket(dist):
    n = jnp.maximum(dist, 0)
    max_exact = REL_BUCKETS // 2
    nf = jnp.maximum(n, 1).astype(F32)
    large = max_exact + (jnp.log(nf / max_exact) / math.log(REL_MAX_DIST / max_exact)
                         * (REL_BUCKETS - max_exact)).astype(jnp.int32)
    large = jnp.minimum(large, REL_BUCKETS - 1)
    return jnp.where(n < max_exact, n, large)


MOBA_HEADS_PER_STEP = 2
MOBA_VT_ROWS = ATTN_HEAD_DIM + 2 * SUBLANES


def _moba_kernel(tbl_ref, qt_ref, k_ref, vt_ref, pen_ref, o_ref,
                 bown_scr, bprev_scr, m_scr, acc_scr, s0_scr, s1_scr,
                 p0_scr, p1_scr, al0_scr, al1_scr):
    hp = pl.program_id(0)
    i = pl.program_id(1)
    BS, HB = MOBA_BLOCK, MOBA_HEADS_PER_STEP
    D = ATTN_HEAD_DIM

    @pl.when(i == 0)
    def _():
        kl = lax.broadcasted_iota(jnp.int32, (BS, BS), 0)
        ql = lax.broadcasted_iota(jnp.int32, (BS, BS), 1)
        d = ql - kl
        b_own = _rel_bucket(d)
        b_prev = _rel_bucket(d + BS)
        for a in range(HB):
            h = hp * HB + a
            far = tbl_ref[h, REL_BUCKETS - 1]

            def lookup(bucket):
                val = jnp.zeros((BS, BS), F32)
                for b in range(REL_BUCKETS):
                    val = jnp.where(bucket == b, tbl_ref[h, b], val)
                return (val - far) * LOG2E

            bown_scr[a] = jnp.where(d >= 0, lookup(b_own), -jnp.inf)
            bprev_scr[a] = lookup(b_prev)

    def krows(a, j):
        return k_ref[a, pl.ds(pl.multiple_of(j * BS, BS), BS), :]

    jp = jnp.maximum(i - 1, 0)
    no_prev = jnp.where(i == 0, -jnp.inf, 0.0)
    for a in range(HB):
        qt = qt_ref[a, 0]
        s_own = _dot(krows(a, i), qt) + bown_scr[a]
        s_prev = _dot(krows(a, jp), qt) + bprev_scr[a] + (pen_ref[a, 0, pl.ds(jp, 1), :] + no_prev)
        m0 = jnp.maximum(jnp.max(s_own, axis=0, keepdims=True), jnp.max(s_prev, axis=0, keepdims=True))
        p_own = jnp.exp2(s_own - m0)
        p_prev = jnp.exp2(s_prev - m0)
        m_scr[a] = m0
        acc_scr[a] = (_dot(vt_ref[a, i], p_own.astype(BF16))
                      + _dot(vt_ref[a, jp], p_prev.astype(BF16)))

    n_far = jnp.maximum(i - 1, 0)
    nb = k_ref.shape[1] // BS

    def clampj(j):
        return jnp.clip(j, 0, nb - 1)

    def scores(j, s_slot):
        for a in range(HB):
            s_slot[a] = _dot(krows(a, clampj(j)), qt_ref[a, 0])

    def softmax(j, s_slot, p_slot, al_slot, mask_tail):
        for a in range(HB):
            pr = pen_ref[a, 0, pl.ds(clampj(j), 1), :]
            if mask_tail:
                pr = pr + jnp.where(j >= n_far, -jnp.inf, 0.0)
            s = s_slot[a]
            m_old = m_scr[a]
            m_new = jnp.maximum(m_old, jnp.max(s, axis=0, keepdims=True) + pr)
            p_slot[a] = jnp.exp2(s - (m_new - pr)).astype(BF16)
            al_slot[a] = jnp.exp2(m_old - m_new)
            m_scr[a] = m_new

    def accumulate(j, p_slot, al_slot):
        for a in range(HB):
            acc_scr[a] = al_slot[a] * acc_scr[a] + _dot(vt_ref[a, clampj(j)], p_slot[a])

    scores(0, s0_scr)
    p1_scr[...] = jnp.zeros(p1_scr.shape, BF16)
    al1_scr[...] = jnp.ones(al1_scr.shape, F32)

    def far_loop(g, carry):
        j = 2 * g
        scores(j + 1, s1_scr)
        accumulate(j - 1, p1_scr, al1_scr)
        softmax(j, s0_scr, p0_scr, al0_scr, False)
        scores(j + 2, s0_scr)
        accumulate(j, p0_scr, al0_scr)
        softmax(j + 1, s1_scr, p1_scr, al1_scr, True)
        return carry

    n_pairs = (n_far + 1) // 2
    lax.fori_loop(0, n_pairs, far_loop, 0)
    accumulate(2 * n_pairs - 1, p1_scr, al1_scr)
    for a in range(HB):
        acc = acc_scr[a]
        o_ref[:, a * D:(a + 1) * D] = (acc[0:D, :] / acc[D:D + 1, :]).T.astype(o_ref.dtype)


def _moba(qt, kn, vt, pen, table_h):
    H, nb, D, BS = qt.shape
    S = nb * BS
    nbp = pen.shape[2]
    HB = MOBA_HEADS_PER_STEP
    return pl.pallas_call(
        _moba_kernel,
        grid=(H // HB, nb),
        in_specs=[
            pl.BlockSpec(memory_space=pltpu.SMEM),
            pl.BlockSpec((HB, 1, D, BS), lambda hp, i: (hp, i, 0, 0)),
            pl.BlockSpec((HB, S, D), lambda hp, i: (hp, 0, 0)),
            pl.BlockSpec((HB, nb, MOBA_VT_ROWS, BS), lambda hp, i: (hp, 0, 0, 0)),
            pl.BlockSpec((HB, 1, nbp, BS), lambda hp, i: (hp, i, 0, 0)),
        ],
        out_specs=pl.BlockSpec((BS, HB * D), lambda hp, i: (i, hp)),
        out_shape=jax.ShapeDtypeStruct((S, H * D), BF16),
        scratch_shapes=[
            pltpu.VMEM((HB, BS, BS), F32),
            pltpu.VMEM((HB, BS, BS), F32),
            pltpu.VMEM((HB, 1, BS), F32),
            pltpu.VMEM((HB, MOBA_VT_ROWS, BS), F32),
            pltpu.VMEM((HB, BS, BS), F32),
            pltpu.VMEM((HB, BS, BS), F32),
            pltpu.VMEM((HB, BS, BS), BF16),
            pltpu.VMEM((HB, BS, BS), BF16),
            pltpu.VMEM((HB, 1, BS), F32),
            pltpu.VMEM((HB, 1, BS), F32),
        ],
        compiler_params=_cparams(("arbitrary", "arbitrary")),
        name="moba",
    )(table_h, qt, kn, vt, pen)


def _mix_kernel(ys_ref, ya_ref, ws_ref, wa_ref, gs_ref, ga_ref, bs_ref, ba_ref, o_ref):
    gs = _sigmoid(gs_ref[...].astype(F32) + bs_ref[...])
    ga = _sigmoid(ga_ref[...].astype(F32) + ba_ref[...])
    o_ref[...] = (gs * _dot(ys_ref[...], ws_ref[...])
                  + ga * _dot(ya_ref[...], wa_ref[...])).astype(o_ref.dtype)


def _mix(y_ssd, y_attn, w_ssm_out, w_attn_out, proj, b_gate, tm=1024, tn=512):
    S = y_ssd.shape[0]
    tm = min(tm, S)
    gs_col = COL_G // tn
    ga_col = (COL_G + D_MODEL) // tn
    nb = D_MODEL // tn
    return pl.pallas_call(
        _mix_kernel,
        grid=(S // tm, D_MODEL // tn),
        in_specs=[
            pl.BlockSpec((tm, D_INNER), lambda m, n: (m, 0)),
            pl.BlockSpec((tm, D_ATTN), lambda m, n: (m, 0)),
            pl.BlockSpec((D_INNER, tn), lambda m, n: (0, n)),
            pl.BlockSpec((D_ATTN, tn), lambda m, n: (0, n)),
            pl.BlockSpec((tm, tn), lambda m, n: (m, gs_col + n)),
            pl.BlockSpec((tm, tn), lambda m, n: (m, ga_col + n)),
            pl.BlockSpec((1, tn), lambda m, n: (0, n)),
            pl.BlockSpec((1, tn), lambda m, n: (0, nb + n)),
        ],
        out_specs=pl.BlockSpec((tm, tn), lambda m, n: (m, n)),
        out_shape=jax.ShapeDtypeStruct((S, D_MODEL), BF16),
        compiler_params=_cparams(("arbitrary", "arbitrary")),
        name="mix",
    )(y_ssd, y_attn, w_ssm_out, w_attn_out, proj, proj, b_gate, b_gate)


def _out_norm_kernel(x_ref, mix_ref, w_ref, nw_ref, x1_ref, h2_ref):
    x1 = x_ref[...] + _dot(mix_ref[...], w_ref[...])
    x1_ref[...] = x1
    ms = jnp.mean(x1 * x1, axis=-1, keepdims=True)
    h2_ref[...] = (x1 * lax.rsqrt(ms + EPS) * nw_ref[...]).astype(h2_ref.dtype)


def _out_norm(x, mix, w_out, ffn_norm_w, tm=512):
    S = x.shape[0]
    tm = min(tm, S)
    return pl.pallas_call(
        _out_norm_kernel,
        grid=(S // tm,),
        in_specs=[
            pl.BlockSpec((tm, D_MODEL), lambda m: (m, 0)),
            pl.BlockSpec((tm, D_MODEL), lambda m: (m, 0)),
            pl.BlockSpec((D_MODEL, D_MODEL), lambda m: (0, 0)),
            pl.BlockSpec((1, D_MODEL), lambda m: (0, 0)),
        ],
        out_specs=[
            pl.BlockSpec((tm, D_MODEL), lambda m: (m, 0)),
            pl.BlockSpec((tm, D_MODEL), lambda m: (m, 0)),
        ],
        out_shape=[
            jax.ShapeDtypeStruct((S, D_MODEL), F32),
            jax.ShapeDtypeStruct((S, D_MODEL), BF16),
        ],
        compiler_params=_cparams(("arbitrary",)),
        name="out_norm",
    )(x, mix, w_out, ffn_norm_w)


def _ffn_up_kernel(h_ref, wg32_ref, wu32_ref, cwg_ref, cwu_ref, cbg_ref, cbu_ref, o_ref,
                   tailg, tailu, padg, padu, wg_ref, wu_ref):
    tm = h_ref.shape[0]

    @pl.when(pl.program_id(1) == 0)
    def _():
        tailg[...] = jnp.zeros(tailg.shape, F32)
        tailu[...] = jnp.zeros(tailu.shape, F32)
        wg_ref[...] = wg32_ref[...].astype(BF16)
        wu_ref[...] = wu32_ref[...].astype(BF16)

    def conv(w_ref, cw_ref, cb_ref, tail, pad):
        pad[0:SUBLANES, :] = tail[...]
        pad[SUBLANES:SUBLANES + tm, :] = _dot(h_ref[...], w_ref[...])
        tail[...] = pad[tm:tm + SUBLANES, :]
        acc = cb_ref[...]
        for k in range(FFN_CONV):
            off = SUBLANES - (FFN_CONV - 1) + k
            acc = acc + cw_ref[k:k + 1, :] * pad[off:off + tm, :]
        return acc

    ug = conv(wg_ref, cwg_ref, cbg_ref, tailg, padg)
    uu = conv(wu_ref, cwu_ref, cbu_ref, tailu, padu)
    o_ref[...] = (ug * _sigmoid(ug) * uu).astype(o_ref.dtype)


def _ffn_up(h2, w_up, conv_w, conv_b, tm=1024, tn=512):
    S = h2.shape[0]
    tm = min(tm, S)
    nb = D_FF // tn
    return pl.pallas_call(
        _ffn_up_kernel,
        grid=(nb, S // tm),
        in_specs=[
            pl.BlockSpec((tm, D_MODEL), lambda n, m: (m, 0)),
            pl.BlockSpec((D_MODEL, tn), lambda n, m: (0, n)),
            pl.BlockSpec((D_MODEL, tn), lambda n, m: (0, nb + n)),
            pl.BlockSpec((FFN_CONV, tn), lambda n, m: (0, n)),
            pl.BlockSpec((FFN_CONV, tn), lambda n, m: (0, nb + n)),
            pl.BlockSpec((1, tn), lambda n, m: (0, n)),
            pl.BlockSpec((1, tn), lambda n, m: (0, nb + n)),
        ],
        out_specs=pl.BlockSpec((tm, tn), lambda n, m: (m, n)),
        out_shape=jax.ShapeDtypeStruct((S, D_FF), BF16),
        scratch_shapes=[
            pltpu.VMEM((SUBLANES, tn), F32),
            pltpu.VMEM((SUBLANES, tn), F32),
            pltpu.VMEM((tm + SUBLANES, tn), F32),
            pltpu.VMEM((tm + SUBLANES, tn), F32),
            pltpu.VMEM((D_MODEL, tn), BF16),
            pltpu.VMEM((D_MODEL, tn), BF16),
        ],
        compiler_params=_cparams(("arbitrary", "arbitrary")),
        name="ffn_up",
    )(h2, w_up, w_up, conv_w, conv_w, conv_b, conv_b)


def _ffn_down_kernel(a_ref, w_ref, x1_ref, o_ref, acc_scr):
    k = pl.program_id(1)

    @pl.when(k == 0)
    def _():
        acc_scr[...] = x1_ref[...]

    acc_scr[...] += _dot(a_ref[...], w_ref[...])

    @pl.when(k == pl.num_programs(1) - 1)
    def _():
        o_ref[...] = acc_scr[...]


def _ffn_down(act, w_down, x1, tm=1024, tk=512):
    S = act.shape[0]
    tm = min(tm, S)
    return pl.pallas_call(
        _ffn_down_kernel,
        grid=(S // tm, D_FF // tk),
        in_specs=[
            pl.BlockSpec((tm, tk), lambda m, k: (m, k)),
            pl.BlockSpec((tk, D_MODEL), lambda m, k: (k, 0)),
            pl.BlockSpec((tm, D_MODEL), lambda m, k: (m, 0)),
        ],
        out_specs=pl.BlockSpec((tm, D_MODEL), lambda m, k: (m, 0)),
        out_shape=jax.ShapeDtypeStruct((S, D_MODEL), F32),
        scratch_shapes=[pltpu.VMEM((tm, D_MODEL), F32)],
        compiler_params=_cparams(("arbitrary", "arbitrary")),
        name="ffn_down",
    )(act, w_down, x1)


def _pad_lanes(v):
    return jnp.pad(v, ((0, 0), (0, LANES - v.shape[1])))


def _layer(x, attn_norm_w, w_in, b_gate, ssm_conv_w, ssm_conv_b, ssm_dt_bias, ssm_a_log, ssm_d,
           ssm_norm_w, q_norm_w, k_norm_w, rel_bias, w_ssm_out, w_attn_out, w_out, ffn_norm_w,
           w_up, ffn_conv_w, ffn_conv_b, w_down):
    o_z, o_xbc = 0, D_INNER
    o_dt = o_xbc + D_XBC
    o_q = o_dt + SSM_HEADS
    o_g = o_q + 3 * D_ATTN
    wt = w_in.T
    wt_main = jnp.concatenate(
        [wt[o_xbc:o_dt], wt[o_q:o_g], wt[o_z:o_xbc], wt[o_g:]], axis=0).astype(BF16)
    wt_dt = jnp.pad(wt[o_dt:o_q], ((0, LANES - SSM_HEADS), (0, 0)))

    proj, dt_raw = _in_proj(x, attn_norm_w[None, :], wt_main, wt_dt)
    y_ssd = _ssd(proj, dt_raw, ssm_conv_w, ssm_conv_b[None, :], _pad_lanes(ssm_dt_bias[None, :]),
                 _pad_lanes(ssm_a_log[None, :]), jnp.repeat(ssm_d, SSM_HEAD_DIM)[None, :],
                 ssm_norm_w[None, :])
    qt, kn, vt, pen = _qk_prep(proj, q_norm_w[None, :], k_norm_w[None, :])
    y_attn = _moba(qt, kn, vt, pen, rel_bias.T)
    mix = _mix(y_ssd, y_attn, w_ssm_out.astype(BF16), w_attn_out.astype(BF16), proj, b_gate[None, :])
    x1, h2 = _out_norm(x, mix, w_out.astype(BF16), ffn_norm_w[None, :])
    act = _ffn_up(h2, w_up, ffn_conv_w, ffn_conv_b[None, :])
    return _ffn_down(act, w_down.astype(BF16), x1)


def kernel(x, attn_norm_w, w_in, b_gate, ssm_conv_w, ssm_conv_b, ssm_dt_bias, ssm_a_log, ssm_d,
           ssm_norm_w, q_norm_w, k_norm_w, rel_bias, w_ssm_out, w_attn_out, w_out, ffn_norm_w,
           w_up, ffn_conv_w, ffn_conv_b, w_down):
    assert x.shape[0] == 1 and attn_norm_w.shape[0] == 1
    out = _layer(x[0], attn_norm_w[0], w_in[0], b_gate[0], ssm_conv_w[0], ssm_conv_b[0],
                 ssm_dt_bias[0], ssm_a_log[0], ssm_d[0], ssm_norm_w[0], q_norm_w[0], k_norm_w[0],
                 rel_bias, w_ssm_out[0], w_attn_out[0], w_out[0], ffn_norm_w[0], w_up[0],
                 ffn_conv_w[0], ffn_conv_b[0], w_down[0])
    return out[None]
```

```python
import functools
import math

import jax
import jax.numpy as jnp
from jax import lax
from jax.experimental import pallas as pl
from jax.experimental.pallas import tpu as pltpu

F32 = jnp.float32
BF16 = jnp.bfloat16

D_MODEL = 2048
D_INNER = 4096
SSM_HEADS = 64
SSM_HEAD_DIM = 64
SSM_GROUPS = 8
SSM_HEADS_PER_GROUP = SSM_HEADS // SSM_GROUPS
SSM_STATE = 128
SSM_CONV = 4
SSM_CHUNK = 256
GROUP_CH = D_INNER // SSM_GROUPS
SSD_GROUPS_PER_STEP = 2
D_XBC = D_INNER + 2 * SSM_GROUPS * SSM_STATE
ATTN_HEADS = 16
ATTN_HEAD_DIM = 128
D_ATTN = ATTN_HEADS * ATTN_HEAD_DIM
MOBA_BLOCK = 256
MOBA_TOPK = 3
REL_BUCKETS = 32
REL_MAX_DIST = 128
D_FF = 5632
FFN_CONV = 3
EPS = 1e-6
LOG2E = 1.4426950408889634

LANES = 128
SUBLANES = 8
D_PROJ = D_XBC + 3 * D_ATTN + 2 * D_INNER
COL_Z = 0
COL_XBC = COL_Z + D_INNER
COL_Q = COL_XBC + D_XBC
COL_K = COL_Q + D_ATTN
COL_V = COL_K + D_ATTN
COL_G = COL_V + D_ATTN
VMEM_LIMIT = 56 * 1024 * 1024


def _cparams(sem):
    return pltpu.CompilerParams(dimension_semantics=sem, vmem_limit_bytes=VMEM_LIMIT)


def _split3(a):
    hi = a.astype(BF16)
    r = a - hi.astype(F32)
    mid = r.astype(BF16)
    lo = (r - mid.astype(F32)).astype(BF16)
    return hi, mid, lo


def _dot(a, b):
    return jnp.dot(a, b, preferred_element_type=F32)


def _dot_nt(a, b):
    return lax.dot_general(a, b, (((1,), (1,)), ((), ())), preferred_element_type=F32)


def _sigmoid(x):
    return 0.5 + 0.5 * jnp.tanh(0.5 * x)


def _silu(x):
    h = 0.5 * x
    return h + h * jnp.tanh(h)


def _in_proj_kernel(x_ref, nw_ref, wa_ref, wb_ref, wdt_ref, o_ref, dt_ref, h_scr, *, na):
    n = pl.program_id(1)

    @pl.when(n == 0)
    def _():
        x = x_ref[...]
        ms = jnp.mean(x * x, axis=-1, keepdims=True)
        h = x * lax.rsqrt(ms + EPS) * nw_ref[...]
        h_scr[...] = h.astype(BF16)
        h_hi, h_mid, _ = _split3(h)
        w_hi, w_mid, _ = _split3(wdt_ref[...])
        dt_ref[...] = _dot_nt(h_hi, w_hi) + _dot_nt(h_hi, w_mid) + _dot_nt(h_mid, w_hi)

    @pl.when(n < na)
    def _():
        o_ref[...] = _dot_nt(h_scr[...], wa_ref[...]).astype(o_ref.dtype)

    @pl.when(n >= na)
    def _():
        o_ref[...] = _dot_nt(h_scr[...], wb_ref[...]).astype(o_ref.dtype)


def _in_proj(x, norm_w, wt_a, wt_b, wt_dt, tm=1024, tn=1024):
    S = x.shape[0]
    tm = min(tm, S)
    na = wt_a.shape[0] // tn
    nb = wt_b.shape[0] // tn
    assert (na + nb) * tn == D_PROJ
    return pl.pallas_call(
        functools.partial(_in_proj_kernel, na=na),
        grid=(S // tm, na + nb),
        in_specs=[
            pl.BlockSpec((tm, D_MODEL), lambda m, n: (m, 0)),
            pl.BlockSpec((1, D_MODEL), lambda m, n: (0, 0)),
            pl.BlockSpec((tn, D_MODEL), lambda m, n: (jnp.minimum(n, na - 1), 0)),
            pl.BlockSpec((tn, D_MODEL), lambda m, n: (jnp.maximum(n - na, 0), 0)),
            pl.BlockSpec((LANES, D_MODEL), lambda m, n: (0, 0)),
        ],
        out_specs=[
            pl.BlockSpec((tm, tn), lambda m, n: (m, n)),
            pl.BlockSpec((tm, LANES), lambda m, n: (m, 0)),
        ],
        out_shape=[
            jax.ShapeDtypeStruct((S, D_PROJ), BF16),
            jax.ShapeDtypeStruct((S, LANES), F32),
        ],
        scratch_shapes=[pltpu.VMEM((tm, D_MODEL), BF16)],
        compiler_params=_cparams(("arbitrary", "arbitrary")),
        name="in_proj",
    )(x, norm_w, wt_a, wt_b, wt_dt)


def _conv_silu(raw_ref, tail_ref, pad_ref, w_ref, b_ref, g, taps):
    L = raw_ref.shape[0]
    pad_ref[0:SUBLANES, :] = tail_ref[g]
    pad_ref[SUBLANES:SUBLANES + L, :] = raw_ref[...].astype(F32)
    tail_ref[g] = pad_ref[L:L + SUBLANES, :]
    acc = b_ref[...]
    for k in range(taps):
        off = SUBLANES - (taps - 1) + k
        acc = acc + w_ref[k:k + 1, :] * pad_ref[off:off + L, :]
    return _silu(acc)


def _ssd_kernel(x_ref, b_ref, c_ref, z_ref, dtr_ref, cwx_ref, cwb_ref, cwc_ref,
                cbx_ref, cbb_ref, cbc_ref, dtb_ref, alog_ref, dsk_ref, nw_ref,
                o_ref,
                tailx, tailb, tailc, padx, padb, padc, acs_scr, acst_scr, lbt_scr, ht_scr):
    c = pl.program_id(0)
    gp = pl.program_id(1)
    L = SSM_CHUNK
    QW = 4 * SSM_HEAD_DIM
    GPS = SSD_GROUPS_PER_STEP

    @pl.when(c == 0)
    def _():
        for gg in range(GPS):
            g0 = gp * GPS + gg
            tailx[g0] = jnp.zeros(tailx.shape[1:], F32)
            tailb[g0] = jnp.zeros(tailb.shape[1:], F32)
            tailc[g0] = jnp.zeros(tailc.shape[1:], F32)
            ht_scr[g0] = jnp.zeros(ht_scr.shape[1:], F32)

    @pl.when(gp == 0)
    def _():
        t = dtr_ref[...] + dtb_ref[...]
        dt = jnp.maximum(t, 0.0) + jnp.log(1.0 + jnp.exp(-jnp.abs(t)))
        a = -jnp.exp(alog_ref[...])
        da = dt * a
        ri = lax.broadcasted_iota(jnp.int32, (L, L), 0)
        ci = lax.broadcasted_iota(jnp.int32, (L, L), 1)
        tri = jnp.where(ri >= ci, 1.0, 0.0).astype(BF16)
        hi, mid, lo = _split3(da)
        a_cs = _dot(tri, hi) + _dot(tri, mid) + _dot(tri, lo)
        a2 = a_cs * LOG2E
        a2t = a2.T
        acs_scr[...] = a2
        acst_scr[...] = a2t
        lbt_scr[...] = a2t - jnp.log(dt.T) * LOG2E

    for gg in range(GPS):
        xs = slice(gg * GROUP_CH, (gg + 1) * GROUP_CH)
        ns = slice(gg * SSM_STATE, (gg + 1) * SSM_STATE)
        _ssd_group(gp * GPS + gg,
                   x_ref.at[:, xs], b_ref.at[:, ns], c_ref.at[:, ns], z_ref.at[:, xs],
                   cwx_ref.at[:, xs], cwb_ref.at[:, ns], cwc_ref.at[:, ns],
                   cbx_ref.at[:, xs], cbb_ref.at[:, ns], cbc_ref.at[:, ns],
                   dsk_ref.at[:, xs], nw_ref.at[:, xs], o_ref.at[:, xs],
                   tailx, tailb, tailc, padx.at[gg], padb.at[gg], padc.at[gg],
                   acs_scr, acst_scr, lbt_scr, ht_scr)


def _ssd_group(g, x_ref, b_ref, c_ref, z_ref, cwx_ref, cwb_ref, cwc_ref, cbx_ref, cbb_ref, cbc_ref,
               dsk_ref, nw_ref, o_ref, tailx, tailb, tailc, padx, padb, padc,
               acs_scr, acst_scr, lbt_scr, ht_scr):
    L = SSM_CHUNK
    QW = 4 * SSM_HEAD_DIM
    xa = _conv_silu(x_ref, tailx, padx, cwx_ref, cbx_ref, g, SSM_CONV)
    ba = _conv_silu(b_ref, tailb, padb, cwb_ref, cbb_ref, g, SSM_CONV)
    ca = _conv_silu(c_ref, tailc, padc, cwc_ref, cbc_ref, g, SSM_CONV)

    xb = xa.astype(BF16)
    bt = ba.T
    cb = _dot(ca.astype(BF16), bt.astype(BF16))
    a_cs = acs_scr[...]
    ri = lax.broadcasted_iota(jnp.int32, (L, L), 0)
    ci = lax.broadcasted_iota(jnp.int32, (L, L), 1)
    causal = ri >= ci
    lane_h = lax.broadcasted_iota(jnp.int32, (L, LANES), 1)
    lane_q = lax.broadcasted_iota(jnp.int32, (1, QW), 1)

    ys = []
    for q in range(GROUP_CH // QW):
        xq = xb[:, q * QW:(q + 1) * QW]
        hq = ht_scr[g, :, q * QW:(q + 1) * QW]
        hqb = hq.astype(BF16)
        acc = jnp.zeros((L, QW), F32)
        st = jnp.zeros((SSM_STATE, QW), F32)
        cdq = jnp.zeros((1, QW), F32)
        for j in range(QW // SSM_HEAD_DIM):
            h = g * SSM_HEADS_PER_GROUP + q * (QW // SSM_HEAD_DIM) + j
            a_row = acst_scr[pl.ds(h, 1), :]
            b_row = lbt_scr[pl.ds(h, 1), :]
            a_col = jnp.sum(jnp.where(lane_h == h, a_cs, 0.0), axis=1, keepdims=True)
            dmat = jnp.where(causal, a_col - b_row, -jnp.inf)
            mh = (cb * jnp.exp2(dmat)).astype(BF16)
            lm = (lane_q >= j * SSM_HEAD_DIM) & (lane_q < (j + 1) * SSM_HEAD_DIM)
            xm = jnp.where(lm, xq, jnp.zeros_like(xq))
            hm = jnp.where(lm, hqb, jnp.zeros_like(hqb))
            ce = (ca * jnp.exp2(a_col)).astype(BF16)
            acc = acc + _dot(mh, xm) + _dot(ce, hm)
            a_last = a_row[:, L - 1:L]
            w_row = jnp.exp2(a_last - b_row)
            st = st + _dot((bt * w_row).astype(BF16), xm)
            cdq = jnp.where(lm, jnp.exp2(a_last), cdq)
        ht_scr[g, :, q * QW:(q + 1) * QW] = hq * cdq + st
        ys.append(acc + xa[:, q * QW:(q + 1) * QW] * dsk_ref[:, q * QW:(q + 1) * QW])

    z = z_ref[...].astype(F32)
    sz = _silu(z)
    ygs = [ys[q] * sz[:, q * QW:(q + 1) * QW] for q in range(len(ys))]
    ss = sum(jnp.sum(v * v, axis=1, keepdims=True) for v in ygs)
    scale = lax.rsqrt(ss * (1.0 / GROUP_CH) + EPS)
    for q in range(len(ys)):
        o_ref[:, q * QW:(q + 1) * QW] = (
            ygs[q] * scale * nw_ref[:, q * QW:(q + 1) * QW]).astype(o_ref.dtype)


def _ssd(proj, dt_raw, conv_w, conv_b, dt_bias, a_log, d_exp, norm_w):
    S = proj.shape[0]
    L, G, N = SSM_CHUNK, SSM_GROUPS, SSM_STATE
    GPS = SSD_GROUPS_PER_STEP
    XW, NW = GPS * GROUP_CH, GPS * N
    bcol = D_INNER // NW
    ccol = bcol + G // GPS
    pxcol = COL_XBC // XW
    pbcol = COL_XBC // NW + bcol
    pccol = pbcol + G // GPS
    zcol = COL_Z // XW
    return pl.pallas_call(
        _ssd_kernel,
        grid=(S // L, G // GPS),
        in_specs=[
            pl.BlockSpec((L, XW), lambda c, g: (c, pxcol + g)),
            pl.BlockSpec((L, NW), lambda c, g: (c, pbcol + g)),
            pl.BlockSpec((L, NW), lambda c, g: (c, pccol + g)),
            pl.BlockSpec((L, XW), lambda c, g: (c, zcol + g)),
            pl.BlockSpec((L, LANES), lambda c, g: (c, 0)),
            pl.BlockSpec((SSM_CONV, XW), lambda c, g: (0, g)),
            pl.BlockSpec((SSM_CONV, NW), lambda c, g: (0, bcol + g)),
            pl.BlockSpec((SSM_CONV, NW), lambda c, g: (0, ccol + g)),
            pl.BlockSpec((1, XW), lambda c, g: (0, g)),
            pl.BlockSpec((1, NW), lambda c, g: (0, bcol + g)),
            pl.BlockSpec((1, NW), lambda c, g: (0, ccol + g)),
            pl.BlockSpec((1, LANES), lambda c, g: (0, 0)),
            pl.BlockSpec((1, LANES), lambda c, g: (0, 0)),
            pl.BlockSpec((1, XW), lambda c, g: (0, g)),
            pl.BlockSpec((1, XW), lambda c, g: (0, g)),
        ],
        out_specs=pl.BlockSpec((L, XW), lambda c, g: (c, g)),
        out_shape=jax.ShapeDtypeStruct((S, D_INNER), BF16),
        scratch_shapes=[
            pltpu.VMEM((G, SUBLANES, GROUP_CH), F32),
            pltpu.VMEM((G, SUBLANES, N), F32),
            pltpu.VMEM((G, SUBLANES, N), F32),
            pltpu.VMEM((GPS, L + SUBLANES, GROUP_CH), F32),
            pltpu.VMEM((GPS, L + SUBLANES, N), F32),
            pltpu.VMEM((GPS, L + SUBLANES, N), F32),
            pltpu.VMEM((L, LANES), F32),
            pltpu.VMEM((LANES, L), F32),
            pltpu.VMEM((LANES, L), F32),
            pltpu.VMEM((G, N, GROUP_CH), F32),
        ],
        compiler_params=_cparams(("arbitrary", "arbitrary")),
        name="ssd",
    )(proj, proj, proj, proj, dt_raw, conv_w, conv_w, conv_w, conv_b, conv_b, conv_b,
      dt_bias, a_log, d_exp, norm_w)


def _qk_prep_kernel(q_ref, k_ref, v_ref, qw_ref, kw_ref, qt_ref, kn_ref, vt_ref, pen_ref, kbar_scr):
    S = q_ref.shape[0]
    BS = MOBA_BLOCK
    nb = S // BS
    nbp = kbar_scr.shape[0]
    kbar_scr[...] = jnp.zeros(kbar_scr.shape, F32)

    def kloop(b, carry):
        rows = pl.ds(pl.multiple_of(b * BS, BS), BS)
        k = k_ref[rows, :].astype(F32)
        kn = k * lax.rsqrt(jnp.mean(k * k, axis=-1, keepdims=True) + EPS) * kw_ref[...]
        kn_ref[0, rows, :] = kn.astype(BF16)
        kbar_scr[pl.ds(b, 1), :] = jnp.mean(kn, axis=0, keepdims=True)
        vt_ref[0, b, 0:ATTN_HEAD_DIM, :] = v_ref[rows, :].astype(F32).T.astype(BF16)
        vt_ref[0, b, ATTN_HEAD_DIM:, :] = jnp.ones((MOBA_VT_ROWS - ATTN_HEAD_DIM, BS), BF16)
        return carry

    lax.fori_loop(0, nb, kloop, 0, unroll=4)
    kb_hi, kb_mid, _ = _split3(kbar_scr[...])

    def qloop(i, carry):
        rows = pl.ds(pl.multiple_of(i * BS, BS), BS)
        q = q_ref[rows, :].astype(F32)
        qn = q * lax.rsqrt(jnp.mean(q * q, axis=-1, keepdims=True) + EPS) * qw_ref[...]
        qnt = (qn * (ATTN_HEAD_DIM ** -0.5)).T
        qt_ref[0, i] = (qnt * LOG2E).astype(BF16)
        q_hi, q_mid, _ = _split3(qnt)
        gate = _dot(kb_hi, q_hi) + _dot(kb_mid, q_hi) + _dot(kb_hi, q_mid)
        blk = lax.broadcasted_iota(jnp.int32, gate.shape, 0)
        blk_f = blk.astype(F32)
        gate = jnp.where(blk < i, gate, -jnp.inf)
        pen = jnp.full(gate.shape, -jnp.inf, F32)
        for _ in range(MOBA_TOPK):
            mx = jnp.max(gate, axis=0, keepdims=True)
            cand = jnp.where((gate == mx) & (mx > -jnp.inf), blk_f, float(nbp))
            idx = jnp.min(cand, axis=0, keepdims=True)
            hit = blk_f == idx
            pen = jnp.where(hit, 0.0, pen)
            gate = jnp.where(hit, -jnp.inf, gate)
        pen_ref[0, i] = pen
        return carry

    lax.fori_loop(0, nb, qloop, 0, unroll=4)


def _qk_prep(proj, q_norm_w, k_norm_w):
    S = proj.shape[0]
    H, D, BS = ATTN_HEADS, ATTN_HEAD_DIM, MOBA_BLOCK
    nb = S // BS
    nbp = -(-nb // SUBLANES) * SUBLANES
    return pl.pallas_call(
        _qk_prep_kernel,
        grid=(H,),
        in_specs=[
            pl.BlockSpec((S, D), lambda h: (0, COL_Q // D + h)),
            pl.BlockSpec((S, D), lambda h: (0, COL_K // D + h)),
            pl.BlockSpec((S, D), lambda h: (0, COL_V // D + h)),
            pl.BlockSpec((1, D), lambda h: (0, 0)),
            pl.BlockSpec((1, D), lambda h: (0, 0)),
        ],
        out_specs=[
            pl.BlockSpec((1, nb, D, BS), lambda h: (h, 0, 0, 0)),
            pl.BlockSpec((1, S, D), lambda h: (h, 0, 0)),
            pl.BlockSpec((1, nb, MOBA_VT_ROWS, BS), lambda h: (h, 0, 0, 0)),
            pl.BlockSpec((1, nb, nbp, BS), lambda h: (h, 0, 0, 0)),
        ],
        out_shape=[
            jax.ShapeDtypeStruct((H, nb, D, BS), BF16),
            jax.ShapeDtypeStruct((H, S, D), BF16),
            jax.ShapeDtypeStruct((H, nb, MOBA_VT_ROWS, BS), BF16),
            jax.ShapeDtypeStruct((H, nb, nbp, BS), F32),
        ],
        scratch_shapes=[pltpu.VMEM((nbp, D), F32)],
        compiler_params=_cparams(("arbitrary",)),
        name="qk_prep",
    )(proj, proj, proj, q_norm_w, k_norm_w)


def _rel_bucket(dist):
    n = jnp.maximum(dist, 0)
    max_exact = REL_BUCKETS // 2
    nf = jnp.maximum(n, 1).astype(F32)
    large = max_exact + (jnp.log(nf / max_exact) / math.log(REL_MAX_DIST / max_exact)
                         * (REL_BUCKETS - max_exact)).astype(jnp.int32)
    large = jnp.minimum(large, REL_BUCKETS - 1)
    return jnp.where(n < max_exact, n, large)


MOBA_HEADS_PER_STEP = 2
MOBA_VT_ROWS = ATTN_HEAD_DIM + 2 * SUBLANES


def _moba_kernel(tbl_ref, qt_ref, k_ref, vt_ref, pen_ref, o_ref,
                 bown_scr, bprev_scr, m_scr, acc_scr, s0_scr, s1_scr,
                 p0_scr, p1_scr, al0_scr, al1_scr):
    hp = pl.program_id(0)
    i = pl.program_id(1)
    BS, HB = MOBA_BLOCK, MOBA_HEADS_PER_STEP
    D = ATTN_HEAD_DIM

    @pl.when(i == 0)
    def _():
        kl = lax.broadcasted_iota(jnp.int32, (BS, BS), 0)
        ql = lax.broadcasted_iota(jnp.int32, (BS, BS), 1)
        d = ql - kl
        b_own = _rel_bucket(d)
        b_prev = _rel_bucket(d + BS)
        for a in range(HB):
            h = hp * HB + a
            far = tbl_ref[h, REL_BUCKETS - 1]

            def lookup(bucket):
                val = jnp.zeros((BS, BS), F32)
                for b in range(REL_BUCKETS):
                    val = jnp.where(bucket == b, tbl_ref[h, b], val)
                return (val - far) * LOG2E

            bown_scr[a] = jnp.where(d >= 0, lookup(b_own), -jnp.inf)
            bprev_scr[a] = lookup(b_prev)

    def krows(a, j):
        return k_ref[a, pl.ds(pl.multiple_of(j * BS, BS), BS), :]

    jp = jnp.maximum(i - 1, 0)
    no_prev = jnp.where(i == 0, -jnp.inf, 0.0)
    for a in range(HB):
        qt = qt_ref[a, 0]
        s_own = _dot(krows(a, i), qt) + bown_scr[a]
        s_prev = _dot(krows(a, jp), qt) + bprev_scr[a] + (pen_ref[a, 0, pl.ds(jp, 1), :] + no_prev)
        m0 = jnp.maximum(jnp.max(s_own, axis=0, keepdims=True), jnp.max(s_prev, axis=0, keepdims=True))
        p_own = jnp.exp2(s_own - m0)
        p_prev = jnp.exp2(s_prev - m0)
        m_scr[a] = m0
        acc_scr[a] = (_dot(vt_ref[a, i], p_own.astype(BF16))
                      + _dot(vt_ref[a, jp], p_prev.astype(BF16)))

    n_far = jnp.maximum(i - 1, 0)
    nb = k_ref.shape[1] // BS

    def clampj(j):
        return jnp.clip(j, 0, nb - 1)

    def scores(j, s_slot):
        for a in range(HB):
            s_slot[a] = _dot(krows(a, clampj(j)), qt_ref[a, 0])

    def softmax(j, s_slot, p_slot, al_slot, mask_tail):
        for a in range(HB):
            pr = pen_ref[a, 0, pl.ds(clampj(j), 1), :]
            if mask_tail:
                pr = pr + jnp.where(j >= n_far, -jnp.inf, 0.0)
            s = s_slot[a]
            m_old = m_scr[a]
            m_new = jnp.maximum(m_old, jnp.max(s, axis=0, keepdims=True) + pr)
            p_slot[a] = jnp.exp2(s - (m_new - pr)).astype(BF16)
            al_slot[a] = jnp.exp2(m_old - m_new)
            m_scr[a] = m_new

    def accumulate(j, p_slot, al_slot):
        for a in range(HB):
            acc_scr[a] = al_slot[a] * acc_scr[a] + _dot(vt_ref[a, clampj(j)], p_slot[a])

    scores(0, s0_scr)
    p1_scr[...] = jnp.zeros(p1_scr.shape, BF16)
    al1_scr[...] = jnp.ones(al1_scr.shape, F32)

    def far_loop(g, carry):
        j = 2 * g
        scores(j + 1, s1_scr)
        accumulate(j - 1, p1_scr, al1_scr)
        softmax(j, s0_scr, p0_scr, al0_scr, False)
        scores(j + 2, s0_scr)
        accumulate(j, p0_scr, al0_scr)
        softmax(j + 1, s1_scr, p1_scr, al1_scr, True)
        return carry

    n_pairs = (n_far + 1) // 2
    lax.fori_loop(0, n_pairs, far_loop, 0)
    accumulate(2 * n_pairs - 1, p1_scr, al1_scr)
    for a in range(HB):
        acc = acc_scr[a]
        o_ref[:, a * D:(a + 1) * D] = (acc[0:D, :] / acc[D:D + 1, :]).T.astype(o_ref.dtype)


def _moba(qt, kn, vt, pen, table_h):
    H, nb, D, BS = qt.shape
    S = nb * BS
    nbp = pen.shape[2]
    HB = MOBA_HEADS_PER_STEP
    return pl.pallas_call(
        _moba_kernel,
        grid=(H // HB, nb),
        in_specs=[
            pl.BlockSpec(memory_space=pltpu.SMEM),
            pl.BlockSpec((HB, 1, D, BS), lambda hp, i: (hp, i, 0, 0)),
            pl.BlockSpec((HB, S, D), lambda hp, i: (hp, 0, 0)),
            pl.BlockSpec((HB, nb, MOBA_VT_ROWS, BS), lambda hp, i: (hp, 0, 0, 0)),
            pl.BlockSpec((HB, 1, nbp, BS), lambda hp, i: (hp, i, 0, 0)),
        ],
        out_specs=pl.BlockSpec((BS, HB * D), lambda hp, i: (i, hp)),
        out_shape=jax.ShapeDtypeStruct((S, H * D), BF16),
        scratch_shapes=[
            pltpu.VMEM((HB, BS, BS), F32),
            pltpu.VMEM((HB, BS, BS), F32),
            pltpu.VMEM((HB, 1, BS), F32),
            pltpu.VMEM((HB, MOBA_VT_ROWS, BS), F32),
            pltpu.VMEM((HB, BS, BS), F32),
            pltpu.VMEM((HB, BS, BS), F32),
            pltpu.VMEM((HB, BS, BS), BF16),
            pltpu.VMEM((HB, BS, BS), BF16),
            pltpu.VMEM((HB, 1, BS), F32),
            pltpu.VMEM((HB, 1, BS), F32),
        ],
        compiler_params=_cparams(("arbitrary", "arbitrary")),
        name="moba",
    )(table_h, qt, kn, vt, pen)


def _mix_kernel(ys_ref, ya_ref, ws_ref, wa_ref, gs_ref, ga_ref, bs_ref, ba_ref, o_ref):
    gs = _sigmoid(gs_ref[...].astype(F32) + bs_ref[...])
    ga = _sigmoid(ga_ref[...].astype(F32) + ba_ref[...])
    o_ref[...] = (gs * _dot(ys_ref[...], ws_ref[...])
                  + ga * _dot(ya_ref[...], wa_ref[...])).astype(o_ref.dtype)


def _mix(y_ssd, y_attn, w_ssm_out, w_attn_out, proj, b_gate, tm=1024, tn=512):
    S = y_ssd.shape[0]
    tm = min(tm, S)
    gs_col = COL_G // tn
    ga_col = (COL_G + D_MODEL) // tn
    nb = D_MODEL // tn
    return pl.pallas_call(
        _mix_kernel,
        grid=(S // tm, D_MODEL // tn),
        in_specs=[
            pl.BlockSpec((tm, D_INNER), lambda m, n: (m, 0)),
            pl.BlockSpec((tm, D_ATTN), lambda m, n: (m, 0)),
            pl.BlockSpec((D_INNER, tn), lambda m, n: (0, n)),
            pl.BlockSpec((D_ATTN, tn), lambda m, n: (0, n)),
            pl.BlockSpec((tm, tn), lambda m, n: (m, gs_col + n)),
            pl.BlockSpec((tm, tn), lambda m, n: (m, ga_col + n)),
            pl.BlockSpec((1, tn), lambda m, n: (0, n)),
            pl.BlockSpec((1, tn), lambda m, n: (0, nb + n)),
        ],
        out_specs=pl.BlockSpec((tm, tn), lambda m, n: (m, n)),
        out_shape=jax.ShapeDtypeStruct((S, D_MODEL), BF16),
        compiler_params=_cparams(("arbitrary", "arbitrary")),
        name="mix",
    )(y_ssd, y_attn, w_ssm_out, w_attn_out, proj, proj, b_gate, b_gate)


def _out_norm_kernel(x_ref, mix_ref, w_ref, nw_ref, x1_ref, h2_ref):
    x1 = x_ref[...] + _dot(mix_ref[...], w_ref[...])
    x1_ref[...] = x1
    ms = jnp.mean(x1 * x1, axis=-1, keepdims=True)
    h2_ref[...] = (x1 * lax.rsqrt(ms + EPS) * nw_ref[...]).astype(h2_ref.dtype)


def _out_norm(x, mix, w_out, ffn_norm_w, tm=512):
    S = x.shape[0]
    tm = min(tm, S)
    return pl.pallas_call(
        _out_norm_kernel,
        grid=(S // tm,),
        in_specs=[
            pl.BlockSpec((tm, D_MODEL), lambda m: (m, 0)),
            pl.BlockSpec((tm, D_MODEL), lambda m: (m, 0)),
            pl.BlockSpec((D_MODEL, D_MODEL), lambda m: (0, 0)),
            pl.BlockSpec((1, D_MODEL), lambda m: (0, 0)),
        ],
        out_specs=[
            pl.BlockSpec((tm, D_MODEL), lambda m: (m, 0)),
            pl.BlockSpec((tm, D_MODEL), lambda m: (m, 0)),
        ],
        out_shape=[
            jax.ShapeDtypeStruct((S, D_MODEL), F32),
            jax.ShapeDtypeStruct((S, D_MODEL), BF16),
        ],
        compiler_params=_cparams(("arbitrary",)),
        name="out_norm",
    )(x, mix, w_out, ffn_norm_w)


def _ffn_up_kernel(h_ref, wg32_ref, wu32_ref, cwg_ref, cwu_ref, cbg_ref, cbu_ref, o_ref,
                   tailg, tailu, padg, padu, wg_ref, wu_ref):
    tm = h_ref.shape[0]

    @pl.when(pl.program_id(1) == 0)
    def _():
        tailg[...] = jnp.zeros(tailg.shape, F32)
        tailu[...] = jnp.zeros(tailu.shape, F32)
        wg_ref[...] = wg32_ref[...].astype(BF16)
        wu_ref[...] = wu32_ref[...].astype(BF16)

    def conv(w_ref, cw_ref, cb_ref, tail, pad):
        pad[0:SUBLANES, :] = tail[...]
        pad[SUBLANES:SUBLANES + tm, :] = _dot(h_ref[...], w_ref[...])
        tail[...] = pad[tm:tm + SUBLANES, :]
        acc = cb_ref[...]
        for k in range(FFN_CONV):
            off = SUBLANES - (FFN_CONV - 1) + k
            acc = acc + cw_ref[k:k + 1, :] * pad[off:off + tm, :]
        return acc

    ug = conv(wg_ref, cwg_ref, cbg_ref, tailg, padg)
    uu = conv(wu_ref, cwu_ref, cbu_ref, tailu, padu)
    o_ref[...] = (_silu(ug) * uu).astype(o_ref.dtype)


def _ffn_up(h2, w_up, conv_w, conv_b, tm=1024, tn=512):
    S = h2.shape[0]
    tm = min(tm, S)
    nb = D_FF // tn
    return pl.pallas_call(
        _ffn_up_kernel,
        grid=(nb, S // tm),
        in_specs=[
            pl.BlockSpec((tm, D_MODEL), lambda n, m: (m, 0)),
            pl.BlockSpec((D_MODEL, tn), lambda n, m: (0, n)),
            pl.BlockSpec((D_MODEL, tn), lambda n, m: (0, nb + n)),
            pl.BlockSpec((FFN_CONV, tn), lambda n, m: (0, n)),
            pl.BlockSpec((FFN_CONV, tn), lambda n, m: (0, nb + n)),
            pl.BlockSpec((1, tn), lambda n, m: (0, n)),
            pl.BlockSpec((1, tn), lambda n, m: (0, nb + n)),
        ],
        out_specs=pl.BlockSpec((tm, tn), lambda n, m: (m, n)),
        out_shape=jax.ShapeDtypeStruct((S, D_FF), BF16),
        scratch_shapes=[
            pltpu.VMEM((SUBLANES, tn), F32),
            pltpu.VMEM((SUBLANES, tn), F32),
            pltpu.VMEM((tm + SUBLANES, tn), F32),
            pltpu.VMEM((tm + SUBLANES, tn), F32),
            pltpu.VMEM((D_MODEL, tn), BF16),
            pltpu.VMEM((D_MODEL, tn), BF16),
        ],
        compiler_params=_cparams(("arbitrary", "arbitrary")),
        name="ffn_up",
    )(h2, w_up, w_up, conv_w, conv_w, conv_b, conv_b)


def _ffn_down_kernel(a_ref, w_ref, x1_ref, o_ref, acc_scr):
    k = pl.program_id(1)

    @pl.when(k == 0)
    def _():
        acc_scr[...] = x1_ref[...]

    acc_scr[...] += _dot(a_ref[...], w_ref[...])

    @pl.when(k == pl.num_programs(1) - 1)
    def _():
        o_ref[...] = acc_scr[...]


def _ffn_down(act, w_down, x1, tm=1024, tk=512):
    S = act.shape[0]
    tm = min(tm, S)
    return pl.pallas_call(
        _ffn_down_kernel,
        grid=(S // tm, D_FF // tk),
        in_specs=[
            pl.BlockSpec((tm, tk), lambda m, k: (m, k)),
            pl.BlockSpec((tk, D_MODEL), lambda m, k: (k, 0)),
            pl.BlockSpec((tm, D_MODEL), lambda m, k: (m, 0)),
        ],
        out_specs=pl.BlockSpec((tm, D_MODEL), lambda m, k: (m, 0)),
        out_shape=jax.ShapeDtypeStruct((S, D_MODEL), F32),
        scratch_shapes=[pltpu.VMEM((tm, D_MODEL), F32)],
        compiler_params=_cparams(("arbitrary", "arbitrary")),
        name="ffn_down",
    )(act, w_down, x1)


def _pad_lanes(v):
    return jnp.pad(v, ((0, 0), (0, LANES - v.shape[1])))


def _layer(x, attn_norm_w, w_in, b_gate, ssm_conv_w, ssm_conv_b, ssm_dt_bias, ssm_a_log, ssm_d,
           ssm_norm_w, q_norm_w, k_norm_w, rel_bias, w_ssm_out, w_attn_out, w_out, ffn_norm_w,
           w_up, ffn_conv_w, ffn_conv_b, w_down):
    o_z, o_xbc = 0, D_INNER
    o_dt = o_xbc + D_XBC
    o_q = o_dt + SSM_HEADS
    o_g = o_q + 3 * D_ATTN
    wt = w_in.T
    wt_a = wt[:o_dt].astype(BF16)
    wt_b = wt[o_q:].astype(BF16)
    wt_dt = jnp.pad(wt[o_dt:o_q], ((0, LANES - SSM_HEADS), (0, 0)))

    proj, dt_raw = _in_proj(x, attn_norm_w[None, :], wt_a, wt_b, wt_dt)
    y_ssd = _ssd(proj, dt_raw, ssm_conv_w, ssm_conv_b[None, :], _pad_lanes(ssm_dt_bias[None, :]),
                 _pad_lanes(ssm_a_log[None, :]), jnp.repeat(ssm_d, SSM_HEAD_DIM)[None, :],
                 ssm_norm_w[None, :])
    qt, kn, vt, pen = _qk_prep(proj, q_norm_w[None, :], k_norm_w[None, :])
    y_attn = _moba(qt, kn, vt, pen, rel_bias.T)
    mix = _mix(y_ssd, y_attn, w_ssm_out.astype(BF16), w_attn_out.astype(BF16), proj, b_gate[None, :])
    x1, h2 = _out_norm(x, mix, w_out.astype(BF16), ffn_norm_w[None, :])
    act = _ffn_up(h2, w_up, ffn_conv_w, ffn_conv_b[None, :])
    return _ffn_down(act, w_down.astype(BF16), x1)


def kernel(x, attn_norm_w, w_in, b_gate, ssm_conv_w, ssm_conv_b, ssm_dt_bias, ssm_a_log, ssm_d,
           ssm_norm_w, q_norm_w, k_norm_w, rel_bias, w_ssm_out, w_attn_out, w_out, ffn_norm_w,
           w_up, ffn_conv_w, ffn_conv_b, w_down):
    assert x.shape[0] == 1 and attn_norm_w.shape[0] == 1
    out = _layer(x[0], attn_norm_w[0], w_in[0], b_gate[0], ssm_conv_w[0], ssm_conv_b[0],
                 ssm_dt_bias[0], ssm_a_log[0], ssm_d[0], ssm_norm_w[0], q_norm_w[0], k_norm_w[0],
                 rel_bias, w_ssm_out[0], w_attn_out[0], w_out[0], ffn_norm_w[0], w_up[0],
                 ffn_conv_w[0], ffn_conv_b[0], w_down[0])
    return out[None]
```

```python
import functools
import math

import jax
import jax.numpy as jnp
from jax import lax
from jax.experimental import pallas as pl
from jax.experimental.pallas import tpu as pltpu

F32 = jnp.float32
BF16 = jnp.bfloat16

D_MODEL = 2048
D_INNER = 4096
SSM_HEADS = 64
SSM_HEAD_DIM = 64
SSM_GROUPS = 8
SSM_HEADS_PER_GROUP = SSM_HEADS // SSM_GROUPS
SSM_STATE = 128
SSM_CONV = 4
SSM_CHUNK = 256
GROUP_CH = D_INNER // SSM_GROUPS
SSD_GROUPS_PER_STEP = 2
D_XBC = D_INNER + 2 * SSM_GROUPS * SSM_STATE
ATTN_HEADS = 16
ATTN_HEAD_DIM = 128
D_ATTN = ATTN_HEADS * ATTN_HEAD_DIM
MOBA_BLOCK = 256
MOBA_TOPK = 3
REL_BUCKETS = 32
REL_MAX_DIST = 128
D_FF = 5632
FFN_CONV = 3
EPS = 1e-6
LOG2E = 1.4426950408889634

LANES = 128
SUBLANES = 8
D_PROJ = D_XBC + 3 * D_ATTN + 2 * D_INNER
COL_Z = 0
COL_XBC = COL_Z + D_INNER
COL_Q = COL_XBC + D_XBC
COL_K = COL_Q + D_ATTN
COL_V = COL_K + D_ATTN
COL_G = COL_V + D_ATTN
VMEM_LIMIT = 56 * 1024 * 1024


def _cparams(sem):
    return pltpu.CompilerParams(dimension_semantics=sem, vmem_limit_bytes=VMEM_LIMIT)


def _split3(a):
    hi = a.astype(BF16)
    r = a - hi.astype(F32)
    mid = r.astype(BF16)
    lo = (r - mid.astype(F32)).astype(BF16)
    return hi, mid, lo


def _dot(a, b):
    return jnp.dot(a, b, preferred_element_type=F32)


def _dot_nt(a, b):
    return lax.dot_general(a, b, (((1,), (1,)), ((), ())), preferred_element_type=F32)


def _sigmoid(x):
    return 0.5 + 0.5 * jnp.tanh(0.5 * x)


def _silu(x):
    h = 0.5 * x
    return h + h * jnp.tanh(h)


def _in_proj_kernel(x_ref, nw_ref, w_ref, wdt_ref, o_ref, dt_ref, h_scr):
    @pl.when(pl.program_id(1) == 0)
    def _():
        x = x_ref[...]
        ms = jnp.mean(x * x, axis=-1, keepdims=True)
        h = x * lax.rsqrt(ms + EPS) * nw_ref[...]
        h_scr[...] = h.astype(BF16)
        h_hi, h_mid, _ = _split3(h)
        w_hi, w_mid, _ = _split3(wdt_ref[...])
        dt_ref[...] = _dot_nt(h_hi, w_hi) + _dot_nt(h_hi, w_mid) + _dot_nt(h_mid, w_hi)

    o_ref[...] = _dot_nt(h_scr[...], w_ref[...]).astype(o_ref.dtype)


def _in_proj(x, norm_w, wt, wt_dt, dt_row0, tm=1024, tn=1024):
    S = x.shape[0]
    tm = min(tm, S)
    assert dt_row0 % tn == 0 and SSM_HEADS % (2 * SUBLANES) == 0
    na = dt_row0 // tn

    def w_rows(m, n):
        return (pl.multiple_of(n * tn + jnp.where(n >= na, SSM_HEADS, 0), SSM_HEADS), 0)

    return pl.pallas_call(
        _in_proj_kernel,
        grid=(S // tm, D_PROJ // tn),
        in_specs=[
            pl.BlockSpec((tm, D_MODEL), lambda m, n: (m, 0)),
            pl.BlockSpec((1, D_MODEL), lambda m, n: (0, 0)),
            pl.BlockSpec((pl.Element(tn), pl.Element(D_MODEL)), w_rows),
            pl.BlockSpec((LANES, D_MODEL), lambda m, n: (0, 0)),
        ],
        out_specs=[
            pl.BlockSpec((tm, tn), lambda m, n: (m, n)),
            pl.BlockSpec((tm, LANES), lambda m, n: (m, 0)),
        ],
        out_shape=[
            jax.ShapeDtypeStruct((S, D_PROJ), BF16),
            jax.ShapeDtypeStruct((S, LANES), F32),
        ],
        scratch_shapes=[pltpu.VMEM((tm, D_MODEL), BF16)],
        compiler_params=_cparams(("arbitrary", "arbitrary")),
        name="in_proj",
    )(x, norm_w, wt, wt_dt)


def _conv_silu(raw_ref, tail_ref, pad_ref, w_ref, b_ref, g, taps):
    L = raw_ref.shape[0]
    pad_ref[0:SUBLANES, :] = tail_ref[g]
    pad_ref[SUBLANES:SUBLANES + L, :] = raw_ref[...].astype(F32)
    tail_ref[g] = pad_ref[L:L + SUBLANES, :]
    acc = b_ref[...]
    for k in range(taps):
        off = SUBLANES - (taps - 1) + k
        acc = acc + w_ref[k:k + 1, :] * pad_ref[off:off + L, :]
    return _silu(acc)


def _ssd_kernel(x_ref, b_ref, c_ref, z_ref, dtr_ref, cwx_ref, cwb_ref, cwc_ref,
                cbx_ref, cbb_ref, cbc_ref, dtb_ref, alog_ref, dsk_ref, nw_ref,
                o_ref,
                tailx, tailb, tailc, padx, padb, padc, acs_scr, acst_scr, lbt_scr, ht_scr):
    c = pl.program_id(0)
    gp = pl.program_id(1)
    L = SSM_CHUNK
    QW = 4 * SSM_HEAD_DIM
    GPS = SSD_GROUPS_PER_STEP

    @pl.when(c == 0)
    def _():
        for gg in range(GPS):
            g0 = gp * GPS + gg
            tailx[g0] = jnp.zeros(tailx.shape[1:], F32)
            tailb[g0] = jnp.zeros(tailb.shape[1:], F32)
            tailc[g0] = jnp.zeros(tailc.shape[1:], F32)
            ht_scr[g0] = jnp.zeros(ht_scr.shape[1:], F32)

    @pl.when(gp == 0)
    def _():
        t = dtr_ref[...] + dtb_ref[...]
        dt = jnp.maximum(t, 0.0) + jnp.log(1.0 + jnp.exp(-jnp.abs(t)))
        a = -jnp.exp(alog_ref[...])
        da = dt * a
        ri = lax.broadcasted_iota(jnp.int32, (L, L), 0)
        ci = lax.broadcasted_iota(jnp.int32, (L, L), 1)
        tri = jnp.where(ri >= ci, 1.0, 0.0).astype(BF16)
        hi, mid, lo = _split3(da)
        a_cs = _dot(tri, hi) + _dot(tri, mid) + _dot(tri, lo)
        a2 = a_cs * LOG2E
        a2t = a2.T
        acs_scr[...] = a2
        acst_scr[...] = a2t
        lbt_scr[...] = a2t - jnp.log(dt.T) * LOG2E

    for gg in range(GPS):
        xs = slice(gg * GROUP_CH, (gg + 1) * GROUP_CH)
        ns = slice(gg * SSM_STATE, (gg + 1) * SSM_STATE)
        _ssd_group(gp * GPS + gg,
                   x_ref.at[:, xs], b_ref.at[:, ns], c_ref.at[:, ns], z_ref.at[:, xs],
                   cwx_ref.at[:, xs], cwb_ref.at[:, ns], cwc_ref.at[:, ns],
                   cbx_ref.at[:, xs], cbb_ref.at[:, ns], cbc_ref.at[:, ns],
                   dsk_ref.at[:, xs], nw_ref.at[:, xs], o_ref.at[:, xs],
                   tailx, tailb, tailc, padx.at[gg], padb.at[gg], padc.at[gg],
                   acs_scr, acst_scr, lbt_scr, ht_scr)


def _ssd_group(g, x_ref, b_ref, c_ref, z_ref, cwx_ref, cwb_ref, cwc_ref, cbx_ref, cbb_ref, cbc_ref,
               dsk_ref, nw_ref, o_ref, tailx, tailb, tailc, padx, padb, padc,
               acs_scr, acst_scr, lbt_scr, ht_scr):
    L = SSM_CHUNK
    QW = 4 * SSM_HEAD_DIM
    xa = _conv_silu(x_ref, tailx, padx, cwx_ref, cbx_ref, g, SSM_CONV)
    ba = _conv_silu(b_ref, tailb, padb, cwb_ref, cbb_ref, g, SSM_CONV)
    ca = _conv_silu(c_ref, tailc, padc, cwc_ref, cbc_ref, g, SSM_CONV)

    xb = xa.astype(BF16)
    bt = ba.T
    cb = _dot(ca.astype(BF16), bt.astype(BF16))
    a_cs = acs_scr[...]
    ri = lax.broadcasted_iota(jnp.int32, (L, L), 0)
    ci = lax.broadcasted_iota(jnp.int32, (L, L), 1)
    causal = ri >= ci
    lane_h = lax.broadcasted_iota(jnp.int32, (L, LANES), 1)
    lane_q = lax.broadcasted_iota(jnp.int32, (1, QW), 1)

    ys = []
    for q in range(GROUP_CH // QW):
        xq = xb[:, q * QW:(q + 1) * QW]
        hq = ht_scr[g, :, q * QW:(q + 1) * QW]
        hqb = hq.astype(BF16)
        acc = jnp.zeros((L, QW), F32)
        st = jnp.zeros((SSM_STATE, QW), F32)
        cdq = jnp.zeros((1, QW), F32)
        for j in range(QW // SSM_HEAD_DIM):
            h = g * SSM_HEADS_PER_GROUP + q * (QW // SSM_HEAD_DIM) + j
            a_row = acst_scr[pl.ds(h, 1), :]
            b_row = lbt_scr[pl.ds(h, 1), :]
            a_col = jnp.sum(jnp.where(lane_h == h, a_cs, 0.0), axis=1, keepdims=True)
            dmat = jnp.where(causal, a_col - b_row, -jnp.inf)
            mh = (cb * jnp.exp2(dmat)).astype(BF16)
            lm = (lane_q >= j * SSM_HEAD_DIM) & (lane_q < (j + 1) * SSM_HEAD_DIM)
            xm = jnp.where(lm, xq, jnp.zeros_like(xq))
            hm = jnp.where(lm, hqb, jnp.zeros_like(hqb))
            ce = (ca * jnp.exp2(a_col)).astype(BF16)
            acc = acc + _dot(mh, xm) + _dot(ce, hm)
            a_last = a_row[:, L - 1:L]
            w_row = jnp.exp2(a_last - b_row)
            st = st + _dot((bt * w_row).astype(BF16), xm)
            cdq = jnp.where(lm, jnp.exp2(a_last), cdq)
        ht_scr[g, :, q * QW:(q + 1) * QW] = hq * cdq + st
        ys.append(acc + xa[:, q * QW:(q + 1) * QW] * dsk_ref[:, q * QW:(q + 1) * QW])

    z = z_ref[...].astype(F32)
    sz = _silu(z)
    ygs = [ys[q] * sz[:, q * QW:(q + 1) * QW] for q in range(len(ys))]
    ss = sum(jnp.sum(v * v, axis=1, keepdims=True) for v in ygs)
    scale = lax.rsqrt(ss * (1.0 / GROUP_CH) + EPS)
    for q in range(len(ys)):
        o_ref[:, q * QW:(q + 1) * QW] = (
            ygs[q] * scale * nw_ref[:, q * QW:(q + 1) * QW]).astype(o_ref.dtype)


def _ssd(proj, dt_raw, conv_w, conv_b, dt_bias, a_log, d_exp, norm_w):
    S = proj.shape[0]
    L, G, N = SSM_CHUNK, SSM_GROUPS, SSM_STATE
    GPS = SSD_GROUPS_PER_STEP
    XW, NW = GPS * GROUP_CH, GPS * N
    bcol = D_INNER // NW
    ccol = bcol + G // GPS
    pxcol = COL_XBC // XW
    pbcol = COL_XBC // NW + bcol
    pccol = pbcol + G // GPS
    zcol = COL_Z // XW
    return pl.pallas_call(
        _ssd_kernel,
        grid=(S // L, G // GPS),
        in_specs=[
            pl.BlockSpec((L, XW), lambda c, g: (c, pxcol + g)),
            pl.BlockSpec((L, NW), lambda c, g: (c, pbcol + g)),
            pl.BlockSpec((L, NW), lambda c, g: (c, pccol + g)),
            pl.BlockSpec((L, XW), lambda c, g: (c, zcol + g)),
            pl.BlockSpec((L, LANES), lambda c, g: (c, 0)),
            pl.BlockSpec((SSM_CONV, XW), lambda c, g: (0, g)),
            pl.BlockSpec((SSM_CONV, NW), lambda c, g: (0, bcol + g)),
            pl.BlockSpec((SSM_CONV, NW), lambda c, g: (0, ccol + g)),
            pl.BlockSpec((1, XW), lambda c, g: (0, g)),
            pl.BlockSpec((1, NW), lambda c, g: (0, bcol + g)),
            pl.BlockSpec((1, NW), lambda c, g: (0, ccol + g)),
            pl.BlockSpec((1, LANES), lambda c, g: (0, 0)),
            pl.BlockSpec((1, LANES), lambda c, g: (0, 0)),
            pl.BlockSpec((1, XW), lambda c, g: (0, g)),
            pl.BlockSpec((1, XW), lambda c, g: (0, g)),
        ],
        out_specs=pl.BlockSpec((L, XW), lambda c, g: (c, g)),
        out_shape=jax.ShapeDtypeStruct((S, D_INNER), BF16),
        scratch_shapes=[
            pltpu.VMEM((G, SUBLANES, GROUP_CH), F32),
            pltpu.VMEM((G, SUBLANES, N), F32),
            pltpu.VMEM((G, SUBLANES, N), F32),
            pltpu.VMEM((GPS, L + SUBLANES, GROUP_CH), F32),
            pltpu.VMEM((GPS, L + SUBLANES, N), F32),
            pltpu.VMEM((GPS, L + SUBLANES, N), F32),
            pltpu.VMEM((L, LANES), F32),
            pltpu.VMEM((LANES, L), F32),
            pltpu.VMEM((LANES, L), F32),
            pltpu.VMEM((G, N, GROUP_CH), F32),
        ],
        compiler_params=_cparams(("arbitrary", "arbitrary")),
        name="ssd",
    )(proj, proj, proj, proj, dt_raw, conv_w, conv_w, conv_w, conv_b, conv_b, conv_b,
      dt_bias, a_log, d_exp, norm_w)


def _qk_prep_kernel(q_ref, k_ref, v_ref, qw_ref, kw_ref, qt_ref, kn_ref, vt_ref, pen_ref, kbar_scr):
    S = q_ref.shape[0]
    BS = MOBA_BLOCK
    nb = S // BS
    nbp = kbar_scr.shape[0]
    kbar_scr[...] = jnp.zeros(kbar_scr.shape, F32)

    def kloop(b, carry):
        rows = pl.ds(pl.multiple_of(b * BS, BS), BS)
        k = k_ref[rows, :].astype(F32)
        kn = k * lax.rsqrt(jnp.mean(k * k, axis=-1, keepdims=True) + EPS) * kw_ref[...]
        kn_ref[0, rows, :] = kn.astype(BF16)
        kbar_scr[pl.ds(b, 1), :] = jnp.mean(kn, axis=0, keepdims=True)
        vt_ref[0, b, 0:ATTN_HEAD_DIM, :] = v_ref[rows, :].astype(F32).T.astype(BF16)
        vt_ref[0, b, ATTN_HEAD_DIM:, :] = jnp.ones((MOBA_VT_ROWS - ATTN_HEAD_DIM, BS), BF16)
        return carry

    lax.fori_loop(0, nb, kloop, 0, unroll=4)
    kb_hi, kb_mid, _ = _split3(kbar_scr[...])

    def qloop(i, carry):
        rows = pl.ds(pl.multiple_of(i * BS, BS), BS)
        q = q_ref[rows, :].astype(F32)
        qn = q * lax.rsqrt(jnp.mean(q * q, axis=-1, keepdims=True) + EPS) * qw_ref[...]
        qnt = (qn * (ATTN_HEAD_DIM ** -0.5)).T
        qt_ref[0, i] = (qnt * LOG2E).astype(BF16)
        q_hi, q_mid, _ = _split3(qnt)
        gate = _dot(kb_hi, q_hi) + _dot(kb_mid, q_hi) + _dot(kb_hi, q_mid)
        blk = lax.broadcasted_iota(jnp.int32, gate.shape, 0)
        blk_f = blk.astype(F32)
        gate = jnp.where(blk < i, gate, -jnp.inf)
        pen = jnp.full(gate.shape, -jnp.inf, F32)
        for _ in range(MOBA_TOPK):
            mx = jnp.max(gate, axis=0, keepdims=True)
            cand = jnp.where((gate == mx) & (mx > -jnp.inf), blk_f, float(nbp))
            idx = jnp.min(cand, axis=0, keepdims=True)
            hit = blk_f == idx
            pen = jnp.where(hit, 0.0, pen)
            gate = jnp.where(hit, -jnp.inf, gate)
        pen_ref[0, i] = pen
        return carry

    lax.fori_loop(0, nb, qloop, 0, unroll=4)


def _qk_prep(proj, q_norm_w, k_norm_w):
    S = proj.shape[0]
    H, D, BS = ATTN_HEADS, ATTN_HEAD_DIM, MOBA_BLOCK
    nb = S // BS
    nbp = -(-nb // SUBLANES) * SUBLANES
    return pl.pallas_call(
        _qk_prep_kernel,
        grid=(H,),
        in_specs=[
            pl.BlockSpec((S, D), lambda h: (0, COL_Q // D + h)),
            pl.BlockSpec((S, D), lambda h: (0, COL_K // D + h)),
            pl.BlockSpec((S, D), lambda h: (0, COL_V // D + h)),
            pl.BlockSpec((1, D), lambda h: (0, 0)),
            pl.BlockSpec((1, D), lambda h: (0, 0)),
        ],
        out_specs=[
            pl.BlockSpec((1, nb, D, BS), lambda h: (h, 0, 0, 0)),
            pl.BlockSpec((1, S, D), lambda h: (h, 0, 0)),
            pl.BlockSpec((1, nb, MOBA_VT_ROWS, BS), lambda h: (h, 0, 0, 0)),
            pl.BlockSpec((1, nb, nbp, BS), lambda h: (h, 0, 0, 0)),
        ],
        out_shape=[
            jax.ShapeDtypeStruct((H, nb, D, BS), BF16),
            jax.ShapeDtypeStruct((H, S, D), BF16),
            jax.ShapeDtypeStruct((H, nb, MOBA_VT_ROWS, BS), BF16),
            jax.ShapeDtypeStruct((H, nb, nbp, BS), F32),
        ],
        scratch_shapes=[pltpu.VMEM((nbp, D), F32)],
        compiler_params=_cparams(("arbitrary",)),
        name="qk_prep",
    )(proj, proj, proj, q_norm_w, k_norm_w)


def _rel_bucket(dist):
    n = jnp.maximum(dist, 0)
    max_exact = REL_BUCKETS // 2
    nf = jnp.maximum(n, 1).astype(F32)
    large = max_exact + (jnp.log(nf / max_exact) / math.log(REL_MAX_DIST / max_exact)
                         * (REL_BUCKETS - max_exact)).astype(jnp.int32)
    large = jnp.minimum(large, REL_BUCKETS - 1)
    return jnp.where(n < max_exact, n, large)


MOBA_HEADS_PER_STEP = 4
MOBA_VT_ROWS = ATTN_HEAD_DIM + 2 * SUBLANES


def _moba_kernel(tbl_ref, qt_ref, k_ref, vt_ref, pen_ref, o_ref,
                 bown_scr, bprev_scr, m_scr, acc_scr, s0_scr, s1_scr,
                 p0_scr, p1_scr, al0_scr, al1_scr):
    hp = pl.program_id(0)
    i = pl.program_id(1)
    BS, HB = MOBA_BLOCK, MOBA_HEADS_PER_STEP
    D = ATTN_HEAD_DIM

    @pl.when(i == 0)
    def _():
        kl = lax.broadcasted_iota(jnp.int32, (BS, BS), 0)
        ql = lax.broadcasted_iota(jnp.int32, (BS, BS), 1)
        d = ql - kl
        b_own = _rel_bucket(d)
        b_prev = _rel_bucket(d + BS)
        for a in range(HB):
            h = hp * HB + a
            far = tbl_ref[h, REL_BUCKETS - 1]

            def lookup(bucket):
                val = jnp.zeros((BS, BS), F32)
                for b in range(REL_BUCKETS):
                    val = jnp.where(bucket == b, tbl_ref[h, b], val)
                return (val - far) * LOG2E

            bown_scr[a] = jnp.where(d >= 0, lookup(b_own), -jnp.inf)
            bprev_scr[a] = lookup(b_prev)

    def krows(a, j):
        return k_ref[a, pl.ds(pl.multiple_of(j * BS, BS), BS), :]

    jp = jnp.maximum(i - 1, 0)
    no_prev = jnp.where(i == 0, -jnp.inf, 0.0)
    for a in range(HB):
        qt = qt_ref[a, 0]
        s_own = _dot(krows(a, i), qt) + bown_scr[a]
        s_prev = _dot(krows(a, jp), qt) + bprev_scr[a] + (pen_ref[a, 0, pl.ds(jp, 1), :] + no_prev)
        m0 = jnp.maximum(jnp.max(s_own, axis=0, keepdims=True), jnp.max(s_prev, axis=0, keepdims=True))
        p_own = jnp.exp2(s_own - m0)
        p_prev = jnp.exp2(s_prev - m0)
        m_scr[a] = m0
        acc_scr[a] = (_dot(vt_ref[a, i], p_own.astype(BF16))
                      + _dot(vt_ref[a, jp], p_prev.astype(BF16)))

    n_far = jnp.maximum(i - 1, 0)
    nb = k_ref.shape[1] // BS

    def clampj(j):
        return jnp.clip(j, 0, nb - 1)

    def scores(j, s_slot):
        for a in range(HB):
            s_slot[a] = _dot(krows(a, clampj(j)), qt_ref[a, 0])

    def softmax(j, s_slot, p_slot, al_slot, mask_tail):
        for a in range(HB):
            pr = pen_ref[a, 0, pl.ds(clampj(j), 1), :]
            if mask_tail:
                pr = pr + jnp.where(j >= n_far, -jnp.inf, 0.0)
            s = s_slot[a]
            m_old = m_scr[a]
            m_new = jnp.maximum(m_old, jnp.max(s, axis=0, keepdims=True) + pr)
            p_slot[a] = jnp.exp2(s - (m_new - pr)).astype(BF16)
            al_slot[a] = jnp.exp2(m_old - m_new)
            m_scr[a] = m_new

    def accumulate(j, p_slot, al_slot):
        for a in range(HB):
            acc_scr[a] = al_slot[a] * acc_scr[a] + _dot(vt_ref[a, clampj(j)], p_slot[a])

    scores(0, s0_scr)
    p1_scr[...] = jnp.zeros(p1_scr.shape, BF16)
    al1_scr[...] = jnp.ones(al1_scr.shape, F32)

    def far_loop(g, carry):
        j = 2 * g
        scores(j + 1, s1_scr)
        accumulate(j - 1, p1_scr, al1_scr)
        softmax(j, s0_scr, p0_scr, al0_scr, False)
        scores(j + 2, s0_scr)
        accumulate(j, p0_scr, al0_scr)
        softmax(j + 1, s1_scr, p1_scr, al1_scr, True)
        return carry

    n_pairs = (n_far + 1) // 2
    lax.fori_loop(0, n_pairs, far_loop, 0)
    accumulate(2 * n_pairs - 1, p1_scr, al1_scr)
    for a in range(HB):
        acc = acc_scr[a]
        o_ref[:, a * D:(a + 1) * D] = (acc[0:D, :] / acc[D:D + 1, :]).T.astype(o_ref.dtype)


def _moba(qt, kn, vt, pen, table_h):
    H, nb, D, BS = qt.shape
    S = nb * BS
    nbp = pen.shape[2]
    HB = MOBA_HEADS_PER_STEP
    return pl.pallas_call(
        _moba_kernel,
        grid=(H // HB, nb),
        in_specs=[
            pl.BlockSpec(memory_space=pltpu.SMEM),
            pl.BlockSpec((HB, 1, D, BS), lambda hp, i: (hp, i, 0, 0)),
            pl.BlockSpec((HB, S, D), lambda hp, i: (hp, 0, 0)),
            pl.BlockSpec((HB, nb, MOBA_VT_ROWS, BS), lambda hp, i: (hp, 0, 0, 0)),
            pl.BlockSpec((HB, 1, nbp, BS), lambda hp, i: (hp, i, 0, 0)),
        ],
        out_specs=pl.BlockSpec((BS, HB * D), lambda hp, i: (i, hp)),
        out_shape=jax.ShapeDtypeStruct((S, H * D), BF16),
        scratch_shapes=[
            pltpu.VMEM((HB, BS, BS), F32),
            pltpu.VMEM((HB, BS, BS), F32),
            pltpu.VMEM((HB, 1, BS), F32),
            pltpu.VMEM((HB, MOBA_VT_ROWS, BS), F32),
            pltpu.VMEM((HB, BS, BS), F32),
            pltpu.VMEM((HB, BS, BS), F32),
            pltpu.VMEM((HB, BS, BS), BF16),
            pltpu.VMEM((HB, BS, BS), BF16),
            pltpu.VMEM((HB, 1, BS), F32),
            pltpu.VMEM((HB, 1, BS), F32),
        ],
        compiler_params=_cparams(("arbitrary", "arbitrary")),
        name="moba",
    )(table_h, qt, kn, vt, pen)


def _mix_kernel(ys_ref, ya_ref, ws_ref, wa_ref, gs_ref, ga_ref, bs_ref, ba_ref, o_ref):
    gs = _sigmoid(gs_ref[...].astype(F32) + bs_ref[...])
    ga = _sigmoid(ga_ref[...].astype(F32) + ba_ref[...])
    o_ref[...] = (gs * _dot(ys_ref[...], ws_ref[...])
                  + ga * _dot(ya_ref[...], wa_ref[...])).astype(o_ref.dtype)


def _mix(y_ssd, y_attn, w_ssm_out, w_attn_out, proj, b_gate, tm=1024, tn=512):
    S = y_ssd.shape[0]
    tm = min(tm, S)
    gs_col = COL_G // tn
    ga_col = (COL_G + D_MODEL) // tn
    nb = D_MODEL // tn
    return pl.pallas_call(
        _mix_kernel,
        grid=(S // tm, D_MODEL // tn),
        in_specs=[
            pl.BlockSpec((tm, D_INNER), lambda m, n: (m, 0)),
            pl.BlockSpec((tm, D_ATTN), lambda m, n: (m, 0)),
            pl.BlockSpec((D_INNER, tn), lambda m, n: (0, n)),
            pl.BlockSpec((D_ATTN, tn), lambda m, n: (0, n)),
            pl.BlockSpec((tm, tn), lambda m, n: (m, gs_col + n)),
            pl.BlockSpec((tm, tn), lambda m, n: (m, ga_col + n)),
            pl.BlockSpec((1, tn), lambda m, n: (0, n)),
            pl.BlockSpec((1, tn), lambda m, n: (0, nb + n)),
        ],
        out_specs=pl.BlockSpec((tm, tn), lambda m, n: (m, n)),
        out_shape=jax.ShapeDtypeStruct((S, D_MODEL), BF16),
        compiler_params=_cparams(("arbitrary", "arbitrary")),
        name="mix",
    )(y_ssd, y_attn, w_ssm_out, w_attn_out, proj, proj, b_gate, b_gate)


def _out_norm_kernel(x_ref, mix_ref, w_ref, nw_ref, x1_ref, h2_ref):
    x1 = x_ref[...] + _dot(mix_ref[...], w_ref[...])
    x1_ref[...] = x1
    ms = jnp.mean(x1 * x1, axis=-1, keepdims=True)
    h2_ref[...] = (x1 * lax.rsqrt(ms + EPS) * nw_ref[...]).astype(h2_ref.dtype)


def _out_norm(x, mix, w_out, ffn_norm_w, tm=512):
    S = x.shape[0]
    tm = min(tm, S)
    return pl.pallas_call(
        _out_norm_kernel,
        grid=(S // tm,),
        in_specs=[
            pl.BlockSpec((tm, D_MODEL), lambda m: (m, 0)),
            pl.BlockSpec((tm, D_MODEL), lambda m: (m, 0)),
            pl.BlockSpec((D_MODEL, D_MODEL), lambda m: (0, 0)),
            pl.BlockSpec((1, D_MODEL), lambda m: (0, 0)),
        ],
        out_specs=[
            pl.BlockSpec((tm, D_MODEL), lambda m: (m, 0)),
            pl.BlockSpec((tm, D_MODEL), lambda m: (m, 0)),
        ],
        out_shape=[
            jax.ShapeDtypeStruct((S, D_MODEL), F32),
            jax.ShapeDtypeStruct((S, D_MODEL), BF16),
        ],
        compiler_params=_cparams(("arbitrary",)),
        name="out_norm",
    )(x, mix, w_out, ffn_norm_w)


def _ffn_up_kernel(h_ref, wg32_ref, wu32_ref, cwg_ref, cwu_ref, cbg_ref, cbu_ref, o_ref,
                   rawg, rawu, wg_ref, wu_ref):
    tm = h_ref.shape[0]
    m = pl.program_id(1)
    last = pl.num_programs(1) - 1
    cur = m % 2
    prev = 1 - cur

    @pl.when(m == 0)
    def _():
        wg_ref[...] = wg32_ref[...].astype(BF16)
        wu_ref[...] = wu32_ref[...].astype(BF16)
        rawg[1] = jnp.zeros(rawg.shape[1:], F32)
        rawu[1] = jnp.zeros(rawu.shape[1:], F32)

    def epilogue():
        def conv(cw_ref, cb_ref, raw):
            acc = cb_ref[...]
            for k in range(FFN_CONV):
                off = SUBLANES - (FFN_CONV - 1) + k
                acc = acc + cw_ref[k:k + 1, :] * raw[prev, off:off + tm, :]
            return acc

        act = _silu(conv(cwg_ref, cbg_ref, rawg)) * conv(cwu_ref, cbu_ref, rawu)
        o_ref[...] = act.astype(o_ref.dtype)
        return act

    @pl.when(m < last)
    def _():
        act = epilogue()
        dep = jnp.max(act.reshape(tm // SUBLANES, SUBLANES, act.shape[1]), axis=0)
        for w_ref, raw in ((wg_ref, rawg), (wu_ref, rawu)):
            raw[cur, 0:SUBLANES, :] = raw[prev, tm:tm + SUBLANES, :]
            r = _dot(h_ref[...], w_ref[...])
            raw[cur, SUBLANES:tm, :] = r[0:tm - SUBLANES, :]
            raw[cur, tm:tm + SUBLANES, :] = jnp.where(m < 0, dep, r[tm - SUBLANES:tm, :])

    @pl.when(m == last)
    def _():
        epilogue()


def _ffn_up(h2, w_up, conv_w, conv_b, tm=1024, tn=512):
    S = h2.shape[0]
    tm = min(tm, S)
    nb = D_FF // tn
    nm = S // tm
    return pl.pallas_call(
        _ffn_up_kernel,
        grid=(nb, nm + 1),
        in_specs=[
            pl.BlockSpec((tm, D_MODEL), lambda n, m: (jnp.minimum(m, nm - 1), 0)),
            pl.BlockSpec((D_MODEL, tn), lambda n, m: (0, n)),
            pl.BlockSpec((D_MODEL, tn), lambda n, m: (0, nb + n)),
            pl.BlockSpec((FFN_CONV, tn), lambda n, m: (0, n)),
            pl.BlockSpec((FFN_CONV, tn), lambda n, m: (0, nb + n)),
            pl.BlockSpec((1, tn), lambda n, m: (0, n)),
            pl.BlockSpec((1, tn), lambda n, m: (0, nb + n)),
        ],
        out_specs=pl.BlockSpec((tm, tn), lambda n, m: (jnp.maximum(m - 1, 0), n)),
        out_shape=jax.ShapeDtypeStruct((S, D_FF), BF16),
        scratch_shapes=[
            pltpu.VMEM((2, tm + SUBLANES, tn), F32),
            pltpu.VMEM((2, tm + SUBLANES, tn), F32),
            pltpu.VMEM((D_MODEL, tn), BF16),
            pltpu.VMEM((D_MODEL, tn), BF16),
        ],
        compiler_params=_cparams(("arbitrary", "arbitrary")),
        name="ffn_up",
    )(h2, w_up, w_up, conv_w, conv_w, conv_b, conv_b)


def _ffn_down_kernel(a_ref, w_ref, x1_ref, o_ref):
    o_ref[...] = x1_ref[...] + _dot(a_ref[...], w_ref[...])


def _ffn_down(act, w_down, x1, tm=1024, tn=512):
    S = act.shape[0]
    tm = min(tm, S)
    return pl.pallas_call(
        _ffn_down_kernel,
        grid=(S // tm, D_MODEL // tn),
        in_specs=[
            pl.BlockSpec((tm, D_FF), lambda m, n: (m, 0)),
            pl.BlockSpec((D_FF, tn), lambda m, n: (0, n)),
            pl.BlockSpec((tm, tn), lambda m, n: (m, n)),
        ],
        out_specs=pl.BlockSpec((tm, tn), lambda m, n: (m, n)),
        out_shape=jax.ShapeDtypeStruct((S, D_MODEL), F32),
        compiler_params=_cparams(("arbitrary", "arbitrary")),
        name="ffn_down",
    )(act, w_down, x1)


def _pad_lanes(v):
    return jnp.pad(v, ((0, 0), (0, LANES - v.shape[1])))


def _layer(x, attn_norm_w, w_in, b_gate, ssm_conv_w, ssm_conv_b, ssm_dt_bias, ssm_a_log, ssm_d,
           ssm_norm_w, q_norm_w, k_norm_w, rel_bias, w_ssm_out, w_attn_out, w_out, ffn_norm_w,
           w_up, ffn_conv_w, ffn_conv_b, w_down):
    o_z, o_xbc = 0, D_INNER
    o_dt = o_xbc + D_XBC
    o_q = o_dt + SSM_HEADS
    o_g = o_q + 3 * D_ATTN
    wt = w_in.T
    wt_dt = jnp.pad(wt[o_dt:o_q], ((0, LANES - SSM_HEADS), (0, 0)))

    proj, dt_raw = _in_proj(x, attn_norm_w[None, :], wt.astype(BF16), wt_dt, o_dt)
    y_ssd = _ssd(proj, dt_raw, ssm_conv_w, ssm_conv_b[None, :], _pad_lanes(ssm_dt_bias[None, :]),
                 _pad_lanes(ssm_a_log[None, :]), jnp.repeat(ssm_d, SSM_HEAD_DIM)[None, :],
                 ssm_norm_w[None, :])
    qt, kn, vt, pen = _qk_prep(proj, q_norm_w[None, :], k_norm_w[None, :])
    y_attn = _moba(qt, kn, vt, pen, rel_bias.T)
    mix = _mix(y_ssd, y_attn, w_ssm_out.astype(BF16), w_attn_out.astype(BF16), proj, b_gate[None, :])
    x1, h2 = _out_norm(x, mix, w_out.astype(BF16), ffn_norm_w[None, :])
    act = _ffn_up(h2, w_up, ffn_conv_w, ffn_conv_b[None, :])
    return _ffn_down(act, w_down.astype(BF16), x1)


def kernel(x, attn_norm_w, w_in, b_gate, ssm_conv_w, ssm_conv_b, ssm_dt_bias, ssm_a_log, ssm_d,
           ssm_norm_w, q_norm_w, k_norm_w, rel_bias, w_ssm_out, w_attn_out, w_out, ffn_norm_w,
           w_up, ffn_conv_w, ffn_conv_b, w_down):
    assert x.shape[0] == 1 and attn_norm_w.shape[0] == 1
    out = _layer(x[0], attn_norm_w[0], w_in[0], b_gate[0], ssm_conv_w[0], ssm_conv_b[0],
                 ssm_dt_bias[0], ssm_a_log[0], ssm_d[0], ssm_norm_w[0], q_norm_w[0], k_norm_w[0],
                 rel_bias, w_ssm_out[0], w_attn_out[0], w_out[0], ffn_norm_w[0], w_up[0],
                 ffn_conv_w[0], ffn_conv_b[0], w_down[0])
    return out[None]
```

```python
import functools
import math

import jax
import jax.numpy as jnp
from jax import lax
from jax.experimental import pallas as pl
from jax.experimental.pallas import tpu as pltpu

F32 = jnp.float32
BF16 = jnp.bfloat16

D_MODEL = 2048
D_INNER = 4096
SSM_HEADS = 64
SSM_HEAD_DIM = 64
SSM_GROUPS = 8
SSM_HEADS_PER_GROUP = SSM_HEADS // SSM_GROUPS
SSM_STATE = 128
SSM_CONV = 4
SSM_CHUNK = 256
GROUP_CH = D_INNER // SSM_GROUPS
SSD_GROUPS_PER_STEP = 2
D_XBC = D_INNER + 2 * SSM_GROUPS * SSM_STATE
ATTN_HEADS = 16
ATTN_HEAD_DIM = 128
D_ATTN = ATTN_HEADS * ATTN_HEAD_DIM
MOBA_BLOCK = 256
MOBA_TOPK = 3
REL_BUCKETS = 32
REL_MAX_DIST = 128
D_FF = 5632
FFN_CONV = 3
EPS = 1e-6
LOG2E = 1.4426950408889634

LANES = 128
SUBLANES = 8
D_PROJ = D_XBC + 3 * D_ATTN + 2 * D_INNER
COL_Z = 0
COL_XBC = COL_Z + D_INNER
COL_Q = COL_XBC + D_XBC
COL_K = COL_Q + D_ATTN
COL_V = COL_K + D_ATTN
COL_G = COL_V + D_ATTN
VMEM_LIMIT = 56 * 1024 * 1024


def _cparams(sem):
    return pltpu.CompilerParams(dimension_semantics=sem, vmem_limit_bytes=VMEM_LIMIT)


def _split3(a):
    hi = a.astype(BF16)
    r = a - hi.astype(F32)
    mid = r.astype(BF16)
    lo = (r - mid.astype(F32)).astype(BF16)
    return hi, mid, lo


def _dot(a, b):
    return jnp.dot(a, b, preferred_element_type=F32)


def _dot_nt(a, b):
    return lax.dot_general(a, b, (((1,), (1,)), ((), ())), preferred_element_type=F32)


def _sigmoid(x):
    return 0.5 + 0.5 * jnp.tanh(0.5 * x)


def _silu(x):
    h = 0.5 * x
    return h + h * jnp.tanh(h)


def _in_proj_kernel(x_ref, nw_ref, w_ref, wdt_ref, o_ref, dt_ref, h_scr):
    @pl.when(pl.program_id(1) == 0)
    def _():
        x = x_ref[...]
        ms = jnp.mean(x * x, axis=-1, keepdims=True)
        h = x * lax.rsqrt(ms + EPS) * nw_ref[...]
        h_scr[...] = h.astype(BF16)
        h_hi, h_mid, _ = _split3(h)
        w_hi, w_mid, _ = _split3(wdt_ref[...])
        dt_ref[...] = _dot_nt(h_hi, w_hi) + _dot_nt(h_hi, w_mid) + _dot_nt(h_mid, w_hi)

    o_ref[...] = _dot_nt(h_scr[...], w_ref[...]).astype(o_ref.dtype)


def _in_proj(x, norm_w, wt, wt_dt, dt_row0, tm=1024, tn=1024):
    S = x.shape[0]
    tm = min(tm, S)
    assert dt_row0 % tn == 0 and SSM_HEADS % (2 * SUBLANES) == 0
    na = dt_row0 // tn

    def w_rows(m, n):
        return (pl.multiple_of(n * tn + jnp.where(n >= na, SSM_HEADS, 0), SSM_HEADS), 0)

    return pl.pallas_call(
        _in_proj_kernel,
        grid=(S // tm, D_PROJ // tn),
        in_specs=[
            pl.BlockSpec((tm, D_MODEL), lambda m, n: (m, 0)),
            pl.BlockSpec((1, D_MODEL), lambda m, n: (0, 0)),
            pl.BlockSpec((pl.Element(tn), pl.Element(D_MODEL)), w_rows),
            pl.BlockSpec((LANES, D_MODEL), lambda m, n: (0, 0)),
        ],
        out_specs=[
            pl.BlockSpec((tm, tn), lambda m, n: (m, n)),
            pl.BlockSpec((tm, LANES), lambda m, n: (m, 0)),
        ],
        out_shape=[
            jax.ShapeDtypeStruct((S, D_PROJ), BF16),
            jax.ShapeDtypeStruct((S, LANES), F32),
        ],
        scratch_shapes=[pltpu.VMEM((tm, D_MODEL), BF16)],
        compiler_params=_cparams(("arbitrary", "arbitrary")),
        name="in_proj",
    )(x, norm_w, wt, wt_dt)


def _conv_silu(raw_ref, tail_ref, pad_ref, w_ref, b_ref, g, taps):
    L = raw_ref.shape[0]
    pad_ref[0:SUBLANES, :] = tail_ref[g]
    pad_ref[SUBLANES:SUBLANES + L, :] = raw_ref[...].astype(F32)
    tail_ref[g] = pad_ref[L:L + SUBLANES, :]
    acc = b_ref[...]
    for k in range(taps):
        off = SUBLANES - (taps - 1) + k
        acc = acc + w_ref[k:k + 1, :] * pad_ref[off:off + L, :]
    return _silu(acc)


def _ssd_kernel(x_ref, b_ref, c_ref, z_ref, dtr_ref, cwx_ref, cwb_ref, cwc_ref,
                cbx_ref, cbb_ref, cbc_ref, dtb_ref, alog_ref, dsk_ref, nw_ref,
                o_ref,
                tailx, tailb, tailc, padx, padb, padc, acs_scr, acst_scr, lbt_scr, ht_scr):
    c = pl.program_id(0)
    gp = pl.program_id(1)
    L = SSM_CHUNK
    QW = 4 * SSM_HEAD_DIM
    GPS = SSD_GROUPS_PER_STEP

    @pl.when(c == 0)
    def _():
        for gg in range(GPS):
            g0 = gp * GPS + gg
            tailx[g0] = jnp.zeros(tailx.shape[1:], F32)
            tailb[g0] = jnp.zeros(tailb.shape[1:], F32)
            tailc[g0] = jnp.zeros(tailc.shape[1:], F32)
            ht_scr[g0] = jnp.zeros(ht_scr.shape[1:], F32)

    @pl.when(gp == 0)
    def _():
        t = dtr_ref[...] + dtb_ref[...]
        dt = jnp.maximum(t, 0.0) + jnp.log(1.0 + jnp.exp(-jnp.abs(t)))
        a = -jnp.exp(alog_ref[...])
        da = dt * a
        ri = lax.broadcasted_iota(jnp.int32, (L, L), 0)
        ci = lax.broadcasted_iota(jnp.int32, (L, L), 1)
        tri = jnp.where(ri >= ci, 1.0, 0.0).astype(BF16)
        hi, mid, lo = _split3(da)
        a_cs = _dot(tri, hi) + _dot(tri, mid) + _dot(tri, lo)
        a2 = a_cs * LOG2E
        a2t = a2.T
        acs_scr[...] = a2
        acst_scr[...] = a2t
        lbt_scr[...] = a2t - jnp.log(dt.T) * LOG2E

    for gg in range(GPS):
        xs = slice(gg * GROUP_CH, (gg + 1) * GROUP_CH)
        ns = slice(gg * SSM_STATE, (gg + 1) * SSM_STATE)
        _ssd_group(gp * GPS + gg,
                   x_ref.at[:, xs], b_ref.at[:, ns], c_ref.at[:, ns], z_ref.at[:, xs],
                   cwx_ref.at[:, xs], cwb_ref.at[:, ns], cwc_ref.at[:, ns],
                   cbx_ref.at[:, xs], cbb_ref.at[:, ns], cbc_ref.at[:, ns],
                   dsk_ref.at[:, xs], nw_ref.at[:, xs], o_ref.at[:, xs],
                   tailx, tailb, tailc, padx.at[gg], padb.at[gg], padc.at[gg],
                   acs_scr, acst_scr, lbt_scr, ht_scr)


def _ssd_group(g, x_ref, b_ref, c_ref, z_ref, cwx_ref, cwb_ref, cwc_ref, cbx_ref, cbb_ref, cbc_ref,
               dsk_ref, nw_ref, o_ref, tailx, tailb, tailc, padx, padb, padc,
               acs_scr, acst_scr, lbt_scr, ht_scr):
    L = SSM_CHUNK
    QW = 4 * SSM_HEAD_DIM
    xa = _conv_silu(x_ref, tailx, padx, cwx_ref, cbx_ref, g, SSM_CONV)
    ba = _conv_silu(b_ref, tailb, padb, cwb_ref, cbb_ref, g, SSM_CONV)
    ca = _conv_silu(c_ref, tailc, padc, cwc_ref, cbc_ref, g, SSM_CONV)

    xb = xa.astype(BF16)
    bt = ba.T
    cb = _dot(ca.astype(BF16), bt.astype(BF16))
    a_cs = acs_scr[...]
    ri = lax.broadcasted_iota(jnp.int32, (L, L), 0)
    ci = lax.broadcasted_iota(jnp.int32, (L, L), 1)
    causal = ri >= ci
    lane_h = lax.broadcasted_iota(jnp.int32, (L, LANES), 1)
    lane_q = lax.broadcasted_iota(jnp.int32, (1, QW), 1)

    ys = []
    for q in range(GROUP_CH // QW):
        xq = xb[:, q * QW:(q + 1) * QW]
        hq = ht_scr[g, :, q * QW:(q + 1) * QW]
        hqb = hq.astype(BF16)
        acc = jnp.zeros((L, QW), F32)
        st = jnp.zeros((SSM_STATE, QW), F32)
        cdq = jnp.zeros((1, QW), F32)
        for j in range(QW // SSM_HEAD_DIM):
            h = g * SSM_HEADS_PER_GROUP + q * (QW // SSM_HEAD_DIM) + j
            a_row = acst_scr[pl.ds(h, 1), :]
            b_row = lbt_scr[pl.ds(h, 1), :]
            a_col = jnp.sum(jnp.where(lane_h == h, a_cs, 0.0), axis=1, keepdims=True)
            dmat = jnp.where(causal, a_col - b_row, -jnp.inf)
            mh = (cb * jnp.exp2(dmat)).astype(BF16)
            lm = (lane_q >= j * SSM_HEAD_DIM) & (lane_q < (j + 1) * SSM_HEAD_DIM)
            xm = jnp.where(lm, xq, jnp.zeros_like(xq))
            hm = jnp.where(lm, hqb, jnp.zeros_like(hqb))
            ce = (ca * jnp.exp2(a_col)).astype(BF16)
            acc = acc + _dot(mh, xm) + _dot(ce, hm)
            a_last = a_row[:, L - 1:L]
            w_row = jnp.exp2(a_last - b_row)
            st = st + _dot((bt * w_row).astype(BF16), xm)
            cdq = jnp.where(lm, jnp.exp2(a_last), cdq)
        ht_scr[g, :, q * QW:(q + 1) * QW] = hq * cdq + st
        ys.append(acc + xa[:, q * QW:(q + 1) * QW] * dsk_ref[:, q * QW:(q + 1) * QW])

    z = z_ref[...].astype(F32)
    sz = _silu(z)
    ygs = [ys[q] * sz[:, q * QW:(q + 1) * QW] for q in range(len(ys))]
    ss = sum(jnp.sum(v * v, axis=1, keepdims=True) for v in ygs)
    scale = lax.rsqrt(ss * (1.0 / GROUP_CH) + EPS)
    for q in range(len(ys)):
        o_ref[:, q * QW:(q + 1) * QW] = (
            ygs[q] * scale * nw_ref[:, q * QW:(q + 1) * QW]).astype(o_ref.dtype)


def _ssd(proj, dt_raw, conv_w, conv_b, dt_bias, a_log, d_exp, norm_w):
    S = proj.shape[0]
    L, G, N = SSM_CHUNK, SSM_GROUPS, SSM_STATE
    GPS = SSD_GROUPS_PER_STEP
    XW, NW = GPS * GROUP_CH, GPS * N
    bcol = D_INNER // NW
    ccol = bcol + G // GPS
    pxcol = COL_XBC // XW
    pbcol = COL_XBC // NW + bcol
    pccol = pbcol + G // GPS
    zcol = COL_Z // XW
    return pl.pallas_call(
        _ssd_kernel,
        grid=(S // L, G // GPS),
        in_specs=[
            pl.BlockSpec((L, XW), lambda c, g: (c, pxcol + g)),
            pl.BlockSpec((L, NW), lambda c, g: (c, pbcol + g)),
            pl.BlockSpec((L, NW), lambda c, g: (c, pccol + g)),
            pl.BlockSpec((L, XW), lambda c, g: (c, zcol + g)),
            pl.BlockSpec((L, LANES), lambda c, g: (c, 0)),
            pl.BlockSpec((SSM_CONV, XW), lambda c, g: (0, g)),
            pl.BlockSpec((SSM_CONV, NW), lambda c, g: (0, bcol + g)),
            pl.BlockSpec((SSM_CONV, NW), lambda c, g: (0, ccol + g)),
            pl.BlockSpec((1, XW), lambda c, g: (0, g)),
            pl.BlockSpec((1, NW), lambda c, g: (0, bcol + g)),
            pl.BlockSpec((1, NW), lambda c, g: (0, ccol + g)),
            pl.BlockSpec((1, LANES), lambda c, g: (0, 0)),
            pl.BlockSpec((1, LANES), lambda c, g: (0, 0)),
            pl.BlockSpec((1, XW), lambda c, g: (0, g)),
            pl.BlockSpec((1, XW), lambda c, g: (0, g)),
        ],
        out_specs=pl.BlockSpec((L, XW), lambda c, g: (c, g)),
        out_shape=jax.ShapeDtypeStruct((S, D_INNER), BF16),
        scratch_shapes=[
            pltpu.VMEM((G, SUBLANES, GROUP_CH), F32),
            pltpu.VMEM((G, SUBLANES, N), F32),
            pltpu.VMEM((G, SUBLANES, N), F32),
            pltpu.VMEM((GPS, L + SUBLANES, GROUP_CH), F32),
            pltpu.VMEM((GPS, L + SUBLANES, N), F32),
            pltpu.VMEM((GPS, L + SUBLANES, N), F32),
            pltpu.VMEM((L, LANES), F32),
            pltpu.VMEM((LANES, L), F32),
            pltpu.VMEM((LANES, L), F32),
            pltpu.VMEM((G, N, GROUP_CH), F32),
        ],
        compiler_params=_cparams(("arbitrary", "arbitrary")),
        name="ssd",
    )(proj, proj, proj, proj, dt_raw, conv_w, conv_w, conv_w, conv_b, conv_b, conv_b,
      dt_bias, a_log, d_exp, norm_w)


def _qk_prep_kernel(q_ref, k_ref, v_ref, qw_ref, kw_ref, qt_ref, kn_ref, vt_ref, pen_ref, kbar_scr):
    S = q_ref.shape[0]
    BS = MOBA_BLOCK
    nb = S // BS
    nbp = kbar_scr.shape[0]
    kbar_scr[...] = jnp.zeros(kbar_scr.shape, F32)

    def kloop(b, carry):
        rows = pl.ds(pl.multiple_of(b * BS, BS), BS)
        k = k_ref[rows, :].astype(F32)
        kn = k * lax.rsqrt(jnp.mean(k * k, axis=-1, keepdims=True) + EPS) * kw_ref[...]
        kn_ref[0, rows, :] = kn.astype(BF16)
        kbar_scr[pl.ds(b, 1), :] = jnp.mean(kn, axis=0, keepdims=True)
        vt_ref[0, b, 0:ATTN_HEAD_DIM, :] = v_ref[rows, :].astype(F32).T.astype(BF16)
        vt_ref[0, b, ATTN_HEAD_DIM:, :] = jnp.ones((MOBA_VT_ROWS - ATTN_HEAD_DIM, BS), BF16)
        return carry

    lax.fori_loop(0, nb, kloop, 0, unroll=4)
    kb_hi, kb_mid, _ = _split3(kbar_scr[...])

    def qloop(i, carry):
        rows = pl.ds(pl.multiple_of(i * BS, BS), BS)
        q = q_ref[rows, :].astype(F32)
        qn = q * lax.rsqrt(jnp.mean(q * q, axis=-1, keepdims=True) + EPS) * qw_ref[...]
        qnt = (qn * (ATTN_HEAD_DIM ** -0.5)).T
        qt_ref[0, i] = (qnt * LOG2E).astype(BF16)
        q_hi, q_mid, _ = _split3(qnt)
        gate = _dot(kb_hi, q_hi) + _dot(kb_mid, q_hi) + _dot(kb_hi, q_mid)
        blk = lax.broadcasted_iota(jnp.int32, gate.shape, 0)
        blk_f = blk.astype(F32)
        gate = jnp.where(blk < i, gate, -jnp.inf)
        pen = jnp.full(gate.shape, -jnp.inf, F32)
        for _ in range(MOBA_TOPK):
            mx = jnp.max(gate, axis=0, keepdims=True)
            cand = jnp.where((gate == mx) & (mx > -jnp.inf), blk_f, float(nbp))
            idx = jnp.min(cand, axis=0, keepdims=True)
            hit = blk_f == idx
            pen = jnp.where(hit, 0.0, pen)
            gate = jnp.where(hit, -jnp.inf, gate)
        pen_ref[0, i] = pen
        return carry

    lax.fori_loop(0, nb, qloop, 0, unroll=4)


def _qk_prep(proj, q_norm_w, k_norm_w):
    S = proj.shape[0]
    H, D, BS = ATTN_HEADS, ATTN_HEAD_DIM, MOBA_BLOCK
    nb = S // BS
    nbp = -(-nb // SUBLANES) * SUBLANES
    return pl.pallas_call(
        _qk_prep_kernel,
        grid=(H,),
        in_specs=[
            pl.BlockSpec((S, D), lambda h: (0, COL_Q // D + h)),
            pl.BlockSpec((S, D), lambda h: (0, COL_K // D + h)),
            pl.BlockSpec((S, D), lambda h: (0, COL_V // D + h)),
            pl.BlockSpec((1, D), lambda h: (0, 0)),
            pl.BlockSpec((1, D), lambda h: (0, 0)),
        ],
        out_specs=[
            pl.BlockSpec((1, nb, D, BS), lambda h: (h, 0, 0, 0)),
            pl.BlockSpec((1, S, D), lambda h: (h, 0, 0)),
            pl.BlockSpec((1, nb, MOBA_VT_ROWS, BS), lambda h: (h, 0, 0, 0)),
            pl.BlockSpec((1, nb, nbp, BS), lambda h: (h, 0, 0, 0)),
        ],
        out_shape=[
            jax.ShapeDtypeStruct((H, nb, D, BS), BF16),
            jax.ShapeDtypeStruct((H, S, D), BF16),
            jax.ShapeDtypeStruct((H, nb, MOBA_VT_ROWS, BS), BF16),
            jax.ShapeDtypeStruct((H, nb, nbp, BS), F32),
        ],
        scratch_shapes=[pltpu.VMEM((nbp, D), F32)],
        compiler_params=_cparams(("arbitrary",)),
        name="qk_prep",
    )(proj, proj, proj, q_norm_w, k_norm_w)


def _rel_bucket(dist):
    n = jnp.maximum(dist, 0)
    max_exact = REL_BUCKETS // 2
    nf = jnp.maximum(n, 1).astype(F32)
    large = max_exact + (jnp.log(nf / max_exact) / math.log(REL_MAX_DIST / max_exact)
                         * (REL_BUCKETS - max_exact)).astype(jnp.int32)
    large = jnp.minimum(large, REL_BUCKETS - 1)
    return jnp.where(n < max_exact, n, large)


MOBA_HEADS_PER_STEP = 4
MOBA_VT_ROWS = ATTN_HEAD_DIM + 2 * SUBLANES


def _moba_kernel(tbl_ref, qt_ref, k_ref, vt_ref, pen_ref, o_ref,
                 bown_scr, bprev_scr, m_scr, acc_scr, s0_scr, s1_scr,
                 p0_scr, p1_scr, al0_scr, al1_scr):
    hp = pl.program_id(0)
    i = pl.program_id(1)
    BS, HB = MOBA_BLOCK, MOBA_HEADS_PER_STEP
    D = ATTN_HEAD_DIM

    @pl.when(i == 0)
    def _():
        kl = lax.broadcasted_iota(jnp.int32, (BS, BS), 0)
        ql = lax.broadcasted_iota(jnp.int32, (BS, BS), 1)
        d = ql - kl
        b_own = _rel_bucket(d)
        b_prev = _rel_bucket(d + BS)
        for a in range(HB):
            h = hp * HB + a
            far = tbl_ref[h, REL_BUCKETS - 1]

            def lookup(bucket):
                val = jnp.zeros((BS, BS), F32)
                for b in range(REL_BUCKETS):
                    val = jnp.where(bucket == b, tbl_ref[h, b], val)
                return (val - far) * LOG2E

            bown_scr[a] = jnp.where(d >= 0, lookup(b_own), -jnp.inf)
            bprev_scr[a] = lookup(b_prev)

    def krows(a, j):
        return k_ref[a, pl.ds(pl.multiple_of(j * BS, BS), BS), :]

    jp = jnp.maximum(i - 1, 0)
    no_prev = jnp.where(i == 0, -jnp.inf, 0.0)
    s_near = []
    for a in range(HB):
        qt = qt_ref[a, 0]
        s_near.append((_dot(krows(a, i), qt), _dot(krows(a, jp), qt)))
    p_near = []
    for a in range(HB):
        s_own = s_near[a][0] + bown_scr[a]
        s_prev = s_near[a][1] + bprev_scr[a] + (pen_ref[a, 0, pl.ds(jp, 1), :] + no_prev)
        m0 = jnp.maximum(jnp.max(s_own, axis=0, keepdims=True), jnp.max(s_prev, axis=0, keepdims=True))
        m_scr[a] = m0
        p_near.append((jnp.exp2(s_own - m0).astype(BF16), jnp.exp2(s_prev - m0).astype(BF16)))
    for a in range(HB):
        acc_scr[a] = (_dot(vt_ref[a, i], p_near[a][0])
                      + _dot(vt_ref[a, jp], p_near[a][1]))

    n_far = jnp.maximum(i - 1, 0)
    nb = k_ref.shape[1] // BS

    def clampj(j):
        return jnp.clip(j, 0, nb - 1)

    def scores(j, s_slot):
        for a in range(HB):
            s_slot[a] = _dot(krows(a, clampj(j)), qt_ref[a, 0])

    def softmax(j, s_slot, p_slot, al_slot, mask_tail):
        for a in range(HB):
            pr = pen_ref[a, 0, pl.ds(clampj(j), 1), :]
            if mask_tail:
                pr = pr + jnp.where(j >= n_far, -jnp.inf, 0.0)
            s = s_slot[a]
            m_old = m_scr[a]
            m_new = jnp.maximum(m_old, jnp.max(s, axis=0, keepdims=True) + pr)
            p_slot[a] = jnp.exp2(s - (m_new - pr)).astype(BF16)
            al_slot[a] = jnp.exp2(m_old - m_new)
            m_scr[a] = m_new

    def accumulate(j, p_slot, al_slot):
        for a in range(HB):
            acc_scr[a] = al_slot[a] * acc_scr[a] + _dot(vt_ref[a, clampj(j)], p_slot[a])

    scores(0, s0_scr)
    p1_scr[...] = jnp.zeros(p1_scr.shape, BF16)
    al1_scr[...] = jnp.ones(al1_scr.shape, F32)

    def far_loop(g, carry):
        j = 2 * g
        scores(j + 1, s1_scr)
        accumulate(j - 1, p1_scr, al1_scr)
        softmax(j, s0_scr, p0_scr, al0_scr, False)
        scores(j + 2, s0_scr)
        accumulate(j, p0_scr, al0_scr)
        softmax(j + 1, s1_scr, p1_scr, al1_scr, True)
        return carry

    n_pairs = (n_far + 1) // 2
    lax.fori_loop(0, n_pairs, far_loop, 0)
    accumulate(2 * n_pairs - 1, p1_scr, al1_scr)
    for a in range(HB):
        acc = acc_scr[a]
        o_ref[:, a * D:(a + 1) * D] = (acc[0:D, :] / acc[D:D + 1, :]).T.astype(o_ref.dtype)


def _moba(qt, kn, vt, pen, table_h):
    H, nb, D, BS = qt.shape
    S = nb * BS
    nbp = pen.shape[2]
    HB = MOBA_HEADS_PER_STEP
    return pl.pallas_call(
        _moba_kernel,
        grid=(H // HB, nb),
        in_specs=[
            pl.BlockSpec(memory_space=pltpu.SMEM),
            pl.BlockSpec((HB, 1, D, BS), lambda hp, i: (hp, i, 0, 0)),
            pl.BlockSpec((HB, S, D), lambda hp, i: (hp, 0, 0)),
            pl.BlockSpec((HB, nb, MOBA_VT_ROWS, BS), lambda hp, i: (hp, 0, 0, 0)),
            pl.BlockSpec((HB, 1, nbp, BS), lambda hp, i: (hp, i, 0, 0)),
        ],
        out_specs=pl.BlockSpec((BS, HB * D), lambda hp, i: (i, hp)),
        out_shape=jax.ShapeDtypeStruct((S, H * D), BF16),
        scratch_shapes=[
            pltpu.VMEM((HB, BS, BS), F32),
            pltpu.VMEM((HB, BS, BS), F32),
            pltpu.VMEM((HB, 1, BS), F32),
            pltpu.VMEM((HB, MOBA_VT_ROWS, BS), F32),
            pltpu.VMEM((HB, BS, BS), F32),
            pltpu.VMEM((HB, BS, BS), F32),
            pltpu.VMEM((HB, BS, BS), BF16),
            pltpu.VMEM((HB, BS, BS), BF16),
            pltpu.VMEM((HB, 1, BS), F32),
            pltpu.VMEM((HB, 1, BS), F32),
        ],
        compiler_params=_cparams(("arbitrary", "arbitrary")),
        name="moba",
    )(table_h, qt, kn, vt, pen)


def _mix_kernel(ys_ref, ya_ref, ws_ref, wa_ref, gs_ref, ga_ref, bs_ref, ba_ref, o_ref):
    gs = _sigmoid(gs_ref[...].astype(F32) + bs_ref[...])
    ga = _sigmoid(ga_ref[...].astype(F32) + ba_ref[...])
    o_ref[...] = (gs * _dot(ys_ref[...], ws_ref[...])
                  + ga * _dot(ya_ref[...], wa_ref[...])).astype(o_ref.dtype)


def _mix(y_ssd, y_attn, w_ssm_out, w_attn_out, proj, b_gate, tm=1024, tn=512):
    S = y_ssd.shape[0]
    tm = min(tm, S)
    gs_col = COL_G // tn
    ga_col = (COL_G + D_MODEL) // tn
    nb = D_MODEL // tn
    return pl.pallas_call(
        _mix_kernel,
        grid=(S // tm, D_MODEL // tn),
        in_specs=[
            pl.BlockSpec((tm, D_INNER), lambda m, n: (m, 0)),
            pl.BlockSpec((tm, D_ATTN), lambda m, n: (m, 0)),
            pl.BlockSpec((D_INNER, tn), lambda m, n: (0, n)),
            pl.BlockSpec((D_ATTN, tn), lambda m, n: (0, n)),
            pl.BlockSpec((tm, tn), lambda m, n: (m, gs_col + n)),
            pl.BlockSpec((tm, tn), lambda m, n: (m, ga_col + n)),
            pl.BlockSpec((1, tn), lambda m, n: (0, n)),
            pl.BlockSpec((1, tn), lambda m, n: (0, nb + n)),
        ],
        out_specs=pl.BlockSpec((tm, tn), lambda m, n: (m, n)),
        out_shape=jax.ShapeDtypeStruct((S, D_MODEL), BF16),
        compiler_params=_cparams(("arbitrary", "arbitrary")),
        name="mix",
    )(y_ssd, y_attn, w_ssm_out, w_attn_out, proj, proj, b_gate, b_gate)


def _out_norm_kernel(x_ref, mix_ref, w_ref, nw_ref, x1_ref, h2_ref):
    x1 = x_ref[...] + _dot(mix_ref[...], w_ref[...])
    x1_ref[...] = x1
    ms = jnp.mean(x1 * x1, axis=-1, keepdims=True)
    h2_ref[...] = (x1 * lax.rsqrt(ms + EPS) * nw_ref[...]).astype(h2_ref.dtype)


def _out_norm(x, mix, w_out, ffn_norm_w, tm=512):
    S = x.shape[0]
    tm = min(tm, S)
    return pl.pallas_call(
        _out_norm_kernel,
        grid=(S // tm,),
        in_specs=[
            pl.BlockSpec((tm, D_MODEL), lambda m: (m, 0)),
            pl.BlockSpec((tm, D_MODEL), lambda m: (m, 0)),
            pl.BlockSpec((D_MODEL, D_MODEL), lambda m: (0, 0)),
            pl.BlockSpec((1, D_MODEL), lambda m: (0, 0)),
        ],
        out_specs=[
            pl.BlockSpec((tm, D_MODEL), lambda m: (m, 0)),
            pl.BlockSpec((tm, D_MODEL), lambda m: (m, 0)),
        ],
        out_shape=[
            jax.ShapeDtypeStruct((S, D_MODEL), F32),
            jax.ShapeDtypeStruct((S, D_MODEL), BF16),
        ],
        compiler_params=_cparams(("arbitrary",)),
        name="out_norm",
    )(x, mix, w_out, ffn_norm_w)


def _ffn_up_kernel(h_ref, wg32_ref, wu32_ref, cwg_ref, cwu_ref, cbg_ref, cbu_ref, o_ref,
                   tailg, tailu, padg, padu, wg_ref, wu_ref):
    tm = h_ref.shape[0]

    @pl.when(pl.program_id(1) == 0)
    def _():
        tailg[...] = jnp.zeros(tailg.shape, F32)
        tailu[...] = jnp.zeros(tailu.shape, F32)
        wg_ref[...] = wg32_ref[...].astype(BF16)
        wu_ref[...] = wu32_ref[...].astype(BF16)

    def conv(w_ref, cw_ref, cb_ref, tail, pad):
        pad[0:SUBLANES, :] = tail[...]
        pad[SUBLANES:SUBLANES + tm, :] = _dot(h_ref[...], w_ref[...])
        tail[...] = pad[tm:tm + SUBLANES, :]
        acc = cb_ref[...]
        for k in range(FFN_CONV):
            off = SUBLANES - (FFN_CONV - 1) + k
            acc = acc + cw_ref[k:k + 1, :] * pad[off:off + tm, :]
        return acc

    ug = conv(wg_ref, cwg_ref, cbg_ref, tailg, padg)
    uu = conv(wu_ref, cwu_ref, cbu_ref, tailu, padu)
    o_ref[...] = (_silu(ug) * uu).astype(o_ref.dtype)


def _ffn_up(h2, w_up, conv_w, conv_b, tm=1024, tn=512):
    S = h2.shape[0]
    tm = min(tm, S)
    nb = D_FF // tn
    return pl.pallas_call(
        _ffn_up_kernel,
        grid=(nb, S // tm),
        in_specs=[
            pl.BlockSpec((tm, D_MODEL), lambda n, m: (m, 0)),
            pl.BlockSpec((D_MODEL, tn), lambda n, m: (0, n)),
            pl.BlockSpec((D_MODEL, tn), lambda n, m: (0, nb + n)),
            pl.BlockSpec((FFN_CONV, tn), lambda n, m: (0, n)),
            pl.BlockSpec((FFN_CONV, tn), lambda n, m: (0, nb + n)),
            pl.BlockSpec((1, tn), lambda n, m: (0, n)),
            pl.BlockSpec((1, tn), lambda n, m: (0, nb + n)),
        ],
        out_specs=pl.BlockSpec((tm, tn), lambda n, m: (m, n)),
        out_shape=jax.ShapeDtypeStruct((S, D_FF), BF16),
        scratch_shapes=[
            pltpu.VMEM((SUBLANES, tn), F32),
            pltpu.VMEM((SUBLANES, tn), F32),
            pltpu.VMEM((tm + SUBLANES, tn), F32),
            pltpu.VMEM((tm + SUBLANES, tn), F32),
            pltpu.VMEM((D_MODEL, tn), BF16),
            pltpu.VMEM((D_MODEL, tn), BF16),
        ],
        compiler_params=_cparams(("arbitrary", "arbitrary")),
        name="ffn_up",
    )(h2, w_up, w_up, conv_w, conv_w, conv_b, conv_b)


def _ffn_down_kernel(a_ref, w_ref, x1_ref, o_ref):
    o_ref[...] = x1_ref[...] + _dot(a_ref[...], w_ref[...])


def _ffn_down(act, w_down, x1, tm=1024, tn=512):
    S = act.shape[0]
    tm = min(tm, S)
    return pl.pallas_call(
        _ffn_down_kernel,
        grid=(S // tm, D_MODEL // tn),
        in_specs=[
            pl.BlockSpec((tm, D_FF), lambda m, n: (m, 0)),
            pl.BlockSpec((D_FF, tn), lambda m, n: (0, n)),
            pl.BlockSpec((tm, tn), lambda m, n: (m, n)),
        ],
        out_specs=pl.BlockSpec((tm, tn), lambda m, n: (m, n)),
        out_shape=jax.ShapeDtypeStruct((S, D_MODEL), F32),
        compiler_params=_cparams(("arbitrary", "arbitrary")),
        name="ffn_down",
    )(act, w_down, x1)


def _pad_lanes(v):
    return jnp.pad(v, ((0, 0), (0, LANES - v.shape[1])))


def _layer(x, attn_norm_w, w_in, b_gate, ssm_conv_w, ssm_conv_b, ssm_dt_bias, ssm_a_log, ssm_d,
           ssm_norm_w, q_norm_w, k_norm_w, rel_bias, w_ssm_out, w_attn_out, w_out, ffn_norm_w,
           w_up, ffn_conv_w, ffn_conv_b, w_down):
    o_z, o_xbc = 0, D_INNER
    o_dt = o_xbc + D_XBC
    o_q = o_dt + SSM_HEADS
    o_g = o_q + 3 * D_ATTN
    wt = w_in.T
    wt_dt = jnp.pad(wt[o_dt:o_q], ((0, LANES - SSM_HEADS), (0, 0)))

    proj, dt_raw = _in_proj(x, attn_norm_w[None, :], wt.astype(BF16), wt_dt, o_dt)
    y_ssd = _ssd(proj, dt_raw, ssm_conv_w, ssm_conv_b[None, :], _pad_lanes(ssm_dt_bias[None, :]),
                 _pad_lanes(ssm_a_log[None, :]), jnp.repeat(ssm_d, SSM_HEAD_DIM)[None, :],
                 ssm_norm_w[None, :])
    qt, kn, vt, pen = _qk_prep(proj, q_norm_w[None, :], k_norm_w[None, :])
    y_attn = _moba(qt, kn, vt, pen, rel_bias.T)
    mix = _mix(y_ssd, y_attn, w_ssm_out.astype(BF16), w_attn_out.astype(BF16), proj, b_gate[None, :])
    x1, h2 = _out_norm(x, mix, w_out.astype(BF16), ffn_norm_w[None, :])
    act = _ffn_up(h2, w_up, ffn_conv_w, ffn_conv_b[None, :])
    return _ffn_down(act, w_down.astype(BF16), x1)


def kernel(x, attn_norm_w, w_in, b_gate, ssm_conv_w, ssm_conv_b, ssm_dt_bias, ssm_a_log, ssm_d,
           ssm_norm_w, q_norm_w, k_norm_w, rel_bias, w_ssm_out, w_attn_out, w_out, ffn_norm_w,
           w_up, ffn_conv_w, ffn_conv_b, w_down):
    assert x.shape[0] == 1 and attn_norm_w.shape[0] == 1
    out = _layer(x[0], attn_norm_w[0], w_in[0], b_gate[0], ssm_conv_w[0], ssm_conv_b[0],
                 ssm_dt_bias[0], ssm_a_log[0], ssm_d[0], ssm_norm_w[0], q_norm_w[0], k_norm_w[0],
                 rel_bias, w_ssm_out[0], w_attn_out[0], w_out[0], ffn_norm_w[0], w_up[0],
                 ffn_conv_w[0], ffn_conv_b[0], w_down[0])
    return out[None]
```

```python
import functools
import math

import jax
import jax.numpy as jnp
from jax import lax
from jax.experimental import pallas as pl
from jax.experimental.pallas import tpu as pltpu

F32 = jnp.float32
BF16 = jnp.bfloat16

D_MODEL = 2048
D_INNER = 4096
SSM_HEADS = 64
SSM_HEAD_DIM = 64
SSM_GROUPS = 8
SSM_HEADS_PER_GROUP = SSM_HEADS // SSM_GROUPS
SSM_STATE = 128
SSM_CONV = 4
SSM_CHUNK = 256
GROUP_CH = D_INNER // SSM_GROUPS
SSD_GROUPS_PER_STEP = 2
D_XBC = D_INNER + 2 * SSM_GROUPS * SSM_STATE
ATTN_HEADS = 16
ATTN_HEAD_DIM = 128
D_ATTN = ATTN_HEADS * ATTN_HEAD_DIM
MOBA_BLOCK = 256
MOBA_TOPK = 3
REL_BUCKETS = 32
REL_MAX_DIST = 128
D_FF = 5632
FFN_CONV = 3
EPS = 1e-6
LOG2E = 1.4426950408889634

LANES = 128
SUBLANES = 8
D_PROJ = D_XBC + 3 * D_ATTN + 2 * D_INNER
COL_Z = 0
COL_XBC = COL_Z + D_INNER
COL_Q = COL_XBC + D_XBC
COL_K = COL_Q + D_ATTN
COL_V = COL_K + D_ATTN
COL_G = COL_V + D_ATTN
VMEM_LIMIT = 56 * 1024 * 1024


def _cparams(sem):
    return pltpu.CompilerParams(dimension_semantics=sem, vmem_limit_bytes=VMEM_LIMIT)


def _split3(a):
    hi = a.astype(BF16)
    r = a - hi.astype(F32)
    mid = r.astype(BF16)
    lo = (r - mid.astype(F32)).astype(BF16)
    return hi, mid, lo


def _dot(a, b):
    return jnp.dot(a, b, preferred_element_type=F32)


def _dot_nt(a, b):
    return lax.dot_general(a, b, (((1,), (1,)), ((), ())), preferred_element_type=F32)


def _sigmoid(x):
    return 0.5 + 0.5 * jnp.tanh(0.5 * x)


def _silu(x):
    h = 0.5 * x
    return h + h * jnp.tanh(h)


def _in_proj_kernel(x_ref, nw_ref, w_ref, wdt_ref, o_ref, dt_ref, h_scr):
    @pl.when(pl.program_id(1) == 0)
    def _():
        x = x_ref[...]
        ms = jnp.mean(x * x, axis=-1, keepdims=True)
        h = x * lax.rsqrt(ms + EPS) * nw_ref[...]
        h_scr[...] = h.astype(BF16)
        h_hi, h_mid, _ = _split3(h)
        w_hi, w_mid, _ = _split3(wdt_ref[...])
        dt_ref[...] = _dot_nt(h_hi, w_hi) + _dot_nt(h_hi, w_mid) + _dot_nt(h_mid, w_hi)

    o_ref[...] = _dot_nt(h_scr[...], w_ref[...]).astype(o_ref.dtype)


def _in_proj(x, norm_w, wt, wt_dt, dt_row0, tm=1024, tn=1024):
    S = x.shape[0]
    tm = min(tm, S)
    assert dt_row0 % tn == 0 and SSM_HEADS % (2 * SUBLANES) == 0
    na = dt_row0 // tn

    def w_rows(m, n):
        return (pl.multiple_of(n * tn + jnp.where(n >= na, SSM_HEADS, 0), SSM_HEADS), 0)

    return pl.pallas_call(
        _in_proj_kernel,
        grid=(S // tm, D_PROJ // tn),
        in_specs=[
            pl.BlockSpec((tm, D_MODEL), lambda m, n: (m, 0)),
            pl.BlockSpec((1, D_MODEL), lambda m, n: (0, 0)),
            pl.BlockSpec((pl.Element(tn), pl.Element(D_MODEL)), w_rows),
            pl.BlockSpec((LANES, D_MODEL), lambda m, n: (0, 0)),
        ],
        out_specs=[
            pl.BlockSpec((tm, tn), lambda m, n: (m, n)),
            pl.BlockSpec((tm, LANES), lambda m, n: (m, 0)),
        ],
        out_shape=[
            jax.ShapeDtypeStruct((S, D_PROJ), BF16),
            jax.ShapeDtypeStruct((S, LANES), F32),
        ],
        scratch_shapes=[pltpu.VMEM((tm, D_MODEL), BF16)],
        compiler_params=_cparams(("arbitrary", "arbitrary")),
        name="in_proj",
    )(x, norm_w, wt, wt_dt)


def _conv_silu(raw_ref, tail_ref, pad_ref, w_ref, b_ref, g, taps):
    L = raw_ref.shape[0]
    pad_ref[0:SUBLANES, :] = tail_ref[g]
    pad_ref[SUBLANES:SUBLANES + L, :] = raw_ref[...].astype(F32)
    tail_ref[g] = pad_ref[L:L + SUBLANES, :]
    acc = b_ref[...]
    for k in range(taps):
        off = SUBLANES - (taps - 1) + k
        acc = acc + w_ref[k:k + 1, :] * pad_ref[off:off + L, :]
    return _silu(acc)


def _ssd_kernel(x_ref, b_ref, c_ref, z_ref, dtr_ref, cwx_ref, cwb_ref, cwc_ref,
                cbx_ref, cbb_ref, cbc_ref, dtb_ref, alog_ref, dsk_ref, nw_ref,
                o_ref,
                tailx, tailb, tailc, padx, padb, padc, acs_scr, acst_scr, lbt_scr, ht_scr):
    c = pl.program_id(0)
    gp = pl.program_id(1)
    L = SSM_CHUNK
    QW = 4 * SSM_HEAD_DIM
    GPS = SSD_GROUPS_PER_STEP

    @pl.when(c == 0)
    def _():
        for gg in range(GPS):
            g0 = gp * GPS + gg
            tailx[g0] = jnp.zeros(tailx.shape[1:], F32)
            tailb[g0] = jnp.zeros(tailb.shape[1:], F32)
            tailc[g0] = jnp.zeros(tailc.shape[1:], F32)
            ht_scr[g0] = jnp.zeros(ht_scr.shape[1:], F32)

    @pl.when(gp == 0)
    def _():
        t = dtr_ref[...] + dtb_ref[...]
        dt = jnp.maximum(t, 0.0) + jnp.log(1.0 + jnp.exp(-jnp.abs(t)))
        a = -jnp.exp(alog_ref[...])
        da = dt * a
        ri = lax.broadcasted_iota(jnp.int32, (L, L), 0)
        ci = lax.broadcasted_iota(jnp.int32, (L, L), 1)
        tri = jnp.where(ri >= ci, 1.0, 0.0).astype(BF16)
        hi, mid, lo = _split3(da)
        a_cs = _dot(tri, hi) + _dot(tri, mid) + _dot(tri, lo)
        a2 = a_cs * LOG2E
        a2t = a2.T
        acs_scr[...] = a2
        acst_scr[...] = a2t
        lbt_scr[...] = a2t - jnp.log(dt.T) * LOG2E

    for gg in range(GPS):
        xs = slice(gg * GROUP_CH, (gg + 1) * GROUP_CH)
        ns = slice(gg * SSM_STATE, (gg + 1) * SSM_STATE)
        _ssd_group(gp * GPS + gg,
                   x_ref.at[:, xs], b_ref.at[:, ns], c_ref.at[:, ns], z_ref.at[:, xs],
                   cwx_ref.at[:, xs], cwb_ref.at[:, ns], cwc_ref.at[:, ns],
                   cbx_ref.at[:, xs], cbb_ref.at[:, ns], cbc_ref.at[:, ns],
                   dsk_ref.at[:, xs], nw_ref.at[:, xs], o_ref.at[:, xs],
                   tailx, tailb, tailc, padx.at[gg], padb.at[gg], padc.at[gg],
                   acs_scr, acst_scr, lbt_scr, ht_scr)


def _ssd_group(g, x_ref, b_ref, c_ref, z_ref, cwx_ref, cwb_ref, cwc_ref, cbx_ref, cbb_ref, cbc_ref,
               dsk_ref, nw_ref, o_ref, tailx, tailb, tailc, padx, padb, padc,
               acs_scr, acst_scr, lbt_scr, ht_scr):
    L = SSM_CHUNK
    QW = 4 * SSM_HEAD_DIM
    xa = _conv_silu(x_ref, tailx, padx, cwx_ref, cbx_ref, g, SSM_CONV)
    ba = _conv_silu(b_ref, tailb, padb, cwb_ref, cbb_ref, g, SSM_CONV)
    ca = _conv_silu(c_ref, tailc, padc, cwc_ref, cbc_ref, g, SSM_CONV)

    xb = xa.astype(BF16)
    bt = ba.T
    cb = _dot(ca.astype(BF16), bt.astype(BF16))
    a_cs = acs_scr[...]
    ri = lax.broadcasted_iota(jnp.int32, (L, L), 0)
    ci = lax.broadcasted_iota(jnp.int32, (L, L), 1)
    causal = ri >= ci
    lane_h = lax.broadcasted_iota(jnp.int32, (L, LANES), 1)
    lane_q = lax.broadcasted_iota(jnp.int32, (1, QW), 1)

    ys = []
    for q in range(GROUP_CH // QW):
        xq = xb[:, q * QW:(q + 1) * QW]
        hq = ht_scr[g, :, q * QW:(q + 1) * QW]
        hqb = hq.astype(BF16)
        acc = jnp.zeros((L, QW), F32)
        st = jnp.zeros((SSM_STATE, QW), F32)
        cdq = jnp.zeros((1, QW), F32)
        for j in range(QW // SSM_HEAD_DIM):
            h = g * SSM_HEADS_PER_GROUP + q * (QW // SSM_HEAD_DIM) + j
            a_row = acst_scr[pl.ds(h, 1), :]
            b_row = lbt_scr[pl.ds(h, 1), :]
            a_col = jnp.sum(jnp.where(lane_h == h, a_cs, 0.0), axis=1, keepdims=True)
            dmat = jnp.where(causal, a_col - b_row, -jnp.inf)
            mh = (cb * jnp.exp2(dmat)).astype(BF16)
            lm = (lane_q >= j * SSM_HEAD_DIM) & (lane_q < (j + 1) * SSM_HEAD_DIM)
            xm = jnp.where(lm, xq, jnp.zeros_like(xq))
            hm = jnp.where(lm, hqb, jnp.zeros_like(hqb))
            ce = (ca * jnp.exp2(a_col)).astype(BF16)
            acc = acc + _dot(mh, xm) + _dot(ce, hm)
            a_last = a_row[:, L - 1:L]
            w_row = jnp.exp2(a_last - b_row)
            st = st + _dot((bt * w_row).astype(BF16), xm)
            cdq = jnp.where(lm, jnp.exp2(a_last), cdq)
        ht_scr[g, :, q * QW:(q + 1) * QW] = hq * cdq + st
        ys.append(acc + xa[:, q * QW:(q + 1) * QW] * dsk_ref[:, q * QW:(q + 1) * QW])

    z = z_ref[...].astype(F32)
    sz = _silu(z)
    ygs = [ys[q] * sz[:, q * QW:(q + 1) * QW] for q in range(len(ys))]
    ss = sum(jnp.sum(v * v, axis=1, keepdims=True) for v in ygs)
    scale = lax.rsqrt(ss * (1.0 / GROUP_CH) + EPS)
    for q in range(len(ys)):
        o_ref[:, q * QW:(q + 1) * QW] = (
            ygs[q] * scale * nw_ref[:, q * QW:(q + 1) * QW]).astype(o_ref.dtype)


def _ssd(proj, dt_raw, conv_w, conv_b, dt_bias, a_log, d_exp, norm_w):
    S = proj.shape[0]
    L, G, N = SSM_CHUNK, SSM_GROUPS, SSM_STATE
    GPS = SSD_GROUPS_PER_STEP
    XW, NW = GPS * GROUP_CH, GPS * N
    bcol = D_INNER // NW
    ccol = bcol + G // GPS
    pxcol = COL_XBC // XW
    pbcol = COL_XBC // NW + bcol
    pccol = pbcol + G // GPS
    zcol = COL_Z // XW
    return pl.pallas_call(
        _ssd_kernel,
        grid=(S // L, G // GPS),
        in_specs=[
            pl.BlockSpec((L, XW), lambda c, g: (c, pxcol + g)),
            pl.BlockSpec((L, NW), lambda c, g: (c, pbcol + g)),
            pl.BlockSpec((L, NW), lambda c, g: (c, pccol + g)),
            pl.BlockSpec((L, XW), lambda c, g: (c, zcol + g)),
            pl.BlockSpec((L, LANES), lambda c, g: (c, 0)),
            pl.BlockSpec((SSM_CONV, XW), lambda c, g: (0, g)),
            pl.BlockSpec((SSM_CONV, NW), lambda c, g: (0, bcol + g)),
            pl.BlockSpec((SSM_CONV, NW), lambda c, g: (0, ccol + g)),
            pl.BlockSpec((1, XW), lambda c, g: (0, g)),
            pl.BlockSpec((1, NW), lambda c, g: (0, bcol + g)),
            pl.BlockSpec((1, NW), lambda c, g: (0, ccol + g)),
            pl.BlockSpec((1, LANES), lambda c, g: (0, 0)),
            pl.BlockSpec((1, LANES), lambda c, g: (0, 0)),
            pl.BlockSpec((1, XW), lambda c, g: (0, g)),
            pl.BlockSpec((1, XW), lambda c, g: (0, g)),
        ],
        out_specs=pl.BlockSpec((L, XW), lambda c, g: (c, g)),
        out_shape=jax.ShapeDtypeStruct((S, D_INNER), BF16),
        scratch_shapes=[
            pltpu.VMEM((G, SUBLANES, GROUP_CH), F32),
            pltpu.VMEM((G, SUBLANES, N), F32),
            pltpu.VMEM((G, SUBLANES, N), F32),
            pltpu.VMEM((GPS, L + SUBLANES, GROUP_CH), F32),
            pltpu.VMEM((GPS, L + SUBLANES, N), F32),
            pltpu.VMEM((GPS, L + SUBLANES, N), F32),
            pltpu.VMEM((L, LANES), F32),
            pltpu.VMEM((LANES, L), F32),
            pltpu.VMEM((LANES, L), F32),
            pltpu.VMEM((G, N, GROUP_CH), F32),
        ],
        compiler_params=_cparams(("arbitrary", "arbitrary")),
        name="ssd",
    )(proj, proj, proj, proj, dt_raw, conv_w, conv_w, conv_w, conv_b, conv_b, conv_b,
      dt_bias, a_log, d_exp, norm_w)


def _qk_prep_kernel(q_ref, k_ref, v_ref, qw_ref, kw_ref, qt_ref, kn_ref, vt_ref, pen_ref, kbar_scr):
    S = q_ref.shape[0]
    BS = MOBA_BLOCK
    nb = S // BS
    nbp = kbar_scr.shape[0]
    kbar_scr[...] = jnp.zeros(kbar_scr.shape, F32)

    def kloop(b, carry):
        rows = pl.ds(pl.multiple_of(b * BS, BS), BS)
        k = k_ref[rows, :].astype(F32)
        kn = k * lax.rsqrt(jnp.mean(k * k, axis=-1, keepdims=True) + EPS) * kw_ref[...]
        kn_ref[0, rows, :] = kn.astype(BF16)
        kbar_scr[pl.ds(b, 1), :] = jnp.mean(kn, axis=0, keepdims=True)
        vt_ref[0, b, 0:ATTN_HEAD_DIM, :] = v_ref[rows, :].astype(F32).T.astype(BF16)
        vt_ref[0, b, ATTN_HEAD_DIM:, :] = jnp.ones((MOBA_VT_ROWS - ATTN_HEAD_DIM, BS), BF16)
        return carry

    lax.fori_loop(0, nb, kloop, 0, unroll=4)
    kb_hi, kb_mid, _ = _split3(kbar_scr[...])

    def qloop(i, carry):
        rows = pl.ds(pl.multiple_of(i * BS, BS), BS)
        q = q_ref[rows, :].astype(F32)
        qn = q * lax.rsqrt(jnp.mean(q * q, axis=-1, keepdims=True) + EPS) * qw_ref[...]
        qnt = (qn * (ATTN_HEAD_DIM ** -0.5)).T
        qt_ref[0, i] = (qnt * LOG2E).astype(BF16)
        q_hi, q_mid, _ = _split3(qnt)
        gate = _dot(kb_hi, q_hi) + _dot(kb_mid, q_hi) + _dot(kb_hi, q_mid)
        blk = lax.broadcasted_iota(jnp.int32, gate.shape, 0)
        blk_f = blk.astype(F32)
        gate = jnp.where(blk < i, gate, -jnp.inf)
        pen = jnp.full(gate.shape, -jnp.inf, F32)
        for _ in range(MOBA_TOPK):
            mx = jnp.max(gate, axis=0, keepdims=True)
            cand = jnp.where((gate == mx) & (mx > -jnp.inf), blk_f, float(nbp))
            idx = jnp.min(cand, axis=0, keepdims=True)
            hit = blk_f == idx
            pen = jnp.where(hit, 0.0, pen)
            gate = jnp.where(hit, -jnp.inf, gate)
        pen_ref[0, i] = pen
        return carry

    lax.fori_loop(0, nb, qloop, 0, unroll=4)


def _qk_prep(proj, q_norm_w, k_norm_w):
    S = proj.shape[0]
    H, D, BS = ATTN_HEADS, ATTN_HEAD_DIM, MOBA_BLOCK
    nb = S // BS
    nbp = -(-nb // SUBLANES) * SUBLANES
    return pl.pallas_call(
        _qk_prep_kernel,
        grid=(H,),
        in_specs=[
            pl.BlockSpec((S, D), lambda h: (0, COL_Q // D + h)),
            pl.BlockSpec((S, D), lambda h: (0, COL_K // D + h)),
            pl.BlockSpec((S, D), lambda h: (0, COL_V // D + h)),
            pl.BlockSpec((1, D), lambda h: (0, 0)),
            pl.BlockSpec((1, D), lambda h: (0, 0)),
        ],
        out_specs=[
            pl.BlockSpec((1, nb, D, BS), lambda h: (h, 0, 0, 0)),
            pl.BlockSpec((1, S, D), lambda h: (h, 0, 0)),
            pl.BlockSpec((1, nb, MOBA_VT_ROWS, BS), lambda h: (h, 0, 0, 0)),
            pl.BlockSpec((1, nb, nbp, BS), lambda h: (h, 0, 0, 0)),
        ],
        out_shape=[
            jax.ShapeDtypeStruct((H, nb, D, BS), BF16),
            jax.ShapeDtypeStruct((H, S, D), BF16),
            jax.ShapeDtypeStruct((H, nb, MOBA_VT_ROWS, BS), BF16),
            jax.ShapeDtypeStruct((H, nb, nbp, BS), F32),
        ],
        scratch_shapes=[pltpu.VMEM((nbp, D), F32)],
        compiler_params=_cparams(("arbitrary",)),
        name="qk_prep",
    )(proj, proj, proj, q_norm_w, k_norm_w)


def _rel_bucket(dist):
    n = jnp.maximum(dist, 0)
    max_exact = REL_BUCKETS // 2
    nf = jnp.maximum(n, 1).astype(F32)
    large = max_exact + (jnp.log(nf / max_exact) / math.log(REL_MAX_DIST / max_exact)
                         * (REL_BUCKETS - max_exact)).astype(jnp.int32)
    large = jnp.minimum(large, REL_BUCKETS - 1)
    return jnp.where(n < max_exact, n, large)


MOBA_HEADS_PER_STEP = 4
MOBA_VT_ROWS = ATTN_HEAD_DIM + 2 * SUBLANES


def _moba_kernel(tbl_ref, qt_ref, k_ref, vt_ref, pen_ref, o_ref,
                 bown_scr, bprev_scr, m_scr, acc_scr, s0_scr, s1_scr, s2_scr, s3_scr,
                 p0_scr, p1_scr, p2_scr, p3_scr, al0_scr, al1_scr, al2_scr, al3_scr):
    hp = pl.program_id(0)
    i = pl.program_id(1)
    BS, HB = MOBA_BLOCK, MOBA_HEADS_PER_STEP
    D = ATTN_HEAD_DIM

    @pl.when(i == 0)
    def _():
        kl = lax.broadcasted_iota(jnp.int32, (BS, BS), 0)
        ql = lax.broadcasted_iota(jnp.int32, (BS, BS), 1)
        d = ql - kl
        b_own = _rel_bucket(d)
        b_prev = _rel_bucket(d + BS)
        for a in range(HB):
            h = hp * HB + a
            far = tbl_ref[h, REL_BUCKETS - 1]

            def lookup(bucket):
                val = jnp.zeros((BS, BS), F32)
                for b in range(REL_BUCKETS):
                    val = jnp.where(bucket == b, tbl_ref[h, b], val)
                return (val - far) * LOG2E

            bown_scr[a] = jnp.where(d >= 0, lookup(b_own), -jnp.inf)
            bprev_scr[a] = lookup(b_prev)

    def krows(a, j):
        return k_ref[a, pl.ds(pl.multiple_of(j * BS, BS), BS), :]

    jp = jnp.maximum(i - 1, 0)
    no_prev = jnp.where(i == 0, -jnp.inf, 0.0)
    s_near = []
    for a in range(HB):
        qt = qt_ref[a, 0]
        s_near.append((_dot(krows(a, i), qt), _dot(krows(a, jp), qt)))
    p_near = []
    for a in range(HB):
        s_own = s_near[a][0] + bown_scr[a]
        s_prev = s_near[a][1] + bprev_scr[a] + (pen_ref[a, 0, pl.ds(jp, 1), :] + no_prev)
        m0 = jnp.maximum(jnp.max(s_own, axis=0, keepdims=True), jnp.max(s_prev, axis=0, keepdims=True))
        m_scr[a] = m0
        p_near.append((jnp.exp2(s_own - m0).astype(BF16), jnp.exp2(s_prev - m0).astype(BF16)))
    for a in range(HB):
        acc_scr[a] = (_dot(vt_ref[a, i], p_near[a][0])
                      + _dot(vt_ref[a, jp], p_near[a][1]))

    n_far = jnp.maximum(i - 1, 0)
    nb = k_ref.shape[1] // BS

    def clampj(j):
        return jnp.clip(j, 0, nb - 1)

    def scores(j, s_slot):
        for a in range(HB):
            s_slot[a] = _dot(krows(a, clampj(j)), qt_ref[a, 0])

    def softmax(j, s_slot, p_slot, al_slot, mask_tail):
        for a in range(HB):
            pr = pen_ref[a, 0, pl.ds(clampj(j), 1), :]
            if mask_tail:
                pr = pr + jnp.where(j >= n_far, -jnp.inf, 0.0)
            s = s_slot[a]
            m_old = m_scr[a]
            m_new = jnp.maximum(m_old, jnp.max(s, axis=0, keepdims=True) + pr)
            p_slot[a] = jnp.exp2(s - (m_new - pr)).astype(BF16)
            al_slot[a] = jnp.exp2(m_old - m_new)
            m_scr[a] = m_new

    def accumulate(j, p_slot, al_slot):
        for a in range(HB):
            acc_scr[a] = al_slot[a] * acc_scr[a] + _dot(vt_ref[a, clampj(j)], p_slot[a])

    s_slots = (s0_scr, s1_scr, s2_scr, s3_scr)
    p_slots = (p0_scr, p1_scr, p2_scr, p3_scr)
    al_slots = (al0_scr, al1_scr, al2_scr, al3_scr)
    scores(0, s0_scr)
    p3_scr[...] = jnp.zeros(p3_scr.shape, BF16)
    al3_scr[...] = jnp.ones(al3_scr.shape, F32)

    def quad_loop(g, carry):
        j = 4 * g
        for t in range(4):
            scores(j + t + 1, s_slots[(t + 1) % 4])
            accumulate(j + t - 1, p_slots[(t - 1) % 4], al_slots[(t - 1) % 4])
            softmax(j + t, s_slots[t], p_slots[t], al_slots[t], False)
        return carry

    n_quads = n_far // 4
    lax.fori_loop(0, n_quads, quad_loop, 0)

    def pair_loop(g, carry):
        j = 4 * n_quads + 2 * g
        scores(j + 1, s1_scr)
        accumulate(j - 1, p3_scr, al3_scr)
        softmax(j, s0_scr, p2_scr, al2_scr, False)
        scores(j + 2, s0_scr)
        accumulate(j, p2_scr, al2_scr)
        softmax(j + 1, s1_scr, p3_scr, al3_scr, True)
        return carry

    n_pairs = (n_far - 4 * n_quads + 1) // 2
    lax.fori_loop(0, n_pairs, pair_loop, 0)
    accumulate(4 * n_quads + 2 * n_pairs - 1, p3_scr, al3_scr)
    for a in range(HB):
        acc = acc_scr[a]
        o_ref[:, a * D:(a + 1) * D] = (acc[0:D, :] / acc[D:D + 1, :]).T.astype(o_ref.dtype)


def _moba(qt, kn, vt, pen, table_h):
    H, nb, D, BS = qt.shape
    S = nb * BS
    nbp = pen.shape[2]
    HB = MOBA_HEADS_PER_STEP
    return pl.pallas_call(
        _moba_kernel,
        grid=(H // HB, nb),
        in_specs=[
            pl.BlockSpec(memory_space=pltpu.SMEM),
            pl.BlockSpec((HB, 1, D, BS), lambda hp, i: (hp, i, 0, 0)),
            pl.BlockSpec((HB, S, D), lambda hp, i: (hp, 0, 0)),
            pl.BlockSpec((HB, nb, MOBA_VT_ROWS, BS), lambda hp, i: (hp, 0, 0, 0)),
            pl.BlockSpec((HB, 1, nbp, BS), lambda hp, i: (hp, i, 0, 0)),
        ],
        out_specs=pl.BlockSpec((BS, HB * D), lambda hp, i: (i, hp)),
        out_shape=jax.ShapeDtypeStruct((S, H * D), BF16),
        scratch_shapes=[
            pltpu.VMEM((HB, BS, BS), F32),
            pltpu.VMEM((HB, BS, BS), F32),
            pltpu.VMEM((HB, 1, BS), F32),
            pltpu.VMEM((HB, MOBA_VT_ROWS, BS), F32),
        ] + [pltpu.VMEM((HB, BS, BS), F32)] * 4 + [pltpu.VMEM((HB, BS, BS), BF16)] * 4
        + [pltpu.VMEM((HB, 1, BS), F32)] * 4,
        compiler_params=_cparams(("arbitrary", "arbitrary")),
        name="moba",
    )(table_h, qt, kn, vt, pen)


def _mix_kernel(ys_ref, ya_ref, ws_ref, wa_ref, gs_ref, ga_ref, bs_ref, ba_ref, o_ref):
    gs = _sigmoid(gs_ref[...].astype(F32) + bs_ref[...])
    ga = _sigmoid(ga_ref[...].astype(F32) + ba_ref[...])
    o_ref[...] = (gs * _dot(ys_ref[...], ws_ref[...])
                  + ga * _dot(ya_ref[...], wa_ref[...])).astype(o_ref.dtype)


def _mix(y_ssd, y_attn, w_ssm_out, w_attn_out, proj, b_gate, tm=1024, tn=512):
    S = y_ssd.shape[0]
    tm = min(tm, S)
    gs_col = COL_G // tn
    ga_col = (COL_G + D_MODEL) // tn
    nb = D_MODEL // tn
    return pl.pallas_call(
        _mix_kernel,
        grid=(S // tm, D_MODEL // tn),
        in_specs=[
            pl.BlockSpec((tm, D_INNER), lambda m, n: (m, 0)),
            pl.BlockSpec((tm, D_ATTN), lambda m, n: (m, 0)),
            pl.BlockSpec((D_INNER, tn), lambda m, n: (0, n)),
            pl.BlockSpec((D_ATTN, tn), lambda m, n: (0, n)),
            pl.BlockSpec((tm, tn), lambda m, n: (m, gs_col + n)),
            pl.BlockSpec((tm, tn), lambda m, n: (m, ga_col + n)),
            pl.BlockSpec((1, tn), lambda m, n: (0, n)),
            pl.BlockSpec((1, tn), lambda m, n: (0, nb + n)),
        ],
        out_specs=pl.BlockSpec((tm, tn), lambda m, n: (m, n)),
        out_shape=jax.ShapeDtypeStruct((S, D_MODEL), BF16),
        compiler_params=_cparams(("arbitrary", "arbitrary")),
        name="mix",
    )(y_ssd, y_attn, w_ssm_out, w_attn_out, proj, proj, b_gate, b_gate)


def _out_norm_kernel(x_ref, mix_ref, w_ref, nw_ref, x1_ref, h2_ref):
    x1 = x_ref[...] + _dot(mix_ref[...], w_ref[...])
    x1_ref[...] = x1
    ms = jnp.mean(x1 * x1, axis=-1, keepdims=True)
    h2_ref[...] = (x1 * lax.rsqrt(ms + EPS) * nw_ref[...]).astype(h2_ref.dtype)


def _out_norm(x, mix, w_out, ffn_norm_w, tm=512):
    S = x.shape[0]
    tm = min(tm, S)
    return pl.pallas_call(
        _out_norm_kernel,
        grid=(S // tm,),
        in_specs=[
            pl.BlockSpec((tm, D_MODEL), lambda m: (m, 0)),
            pl.BlockSpec((tm, D_MODEL), lambda m: (m, 0)),
            pl.BlockSpec((D_MODEL, D_MODEL), lambda m: (0, 0)),
            pl.BlockSpec((1, D_MODEL), lambda m: (0, 0)),
        ],
        out_specs=[
            pl.BlockSpec((tm, D_MODEL), lambda m: (m, 0)),
            pl.BlockSpec((tm, D_MODEL), lambda m: (m, 0)),
        ],
        out_shape=[
            jax.ShapeDtypeStruct((S, D_MODEL), F32),
            jax.ShapeDtypeStruct((S, D_MODEL), BF16),
        ],
        compiler_params=_cparams(("arbitrary",)),
        name="out_norm",
    )(x, mix, w_out, ffn_norm_w)


def _ffn_up_kernel(h_ref, wg32_ref, wu32_ref, cwg_ref, cwu_ref, cbg_ref, cbu_ref, o_ref,
                   tailg, tailu, padg, padu, wg_ref, wu_ref):
    tm = h_ref.shape[0]

    @pl.when(pl.program_id(1) == 0)
    def _():
        tailg[...] = jnp.zeros(tailg.shape, F32)
        tailu[...] = jnp.zeros(tailu.shape, F32)
        wg_ref[...] = wg32_ref[...].astype(BF16)
        wu_ref[...] = wu32_ref[...].astype(BF16)

    def conv(w_ref, cw_ref, cb_ref, tail, pad):
        pad[0:SUBLANES, :] = tail[...]
        pad[SUBLANES:SUBLANES + tm, :] = _dot(h_ref[...], w_ref[...])
        tail[...] = pad[tm:tm + SUBLANES, :]
        acc = cb_ref[...]
        for k in range(FFN_CONV):
            off = SUBLANES - (FFN_CONV - 1) + k
            acc = acc + cw_ref[k:k + 1, :] * pad[off:off + tm, :]
        return acc

    ug = conv(wg_ref, cwg_ref, cbg_ref, tailg, padg)
    uu = conv(wu_ref, cwu_ref, cbu_ref, tailu, padu)
    o_ref[...] = (_silu(ug) * uu).astype(o_ref.dtype)


def _ffn_up(h2, w_up, conv_w, conv_b, tm=1024, tn=512):
    S = h2.shape[0]
    tm = min(tm, S)
    nb = D_FF // tn
    return pl.pallas_call(
        _ffn_up_kernel,
        grid=(nb, S // tm),
        in_specs=[
            pl.BlockSpec((tm, D_MODEL), lambda n, m: (m, 0)),
            pl.BlockSpec((D_MODEL, tn), lambda n, m: (0, n)),
            pl.BlockSpec((D_MODEL, tn), lambda n, m: (0, nb + n)),
            pl.BlockSpec((FFN_CONV, tn), lambda n, m: (0, n)),
            pl.BlockSpec((FFN_CONV, tn), lambda n, m: (0, nb + n)),
            pl.BlockSpec((1, tn), lambda n, m: (0, n)),
            pl.BlockSpec((1, tn), lambda n, m: (0, nb + n)),
        ],
        out_specs=pl.BlockSpec((tm, tn), lambda n, m: (m, n)),
        out_shape=jax.ShapeDtypeStruct((S, D_FF), BF16),
        scratch_shapes=[
            pltpu.VMEM((SUBLANES, tn), F32),
            pltpu.VMEM((SUBLANES, tn), F32),
            pltpu.VMEM((tm + SUBLANES, tn), F32),
            pltpu.VMEM((tm + SUBLANES, tn), F32),
            pltpu.VMEM((D_MODEL, tn), BF16),
            pltpu.VMEM((D_MODEL, tn), BF16),
        ],
        compiler_params=_cparams(("arbitrary", "arbitrary")),
        name="ffn_up",
    )(h2, w_up, w_up, conv_w, conv_w, conv_b, conv_b)


def _ffn_down_kernel(a_ref, w_ref, x1_ref, o_ref):
    o_ref[...] = x1_ref[...] + _dot(a_ref[...], w_ref[...])


def _ffn_down(act, w_down, x1, tm=1024, tn=512):
    S = act.shape[0]
    tm = min(tm, S)
    return pl.pallas_call(
        _ffn_down_kernel,
        grid=(S // tm, D_MODEL // tn),
        in_specs=[
            pl.BlockSpec((tm, D_FF), lambda m, n: (m, 0)),
            pl.BlockSpec((D_FF, tn), lambda m, n: (0, n)),
            pl.BlockSpec((tm, tn), lambda m, n: (m, n)),
        ],
        out_specs=pl.BlockSpec((tm, tn), lambda m, n: (m, n)),
        out_shape=jax.ShapeDtypeStruct((S, D_MODEL), F32),
        compiler_params=_cparams(("arbitrary", "arbitrary")),
        name="ffn_down",
    )(act, w_down, x1)


def _pad_lanes(v):
    return jnp.pad(v, ((0, 0), (0, LANES - v.shape[1])))


def _layer(x, attn_norm_w, w_in, b_gate, ssm_conv_w, ssm_conv_b, ssm_dt_bias, ssm_a_log, ssm_d,
           ssm_norm_w, q_norm_w, k_norm_w, rel_bias, w_ssm_out, w_attn_out, w_out, ffn_norm_w,
           w_up, ffn_conv_w, ffn_conv_b, w_down):
    o_z, o_xbc = 0, D_INNER
    o_dt = o_xbc + D_XBC
    o_q = o_dt + SSM_HEADS
    o_g = o_q + 3 * D_ATTN
    wt = w_in.T
    wt_dt = jnp.pad(wt[o_dt:o_q], ((0, LANES - SSM_HEADS), (0, 0)))

    proj, dt_raw = _in_proj(x, attn_norm_w[None, :], wt.astype(BF16), wt_dt, o_dt)
    y_ssd = _ssd(proj, dt_raw, ssm_conv_w, ssm_conv_b[None, :], _pad_lanes(ssm_dt_bias[None, :]),
                 _pad_lanes(ssm_a_log[None, :]), jnp.repeat(ssm_d, SSM_HEAD_DIM)[None, :],
                 ssm_norm_w[None, :])
    qt, kn, vt, pen = _qk_prep(proj, q_norm_w[None, :], k_norm_w[None, :])
    y_attn = _moba(qt, kn, vt, pen, rel_bias.T)
    mix = _mix(y_ssd, y_attn, w_ssm_out.astype(BF16), w_attn_out.astype(BF16), proj, b_gate[None, :])
    x1, h2 = _out_norm(x, mix, w_out.astype(BF16), ffn_norm_w[None, :])
    act = _ffn_up(h2, w_up, ffn_conv_w, ffn_conv_b[None, :])
    return _ffn_down(act, w_down.astype(BF16), x1)


def kernel(x, attn_norm_w, w_in, b_gate, ssm_conv_w, ssm_conv_b, ssm_dt_bias, ssm_a_log, ssm_d,
           ssm_norm_w, q_norm_w, k_norm_w, rel_bias, w_ssm_out, w_attn_out, w_out, ffn_norm_w,
           w_up, ffn_conv_w, ffn_conv_b, w_down):
    assert x.shape[0] == 1 and attn_norm_w.shape[0] == 1
    out = _layer(x[0], attn_norm_w[0], w_in[0], b_gate[0], ssm_conv_w[0], ssm_conv_b[0],
                 ssm_dt_bias[0], ssm_a_log[0], ssm_d[0], ssm_norm_w[0], q_norm_w[0], k_norm_w[0],
                 rel_bias, w_ssm_out[0], w_attn_out[0], w_out[0], ffn_norm_w[0], w_up[0],
                 ffn_conv_w[0], ffn_conv_b[0], w_down[0])
    return out[None]
```

```python
import functools
import math

import jax
import jax.numpy as jnp
from jax import lax
from jax.experimental import pallas as pl
from jax.experimental.pallas import tpu as pltpu

F32 = jnp.float32
BF16 = jnp.bfloat16

D_MODEL = 2048
D_INNER = 4096
SSM_HEADS = 64
SSM_HEAD_DIM = 64
SSM_GROUPS = 8
SSM_HEADS_PER_GROUP = SSM_HEADS // SSM_GROUPS
SSM_STATE = 128
SSM_CONV = 4
SSM_CHUNK = 256
GROUP_CH = D_INNER // SSM_GROUPS
SSD_GROUPS_PER_STEP = 2
D_XBC = D_INNER + 2 * SSM_GROUPS * SSM_STATE
ATTN_HEADS = 16
ATTN_HEAD_DIM = 128
D_ATTN = ATTN_HEADS * ATTN_HEAD_DIM
MOBA_BLOCK = 256
MOBA_TOPK = 3
REL_BUCKETS = 32
REL_MAX_DIST = 128
D_FF = 5632
FFN_CONV = 3
EPS = 1e-6
LOG2E = 1.4426950408889634

LANES = 128
SUBLANES = 8
D_PROJ = D_XBC + 3 * D_ATTN + 2 * D_INNER
COL_Z = 0
COL_XBC = COL_Z + D_INNER
COL_Q = COL_XBC + D_XBC
COL_K = COL_Q + D_ATTN
COL_V = COL_K + D_ATTN
COL_G = COL_V + D_ATTN
VMEM_LIMIT = 56 * 1024 * 1024


def _cparams(sem):
    return pltpu.CompilerParams(dimension_semantics=sem, vmem_limit_bytes=VMEM_LIMIT)


def _split3(a):
    hi = a.astype(BF16)
    r = a - hi.astype(F32)
    mid = r.astype(BF16)
    lo = (r - mid.astype(F32)).astype(BF16)
    return hi, mid, lo


def _dot(a, b):
    return jnp.dot(a, b, preferred_element_type=F32)


def _dot_nt(a, b):
    return lax.dot_general(a, b, (((1,), (1,)), ((), ())), preferred_element_type=F32)


def _sigmoid(x):
    return 0.5 + 0.5 * jnp.tanh(0.5 * x)


def _silu(x):
    h = 0.5 * x
    return h + h * jnp.tanh(h)


def _in_proj_kernel(x_ref, nw_ref, w_ref, wdt_ref, o_ref, dt_ref, h_scr):
    @pl.when(pl.program_id(1) == 0)
    def _():
        x = x_ref[...]
        ms = jnp.mean(x * x, axis=-1, keepdims=True)
        h = x * lax.rsqrt(ms + EPS) * nw_ref[...]
        h_scr[...] = h.astype(BF16)
        h_hi, h_mid, _ = _split3(h)
        w_hi, w_mid, _ = _split3(wdt_ref[...])
        dt_ref[...] = _dot_nt(h_hi, w_hi) + _dot_nt(h_hi, w_mid) + _dot_nt(h_mid, w_hi)

    o_ref[...] = _dot_nt(h_scr[...], w_ref[...]).astype(o_ref.dtype)


def _in_proj(x, norm_w, wt, wt_dt, dt_row0, tm=1024, tn=1024):
    S = x.shape[0]
    tm = min(tm, S)
    assert dt_row0 % tn == 0 and SSM_HEADS % (2 * SUBLANES) == 0
    na = dt_row0 // tn

    def w_rows(m, n):
        return (pl.multiple_of(n * tn + jnp.where(n >= na, SSM_HEADS, 0), SSM_HEADS), 0)

    return pl.pallas_call(
        _in_proj_kernel,
        grid=(S // tm, D_PROJ // tn),
        in_specs=[
            pl.BlockSpec((tm, D_MODEL), lambda m, n: (m, 0)),
            pl.BlockSpec((1, D_MODEL), lambda m, n: (0, 0)),
            pl.BlockSpec((pl.Element(tn), pl.Element(D_MODEL)), w_rows),
            pl.BlockSpec((LANES, D_MODEL), lambda m, n: (0, 0)),
        ],
        out_specs=[
            pl.BlockSpec((tm, tn), lambda m, n: (m, n)),
            pl.BlockSpec((tm, LANES), lambda m, n: (m, 0)),
        ],
        out_shape=[
            jax.ShapeDtypeStruct((S, D_PROJ), BF16),
            jax.ShapeDtypeStruct((S, LANES), F32),
        ],
        scratch_shapes=[pltpu.VMEM((tm, D_MODEL), BF16)],
        compiler_params=_cparams(("arbitrary", "arbitrary")),
        name="in_proj",
    )(x, norm_w, wt, wt_dt)


def _conv_silu(raw_ref, tail_ref, pad_ref, w_ref, b_ref, g, taps):
    L = raw_ref.shape[0]
    pad_ref[0:SUBLANES, :] = tail_ref[g]
    pad_ref[SUBLANES:SUBLANES + L, :] = raw_ref[...].astype(F32)
    tail_ref[g] = pad_ref[L:L + SUBLANES, :]
    acc = b_ref[...]
    for k in range(taps):
        off = SUBLANES - (taps - 1) + k
        acc = acc + w_ref[k:k + 1, :] * pad_ref[off:off + L, :]
    return _silu(acc)


def _ssd_kernel(x_ref, b_ref, c_ref, z_ref, dtr_ref, cwx_ref, cwb_ref, cwc_ref,
                cbx_ref, cbb_ref, cbc_ref, dtb_ref, alog_ref, dsk_ref, nw_ref,
                o_ref,
                tailx, tailb, tailc, padx, padb, padc, acs_scr, acst_scr, lbt_scr, ht_scr):
    c = pl.program_id(0)
    gp = pl.program_id(1)
    L = SSM_CHUNK
    QW = 4 * SSM_HEAD_DIM
    GPS = SSD_GROUPS_PER_STEP

    @pl.when(c == 0)
    def _():
        for gg in range(GPS):
            g0 = gp * GPS + gg
            tailx[g0] = jnp.zeros(tailx.shape[1:], F32)
            tailb[g0] = jnp.zeros(tailb.shape[1:], F32)
            tailc[g0] = jnp.zeros(tailc.shape[1:], F32)
            ht_scr[g0] = jnp.zeros(ht_scr.shape[1:], F32)

    @pl.when(gp == 0)
    def _():
        t = dtr_ref[...] + dtb_ref[...]
        dt = jnp.maximum(t, 0.0) + jnp.log(1.0 + jnp.exp(-jnp.abs(t)))
        a = -jnp.exp(alog_ref[...])
        da = dt * a
        ri = lax.broadcasted_iota(jnp.int32, (L, L), 0)
        ci = lax.broadcasted_iota(jnp.int32, (L, L), 1)
        tri = jnp.where(ri >= ci, 1.0, 0.0).astype(BF16)
        hi, mid, lo = _split3(da)
        a_cs = _dot(tri, hi) + _dot(tri, mid) + _dot(tri, lo)
        a2 = a_cs * LOG2E
        a2t = a2.T
        acs_scr[...] = a2
        acst_scr[...] = a2t
        lbt_scr[...] = a2t - jnp.log(dt.T) * LOG2E

    for gg in range(GPS):
        xs = slice(gg * GROUP_CH, (gg + 1) * GROUP_CH)
        ns = slice(gg * SSM_STATE, (gg + 1) * SSM_STATE)
        _ssd_group(gp * GPS + gg,
                   x_ref.at[:, xs], b_ref.at[:, ns], c_ref.at[:, ns], z_ref.at[:, xs],
                   cwx_ref.at[:, xs], cwb_ref.at[:, ns], cwc_ref.at[:, ns],
                   cbx_ref.at[:, xs], cbb_ref.at[:, ns], cbc_ref.at[:, ns],
                   dsk_ref.at[:, xs], nw_ref.at[:, xs], o_ref.at[:, xs],
                   tailx, tailb, tailc, padx.at[gg], padb.at[gg], padc.at[gg],
                   acs_scr, acst_scr, lbt_scr, ht_scr)


def _ssd_group(g, x_ref, b_ref, c_ref, z_ref, cwx_ref, cwb_ref, cwc_ref, cbx_ref, cbb_ref, cbc_ref,
               dsk_ref, nw_ref, o_ref, tailx, tailb, tailc, padx, padb, padc,
               acs_scr, acst_scr, lbt_scr, ht_scr):
    L = SSM_CHUNK
    QW = 4 * SSM_HEAD_DIM
    xa = _conv_silu(x_ref, tailx, padx, cwx_ref, cbx_ref, g, SSM_CONV)
    ba = _conv_silu(b_ref, tailb, padb, cwb_ref, cbb_ref, g, SSM_CONV)
    ca = _conv_silu(c_ref, tailc, padc, cwc_ref, cbc_ref, g, SSM_CONV)

    xb = xa.astype(BF16)
    bt = ba.T
    cb = _dot(ca.astype(BF16), bt.astype(BF16))
    a_cs = acs_scr[...]
    ri = lax.broadcasted_iota(jnp.int32, (L, L), 0)
    ci = lax.broadcasted_iota(jnp.int32, (L, L), 1)
    causal = ri >= ci
    lane_h = lax.broadcasted_iota(jnp.int32, (L, LANES), 1)
    lane_q = lax.broadcasted_iota(jnp.int32, (1, QW), 1)

    ys = []
    for q in range(GROUP_CH // QW):
        xq = xb[:, q * QW:(q + 1) * QW]
        hq = ht_scr[g, :, q * QW:(q + 1) * QW]
        hqb = hq.astype(BF16)
        acc = jnp.zeros((L, QW), F32)
        st = jnp.zeros((SSM_STATE, QW), F32)
        cdq = jnp.zeros((1, QW), F32)
        for j in range(QW // SSM_HEAD_DIM):
            h = g * SSM_HEADS_PER_GROUP + q * (QW // SSM_HEAD_DIM) + j
            a_row = acst_scr[pl.ds(h, 1), :]
            b_row = lbt_scr[pl.ds(h, 1), :]
            a_col = jnp.sum(jnp.where(lane_h == h, a_cs, 0.0), axis=1, keepdims=True)
            dmat = jnp.where(causal, a_col - b_row, -jnp.inf)
            mh = (cb * jnp.exp2(dmat)).astype(BF16)
            lm = (lane_q >= j * SSM_HEAD_DIM) & (lane_q < (j + 1) * SSM_HEAD_DIM)
            xm = jnp.where(lm, xq, jnp.zeros_like(xq))
            hm = jnp.where(lm, hqb, jnp.zeros_like(hqb))
            ce = (ca * jnp.exp2(a_col)).astype(BF16)
            acc = acc + _dot(mh, xm) + _dot(ce, hm)
            a_last = a_row[:, L - 1:L]
            w_row = jnp.exp2(a_last - b_row)
            st = st + _dot((bt * w_row).astype(BF16), xm)
            cdq = jnp.where(lm, jnp.exp2(a_last), cdq)
        ht_scr[g, :, q * QW:(q + 1) * QW] = hq * cdq + st
        ys.append(acc + xa[:, q * QW:(q + 1) * QW] * dsk_ref[:, q * QW:(q + 1) * QW])

    z = z_ref[...].astype(F32)
    sz = _silu(z)
    ygs = [ys[q] * sz[:, q * QW:(q + 1) * QW] for q in range(len(ys))]
    ss = sum(jnp.sum(v * v, axis=1, keepdims=True) for v in ygs)
    scale = lax.rsqrt(ss * (1.0 / GROUP_CH) + EPS)
    for q in range(len(ys)):
        o_ref[:, q * QW:(q + 1) * QW] = (
            ygs[q] * scale * nw_ref[:, q * QW:(q + 1) * QW]).astype(o_ref.dtype)


def _ssd(proj, dt_raw, conv_w, conv_b, dt_bias, a_log, d_exp, norm_w):
    S = proj.shape[0]
    L, G, N = SSM_CHUNK, SSM_GROUPS, SSM_STATE
    GPS = SSD_GROUPS_PER_STEP
    XW, NW = GPS * GROUP_CH, GPS * N
    bcol = D_INNER // NW
    ccol = bcol + G // GPS
    pxcol = COL_XBC // XW
    pbcol = COL_XBC // NW + bcol
    pccol = pbcol + G // GPS
    zcol = COL_Z // XW
    return pl.pallas_call(
        _ssd_kernel,
        grid=(S // L, G // GPS),
        in_specs=[
            pl.BlockSpec((L, XW), lambda c, g: (c, pxcol + g)),
            pl.BlockSpec((L, NW), lambda c, g: (c, pbcol + g)),
            pl.BlockSpec((L, NW), lambda c, g: (c, pccol + g)),
            pl.BlockSpec((L, XW), lambda c, g: (c, zcol + g)),
            pl.BlockSpec((L, LANES), lambda c, g: (c, 0)),
            pl.BlockSpec((SSM_CONV, XW), lambda c, g: (0, g)),
            pl.BlockSpec((SSM_CONV, NW), lambda c, g: (0, bcol + g)),
            pl.BlockSpec((SSM_CONV, NW), lambda c, g: (0, ccol + g)),
            pl.BlockSpec((1, XW), lambda c, g: (0, g)),
            pl.BlockSpec((1, NW), lambda c, g: (0, bcol + g)),
            pl.BlockSpec((1, NW), lambda c, g: (0, ccol + g)),
            pl.BlockSpec((1, LANES), lambda c, g: (0, 0)),
            pl.BlockSpec((1, LANES), lambda c, g: (0, 0)),
            pl.BlockSpec((1, XW), lambda c, g: (0, g)),
            pl.BlockSpec((1, XW), lambda c, g: (0, g)),
        ],
        out_specs=pl.BlockSpec((L, XW), lambda c, g: (c, g)),
        out_shape=jax.ShapeDtypeStruct((S, D_INNER), BF16),
        scratch_shapes=[
            pltpu.VMEM((G, SUBLANES, GROUP_CH), F32),
            pltpu.VMEM((G, SUBLANES, N), F32),
            pltpu.VMEM((G, SUBLANES, N), F32),
            pltpu.VMEM((GPS, L + SUBLANES, GROUP_CH), F32),
            pltpu.VMEM((GPS, L + SUBLANES, N), F32),
            pltpu.VMEM((GPS, L + SUBLANES, N), F32),
            pltpu.VMEM((L, LANES), F32),
            pltpu.VMEM((LANES, L), F32),
            pltpu.VMEM((LANES, L), F32),
            pltpu.VMEM((G, N, GROUP_CH), F32),
        ],
        compiler_params=_cparams(("arbitrary", "arbitrary")),
        name="ssd",
    )(proj, proj, proj, proj, dt_raw, conv_w, conv_w, conv_w, conv_b, conv_b, conv_b,
      dt_bias, a_log, d_exp, norm_w)


def _qk_prep_kernel(q_ref, k_ref, v_ref, qw_ref, kw_ref, qt_ref, kn_ref, vt_ref, pen_ref, kbar_scr):
    S = q_ref.shape[0]
    BS = MOBA_BLOCK
    nb = S // BS
    nbp = kbar_scr.shape[0]
    kbar_scr[...] = jnp.zeros(kbar_scr.shape, F32)

    def kloop(b, carry):
        rows = pl.ds(pl.multiple_of(b * BS, BS), BS)
        k = k_ref[rows, :].astype(F32)
        kn = k * lax.rsqrt(jnp.mean(k * k, axis=-1, keepdims=True) + EPS) * kw_ref[...]
        kn_ref[0, rows, :] = kn.astype(BF16)
        kbar_scr[pl.ds(b, 1), :] = jnp.mean(kn, axis=0, keepdims=True)
        vt_ref[0, b, 0:ATTN_HEAD_DIM, :] = v_ref[rows, :].astype(F32).T.astype(BF16)
        vt_ref[0, b, ATTN_HEAD_DIM:, :] = jnp.ones((MOBA_VT_ROWS - ATTN_HEAD_DIM, BS), BF16)
        return carry

    lax.fori_loop(0, nb, kloop, 0, unroll=4)
    kb_hi, kb_mid, _ = _split3(kbar_scr[...])

    def qloop(i, carry):
        rows = pl.ds(pl.multiple_of(i * BS, BS), BS)
        q = q_ref[rows, :].astype(F32)
        qn = q * lax.rsqrt(jnp.mean(q * q, axis=-1, keepdims=True) + EPS) * qw_ref[...]
        qnt = (qn * (ATTN_HEAD_DIM ** -0.5)).T
        qt_ref[0, i] = (qnt * LOG2E).astype(BF16)
        q_hi, q_mid, _ = _split3(qnt)
        gate = _dot(kb_hi, q_hi) + _dot(kb_mid, q_hi) + _dot(kb_hi, q_mid)
        blk = lax.broadcasted_iota(jnp.int32, gate.shape, 0)
        blk_f = blk.astype(F32)
        gate = jnp.where(blk < i, gate, -jnp.inf)
        pen = jnp.full(gate.shape, -jnp.inf, F32)
        for _ in range(MOBA_TOPK):
            mx = jnp.max(gate, axis=0, keepdims=True)
            cand = jnp.where((gate == mx) & (mx > -jnp.inf), blk_f, float(nbp))
            idx = jnp.min(cand, axis=0, keepdims=True)
            hit = blk_f == idx
            pen = jnp.where(hit, 0.0, pen)
            gate = jnp.where(hit, -jnp.inf, gate)
        pen_ref[0, i] = pen
        return carry

    lax.fori_loop(0, nb, qloop, 0, unroll=4)


def _qk_prep(proj, q_norm_w, k_norm_w):
    S = proj.shape[0]
    H, D, BS = ATTN_HEADS, ATTN_HEAD_DIM, MOBA_BLOCK
    nb = S // BS
    nbp = -(-nb // SUBLANES) * SUBLANES
    return pl.pallas_call(
        _qk_prep_kernel,
        grid=(H,),
        in_specs=[
            pl.BlockSpec((S, D), lambda h: (0, COL_Q // D + h)),
            pl.BlockSpec((S, D), lambda h: (0, COL_K // D + h)),
            pl.BlockSpec((S, D), lambda h: (0, COL_V // D + h)),
            pl.BlockSpec((1, D), lambda h: (0, 0)),
            pl.BlockSpec((1, D), lambda h: (0, 0)),
        ],
        out_specs=[
            pl.BlockSpec((1, nb, D, BS), lambda h: (h, 0, 0, 0)),
            pl.BlockSpec((1, S, D), lambda h: (h, 0, 0)),
            pl.BlockSpec((1, nb, MOBA_VT_ROWS, BS), lambda h: (h, 0, 0, 0)),
            pl.BlockSpec((1, nb, nbp, BS), lambda h: (h, 0, 0, 0)),
        ],
        out_shape=[
            jax.ShapeDtypeStruct((H, nb, D, BS), BF16),
            jax.ShapeDtypeStruct((H, S, D), BF16),
            jax.ShapeDtypeStruct((H, nb, MOBA_VT_ROWS, BS), BF16),
            jax.ShapeDtypeStruct((H, nb, nbp, BS), F32),
        ],
        scratch_shapes=[pltpu.VMEM((nbp, D), F32)],
        compiler_params=_cparams(("arbitrary",)),
        name="qk_prep",
    )(proj, proj, proj, q_norm_w, k_norm_w)


def _rel_bucket(dist):
    n = jnp.maximum(dist, 0)
    max_exact = REL_BUCKETS // 2
    nf = jnp.maximum(n, 1).astype(F32)
    large = max_exact + (jnp.log(nf / max_exact) / math.log(REL_MAX_DIST / max_exact)
                         * (REL_BUCKETS - max_exact)).astype(jnp.int32)
    large = jnp.minimum(large, REL_BUCKETS - 1)
    return jnp.where(n < max_exact, n, large)


MOBA_HEADS_PER_STEP = 4
MOBA_VT_ROWS = ATTN_HEAD_DIM + 2 * SUBLANES


def _moba_kernel(tbl_ref, qt_ref, k_ref, vt_ref, pen_ref, o_ref,
                 bown_scr, bprev_scr, m_scr, acc_scr, s0_scr, s1_scr, s2_scr, s3_scr,
                 p0_scr, p1_scr, p2_scr, p3_scr, al0_scr, al1_scr, al2_scr, al3_scr):
    hp = pl.program_id(0)
    i = pl.program_id(1)
    BS, HB = MOBA_BLOCK, MOBA_HEADS_PER_STEP
    D = ATTN_HEAD_DIM

    @pl.when(i == 0)
    def _():
        kl = lax.broadcasted_iota(jnp.int32, (BS, BS), 0)
        ql = lax.broadcasted_iota(jnp.int32, (BS, BS), 1)
        d = ql - kl
        b_own = _rel_bucket(d)
        b_prev = _rel_bucket(d + BS)
        for a in range(HB):
            h = hp * HB + a
            far = tbl_ref[h, REL_BUCKETS - 1]

            def lookup(bucket):
                val = jnp.zeros((BS, BS), F32)
                for b in range(REL_BUCKETS):
                    val = jnp.where(bucket == b, tbl_ref[h, b], val)
                return (val - far) * LOG2E

            bown_scr[a] = jnp.where(d >= 0, lookup(b_own), -jnp.inf)
            bprev_scr[a] = lookup(b_prev)

    def krows(a, j):
        return k_ref[a, pl.ds(pl.multiple_of(j * BS, BS), BS), :]

    jp = jnp.maximum(i - 1, 0)
    no_prev = jnp.where(i == 0, -jnp.inf, 0.0)
    n_far = jnp.maximum(i - 1, 0)
    nb = k_ref.shape[1] // BS

    def clampj(j):
        return jnp.clip(j, 0, nb - 1)

    def scores(j, s_slot):
        for a in range(HB):
            s_slot[a] = _dot(krows(a, clampj(j)), qt_ref[a, 0])

    def softmax(j, s_slot, p_slot, al_slot, mask_tail, bias_scr=None):
        for a in range(HB):
            pr = pen_ref[a, 0, pl.ds(clampj(j), 1), :]
            if mask_tail is True:
                pr = pr + jnp.where(j >= n_far, -jnp.inf, 0.0)
            elif mask_tail is not False:
                pr = pr + mask_tail
            s = s_slot[a]
            if bias_scr is not None:
                s = s + bias_scr[a]
            m_old = m_scr[a]
            m_new = jnp.maximum(m_old, jnp.max(s, axis=0, keepdims=True) + pr)
            p_slot[a] = jnp.exp2(s - (m_new - pr)).astype(BF16)
            al_slot[a] = jnp.exp2(m_old - m_new)
            m_scr[a] = m_new

    def accumulate(j, p_slot, al_slot, maybe_prev=False):
        jv = clampj(j)
        if maybe_prev:
            jv = jnp.where(j < 0, jp, jv)
        for a in range(HB):
            acc_scr[a] = al_slot[a] * acc_scr[a] + _dot(vt_ref[a, jv], p_slot[a])

    s_slots = (s0_scr, s1_scr, s2_scr, s3_scr)
    p_slots = (p0_scr, p1_scr, p2_scr, p3_scr)
    al_slots = (al0_scr, al1_scr, al2_scr, al3_scr)
    scores(i, s2_scr)
    scores(jp, s3_scr)
    scores(0, s0_scr)
    for a in range(HB):
        s = s2_scr[a] + bown_scr[a]
        m0 = jnp.max(s, axis=0, keepdims=True)
        m_scr[a] = m0
        p2_scr[a] = jnp.exp2(s - m0).astype(BF16)
    for a in range(HB):
        acc_scr[a] = _dot(vt_ref[a, i], p2_scr[a])
    softmax(jp, s3_scr, p3_scr, al3_scr, no_prev, bias_scr=bprev_scr)

    def quad_loop(g, carry):
        j = 4 * g
        for t in range(4):
            scores(j + t + 1, s_slots[(t + 1) % 4])
            accumulate(j + t - 1, p_slots[(t - 1) % 4], al_slots[(t - 1) % 4], maybe_prev=(t == 0))
            softmax(j + t, s_slots[t], p_slots[t], al_slots[t], False)
        return carry

    n_quads = n_far // 4
    lax.fori_loop(0, n_quads, quad_loop, 0)

    def pair_loop(g, carry):
        j = 4 * n_quads + 2 * g
        scores(j + 1, s1_scr)
        accumulate(j - 1, p3_scr, al3_scr, maybe_prev=True)
        softmax(j, s0_scr, p2_scr, al2_scr, False)
        scores(j + 2, s0_scr)
        accumulate(j, p2_scr, al2_scr)
        softmax(j + 1, s1_scr, p3_scr, al3_scr, True)
        return carry

    n_pairs = (n_far - 4 * n_quads + 1) // 2
    lax.fori_loop(0, n_pairs, pair_loop, 0)
    accumulate(4 * n_quads + 2 * n_pairs - 1, p3_scr, al3_scr, maybe_prev=True)
    for a in range(HB):
        acc = acc_scr[a]
        o_ref[:, a * D:(a + 1) * D] = (acc[0:D, :] / acc[D:D + 1, :]).T.astype(o_ref.dtype)


def _moba(qt, kn, vt, pen, table_h):
    H, nb, D, BS = qt.shape
    S = nb * BS
    nbp = pen.shape[2]
    HB = MOBA_HEADS_PER_STEP
    return pl.pallas_call(
        _moba_kernel,
        grid=(H // HB, nb),
        in_specs=[
            pl.BlockSpec(memory_space=pltpu.SMEM),
            pl.BlockSpec((HB, 1, D, BS), lambda hp, i: (hp, i, 0, 0)),
            pl.BlockSpec((HB, S, D), lambda hp, i: (hp, 0, 0)),
            pl.BlockSpec((HB, nb, MOBA_VT_ROWS, BS), lambda hp, i: (hp, 0, 0, 0)),
            pl.BlockSpec((HB, 1, nbp, BS), lambda hp, i: (hp, i, 0, 0)),
        ],
        out_specs=pl.BlockSpec((BS, HB * D), lambda hp, i: (i, hp)),
        out_shape=jax.ShapeDtypeStruct((S, H * D), BF16),
        scratch_shapes=[
            pltpu.VMEM((HB, BS, BS), F32),
            pltpu.VMEM((HB, BS, BS), F32),
            pltpu.VMEM((HB, 1, BS), F32),
            pltpu.VMEM((HB, MOBA_VT_ROWS, BS), F32),
        ] + [pltpu.VMEM((HB, BS, BS), F32)] * 4 + [pltpu.VMEM((HB, BS, BS), BF16)] * 4
        + [pltpu.VMEM((HB, 1, BS), F32)] * 4,
        compiler_params=_cparams(("arbitrary", "arbitrary")),
        name="moba",
    )(table_h, qt, kn, vt, pen)


def _mix_kernel(ys_ref, ya_ref, ws_ref, wa_ref, gs_ref, ga_ref, bs_ref, ba_ref, o_ref):
    gs = _sigmoid(gs_ref[...].astype(F32) + bs_ref[...])
    ga = _sigmoid(ga_ref[...].astype(F32) + ba_ref[...])
    o_ref[...] = (gs * _dot(ys_ref[...], ws_ref[...])
                  + ga * _dot(ya_ref[...], wa_ref[...])).astype(o_ref.dtype)


def _mix(y_ssd, y_attn, w_ssm_out, w_attn_out, proj, b_gate, tm=1024, tn=512):
    S = y_ssd.shape[0]
    tm = min(tm, S)
    gs_col = COL_G // tn
    ga_col = (COL_G + D_MODEL) // tn
    nb = D_MODEL // tn
    return pl.pallas_call(
        _mix_kernel,
        grid=(S // tm, D_MODEL // tn),
        in_specs=[
            pl.BlockSpec((tm, D_INNER), lambda m, n: (m, 0)),
            pl.BlockSpec((tm, D_ATTN), lambda m, n: (m, 0)),
            pl.BlockSpec((D_INNER, tn), lambda m, n: (0, n)),
            pl.BlockSpec((D_ATTN, tn), lambda m, n: (0, n)),
            pl.BlockSpec((tm, tn), lambda m, n: (m, gs_col + n)),
            pl.BlockSpec((tm, tn), lambda m, n: (m, ga_col + n)),
            pl.BlockSpec((1, tn), lambda m, n: (0, n)),
            pl.BlockSpec((1, tn), lambda m, n: (0, nb + n)),
        ],
        out_specs=pl.BlockSpec((tm, tn), lambda m, n: (m, n)),
        out_shape=jax.ShapeDtypeStruct((S, D_MODEL), BF16),
        compiler_params=_cparams(("arbitrary", "arbitrary")),
        name="mix",
    )(y_ssd, y_attn, w_ssm_out, w_attn_out, proj, proj, b_gate, b_gate)


def _out_norm_kernel(x_ref, mix_ref, w_ref, nw_ref, x1_ref, h2_ref):
    x1 = x_ref[...] + _dot(mix_ref[...], w_ref[...])
    x1_ref[...] = x1
    ms = jnp.mean(x1 * x1, axis=-1, keepdims=True)
    h2_ref[...] = (x1 * lax.rsqrt(ms + EPS) * nw_ref[...]).astype(h2_ref.dtype)


def _out_norm(x, mix, w_out, ffn_norm_w, tm=512):
    S = x.shape[0]
    tm = min(tm, S)
    return pl.pallas_call(
        _out_norm_kernel,
        grid=(S // tm,),
        in_specs=[
            pl.BlockSpec((tm, D_MODEL), lambda m: (m, 0)),
            pl.BlockSpec((tm, D_MODEL), lambda m: (m, 0)),
            pl.BlockSpec((D_MODEL, D_MODEL), lambda m: (0, 0)),
            pl.BlockSpec((1, D_MODEL), lambda m: (0, 0)),
        ],
        out_specs=[
            pl.BlockSpec((tm, D_MODEL), lambda m: (m, 0)),
            pl.BlockSpec((tm, D_MODEL), lambda m: (m, 0)),
        ],
        out_shape=[
            jax.ShapeDtypeStruct((S, D_MODEL), F32),
            jax.ShapeDtypeStruct((S, D_MODEL), BF16),
        ],
        compiler_params=_cparams(("arbitrary",)),
        name="out_norm",
    )(x, mix, w_out, ffn_norm_w)


def _ffn_up_kernel(h_ref, wg32_ref, wu32_ref, cwg_ref, cwu_ref, cbg_ref, cbu_ref, o_ref,
                   tailg, tailu, padg, padu, wg_ref, wu_ref):
    tm = h_ref.shape[0]

    @pl.when(pl.program_id(1) == 0)
    def _():
        tailg[...] = jnp.zeros(tailg.shape, F32)
        tailu[...] = jnp.zeros(tailu.shape, F32)
        wg_ref[...] = wg32_ref[...].astype(BF16)
        wu_ref[...] = wu32_ref[...].astype(BF16)

    def conv(w_ref, cw_ref, cb_ref, tail, pad):
        pad[0:SUBLANES, :] = tail[...]
        pad[SUBLANES:SUBLANES + tm, :] = _dot(h_ref[...], w_ref[...])
        tail[...] = pad[tm:tm + SUBLANES, :]
        acc = cb_ref[...]
        for k in range(FFN_CONV):
            off = SUBLANES - (FFN_CONV - 1) + k
            acc = acc + cw_ref[k:k + 1, :] * pad[off:off + tm, :]
        return acc

    ug = conv(wg_ref, cwg_ref, cbg_ref, tailg, padg)
    uu = conv(wu_ref, cwu_ref, cbu_ref, tailu, padu)
    o_ref[...] = (_silu(ug) * uu).astype(o_ref.dtype)


def _ffn_up(h2, w_up, conv_w, conv_b, tm=1024, tn=512):
    S = h2.shape[0]
    tm = min(tm, S)
    nb = D_FF // tn
    return pl.pallas_call(
        _ffn_up_kernel,
        grid=(nb, S // tm),
        in_specs=[
            pl.BlockSpec((tm, D_MODEL), lambda n, m: (m, 0)),
            pl.BlockSpec((D_MODEL, tn), lambda n, m: (0, n)),
            pl.BlockSpec((D_MODEL, tn), lambda n, m: (0, nb + n)),
            pl.BlockSpec((FFN_CONV, tn), lambda n, m: (0, n)),
            pl.BlockSpec((FFN_CONV, tn), lambda n, m: (0, nb + n)),
            pl.BlockSpec((1, tn), lambda n, m: (0, n)),
            pl.BlockSpec((1, tn), lambda n, m: (0, nb + n)),
        ],
        out_specs=pl.BlockSpec((tm, tn), lambda n, m: (m, n)),
        out_shape=jax.ShapeDtypeStruct((S, D_FF), BF16),
        scratch_shapes=[
            pltpu.VMEM((SUBLANES, tn), F32),
            pltpu.VMEM((SUBLANES, tn), F32),
            pltpu.VMEM((tm + SUBLANES, tn), F32),
            pltpu.VMEM((tm + SUBLANES, tn), F32),
            pltpu.VMEM((D_MODEL, tn), BF16),
            pltpu.VMEM((D_MODEL, tn), BF16),
        ],
        compiler_params=_cparams(("arbitrary", "arbitrary")),
        name="ffn_up",
    )(h2, w_up, w_up, conv_w, conv_w, conv_b, conv_b)


def _ffn_down_kernel(a_ref, w_ref, x1_ref, o_ref):
    o_ref[...] = x1_ref[...] + _dot(a_ref[...], w_ref[...])


def _ffn_down(act, w_down, x1, tm=1024, tn=512):
    S = act.shape[0]
    tm = min(tm, S)
    return pl.pallas_call(
        _ffn_down_kernel,
        grid=(S // tm, D_MODEL // tn),
        in_specs=[
            pl.BlockSpec((tm, D_FF), lambda m, n: (m, 0)),
            pl.BlockSpec((D_FF, tn), lambda m, n: (0, n)),
            pl.BlockSpec((tm, tn), lambda m, n: (m, n)),
        ],
        out_specs=pl.BlockSpec((tm, tn), lambda m, n: (m, n)),
        out_shape=jax.ShapeDtypeStruct((S, D_MODEL), F32),
        compiler_params=_cparams(("arbitrary", "arbitrary")),
        name="ffn_down",
    )(act, w_down, x1)


def _pad_lanes(v):
    return jnp.pad(v, ((0, 0), (0, LANES - v.shape[1])))


def _layer(x, attn_norm_w, w_in, b_gate, ssm_conv_w, ssm_conv_b, ssm_dt_bias, ssm_a_log, ssm_d,
           ssm_norm_w, q_norm_w, k_norm_w, rel_bias, w_ssm_out, w_attn_out, w_out, ffn_norm_w,
           w_up, ffn_conv_w, ffn_conv_b, w_down):
    o_z, o_xbc = 0, D_INNER
    o_dt = o_xbc + D_XBC
    o_q = o_dt + SSM_HEADS
    o_g = o_q + 3 * D_ATTN
    wt = w_in.T
    wt_dt = jnp.pad(wt[o_dt:o_q], ((0, LANES - SSM_HEADS), (0, 0)))

    proj, dt_raw = _in_proj(x, attn_norm_w[None, :], wt.astype(BF16), wt_dt, o_dt)
    y_ssd = _ssd(proj, dt_raw, ssm_conv_w, ssm_conv_b[None, :], _pad_lanes(ssm_dt_bias[None, :]),
                 _pad_lanes(ssm_a_log[None, :]), jnp.repeat(ssm_d, SSM_HEAD_DIM)[None, :],
                 ssm_norm_w[None, :])
    qt, kn, vt, pen = _qk_prep(proj, q_norm_w[None, :], k_norm_w[None, :])
    y_attn = _moba(qt, kn, vt, pen, rel_bias.T)
    mix = _mix(y_ssd, y_attn, w_ssm_out.astype(BF16), w_attn_out.astype(BF16), proj, b_gate[None, :])
    x1, h2 = _out_norm(x, mix, w_out.astype(BF16), ffn_norm_w[None, :])
    act = _ffn_up(h2, w_up, ffn_conv_w, ffn_conv_b[None, :])
    return _ffn_down(act, w_down.astype(BF16), x1)


def kernel(x, attn_norm_w, w_in, b_gate, ssm_conv_w, ssm_conv_b, ssm_dt_bias, ssm_a_log, ssm_d,
           ssm_norm_w, q_norm_w, k_norm_w, rel_bias, w_ssm_out, w_attn_out, w_out, ffn_norm_w,
           w_up, ffn_conv_w, ffn_conv_b, w_down):
    assert x.shape[0] == 1 and attn_norm_w.shape[0] == 1
    out = _layer(x[0], attn_norm_w[0], w_in[0], b_gate[0], ssm_conv_w[0], ssm_conv_b[0],
                 ssm_dt_bias[0], ssm_a_log[0], ssm_d[0], ssm_norm_w[0], q_norm_w[0], k_norm_w[0],
                 rel_bias, w_ssm_out[0], w_attn_out[0], w_out[0], ffn_norm_w[0], w_up[0],
                 ffn_conv_w[0], ffn_conv_b[0], w_down[0])
    return out[None]
```

```python
import functools
import math

import jax
import jax.numpy as jnp
from jax import lax
from jax.experimental import pallas as pl
from jax.experimental.pallas import tpu as pltpu

F32 = jnp.float32
BF16 = jnp.bfloat16

D_MODEL = 2048
D_INNER = 4096
SSM_HEADS = 64
SSM_HEAD_DIM = 64
SSM_GROUPS = 8
SSM_HEADS_PER_GROUP = SSM_HEADS // SSM_GROUPS
SSM_STATE = 128
SSM_CONV = 4
SSM_CHUNK = 256
GROUP_CH = D_INNER // SSM_GROUPS
SSD_GROUPS_PER_STEP = 2
D_XBC = D_INNER + 2 * SSM_GROUPS * SSM_STATE
ATTN_HEADS = 16
ATTN_HEAD_DIM = 128
D_ATTN = ATTN_HEADS * ATTN_HEAD_DIM
MOBA_BLOCK = 256
MOBA_TOPK = 3
REL_BUCKETS = 32
REL_MAX_DIST = 128
D_FF = 5632
FFN_CONV = 3
EPS = 1e-6
LOG2E = 1.4426950408889634

LANES = 128
SUBLANES = 8
D_PROJ = D_XBC + 3 * D_ATTN + 2 * D_INNER
COL_Z = 0
COL_XBC = COL_Z + D_INNER
COL_Q = COL_XBC + D_XBC
COL_K = COL_Q + D_ATTN
COL_V = COL_K + D_ATTN
COL_G = COL_V + D_ATTN
VMEM_LIMIT = 56 * 1024 * 1024


def _cparams(sem):
    return pltpu.CompilerParams(dimension_semantics=sem, vmem_limit_bytes=VMEM_LIMIT)


def _split3(a):
    hi = a.astype(BF16)
    r = a - hi.astype(F32)
    mid = r.astype(BF16)
    lo = (r - mid.astype(F32)).astype(BF16)
    return hi, mid, lo


def _dot(a, b):
    return jnp.dot(a, b, preferred_element_type=F32)


def _dot_nt(a, b):
    return lax.dot_general(a, b, (((1,), (1,)), ((), ())), preferred_element_type=F32)


def _sigmoid(x):
    return 0.5 + 0.5 * jnp.tanh(0.5 * x)


def _silu(x):
    h = 0.5 * x
    return h + h * jnp.tanh(h)


def _norm_dt_kernel(x_ref, nw_ref, wdt_ref, h_ref, dt_ref):
    x = x_ref[...]
    ms = jnp.mean(x * x, axis=-1, keepdims=True)
    h = x * lax.rsqrt(ms + EPS) * nw_ref[...]
    h_ref[...] = h.astype(BF16)
    h_hi, h_mid, _ = _split3(h)
    w_hi, w_mid, _ = _split3(wdt_ref[...])
    dt_ref[...] = _dot_nt(h_hi, w_hi) + _dot_nt(h_hi, w_mid) + _dot_nt(h_mid, w_hi)


def _norm_dt(x, norm_w, wt_dt, tm=512):
    S = x.shape[0]
    tm = min(tm, S)
    return pl.pallas_call(
        _norm_dt_kernel,
        grid=(S // tm,),
        in_specs=[
            pl.BlockSpec((tm, D_MODEL), lambda m: (m, 0)),
            pl.BlockSpec((1, D_MODEL), lambda m: (0, 0)),
            pl.BlockSpec((LANES, D_MODEL), lambda m: (0, 0)),
        ],
        out_specs=[
            pl.BlockSpec((tm, D_MODEL), lambda m: (m, 0)),
            pl.BlockSpec((tm, LANES), lambda m: (m, 0)),
        ],
        out_shape=[
            jax.ShapeDtypeStruct((S, D_MODEL), BF16),
            jax.ShapeDtypeStruct((S, LANES), F32),
        ],
        compiler_params=_cparams(("arbitrary",)),
        name="norm_dt",
    )(x, norm_w, wt_dt)


def _in_proj_kernel(h_ref, w32_ref, o_ref, w_scr):
    @pl.when(pl.program_id(1) == 0)
    def _():
        w_scr[...] = w32_ref[...].astype(BF16)

    o_ref[...] = _dot_nt(h_ref[...], w_scr[...]).astype(o_ref.dtype)


def _in_proj(h, wt, dt_row0, tm=1024, tn=1024):
    S = h.shape[0]
    tm = min(tm, S)
    assert dt_row0 % tn == 0 and SSM_HEADS % SUBLANES == 0
    na = dt_row0 // tn

    def w_rows(n, m):
        return (pl.multiple_of(n * tn + jnp.where(n >= na, SSM_HEADS, 0), SSM_HEADS), 0)

    return pl.pallas_call(
        _in_proj_kernel,
        grid=(D_PROJ // tn, S // tm),
        in_specs=[
            pl.BlockSpec((tm, D_MODEL), lambda n, m: (m, 0)),
            pl.BlockSpec((pl.Element(tn), pl.Element(D_MODEL)), w_rows),
        ],
        out_specs=pl.BlockSpec((tm, tn), lambda n, m: (m, n)),
        out_shape=jax.ShapeDtypeStruct((S, D_PROJ), BF16),
        scratch_shapes=[pltpu.VMEM((tn, D_MODEL), BF16)],
        compiler_params=_cparams(("arbitrary", "arbitrary")),
        name="in_proj",
    )(h, wt)


def _conv_silu(raw_ref, tail_ref, pad_ref, w_ref, b_ref, g, taps):
    L = raw_ref.shape[0]
    pad_ref[0:SUBLANES, :] = tail_ref[g]
    pad_ref[SUBLANES:SUBLANES + L, :] = raw_ref[...].astype(F32)
    tail_ref[g] = pad_ref[L:L + SUBLANES, :]
    acc = b_ref[...]
    for k in range(taps):
        off = SUBLANES - (taps - 1) + k
        acc = acc + w_ref[k:k + 1, :] * pad_ref[off:off + L, :]
    return _silu(acc)


def _ssd_kernel(x_ref, b_ref, c_ref, z_ref, dtr_ref, cwx_ref, cwb_ref, cwc_ref,
                cbx_ref, cbb_ref, cbc_ref, dtb_ref, alog_ref, dsk_ref, nw_ref,
                o_ref,
                tailx, tailb, tailc, padx, padb, padc, acs_scr, acst_scr, lbt_scr, ht_scr, rhs_scr):
    c = pl.program_id(0)
    gp = pl.program_id(1)
    L = SSM_CHUNK
    QW = 4 * SSM_HEAD_DIM
    GPS = SSD_GROUPS_PER_STEP

    @pl.when(c == 0)
    def _():
        for gg in range(GPS):
            g0 = gp * GPS + gg
            tailx[g0] = jnp.zeros(tailx.shape[1:], F32)
            tailb[g0] = jnp.zeros(tailb.shape[1:], F32)
            tailc[g0] = jnp.zeros(tailc.shape[1:], F32)
            ht_scr[g0] = jnp.zeros(ht_scr.shape[1:], F32)
            rhs_scr[gg] = jnp.zeros(rhs_scr.shape[1:], BF16)

    @pl.when(gp == 0)
    def _():
        t = dtr_ref[...] + dtb_ref[...]
        dt = jnp.maximum(t, 0.0) + jnp.log(1.0 + jnp.exp(-jnp.abs(t)))
        a = -jnp.exp(alog_ref[...])
        da = dt * a
        ri = lax.broadcasted_iota(jnp.int32, (L, L), 0)
        ci = lax.broadcasted_iota(jnp.int32, (L, L), 1)
        tri = jnp.where(ri >= ci, 1.0, 0.0).astype(BF16)
        hi, mid, lo = _split3(da)
        a_cs = _dot(tri, hi) + _dot(tri, mid) + _dot(tri, lo)
        a2 = a_cs * LOG2E
        a2t = a2.T
        acs_scr[...] = a2
        acst_scr[...] = a2t
        lbt_scr[...] = a2t - jnp.log(dt.T) * LOG2E

    for gg in range(GPS):
        xs = slice(gg * GROUP_CH, (gg + 1) * GROUP_CH)
        ns = slice(gg * SSM_STATE, (gg + 1) * SSM_STATE)
        _ssd_group(gp * GPS + gg,
                   x_ref.at[:, xs], b_ref.at[:, ns], c_ref.at[:, ns], z_ref.at[:, xs],
                   cwx_ref.at[:, xs], cwb_ref.at[:, ns], cwc_ref.at[:, ns],
                   cbx_ref.at[:, xs], cbb_ref.at[:, ns], cbc_ref.at[:, ns],
                   dsk_ref.at[:, xs], nw_ref.at[:, xs], o_ref.at[:, xs],
                   tailx, tailb, tailc, padx.at[gg], padb.at[gg], padc.at[gg],
                   acs_scr, acst_scr, lbt_scr, ht_scr,
                   [rhs_scr.at[gg, q] for q in range(GROUP_CH // QW)])


def _ssd_group(g, x_ref, b_ref, c_ref, z_ref, cwx_ref, cwb_ref, cwc_ref, cbx_ref, cbb_ref, cbc_ref,
               dsk_ref, nw_ref, o_ref, tailx, tailb, tailc, padx, padb, padc,
               acs_scr, acst_scr, lbt_scr, ht_scr, rhs_refs):
    L = SSM_CHUNK
    QW = 4 * SSM_HEAD_DIM
    xa = _conv_silu(x_ref, tailx, padx, cwx_ref, cbx_ref, g, SSM_CONV)
    ba = _conv_silu(b_ref, tailb, padb, cwb_ref, cbb_ref, g, SSM_CONV)
    ca = _conv_silu(c_ref, tailc, padc, cwc_ref, cbc_ref, g, SSM_CONV)

    xb = xa.astype(BF16)
    bt = ba.T
    cb = _dot(ca.astype(BF16), bt.astype(BF16))
    a_cs = acs_scr[...]
    ri = lax.broadcasted_iota(jnp.int32, (L, L), 0)
    ci = lax.broadcasted_iota(jnp.int32, (L, L), 1)
    causal = ri >= ci
    lane_h = lax.broadcasted_iota(jnp.int32, (L, LANES), 1)
    lane_q = lax.broadcasted_iota(jnp.int32, (1, QW), 1)

    NH = QW // SSM_HEAD_DIM
    ys = []
    for q in range(GROUP_CH // QW):
        rhs = rhs_refs[q]
        xq = xb[:, q * QW:(q + 1) * QW]
        hq = ht_scr[g, :, q * QW:(q + 1) * QW]
        hqb = hq.astype(BF16)
        mhs, ces, bws = [], [], []
        cdq = jnp.zeros((1, QW), F32)
        for j in range(NH):
            h = g * SSM_HEADS_PER_GROUP + q * NH + j
            cols = slice(j * SSM_HEAD_DIM, (j + 1) * SSM_HEAD_DIM)
            rhs[j * L:(j + 1) * L, cols] = xq[:, cols]
            rhs[NH * L + j * SSM_STATE:NH * L + (j + 1) * SSM_STATE, cols] = hqb[:, cols]
            a_row = acst_scr[pl.ds(h, 1), :]
            b_row = lbt_scr[pl.ds(h, 1), :]
            a_col = jnp.sum(jnp.where(lane_h == h, a_cs, 0.0), axis=1, keepdims=True)
            dmat = jnp.where(causal, a_col - b_row, -jnp.inf)
            mhs.append((cb * jnp.exp2(dmat)).astype(BF16))
            ces.append((ca * jnp.exp2(a_col)).astype(BF16))
            a_last = a_row[:, L - 1:L]
            w_row = jnp.exp2(a_last - b_row)
            bws.append((bt * w_row).astype(BF16))
            lm = (lane_q >= j * SSM_HEAD_DIM) & (lane_q < (j + 1) * SSM_HEAD_DIM)
            cdq = jnp.where(lm, jnp.exp2(a_last), cdq)
        acc = _dot(jnp.concatenate(mhs + ces, axis=1), rhs[...])
        st = _dot(jnp.concatenate(bws, axis=1), rhs[0:NH * L, :])
        ht_scr[g, :, q * QW:(q + 1) * QW] = hq * cdq + st
        ys.append(acc + xa[:, q * QW:(q + 1) * QW] * dsk_ref[:, q * QW:(q + 1) * QW])

    z = z_ref[...].astype(F32)
    sz = _silu(z)
    ygs = [ys[q] * sz[:, q * QW:(q + 1) * QW] for q in range(len(ys))]
    ss = sum(jnp.sum(v * v, axis=1, keepdims=True) for v in ygs)
    scale = lax.rsqrt(ss * (1.0 / GROUP_CH) + EPS)
    for q in range(len(ys)):
        o_ref[:, q * QW:(q + 1) * QW] = (
            ygs[q] * scale * nw_ref[:, q * QW:(q + 1) * QW]).astype(o_ref.dtype)


def _ssd(proj, dt_raw, conv_w, conv_b, dt_bias, a_log, d_exp, norm_w):
    S = proj.shape[0]
    L, G, N = SSM_CHUNK, SSM_GROUPS, SSM_STATE
    GPS = SSD_GROUPS_PER_STEP
    XW, NW = GPS * GROUP_CH, GPS * N
    bcol = D_INNER // NW
    ccol = bcol + G // GPS
    pxcol = COL_XBC // XW
    pbcol = COL_XBC // NW + bcol
    pccol = pbcol + G // GPS
    zcol = COL_Z // XW
    return pl.pallas_call(
        _ssd_kernel,
        grid=(S // L, G // GPS),
        in_specs=[
            pl.BlockSpec((L, XW), lambda c, g: (c, pxcol + g)),
            pl.BlockSpec((L, NW), lambda c, g: (c, pbcol + g)),
            pl.BlockSpec((L, NW), lambda c, g: (c, pccol + g)),
            pl.BlockSpec((L, XW), lambda c, g: (c, zcol + g)),
            pl.BlockSpec((L, LANES), lambda c, g: (c, 0)),
            pl.BlockSpec((SSM_CONV, XW), lambda c, g: (0, g)),
            pl.BlockSpec((SSM_CONV, NW), lambda c, g: (0, bcol + g)),
            pl.BlockSpec((SSM_CONV, NW), lambda c, g: (0, ccol + g)),
            pl.BlockSpec((1, XW), lambda c, g: (0, g)),
            pl.BlockSpec((1, NW), lambda c, g: (0, bcol + g)),
            pl.BlockSpec((1, NW), lambda c, g: (0, ccol + g)),
            pl.BlockSpec((1, LANES), lambda c, g: (0, 0)),
            pl.BlockSpec((1, LANES), lambda c, g: (0, 0)),
            pl.BlockSpec((1, XW), lambda c, g: (0, g)),
            pl.BlockSpec((1, XW), lambda c, g: (0, g)),
        ],
        out_specs=pl.BlockSpec((L, XW), lambda c, g: (c, g)),
        out_shape=jax.ShapeDtypeStruct((S, D_INNER), BF16),
        scratch_shapes=[
            pltpu.VMEM((G, SUBLANES, GROUP_CH), F32),
            pltpu.VMEM((G, SUBLANES, N), F32),
            pltpu.VMEM((G, SUBLANES, N), F32),
            pltpu.VMEM((GPS, L + SUBLANES, GROUP_CH), F32),
            pltpu.VMEM((GPS, L + SUBLANES, N), F32),
            pltpu.VMEM((GPS, L + SUBLANES, N), F32),
            pltpu.VMEM((L, LANES), F32),
            pltpu.VMEM((LANES, L), F32),
            pltpu.VMEM((LANES, L), F32),
            pltpu.VMEM((G, N, GROUP_CH), F32),
            pltpu.VMEM((GPS, GROUP_CH // (4 * SSM_HEAD_DIM), 4 * (L + N), 4 * SSM_HEAD_DIM), BF16),
        ],
        compiler_params=_cparams(("arbitrary", "arbitrary")),
        name="ssd",
    )(proj, proj, proj, proj, dt_raw, conv_w, conv_w, conv_w, conv_b, conv_b, conv_b,
      dt_bias, a_log, d_exp, norm_w)


def _qk_prep_kernel(q_ref, k_ref, v_ref, qw_ref, kw_ref, qt_ref, kn_ref, vt_ref, pen_ref, kbar_scr):
    S = q_ref.shape[0]
    BS = MOBA_BLOCK
    nb = S // BS
    nbp = kbar_scr.shape[0]
    kbar_scr[...] = jnp.zeros(kbar_scr.shape, F32)

    def kloop(b, carry):
        rows = pl.ds(pl.multiple_of(b * BS, BS), BS)
        k = k_ref[rows, :].astype(F32)
        kn = k * lax.rsqrt(jnp.mean(k * k, axis=-1, keepdims=True) + EPS) * kw_ref[...]
        kn_ref[0, rows, :] = kn.astype(BF16)
        kbar_scr[pl.ds(b, 1), :] = jnp.mean(kn, axis=0, keepdims=True)
        vt_ref[0, b, 0:ATTN_HEAD_DIM, :] = v_ref[rows, :].astype(F32).T.astype(BF16)
        vt_ref[0, b, ATTN_HEAD_DIM:, :] = jnp.ones((MOBA_VT_ROWS - ATTN_HEAD_DIM, BS), BF16)
        return carry

    lax.fori_loop(0, nb, kloop, 0, unroll=4)
    kb_hi, kb_mid, _ = _split3(kbar_scr[...])

    def qloop(i, carry):
        rows = pl.ds(pl.multiple_of(i * BS, BS), BS)
        q = q_ref[rows, :].astype(F32)
        qn = q * lax.rsqrt(jnp.mean(q * q, axis=-1, keepdims=True) + EPS) * qw_ref[...]
        qnt = (qn * (ATTN_HEAD_DIM ** -0.5)).T
        qt_ref[0, i] = (qnt * LOG2E).astype(BF16)
        q_hi, q_mid, _ = _split3(qnt)
        gate = _dot(kb_hi, q_hi) + _dot(kb_mid, q_hi) + _dot(kb_hi, q_mid)
        blk = lax.broadcasted_iota(jnp.int32, gate.shape, 0)
        blk_f = blk.astype(F32)
        gate = jnp.where(blk < i, gate, -jnp.inf)
        pen = jnp.full(gate.shape, -jnp.inf, F32)
        for _ in range(MOBA_TOPK):
            mx = jnp.max(gate, axis=0, keepdims=True)
            cand = jnp.where((gate == mx) & (mx > -jnp.inf), blk_f, float(nbp))
            idx = jnp.min(cand, axis=0, keepdims=True)
            hit = blk_f == idx
            pen = jnp.where(hit, 0.0, pen)
            gate = jnp.where(hit, -jnp.inf, gate)
        pen_ref[0, i] = pen
        return carry

    lax.fori_loop(0, nb, qloop, 0, unroll=8)


def _qk_prep(proj, q_norm_w, k_norm_w):
    S = proj.shape[0]
    H, D, BS = ATTN_HEADS, ATTN_HEAD_DIM, MOBA_BLOCK
    nb = S // BS
    nbp = -(-nb // SUBLANES) * SUBLANES
    return pl.pallas_call(
        _qk_prep_kernel,
        grid=(H,),
        in_specs=[
            pl.BlockSpec((S, D), lambda h: (0, COL_Q // D + h)),
            pl.BlockSpec((S, D), lambda h: (0, COL_K // D + h)),
            pl.BlockSpec((S, D), lambda h: (0, COL_V // D + h)),
            pl.BlockSpec((1, D), lambda h: (0, 0)),
            pl.BlockSpec((1, D), lambda h: (0, 0)),
        ],
        out_specs=[
            pl.BlockSpec((1, nb, D, BS), lambda h: (h, 0, 0, 0)),
            pl.BlockSpec((1, S, D), lambda h: (h, 0, 0)),
            pl.BlockSpec((1, nb, MOBA_VT_ROWS, BS), lambda h: (h, 0, 0, 0)),
            pl.BlockSpec((1, nb, nbp, BS), lambda h: (h, 0, 0, 0)),
        ],
        out_shape=[
            jax.ShapeDtypeStruct((H, nb, D, BS), BF16),
            jax.ShapeDtypeStruct((H, S, D), BF16),
            jax.ShapeDtypeStruct((H, nb, MOBA_VT_ROWS, BS), BF16),
            jax.ShapeDtypeStruct((H, nb, nbp, BS), F32),
        ],
        scratch_shapes=[pltpu.VMEM((nbp, D), F32)],
        compiler_params=_cparams(("arbitrary",)),
        name="qk_prep",
    )(proj, proj, proj, q_norm_w, k_norm_w)


def _rel_bucket(dist):
    n = jnp.maximum(dist, 0)
    max_exact = REL_BUCKETS // 2
    nf = jnp.maximum(n, 1).astype(F32)
    large = max_exact + (jnp.log(nf / max_exact) / math.log(REL_MAX_DIST / max_exact)
                         * (REL_BUCKETS - max_exact)).astype(jnp.int32)
    large = jnp.minimum(large, REL_BUCKETS - 1)
    return jnp.where(n < max_exact, n, large)


MOBA_HEADS_PER_STEP = 4
MOBA_VT_ROWS = ATTN_HEAD_DIM + 2 * SUBLANES


def _moba_kernel(tbl_ref, qt_ref, k_ref, vt_ref, pen_ref, o_ref,
                 bown_scr, bprev_scr, m_scr, acc_scr, s0_scr, s1_scr, s2_scr, s3_scr,
                 p0_scr, p1_scr, p2_scr, p3_scr, al0_scr, al1_scr, al2_scr, al3_scr):
    hp = pl.program_id(0)
    i = pl.program_id(1)
    BS, HB = MOBA_BLOCK, MOBA_HEADS_PER_STEP
    D = ATTN_HEAD_DIM

    @pl.when(i == 0)
    def _():
        kl = lax.broadcasted_iota(jnp.int32, (BS, BS), 0)
        ql = lax.broadcasted_iota(jnp.int32, (BS, BS), 1)
        d = ql - kl
        b_own = _rel_bucket(d)
        b_prev = _rel_bucket(d + BS)
        for a in range(HB):
            h = hp * HB + a
            far = tbl_ref[h, REL_BUCKETS - 1]

            def lookup(bucket):
                val = jnp.zeros((BS, BS), F32)
                for b in range(REL_BUCKETS):
                    val = jnp.where(bucket == b, tbl_ref[h, b], val)
                return (val - far) * LOG2E

            bown_scr[a] = jnp.where(d >= 0, lookup(b_own), -jnp.inf)
            bprev_scr[a] = lookup(b_prev)

    def krows(a, j):
        return k_ref[a, pl.ds(pl.multiple_of(j * BS, BS), BS), :]

    jp = jnp.maximum(i - 1, 0)
    no_prev = jnp.where(i == 0, -jnp.inf, 0.0)
    n_far = jnp.maximum(i - 1, 0)
    nb = k_ref.shape[1] // BS

    def clampj(j):
        return jnp.clip(j, 0, nb - 1)

    def scores(j, s_slot):
        for a in range(HB):
            s_slot[a] = _dot(krows(a, clampj(j)), qt_ref[a, 0])

    def softmax(j, s_slot, p_slot, al_slot, mask_tail, bias_scr=None):
        for a in range(HB):
            pr = pen_ref[a, 0, pl.ds(clampj(j), 1), :]
            if mask_tail is True:
                pr = pr + jnp.where(j >= n_far, -jnp.inf, 0.0)
            elif mask_tail is not False:
                pr = pr + mask_tail
            s = s_slot[a]
            if bias_scr is not None:
                s = s + bias_scr[a]
            m_old = m_scr[a]
            m_new = jnp.maximum(m_old, jnp.max(s, axis=0, keepdims=True) + pr)
            p_slot[a] = jnp.exp2(s - (m_new - pr)).astype(BF16)
            al_slot[a] = jnp.exp2(m_old - m_new)
            m_scr[a] = m_new

    def accumulate(j, p_slot, al_slot, maybe_prev=False):
        jv = clampj(j)
        if maybe_prev:
            jv = jnp.where(j < 0, jp, jv)
        for a in range(HB):
            acc_scr[a] = al_slot[a] * acc_scr[a] + _dot(vt_ref[a, jv], p_slot[a])

    s_slots = (s0_scr, s1_scr, s2_scr, s3_scr)
    p_slots = (p0_scr, p1_scr, p2_scr, p3_scr)
    al_slots = (al0_scr, al1_scr, al2_scr, al3_scr)
    scores(i, s2_scr)
    scores(jp, s3_scr)
    scores(0, s0_scr)
    for a in range(HB):
        s = s2_scr[a] + bown_scr[a]
        m0 = jnp.max(s, axis=0, keepdims=True)
        m_scr[a] = m0
        p2_scr[a] = jnp.exp2(s - m0).astype(BF16)
    for a in range(HB):
        acc_scr[a] = _dot(vt_ref[a, i], p2_scr[a])
    softmax(jp, s3_scr, p3_scr, al3_scr, no_prev, bias_scr=bprev_scr)

    def quad_loop(g, carry):
        j = 4 * g
        for t in range(4):
            scores(j + t + 1, s_slots[(t + 1) % 4])
            accumulate(j + t - 1, p_slots[(t - 1) % 4], al_slots[(t - 1) % 4], maybe_prev=(t == 0))
            softmax(j + t, s_slots[t], p_slots[t], al_slots[t], False)
        return carry

    n_quads = n_far // 4
    lax.fori_loop(0, n_quads, quad_loop, 0)

    def pair_loop(g, carry):
        j = 4 * n_quads + 2 * g
        scores(j + 1, s1_scr)
        accumulate(j - 1, p3_scr, al3_scr, maybe_prev=True)
        softmax(j, s0_scr, p2_scr, al2_scr, False)
        scores(j + 2, s0_scr)
        accumulate(j, p2_scr, al2_scr)
        softmax(j + 1, s1_scr, p3_scr, al3_scr, True)
        return carry

    n_pairs = (n_far - 4 * n_quads + 1) // 2
    lax.fori_loop(0, n_pairs, pair_loop, 0)
    accumulate(4 * n_quads + 2 * n_pairs - 1, p3_scr, al3_scr, maybe_prev=True)
    for a in range(HB):
        acc = acc_scr[a]
        o_ref[:, a * D:(a + 1) * D] = (acc[0:D, :] / acc[D:D + 1, :]).T.astype(o_ref.dtype)


def _moba(qt, kn, vt, pen, table_h):
    H, nb, D, BS = qt.shape
    S = nb * BS
    nbp = pen.shape[2]
    HB = MOBA_HEADS_PER_STEP
    return pl.pallas_call(
        _moba_kernel,
        grid=(H // HB, nb),
        in_specs=[
            pl.BlockSpec(memory_space=pltpu.SMEM),
            pl.BlockSpec((HB, 1, D, BS), lambda hp, i: (hp, i, 0, 0)),
            pl.BlockSpec((HB, S, D), lambda hp, i: (hp, 0, 0)),
            pl.BlockSpec((HB, nb, MOBA_VT_ROWS, BS), lambda hp, i: (hp, 0, 0, 0)),
            pl.BlockSpec((HB, 1, nbp, BS), lambda hp, i: (hp, i, 0, 0)),
        ],
        out_specs=pl.BlockSpec((BS, HB * D), lambda hp, i: (i, hp)),
        out_shape=jax.ShapeDtypeStruct((S, H * D), BF16),
        scratch_shapes=[
            pltpu.VMEM((HB, BS, BS), F32),
            pltpu.VMEM((HB, BS, BS), F32),
            pltpu.VMEM((HB, 1, BS), F32),
            pltpu.VMEM((HB, MOBA_VT_ROWS, BS), F32),
        ] + [pltpu.VMEM((HB, BS, BS), F32)] * 4 + [pltpu.VMEM((HB, BS, BS), BF16)] * 4
        + [pltpu.VMEM((HB, 1, BS), F32)] * 4,
        compiler_params=_cparams(("arbitrary", "arbitrary")),
        name="moba",
    )(table_h, qt, kn, vt, pen)


def _mix_kernel(ys_ref, ya_ref, ws_ref, wa_ref, gs_ref, ga_ref, bs_ref, ba_ref, o_ref):
    gs = _sigmoid(gs_ref[...].astype(F32) + bs_ref[...])
    ga = _sigmoid(ga_ref[...].astype(F32) + ba_ref[...])
    o_ref[...] = (gs * _dot(ys_ref[...], ws_ref[...])
                  + ga * _dot(ya_ref[...], wa_ref[...])).astype(o_ref.dtype)


def _mix(y_ssd, y_attn, w_ssm_out, w_attn_out, proj, b_gate, tm=1024, tn=512):
    S = y_ssd.shape[0]
    tm = min(tm, S)
    gs_col = COL_G // tn
    ga_col = (COL_G + D_MODEL) // tn
    nb = D_MODEL // tn
    return pl.pallas_call(
        _mix_kernel,
        grid=(S // tm, D_MODEL // tn),
        in_specs=[
            pl.BlockSpec((tm, D_INNER), lambda m, n: (m, 0)),
            pl.BlockSpec((tm, D_ATTN), lambda m, n: (m, 0)),
            pl.BlockSpec((D_INNER, tn), lambda m, n: (0, n)),
            pl.BlockSpec((D_ATTN, tn), lambda m, n: (0, n)),
            pl.BlockSpec((tm, tn), lambda m, n: (m, gs_col + n)),
            pl.BlockSpec((tm, tn), lambda m, n: (m, ga_col + n)),
            pl.BlockSpec((1, tn), lambda m, n: (0, n)),
            pl.BlockSpec((1, tn), lambda m, n: (0, nb + n)),
        ],
        out_specs=pl.BlockSpec((tm, tn), lambda m, n: (m, n)),
        out_shape=jax.ShapeDtypeStruct((S, D_MODEL), BF16),
        compiler_params=_cparams(("arbitrary", "arbitrary")),
        name="mix",
    )(y_ssd, y_attn, w_ssm_out, w_attn_out, proj, proj, b_gate, b_gate)


def _out_norm_kernel(x_ref, mix_ref, w_ref, nw_ref, x1_ref, h2_ref):
    x1 = x_ref[...] + _dot(mix_ref[...], w_ref[...])
    x1_ref[...] = x1
    ms = jnp.mean(x1 * x1, axis=-1, keepdims=True)
    h2_ref[...] = (x1 * lax.rsqrt(ms + EPS) * nw_ref[...]).astype(h2_ref.dtype)


def _out_norm(x, mix, w_out, ffn_norm_w, tm=512):
    S = x.shape[0]
    tm = min(tm, S)
    return pl.pallas_call(
        _out_norm_kernel,
        grid=(S // tm,),
        in_specs=[
            pl.BlockSpec((tm, D_MODEL), lambda m: (m, 0)),
            pl.BlockSpec((tm, D_MODEL), lambda m: (m, 0)),
            pl.BlockSpec((D_MODEL, D_MODEL), lambda m: (0, 0)),
            pl.BlockSpec((1, D_MODEL), lambda m: (0, 0)),
        ],
        out_specs=[
            pl.BlockSpec((tm, D_MODEL), lambda m: (m, 0)),
            pl.BlockSpec((tm, D_MODEL), lambda m: (m, 0)),
        ],
        out_shape=[
            jax.ShapeDtypeStruct((S, D_MODEL), F32),
            jax.ShapeDtypeStruct((S, D_MODEL), BF16),
        ],
        compiler_params=_cparams(("arbitrary",)),
        name="out_norm",
    )(x, mix, w_out, ffn_norm_w)


def _ffn_up_kernel(h_ref, wg32_ref, wu32_ref, cwg_ref, cwu_ref, cbg_ref, cbu_ref, o_ref,
                   tailg, tailu, padg, padu, wg_ref, wu_ref):
    tm = h_ref.shape[0]

    @pl.when(pl.program_id(1) == 0)
    def _():
        tailg[...] = jnp.zeros(tailg.shape, F32)
        tailu[...] = jnp.zeros(tailu.shape, F32)
        wg_ref[...] = wg32_ref[...].astype(BF16)
        wu_ref[...] = wu32_ref[...].astype(BF16)

    def conv(w_ref, cw_ref, cb_ref, tail, pad):
        pad[0:SUBLANES, :] = tail[...]
        pad[SUBLANES:SUBLANES + tm, :] = _dot(h_ref[...], w_ref[...])
        tail[...] = pad[tm:tm + SUBLANES, :]
        acc = cb_ref[...]
        for k in range(FFN_CONV):
            off = SUBLANES - (FFN_CONV - 1) + k
            acc = acc + cw_ref[k:k + 1, :] * pad[off:off + tm, :]
        return acc

    ug = conv(wg_ref, cwg_ref, cbg_ref, tailg, padg)
    uu = conv(wu_ref, cwu_ref, cbu_ref, tailu, padu)
    o_ref[...] = (_silu(ug) * uu).astype(o_ref.dtype)


def _ffn_up(h2, w_up, conv_w, conv_b, tm=1024, tn=512):
    S = h2.shape[0]
    tm = min(tm, S)
    nb = D_FF // tn
    return pl.pallas_call(
        _ffn_up_kernel,
        grid=(nb, S // tm),
        in_specs=[
            pl.BlockSpec((tm, D_MODEL), lambda n, m: (m, 0)),
            pl.BlockSpec((D_MODEL, tn), lambda n, m: (0, n)),
            pl.BlockSpec((D_MODEL, tn), lambda n, m: (0, nb + n)),
            pl.BlockSpec((FFN_CONV, tn), lambda n, m: (0, n)),
            pl.BlockSpec((FFN_CONV, tn), lambda n, m: (0, nb + n)),
            pl.BlockSpec((1, tn), lambda n, m: (0, n)),
            pl.BlockSpec((1, tn), lambda n, m: (0, nb + n)),
        ],
        out_specs=pl.BlockSpec((tm, tn), lambda n, m: (m, n)),
        out_shape=jax.ShapeDtypeStruct((S, D_FF), BF16),
        scratch_shapes=[
            pltpu.VMEM((SUBLANES, tn), F32),
            pltpu.VMEM((SUBLANES, tn), F32),
            pltpu.VMEM((tm + SUBLANES, tn), F32),
            pltpu.VMEM((tm + SUBLANES, tn), F32),
            pltpu.VMEM((D_MODEL, tn), BF16),
            pltpu.VMEM((D_MODEL, tn), BF16),
        ],
        compiler_params=_cparams(("arbitrary", "arbitrary")),
        name="ffn_up",
    )(h2, w_up, w_up, conv_w, conv_w, conv_b, conv_b)


def _ffn_down_kernel(a_ref, w_ref, x1_ref, o_ref):
    o_ref[...] = x1_ref[...] + _dot(a_ref[...], w_ref[...])


def _ffn_down(act, w_down, x1, tm=1024, tn=512):
    S = act.shape[0]
    tm = min(tm, S)
    return pl.pallas_call(
        _ffn_down_kernel,
        grid=(S // tm, D_MODEL // tn),
        in_specs=[
            pl.BlockSpec((tm, D_FF), lambda m, n: (m, 0)),
            pl.BlockSpec((D_FF, tn), lambda m, n: (0, n)),
            pl.BlockSpec((tm, tn), lambda m, n: (m, n)),
        ],
        out_specs=pl.BlockSpec((tm, tn), lambda m, n: (m, n)),
        out_shape=jax.ShapeDtypeStruct((S, D_MODEL), F32),
        compiler_params=_cparams(("arbitrary", "arbitrary")),
        name="ffn_down",
    )(act, w_down, x1)


def _pad_lanes(v):
    return jnp.pad(v, ((0, 0), (0, LANES - v.shape[1])))


def _layer(x, attn_norm_w, w_in, b_gate, ssm_conv_w, ssm_conv_b, ssm_dt_bias, ssm_a_log, ssm_d,
           ssm_norm_w, q_norm_w, k_norm_w, rel_bias, w_ssm_out, w_attn_out, w_out, ffn_norm_w,
           w_up, ffn_conv_w, ffn_conv_b, w_down):
    o_z, o_xbc = 0, D_INNER
    o_dt = o_xbc + D_XBC
    o_q = o_dt + SSM_HEADS
    o_g = o_q + 3 * D_ATTN
    wt = w_in.T
    wt_dt = jnp.pad(wt[o_dt:o_q], ((0, LANES - SSM_HEADS), (0, 0)))

    h, dt_raw = _norm_dt(x, attn_norm_w[None, :], wt_dt)
    proj = _in_proj(h, wt, o_dt)
    y_ssd = _ssd(proj, dt_raw, ssm_conv_w, ssm_conv_b[None, :], _pad_lanes(ssm_dt_bias[None, :]),
                 _pad_lanes(ssm_a_log[None, :]), jnp.repeat(ssm_d, SSM_HEAD_DIM)[None, :],
                 ssm_norm_w[None, :])
    qt, kn, vt, pen = _qk_prep(proj, q_norm_w[None, :], k_norm_w[None, :])
    y_attn = _moba(qt, kn, vt, pen, rel_bias.T)
    mix = _mix(y_ssd, y_attn, w_ssm_out.astype(BF16), w_attn_out.astype(BF16), proj, b_gate[None, :])
    x1, h2 = _out_norm(x, mix, w_out.astype(BF16), ffn_norm_w[None, :])
    act = _ffn_up(h2, w_up, ffn_conv_w, ffn_conv_b[None, :])
    return _ffn_down(act, w_down.astype(BF16), x1)


def kernel(x, attn_norm_w, w_in, b_gate, ssm_conv_w, ssm_conv_b, ssm_dt_bias, ssm_a_log, ssm_d,
           ssm_norm_w, q_norm_w, k_norm_w, rel_bias, w_ssm_out, w_attn_out, w_out, ffn_norm_w,
           w_up, ffn_conv_w, ffn_conv_b, w_down):
    assert x.shape[0] == 1 and attn_norm_w.shape[0] == 1
    out = _layer(x[0], attn_norm_w[0], w_in[0], b_gate[0], ssm_conv_w[0], ssm_conv_b[0],
                 ssm_dt_bias[0], ssm_a_log[0], ssm_d[0], ssm_norm_w[0], q_norm_w[0], k_norm_w[0],
                 rel_bias, w_ssm_out[0], w_attn_out[0], w_out[0], ffn_norm_w[0], w_up[0],
                 ffn_conv_w[0], ffn_conv_b[0], w_down[0])
    return out[None]
```

```python
import functools
import math

import jax
import jax.numpy as jnp
from jax import lax
from jax.experimental import pallas as pl
from jax.experimental.pallas import tpu as pltpu

F32 = jnp.float32
BF16 = jnp.bfloat16

D_MODEL = 2048
D_INNER = 4096
SSM_HEADS = 64
SSM_HEAD_DIM = 64
SSM_GROUPS = 8
SSM_HEADS_PER_GROUP = SSM_HEADS // SSM_GROUPS
SSM_STATE = 128
SSM_CONV = 4
SSM_CHUNK = 256
GROUP_CH = D_INNER // SSM_GROUPS
SSD_GROUPS_PER_STEP = 4
D_XBC = D_INNER + 2 * SSM_GROUPS * SSM_STATE
ATTN_HEADS = 16
ATTN_HEAD_DIM = 128
D_ATTN = ATTN_HEADS * ATTN_HEAD_DIM
MOBA_BLOCK = 256
MOBA_TOPK = 3
REL_BUCKETS = 32
REL_MAX_DIST = 128
D_FF = 5632
FFN_CONV = 3
EPS = 1e-6
LOG2E = 1.4426950408889634

LANES = 128
SUBLANES = 8
D_PROJ = D_XBC + 3 * D_ATTN + 2 * D_INNER
COL_Z = 0
COL_XBC = COL_Z + D_INNER
COL_Q = COL_XBC + D_XBC
COL_K = COL_Q + D_ATTN
COL_V = COL_K + D_ATTN
COL_G = COL_V + D_ATTN
VMEM_LIMIT = 56 * 1024 * 1024


def _cparams(sem):
    return pltpu.CompilerParams(dimension_semantics=sem, vmem_limit_bytes=VMEM_LIMIT)


def _split3(a):
    hi = a.astype(BF16)
    r = a - hi.astype(F32)
    mid = r.astype(BF16)
    lo = (r - mid.astype(F32)).astype(BF16)
    return hi, mid, lo


def _dot(a, b):
    return jnp.dot(a, b, preferred_element_type=F32)


def _dot_nt(a, b):
    return lax.dot_general(a, b, (((1,), (1,)), ((), ())), preferred_element_type=F32)


def _sigmoid(x):
    return 0.5 + 0.5 * jnp.tanh(0.5 * x)


def _silu(x):
    h = 0.5 * x
    return h + h * jnp.tanh(h)


def _norm_dt_kernel(x_ref, nw_ref, wdt_ref, h_ref, dt_ref):
    x = x_ref[...]
    ms = jnp.mean(x * x, axis=-1, keepdims=True)
    h = x * lax.rsqrt(ms + EPS) * nw_ref[...]
    h_ref[...] = h.astype(BF16)
    h_hi, h_mid, _ = _split3(h)
    w_hi, w_mid, _ = _split3(wdt_ref[...])
    dt_ref[...] = _dot_nt(h_hi, w_hi) + _dot_nt(h_hi, w_mid) + _dot_nt(h_mid, w_hi)


def _norm_dt(x, norm_w, wt_dt, tm=512):
    S = x.shape[0]
    tm = min(tm, S)
    return pl.pallas_call(
        _norm_dt_kernel,
        grid=(S // tm,),
        in_specs=[
            pl.BlockSpec((tm, D_MODEL), lambda m: (m, 0)),
            pl.BlockSpec((1, D_MODEL), lambda m: (0, 0)),
            pl.BlockSpec((LANES, D_MODEL), lambda m: (0, 0)),
        ],
        out_specs=[
            pl.BlockSpec((tm, D_MODEL), lambda m: (m, 0)),
            pl.BlockSpec((tm, LANES), lambda m: (m, 0)),
        ],
        out_shape=[
            jax.ShapeDtypeStruct((S, D_MODEL), BF16),
            jax.ShapeDtypeStruct((S, LANES), F32),
        ],
        compiler_params=_cparams(("arbitrary",)),
        name="norm_dt",
    )(x, norm_w, wt_dt)


def _in_proj_kernel(h_ref, w32_ref, o_ref, w_scr):
    @pl.when(pl.program_id(1) == 0)
    def _():
        w_scr[...] = w32_ref[...].astype(BF16)

    o_ref[...] = _dot_nt(h_ref[...], w_scr[...]).astype(o_ref.dtype)


def _in_proj(h, wt, dt_row0, tm=2048, tn=1024):
    S = h.shape[0]
    tm = min(tm, S)
    assert dt_row0 % tn == 0 and SSM_HEADS % SUBLANES == 0
    na = dt_row0 // tn

    def w_rows(n, m):
        return (pl.multiple_of(n * tn + jnp.where(n >= na, SSM_HEADS, 0), SSM_HEADS), 0)

    return pl.pallas_call(
        _in_proj_kernel,
        grid=(D_PROJ // tn, S // tm),
        in_specs=[
            pl.BlockSpec((tm, D_MODEL), lambda n, m: (m, 0)),
            pl.BlockSpec((pl.Element(tn), pl.Element(D_MODEL)), w_rows),
        ],
        out_specs=pl.BlockSpec((tm, tn), lambda n, m: (m, n)),
        out_shape=jax.ShapeDtypeStruct((S, D_PROJ), BF16),
        scratch_shapes=[pltpu.VMEM((tn, D_MODEL), BF16)],
        compiler_params=_cparams(("arbitrary", "arbitrary")),
        name="in_proj",
    )(h, wt)


def _conv_silu(raw_ref, tail_ref, pad_ref, w_ref, b_ref, g, taps):
    L = raw_ref.shape[0]
    pad_ref[0:SUBLANES, :] = tail_ref[g]
    pad_ref[SUBLANES:SUBLANES + L, :] = raw_ref[...].astype(F32)
    tail_ref[g] = pad_ref[L:L + SUBLANES, :]
    acc = b_ref[...]
    for k in range(taps):
        off = SUBLANES - (taps - 1) + k
        acc = acc + w_ref[k:k + 1, :] * pad_ref[off:off + L, :]
    return _silu(acc)


def _ssd_kernel(x_ref, b_ref, c_ref, z_ref, dtr_ref, cwx_ref, cwb_ref, cwc_ref,
                cbx_ref, cbb_ref, cbc_ref, dtb_ref, alog_ref, dsk_ref, nw_ref,
                o_ref,
                tailx, tailb, tailc, padx, padb, padc, acs_scr, acst_scr, lbt_scr, ht_scr, rhs_scr):
    c = pl.program_id(0)
    gp = pl.program_id(1)
    L = SSM_CHUNK
    QW = 4 * SSM_HEAD_DIM
    GPS = SSD_GROUPS_PER_STEP

    @pl.when(c == 0)
    def _():
        for gg in range(GPS):
            g0 = gp * GPS + gg
            tailx[g0] = jnp.zeros(tailx.shape[1:], F32)
            tailb[g0] = jnp.zeros(tailb.shape[1:], F32)
            tailc[g0] = jnp.zeros(tailc.shape[1:], F32)
            ht_scr[g0] = jnp.zeros(ht_scr.shape[1:], F32)
            rhs_scr[gg] = jnp.zeros(rhs_scr.shape[1:], BF16)

    @pl.when(gp == 0)
    def _():
        t = dtr_ref[...] + dtb_ref[...]
        dt = jnp.maximum(t, 0.0) + jnp.log(1.0 + jnp.exp(-jnp.abs(t)))
        a = -jnp.exp(alog_ref[...])
        da = dt * a
        ri = lax.broadcasted_iota(jnp.int32, (L, L), 0)
        ci = lax.broadcasted_iota(jnp.int32, (L, L), 1)
        tri = jnp.where(ri >= ci, 1.0, 0.0).astype(BF16)
        hi, mid, lo = _split3(da)
        a_cs = _dot(tri, hi) + _dot(tri, mid) + _dot(tri, lo)
        a2 = a_cs * LOG2E
        a2t = a2.T
        acs_scr[...] = a2
        acst_scr[...] = a2t
        lbt_scr[...] = a2t - jnp.log(dt.T) * LOG2E

    for gg in range(GPS):
        xs = slice(gg * GROUP_CH, (gg + 1) * GROUP_CH)
        ns = slice(gg * SSM_STATE, (gg + 1) * SSM_STATE)
        _ssd_group(gp * GPS + gg,
                   x_ref.at[:, xs], b_ref.at[:, ns], c_ref.at[:, ns], z_ref.at[:, xs],
                   cwx_ref.at[:, xs], cwb_ref.at[:, ns], cwc_ref.at[:, ns],
                   cbx_ref.at[:, xs], cbb_ref.at[:, ns], cbc_ref.at[:, ns],
                   dsk_ref.at[:, xs], nw_ref.at[:, xs], o_ref.at[:, xs],
                   tailx, tailb, tailc, padx.at[gg], padb.at[gg], padc.at[gg],
                   acs_scr, acst_scr, lbt_scr, ht_scr,
                   [rhs_scr.at[gg, q] for q in range(GROUP_CH // QW)])


def _ssd_group(g, x_ref, b_ref, c_ref, z_ref, cwx_ref, cwb_ref, cwc_ref, cbx_ref, cbb_ref, cbc_ref,
               dsk_ref, nw_ref, o_ref, tailx, tailb, tailc, padx, padb, padc,
               acs_scr, acst_scr, lbt_scr, ht_scr, rhs_refs):
    L = SSM_CHUNK
    QW = 4 * SSM_HEAD_DIM
    xa = _conv_silu(x_ref, tailx, padx, cwx_ref, cbx_ref, g, SSM_CONV)
    ba = _conv_silu(b_ref, tailb, padb, cwb_ref, cbb_ref, g, SSM_CONV)
    ca = _conv_silu(c_ref, tailc, padc, cwc_ref, cbc_ref, g, SSM_CONV)

    xb = xa.astype(BF16)
    bt = ba.T
    cb = _dot(ca.astype(BF16), bt.astype(BF16))
    a_cs = acs_scr[...]
    ri = lax.broadcasted_iota(jnp.int32, (L, L), 0)
    ci = lax.broadcasted_iota(jnp.int32, (L, L), 1)
    causal = ri >= ci
    lane_h = lax.broadcasted_iota(jnp.int32, (L, LANES), 1)
    lane_q = lax.broadcasted_iota(jnp.int32, (1, QW), 1)

    NH = QW // SSM_HEAD_DIM
    ys = []
    for q in range(GROUP_CH // QW):
        rhs = rhs_refs[q]
        xq = xb[:, q * QW:(q + 1) * QW]
        hq = ht_scr[g, :, q * QW:(q + 1) * QW]
        hqb = hq.astype(BF16)
        mhs, ces, bws = [], [], []
        cdq = jnp.zeros((1, QW), F32)
        for j in range(NH):
            h = g * SSM_HEADS_PER_GROUP + q * NH + j
            cols = slice(j * SSM_HEAD_DIM, (j + 1) * SSM_HEAD_DIM)
            rhs[j * L:(j + 1) * L, cols] = xq[:, cols]
            rhs[NH * L + j * SSM_STATE:NH * L + (j + 1) * SSM_STATE, cols] = hqb[:, cols]
            a_row = acst_scr[pl.ds(h, 1), :]
            b_row = lbt_scr[pl.ds(h, 1), :]
            a_col = jnp.sum(jnp.where(lane_h == h, a_cs, 0.0), axis=1, keepdims=True)
            dmat = jnp.where(causal, a_col - b_row, -jnp.inf)
            mhs.append((cb * jnp.exp2(dmat)).astype(BF16))
            ces.append((ca * jnp.exp2(a_col)).astype(BF16))
            a_last = a_row[:, L - 1:L]
            w_row = jnp.exp2(a_last - b_row)
            bws.append((bt * w_row).astype(BF16))
            lm = (lane_q >= j * SSM_HEAD_DIM) & (lane_q < (j + 1) * SSM_HEAD_DIM)
            cdq = jnp.where(lm, jnp.exp2(a_last), cdq)
        acc = _dot(jnp.concatenate(mhs + ces, axis=1), rhs[...])
        st = _dot(jnp.concatenate(bws, axis=1), rhs[0:NH * L, :])
        ht_scr[g, :, q * QW:(q + 1) * QW] = hq * cdq + st
        ys.append(acc + xa[:, q * QW:(q + 1) * QW] * dsk_ref[:, q * QW:(q + 1) * QW])

    z = z_ref[...].astype(F32)
    sz = _silu(z)
    ygs = [ys[q] * sz[:, q * QW:(q + 1) * QW] for q in range(len(ys))]
    ss = sum(jnp.sum(v * v, axis=1, keepdims=True) for v in ygs)
    scale = lax.rsqrt(ss * (1.0 / GROUP_CH) + EPS)
    for q in range(len(ys)):
        o_ref[:, q * QW:(q + 1) * QW] = (
            ygs[q] * scale * nw_ref[:, q * QW:(q + 1) * QW]).astype(o_ref.dtype)


def _ssd(proj, dt_raw, conv_w, conv_b, dt_bias, a_log, d_exp, norm_w):
    S = proj.shape[0]
    L, G, N = SSM_CHUNK, SSM_GROUPS, SSM_STATE
    GPS = SSD_GROUPS_PER_STEP
    XW, NW = GPS * GROUP_CH, GPS * N
    bcol = D_INNER // NW
    ccol = bcol + G // GPS
    pxcol = COL_XBC // XW
    pbcol = COL_XBC // NW + bcol
    pccol = pbcol + G // GPS
    zcol = COL_Z // XW
    return pl.pallas_call(
        _ssd_kernel,
        grid=(S // L, G // GPS),
        in_specs=[
            pl.BlockSpec((L, XW), lambda c, g: (c, pxcol + g)),
            pl.BlockSpec((L, NW), lambda c, g: (c, pbcol + g)),
            pl.BlockSpec((L, NW), lambda c, g: (c, pccol + g)),
            pl.BlockSpec((L, XW), lambda c, g: (c, zcol + g)),
            pl.BlockSpec((L, LANES), lambda c, g: (c, 0)),
            pl.BlockSpec((SSM_CONV, XW), lambda c, g: (0, g)),
            pl.BlockSpec((SSM_CONV, NW), lambda c, g: (0, bcol + g)),
            pl.BlockSpec((SSM_CONV, NW), lambda c, g: (0, ccol + g)),
            pl.BlockSpec((1, XW), lambda c, g: (0, g)),
            pl.BlockSpec((1, NW), lambda c, g: (0, bcol + g)),
            pl.BlockSpec((1, NW), lambda c, g: (0, ccol + g)),
            pl.BlockSpec((1, LANES), lambda c, g: (0, 0)),
            pl.BlockSpec((1, LANES), lambda c, g: (0, 0)),
            pl.BlockSpec((1, XW), lambda c, g: (0, g)),
            pl.BlockSpec((1, XW), lambda c, g: (0, g)),
        ],
        out_specs=pl.BlockSpec((L, XW), lambda c, g: (c, g)),
        out_shape=jax.ShapeDtypeStruct((S, D_INNER), BF16),
        scratch_shapes=[
            pltpu.VMEM((G, SUBLANES, GROUP_CH), F32),
            pltpu.VMEM((G, SUBLANES, N), F32),
            pltpu.VMEM((G, SUBLANES, N), F32),
            pltpu.VMEM((GPS, L + SUBLANES, GROUP_CH), F32),
            pltpu.VMEM((GPS, L + SUBLANES, N), F32),
            pltpu.VMEM((GPS, L + SUBLANES, N), F32),
            pltpu.VMEM((L, LANES), F32),
            pltpu.VMEM((LANES, L), F32),
            pltpu.VMEM((LANES, L), F32),
            pltpu.VMEM((G, N, GROUP_CH), F32),
            pltpu.VMEM((GPS, GROUP_CH // (4 * SSM_HEAD_DIM), 4 * (L + N), 4 * SSM_HEAD_DIM), BF16),
        ],
        compiler_params=_cparams(("arbitrary", "arbitrary")),
        name="ssd",
    )(proj, proj, proj, proj, dt_raw, conv_w, conv_w, conv_w, conv_b, conv_b, conv_b,
      dt_bias, a_log, d_exp, norm_w)


def _qk_prep_kernel(q_ref, k_ref, v_ref, qw_ref, kw_ref, qt_ref, kn_ref, vt_ref, pen_ref, kbar_scr):
    S = q_ref.shape[0]
    BS = MOBA_BLOCK
    nb = S // BS
    nbp = kbar_scr.shape[0]
    kbar_scr[...] = jnp.zeros(kbar_scr.shape, F32)

    def kloop(b, carry):
        rows = pl.ds(pl.multiple_of(b * BS, BS), BS)
        k = k_ref[rows, :].astype(F32)
        kn = k * lax.rsqrt(jnp.mean(k * k, axis=-1, keepdims=True) + EPS) * kw_ref[...]
        kn_ref[0, rows, :] = kn.astype(BF16)
        kbar_scr[pl.ds(b, 1), :] = jnp.mean(kn, axis=0, keepdims=True)
        vt_ref[0, b, 0:ATTN_HEAD_DIM, :] = v_ref[rows, :].astype(F32).T.astype(BF16)
        vt_ref[0, b, ATTN_HEAD_DIM:, :] = jnp.ones((MOBA_VT_ROWS - ATTN_HEAD_DIM, BS), BF16)
        return carry

    lax.fori_loop(0, nb, kloop, 0, unroll=4)
    kb_hi, kb_mid, _ = _split3(kbar_scr[...])

    def qloop(i, carry):
        rows = pl.ds(pl.multiple_of(i * BS, BS), BS)
        q = q_ref[rows, :].astype(F32)
        qn = q * lax.rsqrt(jnp.mean(q * q, axis=-1, keepdims=True) + EPS) * qw_ref[...]
        qnt = (qn * (ATTN_HEAD_DIM ** -0.5)).T
        qt_ref[0, i] = (qnt * LOG2E).astype(BF16)
        q_hi, q_mid, _ = _split3(qnt)
        gate = _dot(kb_hi, q_hi) + _dot(kb_mid, q_hi) + _dot(kb_hi, q_mid)
        blk = lax.broadcasted_iota(jnp.int32, gate.shape, 0)
        blk_f = blk.astype(F32)
        gate = jnp.where(blk < i, gate, -jnp.inf)
        pen = jnp.full(gate.shape, -jnp.inf, F32)
        for _ in range(MOBA_TOPK):
            mx = jnp.max(gate, axis=0, keepdims=True)
            cand = jnp.where((gate == mx) & (mx > -jnp.inf), blk_f, float(nbp))
            idx = jnp.min(cand, axis=0, keepdims=True)
            hit = blk_f == idx
            pen = jnp.where(hit, 0.0, pen)
            gate = jnp.where(hit, -jnp.inf, gate)
        pen_ref[0, i] = pen
        return carry

    lax.fori_loop(0, nb, qloop, 0, unroll=8)


def _qk_prep(proj, q_norm_w, k_norm_w):
    S = proj.shape[0]
    H, D, BS = ATTN_HEADS, ATTN_HEAD_DIM, MOBA_BLOCK
    nb = S // BS
    nbp = -(-nb // SUBLANES) * SUBLANES
    return pl.pallas_call(
        _qk_prep_kernel,
        grid=(H,),
        in_specs=[
            pl.BlockSpec((S, D), lambda h: (0, COL_Q // D + h)),
            pl.BlockSpec((S, D), lambda h: (0, COL_K // D + h)),
            pl.BlockSpec((S, D), lambda h: (0, COL_V // D + h)),
            pl.BlockSpec((1, D), lambda h: (0, 0)),
            pl.BlockSpec((1, D), lambda h: (0, 0)),
        ],
        out_specs=[
            pl.BlockSpec((1, nb, D, BS), lambda h: (h, 0, 0, 0)),
            pl.BlockSpec((1, S, D), lambda h: (h, 0, 0)),
            pl.BlockSpec((1, nb, MOBA_VT_ROWS, BS), lambda h: (h, 0, 0, 0)),
            pl.BlockSpec((1, nb, nbp, BS), lambda h: (h, 0, 0, 0)),
        ],
        out_shape=[
            jax.ShapeDtypeStruct((H, nb, D, BS), BF16),
            jax.ShapeDtypeStruct((H, S, D), BF16),
            jax.ShapeDtypeStruct((H, nb, MOBA_VT_ROWS, BS), BF16),
            jax.ShapeDtypeStruct((H, nb, nbp, BS), F32),
        ],
        scratch_shapes=[pltpu.VMEM((nbp, D), F32)],
        compiler_params=_cparams(("arbitrary",)),
        name="qk_prep",
    )(proj, proj, proj, q_norm_w, k_norm_w)


def _rel_bucket(dist):
    n = jnp.maximum(dist, 0)
    max_exact = REL_BUCKETS // 2
    nf = jnp.maximum(n, 1).astype(F32)
    large = max_exact + (jnp.log(nf / max_exact) / math.log(REL_MAX_DIST / max_exact)
                         * (REL_BUCKETS - max_exact)).astype(jnp.int32)
    large = jnp.minimum(large, REL_BUCKETS - 1)
    return jnp.where(n < max_exact, n, large)


MOBA_HEADS_PER_STEP = 4
MOBA_VT_ROWS = ATTN_HEAD_DIM + 2 * SUBLANES


def _moba_kernel(tbl_ref, qt_ref, k_ref, vt_ref, pen_ref, o_ref,
                 bown_scr, bprev_scr, m_scr, acc_scr, s0_scr, s1_scr, s2_scr, s3_scr,
                 p0_scr, p1_scr, p2_scr, p3_scr, al0_scr, al1_scr, al2_scr, al3_scr):
    hp = pl.program_id(0)
    i = pl.program_id(1)
    BS, HB = MOBA_BLOCK, MOBA_HEADS_PER_STEP
    D = ATTN_HEAD_DIM

    @pl.when(i == 0)
    def _():
        kl = lax.broadcasted_iota(jnp.int32, (BS, BS), 0)
        ql = lax.broadcasted_iota(jnp.int32, (BS, BS), 1)
        d = ql - kl
        b_own = _rel_bucket(d)
        b_prev = _rel_bucket(d + BS)
        for a in range(HB):
            h = hp * HB + a
            far = tbl_ref[h, REL_BUCKETS - 1]

            def lookup(bucket):
                val = jnp.zeros((BS, BS), F32)
                for b in range(REL_BUCKETS):
                    val = jnp.where(bucket == b, tbl_ref[h, b], val)
                return (val - far) * LOG2E

            bown_scr[a] = jnp.where(d >= 0, lookup(b_own), -jnp.inf)
            bprev_scr[a] = lookup(b_prev)

    def krows(a, j):
        return k_ref[a, pl.ds(pl.multiple_of(j * BS, BS), BS), :]

    jp = jnp.maximum(i - 1, 0)
    no_prev = jnp.where(i == 0, -jnp.inf, 0.0)
    n_far = jnp.maximum(i - 1, 0)
    nb = k_ref.shape[1] // BS

    def clampj(j):
        return jnp.clip(j, 0, nb - 1)

    def scores(j, s_slot):
        for a in range(HB):
            s_slot[a] = _dot(krows(a, clampj(j)), qt_ref[a, 0])

    def softmax(j, s_slot, p_slot, al_slot, mask_tail, bias_scr=None):
        for a in range(HB):
            pr = pen_ref[a, 0, pl.ds(clampj(j), 1), :]
            if mask_tail is True:
                pr = pr + jnp.where(j >= n_far, -jnp.inf, 0.0)
            elif mask_tail is not False:
                pr = pr + mask_tail
            s = s_slot[a]
            if bias_scr is not None:
                s = s + bias_scr[a]
            m_old = m_scr[a]
            m_new = jnp.maximum(m_old, jnp.max(s, axis=0, keepdims=True) + pr)
            p_slot[a] = jnp.exp2(s - (m_new - pr)).astype(BF16)
            al_slot[a] = jnp.exp2(m_old - m_new)
            m_scr[a] = m_new

    def accumulate(j, p_slot, al_slot, maybe_prev=False):
        jv = clampj(j)
        if maybe_prev:
            jv = jnp.where(j < 0, jp, jv)
        for a in range(HB):
            acc_scr[a] = al_slot[a] * acc_scr[a] + _dot(vt_ref[a, jv], p_slot[a])

    s_slots = (s0_scr, s1_scr, s2_scr, s3_scr)
    p_slots = (p0_scr, p1_scr, p2_scr, p3_scr)
    al_slots = (al0_scr, al1_scr, al2_scr, al3_scr)
    scores(i, s2_scr)
    scores(jp, s3_scr)
    scores(0, s0_scr)
    for a in range(HB):
        s = s2_scr[a] + bown_scr[a]
        m0 = jnp.max(s, axis=0, keepdims=True)
        m_scr[a] = m0
        p2_scr[a] = jnp.exp2(s - m0).astype(BF16)
    for a in range(HB):
        acc_scr[a] = _dot(vt_ref[a, i], p2_scr[a])
    softmax(jp, s3_scr, p3_scr, al3_scr, no_prev, bias_scr=bprev_scr)

    def quad_loop(g, carry):
        j = 4 * g
        for t in range(4):
            scores(j + t + 1, s_slots[(t + 1) % 4])
            accumulate(j + t - 1, p_slots[(t - 1) % 4], al_slots[(t - 1) % 4], maybe_prev=(t == 0))
            softmax(j + t, s_slots[t], p_slots[t], al_slots[t], False)
        return carry

    n_quads = n_far // 4
    lax.fori_loop(0, n_quads, quad_loop, 0)

    def pair_loop(g, carry):
        j = 4 * n_quads + 2 * g
        scores(j + 1, s1_scr)
        accumulate(j - 1, p3_scr, al3_scr, maybe_prev=True)
        softmax(j, s0_scr, p2_scr, al2_scr, False)
        scores(j + 2, s0_scr)
        accumulate(j, p2_scr, al2_scr)
        softmax(j + 1, s1_scr, p3_scr, al3_scr, True)
        return carry

    n_pairs = (n_far - 4 * n_quads + 1) // 2
    lax.fori_loop(0, n_pairs, pair_loop, 0)
    accumulate(4 * n_quads + 2 * n_pairs - 1, p3_scr, al3_scr, maybe_prev=True)
    for a in range(HB):
        acc = acc_scr[a]
        o_ref[:, a * D:(a + 1) * D] = (acc[0:D, :] / acc[D:D + 1, :]).T.astype(o_ref.dtype)


def _moba(qt, kn, vt, pen, table_h):
    H, nb, D, BS = qt.shape
    S = nb * BS
    nbp = pen.shape[2]
    HB = MOBA_HEADS_PER_STEP
    return pl.pallas_call(
        _moba_kernel,
        grid=(H // HB, nb),
        in_specs=[
            pl.BlockSpec(memory_space=pltpu.SMEM),
            pl.BlockSpec((HB, 1, D, BS), lambda hp, i: (hp, i, 0, 0)),
            pl.BlockSpec((HB, S, D), lambda hp, i: (hp, 0, 0)),
            pl.BlockSpec((HB, nb, MOBA_VT_ROWS, BS), lambda hp, i: (hp, 0, 0, 0)),
            pl.BlockSpec((HB, 1, nbp, BS), lambda hp, i: (hp, i, 0, 0)),
        ],
        out_specs=pl.BlockSpec((BS, HB * D), lambda hp, i: (i, hp)),
        out_shape=jax.ShapeDtypeStruct((S, H * D), BF16),
        scratch_shapes=[
            pltpu.VMEM((HB, BS, BS), F32),
            pltpu.VMEM((HB, BS, BS), F32),
            pltpu.VMEM((HB, 1, BS), F32),
            pltpu.VMEM((HB, MOBA_VT_ROWS, BS), F32),
        ] + [pltpu.VMEM((HB, BS, BS), F32)] * 4 + [pltpu.VMEM((HB, BS, BS), BF16)] * 4
        + [pltpu.VMEM((HB, 1, BS), F32)] * 4,
        compiler_params=_cparams(("arbitrary", "arbitrary")),
        name="moba",
    )(table_h, qt, kn, vt, pen)


def _mix_kernel(ys_ref, ya_ref, ws_ref, wa_ref, gs_ref, ga_ref, bs_ref, ba_ref, o_ref):
    gs = _sigmoid(gs_ref[...].astype(F32) + bs_ref[...])
    ga = _sigmoid(ga_ref[...].astype(F32) + ba_ref[...])
    o_ref[...] = (gs * _dot(ys_ref[...], ws_ref[...])
                  + ga * _dot(ya_ref[...], wa_ref[...])).astype(o_ref.dtype)


def _mix(y_ssd, y_attn, w_ssm_out, w_attn_out, proj, b_gate, tm=1024, tn=512):
    S = y_ssd.shape[0]
    tm = min(tm, S)
    gs_col = COL_G // tn
    ga_col = (COL_G + D_MODEL) // tn
    nb = D_MODEL // tn
    return pl.pallas_call(
        _mix_kernel,
        grid=(S // tm, D_MODEL // tn),
        in_specs=[
            pl.BlockSpec((tm, D_INNER), lambda m, n: (m, 0)),
            pl.BlockSpec((tm, D_ATTN), lambda m, n: (m, 0)),
            pl.BlockSpec((D_INNER, tn), lambda m, n: (0, n)),
            pl.BlockSpec((D_ATTN, tn), lambda m, n: (0, n)),
            pl.BlockSpec((tm, tn), lambda m, n: (m, gs_col + n)),
            pl.BlockSpec((tm, tn), lambda m, n: (m, ga_col + n)),
            pl.BlockSpec((1, tn), lambda m, n: (0, n)),
            pl.BlockSpec((1, tn), lambda m, n: (0, nb + n)),
        ],
        out_specs=pl.BlockSpec((tm, tn), lambda m, n: (m, n)),
        out_shape=jax.ShapeDtypeStruct((S, D_MODEL), BF16),
        compiler_params=_cparams(("arbitrary", "arbitrary")),
        name="mix",
    )(y_ssd, y_attn, w_ssm_out, w_attn_out, proj, proj, b_gate, b_gate)


def _out_norm_kernel(x_ref, mix_ref, w_ref, nw_ref, x1_ref, h2_ref):
    x1 = x_ref[...] + _dot(mix_ref[...], w_ref[...])
    x1_ref[...] = x1
    ms = jnp.mean(x1 * x1, axis=-1, keepdims=True)
    h2_ref[...] = (x1 * lax.rsqrt(ms + EPS) * nw_ref[...]).astype(h2_ref.dtype)


def _out_norm(x, mix, w_out, ffn_norm_w, tm=512):
    S = x.shape[0]
    tm = min(tm, S)
    return pl.pallas_call(
        _out_norm_kernel,
        grid=(S // tm,),
        in_specs=[
            pl.BlockSpec((tm, D_MODEL), lambda m: (m, 0)),
            pl.BlockSpec((tm, D_MODEL), lambda m: (m, 0)),
            pl.BlockSpec((D_MODEL, D_MODEL), lambda m: (0, 0)),
            pl.BlockSpec((1, D_MODEL), lambda m: (0, 0)),
        ],
        out_specs=[
            pl.BlockSpec((tm, D_MODEL), lambda m: (m, 0)),
            pl.BlockSpec((tm, D_MODEL), lambda m: (m, 0)),
        ],
        out_shape=[
            jax.ShapeDtypeStruct((S, D_MODEL), F32),
            jax.ShapeDtypeStruct((S, D_MODEL), BF16),
        ],
        compiler_params=_cparams(("arbitrary",)),
        name="out_norm",
    )(x, mix, w_out, ffn_norm_w)


def _ffn_up_kernel(h_ref, wg32_ref, wu32_ref, cwg_ref, cwu_ref, cbg_ref, cbu_ref, o_ref,
                   tailg, tailu, padg, padu, wg_ref, wu_ref):
    tm = h_ref.shape[0]

    @pl.when(pl.program_id(1) == 0)
    def _():
        tailg[...] = jnp.zeros(tailg.shape, F32)
        tailu[...] = jnp.zeros(tailu.shape, F32)
        wg_ref[...] = wg32_ref[...].astype(BF16)
        wu_ref[...] = wu32_ref[...].astype(BF16)

    def conv(w_ref, cw_ref, cb_ref, tail, pad):
        pad[0:SUBLANES, :] = tail[...]
        pad[SUBLANES:SUBLANES + tm, :] = _dot(h_ref[...], w_ref[...])
        tail[...] = pad[tm:tm + SUBLANES, :]
        acc = cb_ref[...]
        for k in range(FFN_CONV):
            off = SUBLANES - (FFN_CONV - 1) + k
            acc = acc + cw_ref[k:k + 1, :] * pad[off:off + tm, :]
        return acc

    ug = conv(wg_ref, cwg_ref, cbg_ref, tailg, padg)
    uu = conv(wu_ref, cwu_ref, cbu_ref, tailu, padu)
    o_ref[...] = (_silu(ug) * uu).astype(o_ref.dtype)


def _ffn_up(h2, w_up, conv_w, conv_b, tm=1024, tn=512):
    S = h2.shape[0]
    tm = min(tm, S)
    nb = D_FF // tn
    return pl.pallas_call(
        _ffn_up_kernel,
        grid=(nb, S // tm),
        in_specs=[
            pl.BlockSpec((tm, D_MODEL), lambda n, m: (m, 0)),
            pl.BlockSpec((D_MODEL, tn), lambda n, m: (0, n)),
            pl.BlockSpec((D_MODEL, tn), lambda n, m: (0, nb + n)),
            pl.BlockSpec((FFN_CONV, tn), lambda n, m: (0, n)),
            pl.BlockSpec((FFN_CONV, tn), lambda n, m: (0, nb + n)),
            pl.BlockSpec((1, tn), lambda n, m: (0, n)),
            pl.BlockSpec((1, tn), lambda n, m: (0, nb + n)),
        ],
        out_specs=pl.BlockSpec((tm, tn), lambda n, m: (m, n)),
        out_shape=jax.ShapeDtypeStruct((S, D_FF), BF16),
        scratch_shapes=[
            pltpu.VMEM((SUBLANES, tn), F32),
            pltpu.VMEM((SUBLANES, tn), F32),
            pltpu.VMEM((tm + SUBLANES, tn), F32),
            pltpu.VMEM((tm + SUBLANES, tn), F32),
            pltpu.VMEM((D_MODEL, tn), BF16),
            pltpu.VMEM((D_MODEL, tn), BF16),
        ],
        compiler_params=_cparams(("arbitrary", "arbitrary")),
        name="ffn_up",
    )(h2, w_up, w_up, conv_w, conv_w, conv_b, conv_b)


def _ffn_down_kernel(a_ref, w_ref, x1_ref, o_ref):
    o_ref[...] = x1_ref[...] + _dot(a_ref[...], w_ref[...])


def _ffn_down(act, w_down, x1, tm=1024, tn=512):
    S = act.shape[0]
    tm = min(tm, S)
    return pl.pallas_call(
        _ffn_down_kernel,
        grid=(S // tm, D_MODEL // tn),
        in_specs=[
            pl.BlockSpec((tm, D_FF), lambda m, n: (m, 0)),
            pl.BlockSpec((D_FF, tn), lambda m, n: (0, n)),
            pl.BlockSpec((tm, tn), lambda m, n: (m, n)),
        ],
        out_specs=pl.BlockSpec((tm, tn), lambda m, n: (m, n)),
        out_shape=jax.ShapeDtypeStruct((S, D_MODEL), F32),
        compiler_params=_cparams(("arbitrary", "arbitrary")),
        name="ffn_down",
    )(act, w_down, x1)


def _pad_lanes(v):
    return jnp.pad(v, ((0, 0), (0, LANES - v.shape[1])))


def _layer(x, attn_norm_w, w_in, b_gate, ssm_conv_w, ssm_conv_b, ssm_dt_bias, ssm_a_log, ssm_d,
           ssm_norm_w, q_norm_w, k_norm_w, rel_bias, w_ssm_out, w_attn_out, w_out, ffn_norm_w,
           w_up, ffn_conv_w, ffn_conv_b, w_down):
    o_z, o_xbc = 0, D_INNER
    o_dt = o_xbc + D_XBC
    o_q = o_dt + SSM_HEADS
    o_g = o_q + 3 * D_ATTN
    wt = w_in.T
    wt_dt = jnp.pad(wt[o_dt:o_q], ((0, LANES - SSM_HEADS), (0, 0)))

    h, dt_raw = _norm_dt(x, attn_norm_w[None, :], wt_dt)
    proj = _in_proj(h, wt, o_dt)
    y_ssd = _ssd(proj, dt_raw, ssm_conv_w, ssm_conv_b[None, :], _pad_lanes(ssm_dt_bias[None, :]),
                 _pad_lanes(ssm_a_log[None, :]), jnp.repeat(ssm_d, SSM_HEAD_DIM)[None, :],
                 ssm_norm_w[None, :])
    qt, kn, vt, pen = _qk_prep(proj, q_norm_w[None, :], k_norm_w[None, :])
    y_attn = _moba(qt, kn, vt, pen, rel_bias.T)
    mix = _mix(y_ssd, y_attn, w_ssm_out.astype(BF16), w_attn_out.astype(BF16), proj, b_gate[None, :])
    x1, h2 = _out_norm(x, mix, w_out.astype(BF16), ffn_norm_w[None, :])
    act = _ffn_up(h2, w_up, ffn_conv_w, ffn_conv_b[None, :])
    return _ffn_down(act, w_down.astype(BF16), x1)


def kernel(x, attn_norm_w, w_in, b_gate, ssm_conv_w, ssm_conv_b, ssm_dt_bias, ssm_a_log, ssm_d,
           ssm_norm_w, q_norm_w, k_norm_w, rel_bias, w_ssm_out, w_attn_out, w_out, ffn_norm_w,
           w_up, ffn_conv_w, ffn_conv_b, w_down):
    assert x.shape[0] == 1 and attn_norm_w.shape[0] == 1
    out = _layer(x[0], attn_norm_w[0], w_in[0], b_gate[0], ssm_conv_w[0], ssm_conv_b[0],
                 ssm_dt_bias[0], ssm_a_log[0], ssm_d[0], ssm_norm_w[0], q_norm_w[0], k_norm_w[0],
                 rel_bias, w_ssm_out[0], w_attn_out[0], w_out[0], ffn_norm_w[0], w_up[0],
                 ffn_conv_w[0], ffn_conv_b[0], w_down[0])
    return out[None]
```

```python
import functools
import math

import jax
import jax.numpy as jnp
from jax import lax
from jax.experimental import pallas as pl
from jax.experimental.pallas import tpu as pltpu

F32 = jnp.float32
BF16 = jnp.bfloat16

D_MODEL = 2048
D_INNER = 4096
SSM_HEADS = 64
SSM_HEAD_DIM = 64
SSM_GROUPS = 8
SSM_HEADS_PER_GROUP = SSM_HEADS // SSM_GROUPS
SSM_STATE = 128
SSM_CONV = 4
SSM_CHUNK = 256
GROUP_CH = D_INNER // SSM_GROUPS
SSD_GROUPS_PER_STEP = 8
D_XBC = D_INNER + 2 * SSM_GROUPS * SSM_STATE
ATTN_HEADS = 16
ATTN_HEAD_DIM = 128
D_ATTN = ATTN_HEADS * ATTN_HEAD_DIM
MOBA_BLOCK = 256
MOBA_TOPK = 3
REL_BUCKETS = 32
REL_MAX_DIST = 128
D_FF = 5632
FFN_CONV = 3
EPS = 1e-6
LOG2E = 1.4426950408889634

LANES = 128
SUBLANES = 8
D_PROJ = D_XBC + 3 * D_ATTN + 2 * D_INNER
COL_Z = 0
COL_XBC = COL_Z + D_INNER
COL_Q = COL_XBC + D_XBC
COL_K = COL_Q + D_ATTN
COL_V = COL_K + D_ATTN
COL_G = COL_V + D_ATTN
VMEM_LIMIT = 56 * 1024 * 1024


def _cparams(sem):
    return pltpu.CompilerParams(dimension_semantics=sem, vmem_limit_bytes=VMEM_LIMIT)


def _split3(a):
    hi = a.astype(BF16)
    r = a - hi.astype(F32)
    mid = r.astype(BF16)
    lo = (r - mid.astype(F32)).astype(BF16)
    return hi, mid, lo


def _dot(a, b):
    return jnp.dot(a, b, preferred_element_type=F32)


def _dot_nt(a, b):
    return lax.dot_general(a, b, (((1,), (1,)), ((), ())), preferred_element_type=F32)


def _sigmoid(x):
    return 0.5 + 0.5 * jnp.tanh(0.5 * x)


def _silu(x):
    h = 0.5 * x
    return h + h * jnp.tanh(h)


def _norm_dt_kernel(x_ref, nw_ref, wdt_ref, h_ref, dt_ref):
    x = x_ref[...]
    ms = jnp.mean(x * x, axis=-1, keepdims=True)
    h = x * lax.rsqrt(ms + EPS) * nw_ref[...]
    h_ref[...] = h.astype(BF16)
    h_hi, h_mid, _ = _split3(h)
    w_hi, w_mid, _ = _split3(wdt_ref[...])
    dt_ref[...] = _dot_nt(h_hi, w_hi) + _dot_nt(h_hi, w_mid) + _dot_nt(h_mid, w_hi)


def _norm_dt(x, norm_w, wt_dt, tm=512):
    S = x.shape[0]
    tm = min(tm, S)
    return pl.pallas_call(
        _norm_dt_kernel,
        grid=(S // tm,),
        in_specs=[
            pl.BlockSpec((tm, D_MODEL), lambda m: (m, 0)),
            pl.BlockSpec((1, D_MODEL), lambda m: (0, 0)),
            pl.BlockSpec((LANES, D_MODEL), lambda m: (0, 0)),
        ],
        out_specs=[
            pl.BlockSpec((tm, D_MODEL), lambda m: (m, 0)),
            pl.BlockSpec((tm, LANES), lambda m: (m, 0)),
        ],
        out_shape=[
            jax.ShapeDtypeStruct((S, D_MODEL), BF16),
            jax.ShapeDtypeStruct((S, LANES), F32),
        ],
        compiler_params=_cparams(("arbitrary",)),
        name="norm_dt",
    )(x, norm_w, wt_dt)


def _in_proj_kernel(h_ref, w32_ref, o_ref, w_scr):
    @pl.when(pl.program_id(1) == 0)
    def _():
        w_scr[...] = w32_ref[...].astype(BF16)

    o_ref[...] = _dot_nt(h_ref[...], w_scr[...]).astype(o_ref.dtype)


def _in_proj(h, wt, dt_row0, tm=2048, tn=1024):
    S = h.shape[0]
    tm = min(tm, S)
    assert dt_row0 % tn == 0 and SSM_HEADS % SUBLANES == 0
    na = dt_row0 // tn

    def w_rows(n, m):
        return (pl.multiple_of(n * tn + jnp.where(n >= na, SSM_HEADS, 0), SSM_HEADS), 0)

    return pl.pallas_call(
        _in_proj_kernel,
        grid=(D_PROJ // tn, S // tm),
        in_specs=[
            pl.BlockSpec((tm, D_MODEL), lambda n, m: (m, 0)),
            pl.BlockSpec((pl.Element(tn), pl.Element(D_MODEL)), w_rows),
        ],
        out_specs=pl.BlockSpec((tm, tn), lambda n, m: (m, n)),
        out_shape=jax.ShapeDtypeStruct((S, D_PROJ), BF16),
        scratch_shapes=[pltpu.VMEM((tn, D_MODEL), BF16)],
        compiler_params=_cparams(("arbitrary", "arbitrary")),
        name="in_proj",
    )(h, wt)


def _conv_silu(raw_ref, tail_ref, pad_ref, w_ref, b_ref, g, taps):
    L = raw_ref.shape[0]
    pad_ref[0:SUBLANES, :] = tail_ref[g]
    pad_ref[SUBLANES:SUBLANES + L, :] = raw_ref[...].astype(F32)
    tail_ref[g] = pad_ref[L:L + SUBLANES, :]
    acc = b_ref[...]
    for k in range(taps):
        off = SUBLANES - (taps - 1) + k
        acc = acc + w_ref[k:k + 1, :] * pad_ref[off:off + L, :]
    return _silu(acc)


def _ssd_kernel(x_ref, b_ref, c_ref, z_ref, dtr_ref, cwx_ref, cwb_ref, cwc_ref,
                cbx_ref, cbb_ref, cbc_ref, dtb_ref, alog_ref, dsk_ref, nw_ref,
                o_ref,
                tailx, tailb, tailc, padx, padb, padc, acs_scr, acst_scr, lbt_scr, ht_scr, rhs_scr):
    c = pl.program_id(0)
    gp = pl.program_id(1)
    L = SSM_CHUNK
    QW = 4 * SSM_HEAD_DIM
    GPS = SSD_GROUPS_PER_STEP

    @pl.when(c == 0)
    def _():
        for gg in range(GPS):
            g0 = gp * GPS + gg
            tailx[g0] = jnp.zeros(tailx.shape[1:], F32)
            tailb[g0] = jnp.zeros(tailb.shape[1:], F32)
            tailc[g0] = jnp.zeros(tailc.shape[1:], F32)
            ht_scr[g0] = jnp.zeros(ht_scr.shape[1:], F32)
            rhs_scr[gg] = jnp.zeros(rhs_scr.shape[1:], BF16)

    @pl.when(gp == 0)
    def _():
        t = dtr_ref[...] + dtb_ref[...]
        dt = jnp.maximum(t, 0.0) + jnp.log(1.0 + jnp.exp(-jnp.abs(t)))
        a = -jnp.exp(alog_ref[...])
        da = dt * a
        ri = lax.broadcasted_iota(jnp.int32, (L, L), 0)
        ci = lax.broadcasted_iota(jnp.int32, (L, L), 1)
        tri = jnp.where(ri >= ci, 1.0, 0.0).astype(BF16)
        hi, mid, lo = _split3(da)
        a_cs = _dot(tri, hi) + _dot(tri, mid) + _dot(tri, lo)
        a2 = a_cs * LOG2E
        a2t = a2.T
        acs_scr[...] = a2
        acst_scr[...] = a2t
        lbt_scr[...] = a2t - jnp.log(dt.T) * LOG2E

    for gg in range(GPS):
        xs = slice(gg * GROUP_CH, (gg + 1) * GROUP_CH)
        ns = slice(gg * SSM_STATE, (gg + 1) * SSM_STATE)
        _ssd_group(gp * GPS + gg,
                   x_ref.at[:, xs], b_ref.at[:, ns], c_ref.at[:, ns], z_ref.at[:, xs],
                   cwx_ref.at[:, xs], cwb_ref.at[:, ns], cwc_ref.at[:, ns],
                   cbx_ref.at[:, xs], cbb_ref.at[:, ns], cbc_ref.at[:, ns],
                   dsk_ref.at[:, xs], nw_ref.at[:, xs], o_ref.at[:, xs],
                   tailx, tailb, tailc, padx.at[gg], padb.at[gg], padc.at[gg],
                   acs_scr, acst_scr, lbt_scr, ht_scr,
                   [rhs_scr.at[gg, q] for q in range(GROUP_CH // QW)])


def _ssd_group(g, x_ref, b_ref, c_ref, z_ref, cwx_ref, cwb_ref, cwc_ref, cbx_ref, cbb_ref, cbc_ref,
               dsk_ref, nw_ref, o_ref, tailx, tailb, tailc, padx, padb, padc,
               acs_scr, acst_scr, lbt_scr, ht_scr, rhs_refs):
    L = SSM_CHUNK
    QW = 4 * SSM_HEAD_DIM
    xa = _conv_silu(x_ref, tailx, padx, cwx_ref, cbx_ref, g, SSM_CONV)
    ba = _conv_silu(b_ref, tailb, padb, cwb_ref, cbb_ref, g, SSM_CONV)
    ca = _conv_silu(c_ref, tailc, padc, cwc_ref, cbc_ref, g, SSM_CONV)

    xb = xa.astype(BF16)
    bt = ba.T
    cb = _dot(ca.astype(BF16), bt.astype(BF16))
    a_cs = acs_scr[...]
    ri = lax.broadcasted_iota(jnp.int32, (L, L), 0)
    ci = lax.broadcasted_iota(jnp.int32, (L, L), 1)
    causal = ri >= ci
    lane_h = lax.broadcasted_iota(jnp.int32, (L, LANES), 1)
    lane_q = lax.broadcasted_iota(jnp.int32, (1, QW), 1)

    NH = QW // SSM_HEAD_DIM
    ys = []
    for q in range(GROUP_CH // QW):
        rhs = rhs_refs[q]
        xq = xb[:, q * QW:(q + 1) * QW]
        hq = ht_scr[g, :, q * QW:(q + 1) * QW]
        hqb = hq.astype(BF16)
        mhs, ces, bws = [], [], []
        cdq = jnp.zeros((1, QW), F32)
        for j in range(NH):
            h = g * SSM_HEADS_PER_GROUP + q * NH + j
            cols = slice(j * SSM_HEAD_DIM, (j + 1) * SSM_HEAD_DIM)
            rhs[j * L:(j + 1) * L, cols] = xq[:, cols]
            rhs[NH * L + j * SSM_STATE:NH * L + (j + 1) * SSM_STATE, cols] = hqb[:, cols]
            a_row = acst_scr[pl.ds(h, 1), :]
            b_row = lbt_scr[pl.ds(h, 1), :]
            a_col = jnp.sum(jnp.where(lane_h == h, a_cs, 0.0), axis=1, keepdims=True)
            dmat = jnp.where(causal, a_col - b_row, -jnp.inf)
            mhs.append((cb * jnp.exp2(dmat)).astype(BF16))
            ces.append((ca * jnp.exp2(a_col)).astype(BF16))
            a_last = a_row[:, L - 1:L]
            w_row = jnp.exp2(a_last - b_row)
            bws.append((bt * w_row).astype(BF16))
            lm = (lane_q >= j * SSM_HEAD_DIM) & (lane_q < (j + 1) * SSM_HEAD_DIM)
            cdq = jnp.where(lm, jnp.exp2(a_last), cdq)
        acc = _dot(jnp.concatenate(mhs + ces, axis=1), rhs[...])
        st = _dot(jnp.concatenate(bws, axis=1), rhs[0:NH * L, :])
        ht_scr[g, :, q * QW:(q + 1) * QW] = hq * cdq + st
        ys.append(acc + xa[:, q * QW:(q + 1) * QW] * dsk_ref[:, q * QW:(q + 1) * QW])

    z = z_ref[...].astype(F32)
    sz = _silu(z)
    ygs = [ys[q] * sz[:, q * QW:(q + 1) * QW] for q in range(len(ys))]
    ss = sum(jnp.sum(v * v, axis=1, keepdims=True) for v in ygs)
    scale = lax.rsqrt(ss * (1.0 / GROUP_CH) + EPS)
    for q in range(len(ys)):
        o_ref[:, q * QW:(q + 1) * QW] = (
            ygs[q] * scale * nw_ref[:, q * QW:(q + 1) * QW]).astype(o_ref.dtype)


def _ssd(proj, dt_raw, conv_w, conv_b, dt_bias, a_log, d_exp, norm_w):
    S = proj.shape[0]
    L, G, N = SSM_CHUNK, SSM_GROUPS, SSM_STATE
    GPS = SSD_GROUPS_PER_STEP
    XW, NW = GPS * GROUP_CH, GPS * N
    bcol = D_INNER // NW
    ccol = bcol + G // GPS
    pxcol = COL_XBC // XW
    pbcol = COL_XBC // NW + bcol
    pccol = pbcol + G // GPS
    zcol = COL_Z // XW
    return pl.pallas_call(
        _ssd_kernel,
        grid=(S // L, G // GPS),
        in_specs=[
            pl.BlockSpec((L, XW), lambda c, g: (c, pxcol + g)),
            pl.BlockSpec((L, NW), lambda c, g: (c, pbcol + g)),
            pl.BlockSpec((L, NW), lambda c, g: (c, pccol + g)),
            pl.BlockSpec((L, XW), lambda c, g: (c, zcol + g)),
            pl.BlockSpec((L, LANES), lambda c, g: (c, 0)),
            pl.BlockSpec((SSM_CONV, XW), lambda c, g: (0, g)),
            pl.BlockSpec((SSM_CONV, NW), lambda c, g: (0, bcol + g)),
            pl.BlockSpec((SSM_CONV, NW), lambda c, g: (0, ccol + g)),
            pl.BlockSpec((1, XW), lambda c, g: (0, g)),
            pl.BlockSpec((1, NW), lambda c, g: (0, bcol + g)),
            pl.BlockSpec((1, NW), lambda c, g: (0, ccol + g)),
            pl.BlockSpec((1, LANES), lambda c, g: (0, 0)),
            pl.BlockSpec((1, LANES), lambda c, g: (0, 0)),
            pl.BlockSpec((1, XW), lambda c, g: (0, g)),
            pl.BlockSpec((1, XW), lambda c, g: (0, g)),
        ],
        out_specs=pl.BlockSpec((L, XW), lambda c, g: (c, g)),
        out_shape=jax.ShapeDtypeStruct((S, D_INNER), BF16),
        scratch_shapes=[
            pltpu.VMEM((G, SUBLANES, GROUP_CH), F32),
            pltpu.VMEM((G, SUBLANES, N), F32),
            pltpu.VMEM((G, SUBLANES, N), F32),
            pltpu.VMEM((GPS, L + SUBLANES, GROUP_CH), F32),
            pltpu.VMEM((GPS, L + SUBLANES, N), F32),
            pltpu.VMEM((GPS, L + SUBLANES, N), F32),
            pltpu.VMEM((L, LANES), F32),
            pltpu.VMEM((LANES, L), F32),
            pltpu.VMEM((LANES, L), F32),
            pltpu.VMEM((G, N, GROUP_CH), F32),
            pltpu.VMEM((GPS, GROUP_CH // (4 * SSM_HEAD_DIM), 4 * (L + N), 4 * SSM_HEAD_DIM), BF16),
        ],
        compiler_params=_cparams(("arbitrary", "arbitrary")),
        name="ssd",
    )(proj, proj, proj, proj, dt_raw, conv_w, conv_w, conv_w, conv_b, conv_b, conv_b,
      dt_bias, a_log, d_exp, norm_w)


def _qk_prep_kernel(q_ref, k_ref, v_ref, qw_ref, kw_ref, qt_ref, kn_ref, vt_ref, pen_ref, kbar_scr):
    S = q_ref.shape[0]
    BS = MOBA_BLOCK
    nb = S // BS
    nbp = kbar_scr.shape[0]
    kbar_scr[...] = jnp.zeros(kbar_scr.shape, F32)

    def kloop(b, carry):
        rows = pl.ds(pl.multiple_of(b * BS, BS), BS)
        k = k_ref[rows, :].astype(F32)
        kn = k * lax.rsqrt(jnp.mean(k * k, axis=-1, keepdims=True) + EPS) * kw_ref[...]
        kn_ref[0, rows, :] = kn.astype(BF16)
        kbar_scr[pl.ds(b, 1), :] = jnp.mean(kn, axis=0, keepdims=True)
        vt_ref[0, b, 0:ATTN_HEAD_DIM, :] = v_ref[rows, :].astype(F32).T.astype(BF16)
        vt_ref[0, b, ATTN_HEAD_DIM:, :] = jnp.ones((MOBA_VT_ROWS - ATTN_HEAD_DIM, BS), BF16)
        return carry

    lax.fori_loop(0, nb, kloop, 0, unroll=4)
    kb_hi, kb_mid, _ = _split3(kbar_scr[...])

    def qloop(i, carry):
        rows = pl.ds(pl.multiple_of(i * BS, BS), BS)
        q = q_ref[rows, :].astype(F32)
        qn = q * lax.rsqrt(jnp.mean(q * q, axis=-1, keepdims=True) + EPS) * qw_ref[...]
        qnt = (qn * (ATTN_HEAD_DIM ** -0.5)).T
        qt_ref[0, i] = (qnt * LOG2E).astype(BF16)
        q_hi, q_mid, _ = _split3(qnt)
        gate = _dot(kb_hi, q_hi) + _dot(kb_mid, q_hi) + _dot(kb_hi, q_mid)
        blk = lax.broadcasted_iota(jnp.int32, gate.shape, 0)
        blk_f = blk.astype(F32)
        gate = jnp.where(blk < i, gate, -jnp.inf)
        pen = jnp.full(gate.shape, -jnp.inf, F32)
        for _ in range(MOBA_TOPK):
            mx = jnp.max(gate, axis=0, keepdims=True)
            cand = jnp.where((gate == mx) & (mx > -jnp.inf), blk_f, float(nbp))
            idx = jnp.min(cand, axis=0, keepdims=True)
            hit = blk_f == idx
            pen = jnp.where(hit, 0.0, pen)
            gate = jnp.where(hit, -jnp.inf, gate)
        pen_ref[0, i] = pen
        return carry

    lax.fori_loop(0, nb, qloop, 0, unroll=8)


def _qk_prep(proj, q_norm_w, k_norm_w):
    S = proj.shape[0]
    H, D, BS = ATTN_HEADS, ATTN_HEAD_DIM, MOBA_BLOCK
    nb = S // BS
    nbp = -(-nb // SUBLANES) * SUBLANES
    return pl.pallas_call(
        _qk_prep_kernel,
        grid=(H,),
        in_specs=[
            pl.BlockSpec((S, D), lambda h: (0, COL_Q // D + h)),
            pl.BlockSpec((S, D), lambda h: (0, COL_K // D + h)),
            pl.BlockSpec((S, D), lambda h: (0, COL_V // D + h)),
            pl.BlockSpec((1, D), lambda h: (0, 0)),
            pl.BlockSpec((1, D), lambda h: (0, 0)),
        ],
        out_specs=[
            pl.BlockSpec((1, nb, D, BS), lambda h: (h, 0, 0, 0)),
            pl.BlockSpec((1, S, D), lambda h: (h, 0, 0)),
            pl.BlockSpec((1, nb, MOBA_VT_ROWS, BS), lambda h: (h, 0, 0, 0)),
            pl.BlockSpec((1, nb, nbp, BS), lambda h: (h, 0, 0, 0)),
        ],
        out_shape=[
            jax.ShapeDtypeStruct((H, nb, D, BS), BF16),
            jax.ShapeDtypeStruct((H, S, D), BF16),
            jax.ShapeDtypeStruct((H, nb, MOBA_VT_ROWS, BS), BF16),
            jax.ShapeDtypeStruct((H, nb, nbp, BS), F32),
        ],
        scratch_shapes=[pltpu.VMEM((nbp, D), F32)],
        compiler_params=_cparams(("arbitrary",)),
        name="qk_prep",
    )(proj, proj, proj, q_norm_w, k_norm_w)


def _rel_bucket(dist):
    n = jnp.maximum(dist, 0)
    max_exact = REL_BUCKETS // 2
    nf = jnp.maximum(n, 1).astype(F32)
    large = max_exact + (jnp.log(nf / max_exact) / math.log(REL_MAX_DIST / max_exact)
                         * (REL_BUCKETS - max_exact)).astype(jnp.int32)
    large = jnp.minimum(large, REL_BUCKETS - 1)
    return jnp.where(n < max_exact, n, large)


MOBA_HEADS_PER_STEP = 4
MOBA_VT_ROWS = ATTN_HEAD_DIM + 2 * SUBLANES


def _moba_kernel(tbl_ref, qt_ref, k_ref, vt_ref, pen_ref, o_ref,
                 bown_scr, bprev_scr, m_scr, acc_scr, s0_scr, s1_scr, s2_scr, s3_scr,
                 p0_scr, p1_scr, p2_scr, p3_scr, al0_scr, al1_scr, al2_scr, al3_scr):
    hp = pl.program_id(0)
    i = pl.program_id(1)
    BS, HB = MOBA_BLOCK, MOBA_HEADS_PER_STEP
    D = ATTN_HEAD_DIM

    @pl.when(i == 0)
    def _():
        kl = lax.broadcasted_iota(jnp.int32, (BS, BS), 0)
        ql = lax.broadcasted_iota(jnp.int32, (BS, BS), 1)
        d = ql - kl
        b_own = _rel_bucket(d)
        b_prev = _rel_bucket(d + BS)
        for a in range(HB):
            h = hp * HB + a
            far = tbl_ref[h, REL_BUCKETS - 1]

            def lookup(bucket):
                val = jnp.zeros((BS, BS), F32)
                for b in range(REL_BUCKETS):
                    val = jnp.where(bucket == b, tbl_ref[h, b], val)
                return (val - far) * LOG2E

            bown_scr[a] = jnp.where(d >= 0, lookup(b_own), -jnp.inf)
            bprev_scr[a] = lookup(b_prev)

    def krows(a, j):
        return k_ref[a, pl.ds(pl.multiple_of(j * BS, BS), BS), :]

    jp = jnp.maximum(i - 1, 0)
    no_prev = jnp.where(i == 0, -jnp.inf, 0.0)
    n_far = jnp.maximum(i - 1, 0)
    nb = k_ref.shape[1] // BS

    def clampj(j):
        return jnp.clip(j, 0, nb - 1)

    def scores(j, s_slot):
        for a in range(HB):
            s_slot[a] = _dot(krows(a, clampj(j)), qt_ref[a, 0])

    def softmax(j, s_slot, p_slot, al_slot, mask_tail, bias_scr=None):
        for a in range(HB):
            pr = pen_ref[a, 0, pl.ds(clampj(j), 1), :]
            if mask_tail is True:
                pr = pr + jnp.where(j >= n_far, -jnp.inf, 0.0)
            elif mask_tail is not False:
                pr = pr + mask_tail
            s = s_slot[a]
            if bias_scr is not None:
                s = s + bias_scr[a]
            m_old = m_scr[a]
            m_new = jnp.maximum(m_old, jnp.max(s, axis=0, keepdims=True) + pr)
            p_slot[a] = jnp.exp2(s - (m_new - pr)).astype(BF16)
            al_slot[a] = jnp.exp2(m_old - m_new)
            m_scr[a] = m_new

    def accumulate(j, p_slot, al_slot, maybe_prev=False):
        jv = clampj(j)
        if maybe_prev:
            jv = jnp.where(j < 0, jp, jv)
        for a in range(HB):
            acc_scr[a] = al_slot[a] * acc_scr[a] + _dot(vt_ref[a, jv], p_slot[a])

    s_slots = (s0_scr, s1_scr, s2_scr, s3_scr)
    p_slots = (p0_scr, p1_scr, p2_scr, p3_scr)
    al_slots = (al0_scr, al1_scr, al2_scr, al3_scr)
    scores(i, s2_scr)
    scores(jp, s3_scr)
    scores(0, s0_scr)
    for a in range(HB):
        s = s2_scr[a] + bown_scr[a]
        m0 = jnp.max(s, axis=0, keepdims=True)
        m_scr[a] = m0
        p2_scr[a] = jnp.exp2(s - m0).astype(BF16)
    for a in range(HB):
        acc_scr[a] = _dot(vt_ref[a, i], p2_scr[a])
    softmax(jp, s3_scr, p3_scr, al3_scr, no_prev, bias_scr=bprev_scr)

    def quad_loop(g, carry):
        j = 4 * g
        for t in range(4):
            scores(j + t + 1, s_slots[(t + 1) % 4])
            accumulate(j + t - 1, p_slots[(t - 1) % 4], al_slots[(t - 1) % 4], maybe_prev=(t == 0))
            softmax(j + t, s_slots[t], p_slots[t], al_slots[t], False)
        return carry

    n_quads = n_far // 4
    lax.fori_loop(0, n_quads, quad_loop, 0)

    def pair_loop(g, carry):
        j = 4 * n_quads + 2 * g
        scores(j + 1, s1_scr)
        accumulate(j - 1, p3_scr, al3_scr, maybe_prev=True)
        softmax(j, s0_scr, p2_scr, al2_scr, False)
        scores(j + 2, s0_scr)
        accumulate(j, p2_scr, al2_scr)
        softmax(j + 1, s1_scr, p3_scr, al3_scr, True)
        return carry

    n_pairs = (n_far - 4 * n_quads + 1) // 2
    lax.fori_loop(0, n_pairs, pair_loop, 0)
    accumulate(4 * n_quads + 2 * n_pairs - 1, p3_scr, al3_scr, maybe_prev=True)
    for a in range(HB):
        acc = acc_scr[a]
        o_ref[:, a * D:(a + 1) * D] = (acc[0:D, :] / acc[D:D + 1, :]).T.astype(o_ref.dtype)


def _moba(qt, kn, vt, pen, table_h):
    H, nb, D, BS = qt.shape
    S = nb * BS
    nbp = pen.shape[2]
    HB = MOBA_HEADS_PER_STEP
    return pl.pallas_call(
        _moba_kernel,
        grid=(H // HB, nb),
        in_specs=[
            pl.BlockSpec(memory_space=pltpu.SMEM),
            pl.BlockSpec((HB, 1, D, BS), lambda hp, i: (hp, i, 0, 0)),
            pl.BlockSpec((HB, S, D), lambda hp, i: (hp, 0, 0)),
            pl.BlockSpec((HB, nb, MOBA_VT_ROWS, BS), lambda hp, i: (hp, 0, 0, 0)),
            pl.BlockSpec((HB, 1, nbp, BS), lambda hp, i: (hp, i, 0, 0)),
        ],
        out_specs=pl.BlockSpec((BS, HB * D), lambda hp, i: (i, hp)),
        out_shape=jax.ShapeDtypeStruct((S, H * D), BF16),
        scratch_shapes=[
            pltpu.VMEM((HB, BS, BS), F32),
            pltpu.VMEM((HB, BS, BS), F32),
            pltpu.VMEM((HB, 1, BS), F32),
            pltpu.VMEM((HB, MOBA_VT_ROWS, BS), F32),
        ] + [pltpu.VMEM((HB, BS, BS), F32)] * 4 + [pltpu.VMEM((HB, BS, BS), BF16)] * 4
        + [pltpu.VMEM((HB, 1, BS), F32)] * 4,
        compiler_params=_cparams(("arbitrary", "arbitrary")),
        name="moba",
    )(table_h, qt, kn, vt, pen)


def _mix_kernel(ys_ref, ya_ref, ws_ref, wa_ref, gs_ref, ga_ref, bs_ref, ba_ref, o_ref):
    gs = _sigmoid(gs_ref[...].astype(F32) + bs_ref[...])
    ga = _sigmoid(ga_ref[...].astype(F32) + ba_ref[...])
    o_ref[...] = (gs * _dot(ys_ref[...], ws_ref[...])
                  + ga * _dot(ya_ref[...], wa_ref[...])).astype(o_ref.dtype)


def _mix(y_ssd, y_attn, w_ssm_out, w_attn_out, proj, b_gate, tm=1024, tn=512):
    S = y_ssd.shape[0]
    tm = min(tm, S)
    gs_col = COL_G // tn
    ga_col = (COL_G + D_MODEL) // tn
    nb = D_MODEL // tn
    return pl.pallas_call(
        _mix_kernel,
        grid=(S // tm, D_MODEL // tn),
        in_specs=[
            pl.BlockSpec((tm, D_INNER), lambda m, n: (m, 0)),
            pl.BlockSpec((tm, D_ATTN), lambda m, n: (m, 0)),
            pl.BlockSpec((D_INNER, tn), lambda m, n: (0, n)),
            pl.BlockSpec((D_ATTN, tn), lambda m, n: (0, n)),
            pl.BlockSpec((tm, tn), lambda m, n: (m, gs_col + n)),
            pl.BlockSpec((tm, tn), lambda m, n: (m, ga_col + n)),
            pl.BlockSpec((1, tn), lambda m, n: (0, n)),
            pl.BlockSpec((1, tn), lambda m, n: (0, nb + n)),
        ],
        out_specs=pl.BlockSpec((tm, tn), lambda m, n: (m, n)),
        out_shape=jax.ShapeDtypeStruct((S, D_MODEL), BF16),
        compiler_params=_cparams(("arbitrary", "arbitrary")),
        name="mix",
    )(y_ssd, y_attn, w_ssm_out, w_attn_out, proj, proj, b_gate, b_gate)


def _out_norm_kernel(x_ref, mix_ref, w_ref, nw_ref, x1_ref, h2_ref):
    x1 = x_ref[...] + _dot(mix_ref[...], w_ref[...])
    x1_ref[...] = x1
    ms = jnp.mean(x1 * x1, axis=-1, keepdims=True)
    h2_ref[...] = (x1 * lax.rsqrt(ms + EPS) * nw_ref[...]).astype(h2_ref.dtype)


def _out_norm(x, mix, w_out, ffn_norm_w, tm=512):
    S = x.shape[0]
    tm = min(tm, S)
    return pl.pallas_call(
        _out_norm_kernel,
        grid=(S // tm,),
        in_specs=[
            pl.BlockSpec((tm, D_MODEL), lambda m: (m, 0)),
            pl.BlockSpec((tm, D_MODEL), lambda m: (m, 0)),
            pl.BlockSpec((D_MODEL, D_MODEL), lambda m: (0, 0)),
            pl.BlockSpec((1, D_MODEL), lambda m: (0, 0)),
        ],
        out_specs=[
            pl.BlockSpec((tm, D_MODEL), lambda m: (m, 0)),
            pl.BlockSpec((tm, D_MODEL), lambda m: (m, 0)),
        ],
        out_shape=[
            jax.ShapeDtypeStruct((S, D_MODEL), F32),
            jax.ShapeDtypeStruct((S, D_MODEL), BF16),
        ],
        compiler_params=_cparams(("arbitrary",)),
        name="out_norm",
    )(x, mix, w_out, ffn_norm_w)


def _ffn_up_kernel(h_ref, wg32_ref, wu32_ref, cwg_ref, cwu_ref, cbg_ref, cbu_ref, o_ref,
                   tailg, tailu, padg, padu, wg_ref, wu_ref):
    tm = h_ref.shape[0]

    @pl.when(pl.program_id(1) == 0)
    def _():
        tailg[...] = jnp.zeros(tailg.shape, F32)
        tailu[...] = jnp.zeros(tailu.shape, F32)
        wg_ref[...] = wg32_ref[...].astype(BF16)
        wu_ref[...] = wu32_ref[...].astype(BF16)

    def conv(w_ref, cw_ref, cb_ref, tail, pad):
        pad[0:SUBLANES, :] = tail[...]
        pad[SUBLANES:SUBLANES + tm, :] = _dot(h_ref[...], w_ref[...])
        tail[...] = pad[tm:tm + SUBLANES, :]
        acc = cb_ref[...]
        for k in range(FFN_CONV):
            off = SUBLANES - (FFN_CONV - 1) + k
            acc = acc + cw_ref[k:k + 1, :] * pad[off:off + tm, :]
        return acc

    ug = conv(wg_ref, cwg_ref, cbg_ref, tailg, padg)
    uu = conv(wu_ref, cwu_ref, cbu_ref, tailu, padu)
    o_ref[...] = (_silu(ug) * uu).astype(o_ref.dtype)


def _ffn_up(h2, w_up, conv_w, conv_b, tm=1024, tn=512):
    S = h2.shape[0]
    tm = min(tm, S)
    nb = D_FF // tn
    return pl.pallas_call(
        _ffn_up_kernel,
        grid=(nb, S // tm),
        in_specs=[
            pl.BlockSpec((tm, D_MODEL), lambda n, m: (m, 0)),
            pl.BlockSpec((D_MODEL, tn), lambda n, m: (0, n)),
            pl.BlockSpec((D_MODEL, tn), lambda n, m: (0, nb + n)),
            pl.BlockSpec((FFN_CONV, tn), lambda n, m: (0, n)),
            pl.BlockSpec((FFN_CONV, tn), lambda n, m: (0, nb + n)),
            pl.BlockSpec((1, tn), lambda n, m: (0, n)),
            pl.BlockSpec((1, tn), lambda n, m: (0, nb + n)),
        ],
        out_specs=pl.BlockSpec((tm, tn), lambda n, m: (m, n)),
        out_shape=jax.ShapeDtypeStruct((S, D_FF), BF16),
        scratch_shapes=[
            pltpu.VMEM((SUBLANES, tn), F32),
            pltpu.VMEM((SUBLANES, tn), F32),
            pltpu.VMEM((tm + SUBLANES, tn), F32),
            pltpu.VMEM((tm + SUBLANES, tn), F32),
            pltpu.VMEM((D_MODEL, tn), BF16),
            pltpu.VMEM((D_MODEL, tn), BF16),
        ],
        compiler_params=_cparams(("arbitrary", "arbitrary")),
        name="ffn_up",
    )(h2, w_up, w_up, conv_w, conv_w, conv_b, conv_b)


def _ffn_down_kernel(a_ref, w_ref, x1_ref, o_ref):
    o_ref[...] = x1_ref[...] + _dot(a_ref[...], w_ref[...])


def _ffn_down(act, w_down, x1, tm=1024, tn=512):
    S = act.shape[0]
    tm = min(tm, S)
    return pl.pallas_call(
        _ffn_down_kernel,
        grid=(S // tm, D_MODEL // tn),
        in_specs=[
            pl.BlockSpec((tm, D_FF), lambda m, n: (m, 0)),
            pl.BlockSpec((D_FF, tn), lambda m, n: (0, n)),
            pl.BlockSpec((tm, tn), lambda m, n: (m, n)),
        ],
        out_specs=pl.BlockSpec((tm, tn), lambda m, n: (m, n)),
        out_shape=jax.ShapeDtypeStruct((S, D_MODEL), F32),
        compiler_params=_cparams(("arbitrary", "arbitrary")),
        name="ffn_down",
    )(act, w_down, x1)


def _pad_lanes(v):
    return jnp.pad(v, ((0, 0), (0, LANES - v.shape[1])))


def _layer(x, attn_norm_w, w_in, b_gate, ssm_conv_w, ssm_conv_b, ssm_dt_bias, ssm_a_log, ssm_d,
           ssm_norm_w, q_norm_w, k_norm_w, rel_bias, w_ssm_out, w_attn_out, w_out, ffn_norm_w,
           w_up, ffn_conv_w, ffn_conv_b, w_down):
    o_z, o_xbc = 0, D_INNER
    o_dt = o_xbc + D_XBC
    o_q = o_dt + SSM_HEADS
    o_g = o_q + 3 * D_ATTN
    wt = w_in.T
    wt_dt = jnp.pad(wt[o_dt:o_q], ((0, LANES - SSM_HEADS), (0, 0)))

    h, dt_raw = _norm_dt(x, attn_norm_w[None, :], wt_dt)
    proj = _in_proj(h, wt, o_dt)
    y_ssd = _ssd(proj, dt_raw, ssm_conv_w, ssm_conv_b[None, :], _pad_lanes(ssm_dt_bias[None, :]),
                 _pad_lanes(ssm_a_log[None, :]), jnp.repeat(ssm_d, SSM_HEAD_DIM)[None, :],
                 ssm_norm_w[None, :])
    qt, kn, vt, pen = _qk_prep(proj, q_norm_w[None, :], k_norm_w[None, :])
    y_attn = _moba(qt, kn, vt, pen, rel_bias.T)
    mix = _mix(y_ssd, y_attn, w_ssm_out.astype(BF16), w_attn_out.astype(BF16), proj, b_gate[None, :])
    x1, h2 = _out_norm(x, mix, w_out.astype(BF16), ffn_norm_w[None, :])
    act = _ffn_up(h2, w_up, ffn_conv_w, ffn_conv_b[None, :])
    return _ffn_down(act, w_down.astype(BF16), x1)


def kernel(x, attn_norm_w, w_in, b_gate, ssm_conv_w, ssm_conv_b, ssm_dt_bias, ssm_a_log, ssm_d,
           ssm_norm_w, q_norm_w, k_norm_w, rel_bias, w_ssm_out, w_attn_out, w_out, ffn_norm_w,
           w_up, ffn_conv_w, ffn_conv_b, w_down):
    assert x.shape[0] == 1 and attn_norm_w.shape[0] == 1
    out = _layer(x[0], attn_norm_w[0], w_in[0], b_gate[0], ssm_conv_w[0], ssm_conv_b[0],
                 ssm_dt_bias[0], ssm_a_log[0], ssm_d[0], ssm_norm_w[0], q_norm_w[0], k_norm_w[0],
                 rel_bias, w_ssm_out[0], w_attn_out[0], w_out[0], ffn_norm_w[0], w_up[0],
                 ffn_conv_w[0], ffn_conv_b[0], w_down[0])
    return out[None]
```

```python
import math

import jax
import jax.numpy as jnp
from jax import lax
from jax.experimental import pallas as pl
from jax.experimental.pallas import tpu as pltpu

F32 = jnp.float32
BF16 = jnp.bfloat16

D_MODEL = 2048
D_INNER = 4096
SSM_HEADS = 64
SSM_HEAD_DIM = 64
SSM_GROUPS = 8
SSM_HEADS_PER_GROUP = SSM_HEADS // SSM_GROUPS
SSM_STATE = 128
SSM_CONV = 4
SSM_CHUNK = 256
GROUP_CH = D_INNER // SSM_GROUPS
SSD_GROUPS_PER_STEP = 8
D_XBC = D_INNER + 2 * SSM_GROUPS * SSM_STATE
ATTN_HEADS = 16
ATTN_HEAD_DIM = 128
D_ATTN = ATTN_HEADS * ATTN_HEAD_DIM
MOBA_BLOCK = 256
MOBA_TOPK = 3
REL_BUCKETS = 32
REL_MAX_DIST = 128
D_FF = 5632
FFN_CONV = 3
EPS = 1e-6
LOG2E = 1.4426950408889634

LANES = 128
SUBLANES = 8
D_PROJ = D_XBC + 3 * D_ATTN + 2 * D_INNER
COL_Z = 0
COL_XBC = COL_Z + D_INNER
COL_Q = COL_XBC + D_XBC
COL_K = COL_Q + D_ATTN
COL_V = COL_K + D_ATTN
COL_G = COL_V + D_ATTN
VMEM_LIMIT = 56 * 1024 * 1024


def _cparams(sem):
    return pltpu.CompilerParams(dimension_semantics=sem, vmem_limit_bytes=VMEM_LIMIT)


def _split3(a):
    hi = a.astype(BF16)
    r = a - hi.astype(F32)
    mid = r.astype(BF16)
    lo = (r - mid.astype(F32)).astype(BF16)
    return hi, mid, lo


def _dot(a, b):
    return jnp.dot(a, b, preferred_element_type=F32)


def _dot_nt(a, b):
    return lax.dot_general(a, b, (((1,), (1,)), ((), ())), preferred_element_type=F32)


def _sigmoid(x):
    return 0.5 + 0.5 * jnp.tanh(0.5 * x)


def _silu(x):
    h = 0.5 * x
    return h + h * jnp.tanh(h)


def _norm_dt_kernel(x_ref, nw_ref, wdt_ref, h_ref, dt_ref):
    x = x_ref[...]
    ms = jnp.mean(x * x, axis=-1, keepdims=True)
    h = x * lax.rsqrt(ms + EPS) * nw_ref[...]
    h_ref[...] = h.astype(BF16)
    h_hi, h_mid, _ = _split3(h)
    w_hi, w_mid, _ = _split3(wdt_ref[...])
    dt_ref[...] = _dot_nt(h_hi, w_hi) + _dot_nt(h_hi, w_mid) + _dot_nt(h_mid, w_hi)


def _norm_dt(x, norm_w, wt_dt, tm=512):
    S = x.shape[0]
    tm = min(tm, S)
    return pl.pallas_call(
        _norm_dt_kernel,
        grid=(S // tm,),
        in_specs=[
            pl.BlockSpec((tm, D_MODEL), lambda m: (m, 0)),
            pl.BlockSpec((1, D_MODEL), lambda m: (0, 0)),
            pl.BlockSpec((LANES, D_MODEL), lambda m: (0, 0)),
        ],
        out_specs=[
            pl.BlockSpec((tm, D_MODEL), lambda m: (m, 0)),
            pl.BlockSpec((tm, LANES), lambda m: (m, 0)),
        ],
        out_shape=[
            jax.ShapeDtypeStruct((S, D_MODEL), BF16),
            jax.ShapeDtypeStruct((S, LANES), F32),
        ],
        compiler_params=_cparams(("arbitrary",)),
        name="norm_dt",
    )(x, norm_w, wt_dt)


def _in_proj_kernel(h_ref, w32_ref, o_ref, w_scr):
    @pl.when(pl.program_id(1) == 0)
    def _():
        w_scr[...] = w32_ref[...].astype(BF16)

    o_ref[...] = _dot_nt(h_ref[...], w_scr[...]).astype(o_ref.dtype)


def _in_proj(h, wt, dt_row0, tm=2048, tn=1024):
    S = h.shape[0]
    tm = min(tm, S)
    assert dt_row0 % tn == 0 and SSM_HEADS % SUBLANES == 0
    na = dt_row0 // tn

    def w_rows(n, m):
        return (pl.multiple_of(n * tn + jnp.where(n >= na, SSM_HEADS, 0), SSM_HEADS), 0)

    return pl.pallas_call(
        _in_proj_kernel,
        grid=(D_PROJ // tn, S // tm),
        in_specs=[
            pl.BlockSpec((tm, D_MODEL), lambda n, m: (m, 0)),
            pl.BlockSpec((pl.Element(tn), pl.Element(D_MODEL)), w_rows),
        ],
        out_specs=pl.BlockSpec((tm, tn), lambda n, m: (m, n)),
        out_shape=jax.ShapeDtypeStruct((S, D_PROJ), BF16),
        scratch_shapes=[pltpu.VMEM((tn, D_MODEL), BF16)],
        compiler_params=_cparams(("arbitrary", "arbitrary")),
        name="in_proj",
    )(h, wt)


def _conv_silu(raw_ref, tail_ref, pad_ref, w_ref, b_ref, g, taps):
    L = raw_ref.shape[0]
    pad_ref[0:SUBLANES, :] = tail_ref[g]
    pad_ref[SUBLANES:SUBLANES + L, :] = raw_ref[...].astype(F32)
    tail_ref[g] = pad_ref[L:L + SUBLANES, :]
    acc = b_ref[...]
    for k in range(taps):
        off = SUBLANES - (taps - 1) + k
        acc = acc + w_ref[k:k + 1, :] * pad_ref[off:off + L, :]
    return _silu(acc)


def _ssd_kernel(x_ref, b_ref, c_ref, z_ref, dtr_ref, cwx_ref, cwb_ref, cwc_ref,
                cbx_ref, cbb_ref, cbc_ref, dtb_ref, alog_ref, dsk_ref, nw_ref,
                o_ref,
                tailx, tailb, tailc, padx, padb, padc, acs_scr, acst_scr, lbt_scr, ht_scr, rhs_scr):
    c = pl.program_id(0)
    gp = pl.program_id(1)
    L = SSM_CHUNK
    QW = 4 * SSM_HEAD_DIM
    GPS = SSD_GROUPS_PER_STEP

    @pl.when(c == 0)
    def _():
        for gg in range(GPS):
            g0 = gp * GPS + gg
            tailx[g0] = jnp.zeros(tailx.shape[1:], F32)
            tailb[g0] = jnp.zeros(tailb.shape[1:], F32)
            tailc[g0] = jnp.zeros(tailc.shape[1:], F32)
            ht_scr[g0] = jnp.zeros(ht_scr.shape[1:], F32)
            rhs_scr[gg] = jnp.zeros(rhs_scr.shape[1:], BF16)

    @pl.when(gp == 0)
    def _():
        t = dtr_ref[...] + dtb_ref[...]
        dt = jnp.maximum(t, 0.0) + jnp.log(1.0 + jnp.exp(-jnp.abs(t)))
        a = -jnp.exp(alog_ref[...])
        da = dt * a
        ri = lax.broadcasted_iota(jnp.int32, (L, L), 0)
        ci = lax.broadcasted_iota(jnp.int32, (L, L), 1)
        tri = jnp.where(ri >= ci, 1.0, 0.0).astype(BF16)
        hi, mid, lo = _split3(da)
        a_cs = _dot(tri, hi) + _dot(tri, mid) + _dot(tri, lo)
        a2 = a_cs * LOG2E
        a2t = a2.T
        acs_scr[...] = a2
        acst_scr[...] = a2t
        lbt_scr[...] = a2t - jnp.log(dt.T) * LOG2E

    for gg in range(GPS):
        xs = slice(gg * GROUP_CH, (gg + 1) * GROUP_CH)
        ns = slice(gg * SSM_STATE, (gg + 1) * SSM_STATE)
        _ssd_group(gp * GPS + gg,
                   x_ref.at[:, xs], b_ref.at[:, ns], c_ref.at[:, ns], z_ref.at[:, xs],
                   cwx_ref.at[:, xs], cwb_ref.at[:, ns], cwc_ref.at[:, ns],
                   cbx_ref.at[:, xs], cbb_ref.at[:, ns], cbc_ref.at[:, ns],
                   dsk_ref.at[:, xs], nw_ref.at[:, xs], o_ref.at[:, xs],
                   tailx, tailb, tailc, padx.at[gg], padb.at[gg], padc.at[gg],
                   acs_scr, acst_scr, lbt_scr, ht_scr,
                   [rhs_scr.at[gg, q] for q in range(GROUP_CH // QW)])


def _ssd_group(g, x_ref, b_ref, c_ref, z_ref, cwx_ref, cwb_ref, cwc_ref, cbx_ref, cbb_ref, cbc_ref,
               dsk_ref, nw_ref, o_ref, tailx, tailb, tailc, padx, padb, padc,
               acs_scr, acst_scr, lbt_scr, ht_scr, rhs_refs):
    L = SSM_CHUNK
    QW = 4 * SSM_HEAD_DIM
    xa = _conv_silu(x_ref, tailx, padx, cwx_ref, cbx_ref, g, SSM_CONV)
    ba = _conv_silu(b_ref, tailb, padb, cwb_ref, cbb_ref, g, SSM_CONV)
    ca = _conv_silu(c_ref, tailc, padc, cwc_ref, cbc_ref, g, SSM_CONV)

    xb = xa.astype(BF16)
    bt = ba.T
    cb = _dot(ca.astype(BF16), bt.astype(BF16))
    a_cs = acs_scr[...]
    ri = lax.broadcasted_iota(jnp.int32, (L, L), 0)
    ci = lax.broadcasted_iota(jnp.int32, (L, L), 1)
    causal = ri >= ci
    lane_h = lax.broadcasted_iota(jnp.int32, (L, LANES), 1)
    lane_q = lax.broadcasted_iota(jnp.int32, (1, QW), 1)

    NH = QW // SSM_HEAD_DIM
    ys = []
    for q in range(GROUP_CH // QW):
        rhs = rhs_refs[q]
        xq = xb[:, q * QW:(q + 1) * QW]
        hq = ht_scr[g, :, q * QW:(q + 1) * QW]
        hqb = hq.astype(BF16)
        mhs, ces, bws = [], [], []
        cdq = jnp.zeros((1, QW), F32)
        for j in range(NH):
            h = g * SSM_HEADS_PER_GROUP + q * NH + j
            cols = slice(j * SSM_HEAD_DIM, (j + 1) * SSM_HEAD_DIM)
            rhs[j * L:(j + 1) * L, cols] = xq[:, cols]
            rhs[NH * L + j * SSM_STATE:NH * L + (j + 1) * SSM_STATE, cols] = hqb[:, cols]
            a_row = acst_scr[pl.ds(h, 1), :]
            b_row = lbt_scr[pl.ds(h, 1), :]
            a_col = jnp.sum(jnp.where(lane_h == h, a_cs, 0.0), axis=1, keepdims=True)
            dmat = jnp.where(causal, a_col - b_row, -jnp.inf)
            mhs.append((cb * jnp.exp2(dmat)).astype(BF16))
            ces.append((ca * jnp.exp2(a_col)).astype(BF16))
            a_last = a_row[:, L - 1:L]
            w_row = jnp.exp2(a_last - b_row)
            bws.append((bt * w_row).astype(BF16))
            lm = (lane_q >= j * SSM_HEAD_DIM) & (lane_q < (j + 1) * SSM_HEAD_DIM)
            cdq = jnp.where(lm, jnp.exp2(a_last), cdq)
        acc = _dot(jnp.concatenate(mhs + ces, axis=1), rhs[...])
        st = _dot(jnp.concatenate(bws, axis=1), rhs[0:NH * L, :])
        ht_scr[g, :, q * QW:(q + 1) * QW] = hq * cdq + st
        ys.append(acc + xa[:, q * QW:(q + 1) * QW] * dsk_ref[:, q * QW:(q + 1) * QW])

    z = z_ref[...].astype(F32)
    sz = _silu(z)
    ygs = [ys[q] * sz[:, q * QW:(q + 1) * QW] for q in range(len(ys))]
    ss = sum(jnp.sum(v * v, axis=1, keepdims=True) for v in ygs)
    scale = lax.rsqrt(ss * (1.0 / GROUP_CH) + EPS)
    for q in range(len(ys)):
        o_ref[:, q * QW:(q + 1) * QW] = (
            ygs[q] * scale * nw_ref[:, q * QW:(q + 1) * QW]).astype(o_ref.dtype)


def _ssd(proj, dt_raw, conv_w, conv_b, dt_bias, a_log, d_exp, norm_w):
    S = proj.shape[0]
    L, G, N = SSM_CHUNK, SSM_GROUPS, SSM_STATE
    GPS = SSD_GROUPS_PER_STEP
    XW, NW = GPS * GROUP_CH, GPS * N
    bcol = D_INNER // NW
    ccol = bcol + G // GPS
    pxcol = COL_XBC // XW
    pbcol = COL_XBC // NW + bcol
    pccol = pbcol + G // GPS
    zcol = COL_Z // XW
    return pl.pallas_call(
        _ssd_kernel,
        grid=(S // L, G // GPS),
        in_specs=[
            pl.BlockSpec((L, XW), lambda c, g: (c, pxcol + g)),
            pl.BlockSpec((L, NW), lambda c, g: (c, pbcol + g)),
            pl.BlockSpec((L, NW), lambda c, g: (c, pccol + g)),
            pl.BlockSpec((L, XW), lambda c, g: (c, zcol + g)),
            pl.BlockSpec((L, LANES), lambda c, g: (c, 0)),
            pl.BlockSpec((SSM_CONV, XW), lambda c, g: (0, g)),
            pl.BlockSpec((SSM_CONV, NW), lambda c, g: (0, bcol + g)),
            pl.BlockSpec((SSM_CONV, NW), lambda c, g: (0, ccol + g)),
            pl.BlockSpec((1, XW), lambda c, g: (0, g)),
            pl.BlockSpec((1, NW), lambda c, g: (0, bcol + g)),
            pl.BlockSpec((1, NW), lambda c, g: (0, ccol + g)),
            pl.BlockSpec((1, LANES), lambda c, g: (0, 0)),
            pl.BlockSpec((1, LANES), lambda c, g: (0, 0)),
            pl.BlockSpec((1, XW), lambda c, g: (0, g)),
            pl.BlockSpec((1, XW), lambda c, g: (0, g)),
        ],
        out_specs=pl.BlockSpec((L, XW), lambda c, g: (c, g)),
        out_shape=jax.ShapeDtypeStruct((S, D_INNER), BF16),
        scratch_shapes=[
            pltpu.VMEM((G, SUBLANES, GROUP_CH), F32),
            pltpu.VMEM((G, SUBLANES, N), F32),
            pltpu.VMEM((G, SUBLANES, N), F32),
            pltpu.VMEM((GPS, L + SUBLANES, GROUP_CH), F32),
            pltpu.VMEM((GPS, L + SUBLANES, N), F32),
            pltpu.VMEM((GPS, L + SUBLANES, N), F32),
            pltpu.VMEM((L, LANES), F32),
            pltpu.VMEM((LANES, L), F32),
            pltpu.VMEM((LANES, L), F32),
            pltpu.VMEM((G, N, GROUP_CH), F32),
            pltpu.VMEM((GPS, GROUP_CH // (4 * SSM_HEAD_DIM), 4 * (L + N), 4 * SSM_HEAD_DIM), BF16),
        ],
        compiler_params=_cparams(("arbitrary", "arbitrary")),
        name="ssd",
    )(proj, proj, proj, proj, dt_raw, conv_w, conv_w, conv_w, conv_b, conv_b, conv_b,
      dt_bias, a_log, d_exp, norm_w)


def _qk_prep_kernel(q_ref, k_ref, v_ref, qw_ref, kw_ref, qt_ref, kn_ref, vt_ref, pen_ref, kbar_scr):
    S = q_ref.shape[0]
    BS = MOBA_BLOCK
    nb = S // BS
    nbp = kbar_scr.shape[0]
    kbar_scr[...] = jnp.zeros(kbar_scr.shape, F32)
    eye = (lax.broadcasted_iota(jnp.int32, (ATTN_HEAD_DIM, ATTN_HEAD_DIM), 0)
           == lax.broadcasted_iota(jnp.int32, (ATTN_HEAD_DIM, ATTN_HEAD_DIM), 1)).astype(BF16)

    def kloop(b, carry):
        rows = pl.ds(pl.multiple_of(b * BS, BS), BS)
        k = k_ref[rows, :].astype(F32)
        kn = k * lax.rsqrt(jnp.mean(k * k, axis=-1, keepdims=True) + EPS) * kw_ref[...]
        kn_ref[0, rows, :] = kn.astype(BF16)
        kbar_scr[pl.ds(b, 1), :] = jnp.mean(kn, axis=0, keepdims=True)
        vt_ref[0, b, 0:ATTN_HEAD_DIM, :] = _dot_nt(eye, v_ref[rows, :]).astype(BF16)
        vt_ref[0, b, ATTN_HEAD_DIM:, :] = jnp.ones((MOBA_VT_ROWS - ATTN_HEAD_DIM, BS), BF16)
        return carry

    lax.fori_loop(0, nb, kloop, 0, unroll=4)
    kb_hi, kb_mid, _ = _split3(kbar_scr[...])

    def qloop(i, carry):
        rows = pl.ds(pl.multiple_of(i * BS, BS), BS)
        q = q_ref[rows, :].astype(F32)
        qn = q * lax.rsqrt(jnp.mean(q * q, axis=-1, keepdims=True) + EPS) * qw_ref[...]
        qnt = (qn * (ATTN_HEAD_DIM ** -0.5)).T
        qt_ref[0, i] = (qnt * LOG2E).astype(BF16)
        q_hi, q_mid, _ = _split3(qnt)
        gate = _dot(kb_hi, q_hi) + _dot(kb_mid, q_hi) + _dot(kb_hi, q_mid)
        blk = lax.broadcasted_iota(jnp.int32, gate.shape, 0)
        blk_f = blk.astype(F32)
        gate = jnp.where(blk < i, gate, -jnp.inf)
        pen = jnp.full(gate.shape, -jnp.inf, F32)
        for _ in range(MOBA_TOPK):
            mx = jnp.max(gate, axis=0, keepdims=True)
            cand = jnp.where((gate == mx) & (mx > -jnp.inf), blk_f, float(nbp))
            idx = jnp.min(cand, axis=0, keepdims=True)
            hit = blk_f == idx
            pen = jnp.where(hit, 0.0, pen)
            gate = jnp.where(hit, -jnp.inf, gate)
        pen_ref[0, i] = pen
        return carry

    lax.fori_loop(0, nb, qloop, 0, unroll=8)


def _qk_prep(proj, q_norm_w, k_norm_w):
    S = proj.shape[0]
    H, D, BS = ATTN_HEADS, ATTN_HEAD_DIM, MOBA_BLOCK
    nb = S // BS
    nbp = -(-nb // SUBLANES) * SUBLANES
    return pl.pallas_call(
        _qk_prep_kernel,
        grid=(H,),
        in_specs=[
            pl.BlockSpec((S, D), lambda h: (0, COL_Q // D + h)),
            pl.BlockSpec((S, D), lambda h: (0, COL_K // D + h)),
            pl.BlockSpec((S, D), lambda h: (0, COL_V // D + h)),
            pl.BlockSpec((1, D), lambda h: (0, 0)),
            pl.BlockSpec((1, D), lambda h: (0, 0)),
        ],
        out_specs=[
            pl.BlockSpec((1, nb, D, BS), lambda h: (h, 0, 0, 0)),
            pl.BlockSpec((1, S, D), lambda h: (h, 0, 0)),
            pl.BlockSpec((1, nb, MOBA_VT_ROWS, BS), lambda h: (h, 0, 0, 0)),
            pl.BlockSpec((1, nb, nbp, BS), lambda h: (h, 0, 0, 0)),
        ],
        out_shape=[
            jax.ShapeDtypeStruct((H, nb, D, BS), BF16),
            jax.ShapeDtypeStruct((H, S, D), BF16),
            jax.ShapeDtypeStruct((H, nb, MOBA_VT_ROWS, BS), BF16),
            jax.ShapeDtypeStruct((H, nb, nbp, BS), F32),
        ],
        scratch_shapes=[pltpu.VMEM((nbp, D), F32)],
        compiler_params=_cparams(("arbitrary",)),
        name="qk_prep",
    )(proj, proj, proj, q_norm_w, k_norm_w)


def _rel_bucket(dist):
    n = jnp.maximum(dist, 0)
    max_exact = REL_BUCKETS // 2
    nf = jnp.maximum(n, 1).astype(F32)
    large = max_exact + (jnp.log(nf / max_exact) / math.log(REL_MAX_DIST / max_exact)
                         * (REL_BUCKETS - max_exact)).astype(jnp.int32)
    large = jnp.minimum(large, REL_BUCKETS - 1)
    return jnp.where(n < max_exact, n, large)


MOBA_HEADS_PER_STEP = 4
MOBA_VT_ROWS = ATTN_HEAD_DIM + 2 * SUBLANES


def _moba_kernel(tbl_ref, qt_ref, k_ref, vt_ref, pen_ref, o_ref,
                 bown_scr, bprev_scr, m_scr, acc_scr, s0_scr, s1_scr, s2_scr, s3_scr,
                 p0_scr, p1_scr, p2_scr, p3_scr, al0_scr, al1_scr, al2_scr, al3_scr):
    hp = pl.program_id(0)
    i = pl.program_id(1)
    BS, HB = MOBA_BLOCK, MOBA_HEADS_PER_STEP
    D = ATTN_HEAD_DIM

    @pl.when(i == 0)
    def _():
        kl = lax.broadcasted_iota(jnp.int32, (BS, BS), 0)
        ql = lax.broadcasted_iota(jnp.int32, (BS, BS), 1)
        d = ql - kl
        b_own = _rel_bucket(d)
        b_prev = _rel_bucket(d + BS)
        for a in range(HB):
            h = hp * HB + a
            far = tbl_ref[h, REL_BUCKETS - 1]

            def lookup(bucket):
                val = jnp.zeros((BS, BS), F32)
                for b in range(REL_BUCKETS):
                    val = jnp.where(bucket == b, tbl_ref[h, b], val)
                return (val - far) * LOG2E

            bown_scr[a] = jnp.where(d >= 0, lookup(b_own), -jnp.inf)
            bprev_scr[a] = lookup(b_prev)

    def krows(a, j):
        return k_ref[a, pl.ds(pl.multiple_of(j * BS, BS), BS), :]

    jp = jnp.maximum(i - 1, 0)
    no_prev = jnp.where(i == 0, -jnp.inf, 0.0)
    n_far = jnp.maximum(i - 1, 0)
    nb = k_ref.shape[1] // BS

    def clampj(j):
        return jnp.clip(j, 0, nb - 1)

    def scores(j, s_slot):
        for a in range(HB):
            s_slot[a] = _dot(krows(a, clampj(j)), qt_ref[a, 0])

    def softmax(j, s_slot, p_slot, al_slot, mask_tail, bias_scr=None):
        for a in range(HB):
            pr = pen_ref[a, 0, pl.ds(clampj(j), 1), :]
            if mask_tail is True:
                pr = pr + jnp.where(j >= n_far, -jnp.inf, 0.0)
            elif mask_tail is not False:
                pr = pr + mask_tail
            s = s_slot[a]
            if bias_scr is not None:
                s = s + bias_scr[a]
            m_old = m_scr[a]
            m_new = jnp.maximum(m_old, jnp.max(s, axis=0, keepdims=True) + pr)
            p_slot[a] = jnp.exp2(s - (m_new - pr)).astype(BF16)
            al_slot[a] = jnp.exp2(m_old - m_new)
            m_scr[a] = m_new

    def accumulate(j, p_slot, al_slot, maybe_prev=False):
        jv = clampj(j)
        if maybe_prev:
            jv = jnp.where(j < 0, jp, jv)
        for a in range(HB):
            acc_scr[a] = al_slot[a] * acc_scr[a] + _dot(vt_ref[a, jv], p_slot[a])

    s_slots = (s0_scr, s1_scr, s2_scr, s3_scr)
    p_slots = (p0_scr, p1_scr, p2_scr, p3_scr)
    al_slots = (al0_scr, al1_scr, al2_scr, al3_scr)
    scores(i, s2_scr)
    scores(jp, s3_scr)
    scores(0, s0_scr)
    for a in range(HB):
        s = s2_scr[a] + bown_scr[a]
        m0 = jnp.max(s, axis=0, keepdims=True)
        m_scr[a] = m0
        p2_scr[a] = jnp.exp2(s - m0).astype(BF16)
    for a in range(HB):
        acc_scr[a] = _dot(vt_ref[a, i], p2_scr[a])
    softmax(jp, s3_scr, p3_scr, al3_scr, no_prev, bias_scr=bprev_scr)

    def quad_loop(g, carry):
        j = 4 * g
        for t in range(4):
            scores(j + t + 1, s_slots[(t + 1) % 4])
            accumulate(j + t - 1, p_slots[(t - 1) % 4], al_slots[(t - 1) % 4], maybe_prev=(t == 0))
            softmax(j + t, s_slots[t], p_slots[t], al_slots[t], False)
        return carry

    n_quads = n_far // 4
    lax.fori_loop(0, n_quads, quad_loop, 0)

    def pair_loop(g, carry):
        j = 4 * n_quads + 2 * g
        scores(j + 1, s1_scr)
        accumulate(j - 1, p3_scr, al3_scr, maybe_prev=True)
        softmax(j, s0_scr, p2_scr, al2_scr, False)
        scores(j + 2, s0_scr)
        accumulate(j, p2_scr, al2_scr)
        softmax(j + 1, s1_scr, p3_scr, al3_scr, True)
        return carry

    n_pairs = (n_far - 4 * n_quads + 1) // 2
    lax.fori_loop(0, n_pairs, pair_loop, 0)
    accumulate(4 * n_quads + 2 * n_pairs - 1, p3_scr, al3_scr, maybe_prev=True)
    for a in range(HB):
        acc = acc_scr[a]
        o_ref[:, a * D:(a + 1) * D] = (acc[0:D, :] / acc[D:D + 1, :]).T.astype(o_ref.dtype)


def _moba(qt, kn, vt, pen, table_h):
    H, nb, D, BS = qt.shape
    S = nb * BS
    nbp = pen.shape[2]
    HB = MOBA_HEADS_PER_STEP
    return pl.pallas_call(
        _moba_kernel,
        grid=(H // HB, nb),
        in_specs=[
            pl.BlockSpec(memory_space=pltpu.SMEM),
            pl.BlockSpec((HB, 1, D, BS), lambda hp, i: (hp, i, 0, 0)),
            pl.BlockSpec((HB, S, D), lambda hp, i: (hp, 0, 0)),
            pl.BlockSpec((HB, nb, MOBA_VT_ROWS, BS), lambda hp, i: (hp, 0, 0, 0)),
            pl.BlockSpec((HB, 1, nbp, BS), lambda hp, i: (hp, i, 0, 0)),
        ],
        out_specs=pl.BlockSpec((BS, HB * D), lambda hp, i: (i, hp)),
        out_shape=jax.ShapeDtypeStruct((S, H * D), BF16),
        scratch_shapes=[
            pltpu.VMEM((HB, BS, BS), F32),
            pltpu.VMEM((HB, BS, BS), F32),
            pltpu.VMEM((HB, 1, BS), F32),
            pltpu.VMEM((HB, MOBA_VT_ROWS, BS), F32),
        ] + [pltpu.VMEM((HB, BS, BS), F32)] * 4 + [pltpu.VMEM((HB, BS, BS), BF16)] * 4
        + [pltpu.VMEM((HB, 1, BS), F32)] * 4,
        compiler_params=_cparams(("arbitrary", "arbitrary")),
        name="moba",
    )(table_h, qt, kn, vt, pen)


def _mix_kernel(ys_ref, ya_ref, ws_ref, wa_ref, gs_ref, ga_ref, bs_ref, ba_ref, o_ref):
    gs = _sigmoid(gs_ref[...].astype(F32) + bs_ref[...])
    ga = _sigmoid(ga_ref[...].astype(F32) + ba_ref[...])
    o_ref[...] = (gs * _dot(ys_ref[...], ws_ref[...])
                  + ga * _dot(ya_ref[...], wa_ref[...])).astype(o_ref.dtype)


def _mix(y_ssd, y_attn, w_ssm_out, w_attn_out, proj, b_gate, tm=1024, tn=512):
    S = y_ssd.shape[0]
    tm = min(tm, S)
    gs_col = COL_G // tn
    ga_col = (COL_G + D_MODEL) // tn
    nb = D_MODEL // tn
    return pl.pallas_call(
        _mix_kernel,
        grid=(S // tm, D_MODEL // tn),
        in_specs=[
            pl.BlockSpec((tm, D_INNER), lambda m, n: (m, 0)),
            pl.BlockSpec((tm, D_ATTN), lambda m, n: (m, 0)),
            pl.BlockSpec((D_INNER, tn), lambda m, n: (0, n)),
            pl.BlockSpec((D_ATTN, tn), lambda m, n: (0, n)),
            pl.BlockSpec((tm, tn), lambda m, n: (m, gs_col + n)),
            pl.BlockSpec((tm, tn), lambda m, n: (m, ga_col + n)),
            pl.BlockSpec((1, tn), lambda m, n: (0, n)),
            pl.BlockSpec((1, tn), lambda m, n: (0, nb + n)),
        ],
        out_specs=pl.BlockSpec((tm, tn), lambda m, n: (m, n)),
        out_shape=jax.ShapeDtypeStruct((S, D_MODEL), BF16),
        compiler_params=_cparams(("arbitrary", "arbitrary")),
        name="mix",
    )(y_ssd, y_attn, w_ssm_out, w_attn_out, proj, proj, b_gate, b_gate)


def _out_norm_kernel(x_ref, mix_ref, w_ref, nw_ref, x1_ref, h2_ref):
    x1 = x_ref[...] + _dot(mix_ref[...], w_ref[...])
    x1_ref[...] = x1
    ms = jnp.mean(x1 * x1, axis=-1, keepdims=True)
    h2_ref[...] = (x1 * lax.rsqrt(ms + EPS) * nw_ref[...]).astype(h2_ref.dtype)


def _out_norm(x, mix, w_out, ffn_norm_w, tm=512):
    S = x.shape[0]
    tm = min(tm, S)
    return pl.pallas_call(
        _out_norm_kernel,
        grid=(S // tm,),
        in_specs=[
            pl.BlockSpec((tm, D_MODEL), lambda m: (m, 0)),
            pl.BlockSpec((tm, D_MODEL), lambda m: (m, 0)),
            pl.BlockSpec((D_MODEL, D_MODEL), lambda m: (0, 0)),
            pl.BlockSpec((1, D_MODEL), lambda m: (0, 0)),
        ],
        out_specs=[
            pl.BlockSpec((tm, D_MODEL), lambda m: (m, 0)),
            pl.BlockSpec((tm, D_MODEL), lambda m: (m, 0)),
        ],
        out_shape=[
            jax.ShapeDtypeStruct((S, D_MODEL), F32),
            jax.ShapeDtypeStruct((S, D_MODEL), BF16),
        ],
        compiler_params=_cparams(("arbitrary",)),
        name="out_norm",
    )(x, mix, w_out, ffn_norm_w)


def _ffn_up_kernel(h_ref, wg32_ref, wu32_ref, cwg_ref, cwu_ref, cbg_ref, cbu_ref, o_ref,
                   tailg, tailu, padg, padu, wg_ref, wu_ref):
    tm = h_ref.shape[0]

    @pl.when(pl.program_id(1) == 0)
    def _():
        tailg[...] = jnp.zeros(tailg.shape, F32)
        tailu[...] = jnp.zeros(tailu.shape, F32)
        wg_ref[...] = wg32_ref[...].astype(BF16)
        wu_ref[...] = wu32_ref[...].astype(BF16)

    def conv(w_ref, cw_ref, cb_ref, tail, pad):
        pad[0:SUBLANES, :] = tail[...]
        pad[SUBLANES:SUBLANES + tm, :] = _dot(h_ref[...], w_ref[...])
        tail[...] = pad[tm:tm + SUBLANES, :]
        acc = cb_ref[...]
        for k in range(FFN_CONV):
            off = SUBLANES - (FFN_CONV - 1) + k
            acc = acc + cw_ref[k:k + 1, :] * pad[off:off + tm, :]
        return acc

    ug = conv(wg_ref, cwg_ref, cbg_ref, tailg, padg)
    uu = conv(wu_ref, cwu_ref, cbu_ref, tailu, padu)
    o_ref[...] = (_silu(ug) * uu).astype(o_ref.dtype)


def _ffn_up(h2, w_up, conv_w, conv_b, tm=1024, tn=512):
    S = h2.shape[0]
    tm = min(tm, S)
    nb = D_FF // tn
    return pl.pallas_call(
        _ffn_up_kernel,
        grid=(nb, S // tm),
        in_specs=[
            pl.BlockSpec((tm, D_MODEL), lambda n, m: (m, 0)),
            pl.BlockSpec((D_MODEL, tn), lambda n, m: (0, n)),
            pl.BlockSpec((D_MODEL, tn), lambda n, m: (0, nb + n)),
            pl.BlockSpec((FFN_CONV, tn), lambda n, m: (0, n)),
            pl.BlockSpec((FFN_CONV, tn), lambda n, m: (0, nb + n)),
            pl.BlockSpec((1, tn), lambda n, m: (0, n)),
            pl.BlockSpec((1, tn), lambda n, m: (0, nb + n)),
        ],
        out_specs=pl.BlockSpec((tm, tn), lambda n, m: (m, n)),
        out_shape=jax.ShapeDtypeStruct((S, D_FF), BF16),
        scratch_shapes=[
            pltpu.VMEM((SUBLANES, tn), F32),
            pltpu.VMEM((SUBLANES, tn), F32),
            pltpu.VMEM((tm + SUBLANES, tn), F32),
            pltpu.VMEM((tm + SUBLANES, tn), F32),
            pltpu.VMEM((D_MODEL, tn), BF16),
            pltpu.VMEM((D_MODEL, tn), BF16),
        ],
        compiler_params=_cparams(("arbitrary", "arbitrary")),
        name="ffn_up",
    )(h2, w_up, w_up, conv_w, conv_w, conv_b, conv_b)


def _ffn_down_kernel(a_ref, w_ref, x1_ref, o_ref):
    o_ref[...] = x1_ref[...] + _dot(a_ref[...], w_ref[...])


def _ffn_down(act, w_down, x1, tm=1024, tn=512):
    S = act.shape[0]
    tm = min(tm, S)
    return pl.pallas_call(
        _ffn_down_kernel,
        grid=(S // tm, D_MODEL // tn),
        in_specs=[
            pl.BlockSpec((tm, D_FF), lambda m, n: (m, 0)),
            pl.BlockSpec((D_FF, tn), lambda m, n: (0, n)),
            pl.BlockSpec((tm, tn), lambda m, n: (m, n)),
        ],
        out_specs=pl.BlockSpec((tm, tn), lambda m, n: (m, n)),
        out_shape=jax.ShapeDtypeStruct((S, D_MODEL), F32),
        compiler_params=_cparams(("arbitrary", "arbitrary")),
        name="ffn_down",
    )(act, w_down, x1)


def _pad_lanes(v):
    return jnp.pad(v, ((0, 0), (0, LANES - v.shape[1])))


def _layer(x, attn_norm_w, w_in, b_gate, ssm_conv_w, ssm_conv_b, ssm_dt_bias, ssm_a_log, ssm_d,
           ssm_norm_w, q_norm_w, k_norm_w, rel_bias, w_ssm_out, w_attn_out, w_out, ffn_norm_w,
           w_up, ffn_conv_w, ffn_conv_b, w_down):
    o_dt = D_INNER + D_XBC
    o_q = o_dt + SSM_HEADS
    wt = w_in.T
    wt_dt = jnp.pad(wt[o_dt:o_q], ((0, LANES - SSM_HEADS), (0, 0)))

    h, dt_raw = _norm_dt(x, attn_norm_w[None, :], wt_dt)
    proj = _in_proj(h, wt, o_dt)
    y_ssd = _ssd(proj, dt_raw, ssm_conv_w, ssm_conv_b[None, :], _pad_lanes(ssm_dt_bias[None, :]),
                 _pad_lanes(ssm_a_log[None, :]), jnp.repeat(ssm_d, SSM_HEAD_DIM)[None, :],
                 ssm_norm_w[None, :])
    qt, kn, vt, pen = _qk_prep(proj, q_norm_w[None, :], k_norm_w[None, :])
    y_attn = _moba(qt, kn, vt, pen, rel_bias.T)
    mix = _mix(y_ssd, y_attn, w_ssm_out.astype(BF16), w_attn_out.astype(BF16), proj, b_gate[None, :])
    x1, h2 = _out_norm(x, mix, w_out.astype(BF16), ffn_norm_w[None, :])
    act = _ffn_up(h2, w_up, ffn_conv_w, ffn_conv_b[None, :])
    return _ffn_down(act, w_down.astype(BF16), x1)


def kernel(x, attn_norm_w, w_in, b_gate, ssm_conv_w, ssm_conv_b, ssm_dt_bias, ssm_a_log, ssm_d,
           ssm_norm_w, q_norm_w, k_norm_w, rel_bias, w_ssm_out, w_attn_out, w_out, ffn_norm_w,
           w_up, ffn_conv_w, ffn_conv_b, w_down):
    assert x.shape[0] == 1 and attn_norm_w.shape[0] == 1
    out = _layer(x[0], attn_norm_w[0], w_in[0], b_gate[0], ssm_conv_w[0], ssm_conv_b[0],
                 ssm_dt_bias[0], ssm_a_log[0], ssm_d[0], ssm_norm_w[0], q_norm_w[0], k_norm_w[0],
                 rel_bias, w_ssm_out[0], w_attn_out[0], w_out[0], ffn_norm_w[0], w_up[0],
                 ffn_conv_w[0], ffn_conv_b[0], w_down[0])
    return out[None]
```

```python
import math

import jax
import jax.numpy as jnp
from jax import lax
from jax.experimental import pallas as pl
from jax.experimental.pallas import tpu as pltpu

F32 = jnp.float32
BF16 = jnp.bfloat16

D_MODEL = 2048
D_INNER = 4096
SSM_HEADS = 64
SSM_HEAD_DIM = 64
SSM_GROUPS = 8
SSM_HEADS_PER_GROUP = SSM_HEADS // SSM_GROUPS
SSM_STATE = 128
SSM_CONV = 4
SSM_CHUNK = 256
GROUP_CH = D_INNER // SSM_GROUPS
SSD_GROUPS_PER_STEP = 8
D_XBC = D_INNER + 2 * SSM_GROUPS * SSM_STATE
ATTN_HEADS = 16
ATTN_HEAD_DIM = 128
D_ATTN = ATTN_HEADS * ATTN_HEAD_DIM
MOBA_BLOCK = 256
MOBA_TOPK = 3
REL_BUCKETS = 32
REL_MAX_DIST = 128
D_FF = 5632
FFN_CONV = 3
EPS = 1e-6
LOG2E = 1.4426950408889634

LANES = 128
SUBLANES = 8
D_PROJ = D_XBC + 3 * D_ATTN + 2 * D_INNER
COL_Z = 0
COL_XBC = COL_Z + D_INNER
COL_Q = COL_XBC + D_XBC
COL_K = COL_Q + D_ATTN
COL_V = COL_K + D_ATTN
COL_G = COL_V + D_ATTN
VMEM_LIMIT = 56 * 1024 * 1024


def _cparams(sem):
    return pltpu.CompilerParams(dimension_semantics=sem, vmem_limit_bytes=VMEM_LIMIT)


def _split3(a):
    hi = a.astype(BF16)
    r = a - hi.astype(F32)
    mid = r.astype(BF16)
    lo = (r - mid.astype(F32)).astype(BF16)
    return hi, mid, lo


def _dot(a, b):
    return jnp.dot(a, b, preferred_element_type=F32)


def _dot_nt(a, b):
    return lax.dot_general(a, b, (((1,), (1,)), ((), ())), preferred_element_type=F32)


def _sigmoid(x):
    return 0.5 + 0.5 * jnp.tanh(0.5 * x)


def _silu(x):
    h = 0.5 * x
    return h + h * jnp.tanh(h)


def _norm_dt_kernel(x_ref, nw_ref, wdt_ref, h_ref, dt_ref):
    x = x_ref[...]
    ms = jnp.mean(x * x, axis=-1, keepdims=True)
    h = x * lax.rsqrt(ms + EPS) * nw_ref[...]
    h_ref[...] = h.astype(BF16)
    h_hi, h_mid, _ = _split3(h)
    w_hi, w_mid, _ = _split3(wdt_ref[...])
    dt_ref[...] = _dot_nt(h_hi, w_hi) + _dot_nt(h_hi, w_mid) + _dot_nt(h_mid, w_hi)


def _norm_dt(x, norm_w, wt_dt, tm=512):
    S = x.shape[0]
    tm = min(tm, S)
    return pl.pallas_call(
        _norm_dt_kernel,
        grid=(S // tm,),
        in_specs=[
            pl.BlockSpec((tm, D_MODEL), lambda m: (m, 0)),
            pl.BlockSpec((1, D_MODEL), lambda m: (0, 0)),
            pl.BlockSpec((LANES, D_MODEL), lambda m: (0, 0)),
        ],
        out_specs=[
            pl.BlockSpec((tm, D_MODEL), lambda m: (m, 0)),
            pl.BlockSpec((tm, LANES), lambda m: (m, 0)),
        ],
        out_shape=[
            jax.ShapeDtypeStruct((S, D_MODEL), BF16),
            jax.ShapeDtypeStruct((S, LANES), F32),
        ],
        compiler_params=_cparams(("arbitrary",)),
        name="norm_dt",
    )(x, norm_w, wt_dt)


def _in_proj_kernel(h_ref, w32_ref, o_ref, w_scr):
    @pl.when(pl.program_id(1) == 0)
    def _():
        w_scr[...] = w32_ref[...].astype(BF16)

    o_ref[...] = _dot_nt(h_ref[...], w_scr[...]).astype(o_ref.dtype)


def _in_proj(h, wt, dt_row0, tm=2048, tn=1024):
    S = h.shape[0]
    tm = min(tm, S)
    assert dt_row0 % tn == 0 and SSM_HEADS % SUBLANES == 0
    na = dt_row0 // tn

    def w_rows(n, m):
        return (pl.multiple_of(n * tn + jnp.where(n >= na, SSM_HEADS, 0), SSM_HEADS), 0)

    return pl.pallas_call(
        _in_proj_kernel,
        grid=(D_PROJ // tn, S // tm),
        in_specs=[
            pl.BlockSpec((tm, D_MODEL), lambda n, m: (m, 0)),
            pl.BlockSpec((pl.Element(tn), pl.Element(D_MODEL)), w_rows),
        ],
        out_specs=pl.BlockSpec((tm, tn), lambda n, m: (m, n)),
        out_shape=jax.ShapeDtypeStruct((S, D_PROJ), BF16),
        scratch_shapes=[pltpu.VMEM((tn, D_MODEL), BF16)],
        compiler_params=_cparams(("arbitrary", "arbitrary")),
        name="in_proj",
    )(h, wt)


def _conv_silu(raw_ref, tail_ref, pad_ref, w_ref, b_ref, g, taps):
    L = raw_ref.shape[0]
    pad_ref[0:SUBLANES, :] = tail_ref[g]
    pad_ref[SUBLANES:SUBLANES + L, :] = raw_ref[...].astype(F32)
    tail_ref[g] = pad_ref[L:L + SUBLANES, :]
    acc = b_ref[...]
    for k in range(taps):
        off = SUBLANES - (taps - 1) + k
        acc = acc + w_ref[k:k + 1, :] * pad_ref[off:off + L, :]
    return _silu(acc)


def _ssd_kernel(x_ref, b_ref, c_ref, z_ref, dtr_ref, cwx_ref, cwb_ref, cwc_ref,
                cbx_ref, cbb_ref, cbc_ref, dtb_ref, alog_ref, dsk_ref, nw_ref,
                o_ref,
                tailx, tailb, tailc, padx, padb, padc, acs_scr, acst_scr, lbt_scr, ht_scr, rhs_scr):
    c = pl.program_id(0)
    gp = pl.program_id(1)
    L = SSM_CHUNK
    QW = 4 * SSM_HEAD_DIM
    GPS = SSD_GROUPS_PER_STEP

    @pl.when(c == 0)
    def _():
        for gg in range(GPS):
            g0 = gp * GPS + gg
            tailx[g0] = jnp.zeros(tailx.shape[1:], F32)
            tailb[g0] = jnp.zeros(tailb.shape[1:], F32)
            tailc[g0] = jnp.zeros(tailc.shape[1:], F32)
            ht_scr[g0] = jnp.zeros(ht_scr.shape[1:], F32)
            rhs_scr[gg] = jnp.zeros(rhs_scr.shape[1:], BF16)

    @pl.when(gp == 0)
    def _():
        t = dtr_ref[...] + dtb_ref[...]
        dt = jnp.maximum(t, 0.0) + jnp.log(1.0 + jnp.exp(-jnp.abs(t)))
        a = -jnp.exp(alog_ref[...])
        da = dt * a
        ri = lax.broadcasted_iota(jnp.int32, (L, L), 0)
        ci = lax.broadcasted_iota(jnp.int32, (L, L), 1)
        tri = jnp.where(ri >= ci, 1.0, 0.0).astype(BF16)
        hi, mid, lo = _split3(da)
        a_cs = _dot(tri, hi) + _dot(tri, mid) + _dot(tri, lo)
        a2 = a_cs * LOG2E
        a2t = a2.T
        acs_scr[...] = a2
        acst_scr[...] = a2t
        lbt_scr[...] = a2t - jnp.log(dt.T) * LOG2E

    for gg in range(GPS):
        xs = slice(gg * GROUP_CH, (gg + 1) * GROUP_CH)
        ns = slice(gg * SSM_STATE, (gg + 1) * SSM_STATE)
        _ssd_group(gp * GPS + gg,
                   x_ref.at[:, xs], b_ref.at[:, ns], c_ref.at[:, ns], z_ref.at[:, xs],
                   cwx_ref.at[:, xs], cwb_ref.at[:, ns], cwc_ref.at[:, ns],
                   cbx_ref.at[:, xs], cbb_ref.at[:, ns], cbc_ref.at[:, ns],
                   dsk_ref.at[:, xs], nw_ref.at[:, xs], o_ref.at[:, xs],
                   tailx, tailb, tailc, padx.at[gg], padb.at[gg], padc.at[gg],
                   acs_scr, acst_scr, lbt_scr, ht_scr,
                   [rhs_scr.at[gg, q] for q in range(GROUP_CH // QW)])


def _ssd_group(g, x_ref, b_ref, c_ref, z_ref, cwx_ref, cwb_ref, cwc_ref, cbx_ref, cbb_ref, cbc_ref,
               dsk_ref, nw_ref, o_ref, tailx, tailb, tailc, padx, padb, padc,
               acs_scr, acst_scr, lbt_scr, ht_scr, rhs_refs):
    L = SSM_CHUNK
    QW = 4 * SSM_HEAD_DIM
    xa = _conv_silu(x_ref, tailx, padx, cwx_ref, cbx_ref, g, SSM_CONV)
    ba = _conv_silu(b_ref, tailb, padb, cwb_ref, cbb_ref, g, SSM_CONV)
    ca = _conv_silu(c_ref, tailc, padc, cwc_ref, cbc_ref, g, SSM_CONV)

    xb = xa.astype(BF16)
    bt = ba.T
    cb = _dot(ca.astype(BF16), bt.astype(BF16))
    a_cs = acs_scr[...]
    ri = lax.broadcasted_iota(jnp.int32, (L, L), 0)
    ci = lax.broadcasted_iota(jnp.int32, (L, L), 1)
    causal = ri >= ci
    lane_h = lax.broadcasted_iota(jnp.int32, (L, LANES), 1)
    lane_q = lax.broadcasted_iota(jnp.int32, (1, QW), 1)

    NH = QW // SSM_HEAD_DIM
    ys = []
    for q in range(GROUP_CH // QW):
        rhs = rhs_refs[q]
        xq = xb[:, q * QW:(q + 1) * QW]
        hq = ht_scr[g, :, q * QW:(q + 1) * QW]
        hqb = hq.astype(BF16)
        mhs, ces, bws = [], [], []
        cdq = jnp.zeros((1, QW), F32)
        for j in range(NH):
            h = g * SSM_HEADS_PER_GROUP + q * NH + j
            cols = slice(j * SSM_HEAD_DIM, (j + 1) * SSM_HEAD_DIM)
            rhs[j * L:(j + 1) * L, cols] = xq[:, cols]
            rhs[NH * L + j * SSM_STATE:NH * L + (j + 1) * SSM_STATE, cols] = hqb[:, cols]
            a_row = acst_scr[pl.ds(h, 1), :]
            b_row = lbt_scr[pl.ds(h, 1), :]
            a_col = jnp.sum(jnp.where(lane_h == h, a_cs, 0.0), axis=1, keepdims=True)
            dmat = jnp.where(causal, a_col - b_row, -jnp.inf)
            mhs.append((cb * jnp.exp2(dmat)).astype(BF16))
            ces.append((ca * jnp.exp2(a_col)).astype(BF16))
            a_last = a_row[:, L - 1:L]
            w_row = jnp.exp2(a_last - b_row)
            bws.append((bt * w_row).astype(BF16))
            lm = (lane_q >= j * SSM_HEAD_DIM) & (lane_q < (j + 1) * SSM_HEAD_DIM)
            cdq = jnp.where(lm, jnp.exp2(a_last), cdq)
        acc = _dot(jnp.concatenate(mhs + ces, axis=1), rhs[...])
        st = _dot(jnp.concatenate(bws, axis=1), rhs[0:NH * L, :])
        ht_scr[g, :, q * QW:(q + 1) * QW] = hq * cdq + st
        ys.append(acc + xa[:, q * QW:(q + 1) * QW] * dsk_ref[:, q * QW:(q + 1) * QW])

    z = z_ref[...].astype(F32)
    sz = _silu(z)
    ygs = [ys[q] * sz[:, q * QW:(q + 1) * QW] for q in range(len(ys))]
    ss = sum(jnp.sum(v * v, axis=1, keepdims=True) for v in ygs)
    scale = lax.rsqrt(ss * (1.0 / GROUP_CH) + EPS)
    for q in range(len(ys)):
        o_ref[:, q * QW:(q + 1) * QW] = (
            ygs[q] * scale * nw_ref[:, q * QW:(q + 1) * QW]).astype(o_ref.dtype)


def _ssd(proj, dt_raw, conv_w, conv_b, dt_bias, a_log, d_exp, norm_w):
    S = proj.shape[0]
    L, G, N = SSM_CHUNK, SSM_GROUPS, SSM_STATE
    GPS = SSD_GROUPS_PER_STEP
    XW, NW = GPS * GROUP_CH, GPS * N
    bcol = D_INNER // NW
    ccol = bcol + G // GPS
    pxcol = COL_XBC // XW
    pbcol = COL_XBC // NW + bcol
    pccol = pbcol + G // GPS
    zcol = COL_Z // XW
    return pl.pallas_call(
        _ssd_kernel,
        grid=(S // L, G // GPS),
        in_specs=[
            pl.BlockSpec((L, XW), lambda c, g: (c, pxcol + g)),
            pl.BlockSpec((L, NW), lambda c, g: (c, pbcol + g)),
            pl.BlockSpec((L, NW), lambda c, g: (c, pccol + g)),
            pl.BlockSpec((L, XW), lambda c, g: (c, zcol + g)),
            pl.BlockSpec((L, LANES), lambda c, g: (c, 0)),
            pl.BlockSpec((SSM_CONV, XW), lambda c, g: (0, g)),
            pl.BlockSpec((SSM_CONV, NW), lambda c, g: (0, bcol + g)),
            pl.BlockSpec((SSM_CONV, NW), lambda c, g: (0, ccol + g)),
            pl.BlockSpec((1, XW), lambda c, g: (0, g)),
            pl.BlockSpec((1, NW), lambda c, g: (0, bcol + g)),
            pl.BlockSpec((1, NW), lambda c, g: (0, ccol + g)),
            pl.BlockSpec((1, LANES), lambda c, g: (0, 0)),
            pl.BlockSpec((1, LANES), lambda c, g: (0, 0)),
            pl.BlockSpec((1, XW), lambda c, g: (0, g)),
            pl.BlockSpec((1, XW), lambda c, g: (0, g)),
        ],
        out_specs=pl.BlockSpec((L, XW), lambda c, g: (c, g)),
        out_shape=jax.ShapeDtypeStruct((S, D_INNER), BF16),
        scratch_shapes=[
            pltpu.VMEM((G, SUBLANES, GROUP_CH), F32),
            pltpu.VMEM((G, SUBLANES, N), F32),
            pltpu.VMEM((G, SUBLANES, N), F32),
            pltpu.VMEM((GPS, L + SUBLANES, GROUP_CH), F32),
            pltpu.VMEM((GPS, L + SUBLANES, N), F32),
            pltpu.VMEM((GPS, L + SUBLANES, N), F32),
            pltpu.VMEM((L, LANES), F32),
            pltpu.VMEM((LANES, L), F32),
            pltpu.VMEM((LANES, L), F32),
            pltpu.VMEM((G, N, GROUP_CH), F32),
            pltpu.VMEM((GPS, GROUP_CH // (4 * SSM_HEAD_DIM), 4 * (L + N), 4 * SSM_HEAD_DIM), BF16),
        ],
        compiler_params=_cparams(("arbitrary", "arbitrary")),
        name="ssd",
    )(proj, proj, proj, proj, dt_raw, conv_w, conv_w, conv_w, conv_b, conv_b, conv_b,
      dt_bias, a_log, d_exp, norm_w)


def _qk_prep_kernel(q_ref, k_ref, v_ref, qw_ref, kw_ref, qt_ref, kn_ref, vt_ref, pen_ref, kbar_scr):
    S = q_ref.shape[0]
    BS = MOBA_BLOCK
    nb = S // BS
    nbp = kbar_scr.shape[0]
    kbar_scr[...] = jnp.zeros(kbar_scr.shape, F32)
    eye = (lax.broadcasted_iota(jnp.int32, (ATTN_HEAD_DIM, ATTN_HEAD_DIM), 0)
           == lax.broadcasted_iota(jnp.int32, (ATTN_HEAD_DIM, ATTN_HEAD_DIM), 1)).astype(BF16)

    def kloop(b, carry):
        rows = pl.ds(pl.multiple_of(b * BS, BS), BS)
        k = k_ref[rows, :].astype(F32)
        kn = k * lax.rsqrt(jnp.mean(k * k, axis=-1, keepdims=True) + EPS) * kw_ref[...]
        kn_ref[0, rows, :] = kn.astype(BF16)
        kbar_scr[pl.ds(b, 1), :] = jnp.mean(kn, axis=0, keepdims=True)
        vt_ref[0, b, 0:ATTN_HEAD_DIM, :] = _dot_nt(eye, v_ref[rows, :]).astype(BF16)
        vt_ref[0, b, ATTN_HEAD_DIM:, :] = jnp.ones((MOBA_VT_ROWS - ATTN_HEAD_DIM, BS), BF16)
        return carry

    lax.fori_loop(0, nb, kloop, 0, unroll=4)
    kb_hi, kb_mid, _ = _split3(kbar_scr[...])

    def qloop(i, carry):
        rows = pl.ds(pl.multiple_of(i * BS, BS), BS)
        q = q_ref[rows, :].astype(F32)
        qn = q * lax.rsqrt(jnp.mean(q * q, axis=-1, keepdims=True) + EPS) * qw_ref[...]
        qnt = (qn * (ATTN_HEAD_DIM ** -0.5)).T
        qt_ref[0, i] = (qnt * LOG2E).astype(BF16)
        q_hi, q_mid, _ = _split3(qnt)
        gate = _dot(kb_hi, q_hi) + _dot(kb_mid, q_hi) + _dot(kb_hi, q_mid)
        blk = lax.broadcasted_iota(jnp.int32, gate.shape, 0)
        blk_f = blk.astype(F32)
        gate = jnp.where(blk < i, gate, -jnp.inf)
        pen = jnp.full(gate.shape, -jnp.inf, F32)
        for _ in range(MOBA_TOPK):
            mx = jnp.max(gate, axis=0, keepdims=True)
            cand = jnp.where((gate == mx) & (mx > -jnp.inf), blk_f, float(nbp))
            idx = jnp.min(cand, axis=0, keepdims=True)
            hit = blk_f == idx
            pen = jnp.where(hit, 0.0, pen)
            gate = jnp.where(hit, -jnp.inf, gate)
        pen_ref[0, i] = pen
        return carry

    lax.fori_loop(0, nb, qloop, 0, unroll=8)


def _qk_prep(proj, q_norm_w, k_norm_w):
    S = proj.shape[0]
    H, D, BS = ATTN_HEADS, ATTN_HEAD_DIM, MOBA_BLOCK
    nb = S // BS
    nbp = -(-nb // SUBLANES) * SUBLANES
    return pl.pallas_call(
        _qk_prep_kernel,
        grid=(H,),
        in_specs=[
            pl.BlockSpec((S, D), lambda h: (0, COL_Q // D + h)),
            pl.BlockSpec((S, D), lambda h: (0, COL_K // D + h)),
            pl.BlockSpec((S, D), lambda h: (0, COL_V // D + h)),
            pl.BlockSpec((1, D), lambda h: (0, 0)),
            pl.BlockSpec((1, D), lambda h: (0, 0)),
        ],
        out_specs=[
            pl.BlockSpec((1, nb, D, BS), lambda h: (h, 0, 0, 0)),
            pl.BlockSpec((1, S, D), lambda h: (h, 0, 0)),
            pl.BlockSpec((1, nb, MOBA_VT_ROWS, BS), lambda h: (h, 0, 0, 0)),
            pl.BlockSpec((1, nb, nbp, BS), lambda h: (h, 0, 0, 0)),
        ],
        out_shape=[
            jax.ShapeDtypeStruct((H, nb, D, BS), BF16),
            jax.ShapeDtypeStruct((H, S, D), BF16),
            jax.ShapeDtypeStruct((H, nb, MOBA_VT_ROWS, BS), BF16),
            jax.ShapeDtypeStruct((H, nb, nbp, BS), F32),
        ],
        scratch_shapes=[pltpu.VMEM((nbp, D), F32)],
        compiler_params=_cparams(("arbitrary",)),
        name="qk_prep",
    )(proj, proj, proj, q_norm_w, k_norm_w)


def _rel_bucket(dist):
    n = jnp.maximum(dist, 0)
    max_exact = REL_BUCKETS // 2
    nf = jnp.maximum(n, 1).astype(F32)
    large = max_exact + (jnp.log(nf / max_exact) / math.log(REL_MAX_DIST / max_exact)
                         * (REL_BUCKETS - max_exact)).astype(jnp.int32)
    large = jnp.minimum(large, REL_BUCKETS - 1)
    return jnp.where(n < max_exact, n, large)


MOBA_HEADS_PER_STEP = 4
MOBA_VT_ROWS = ATTN_HEAD_DIM + 2 * SUBLANES


def _moba_kernel(tbl_ref, qt_ref, k_ref, vt_ref, pen_ref, o_ref,
                 bown_scr, bprev_scr, m_scr, acc_scr, s0_scr, s1_scr, s2_scr, s3_scr,
                 p0_scr, p1_scr, p2_scr, p3_scr, al0_scr, al1_scr, al2_scr, al3_scr):
    hp = pl.program_id(0)
    i = pl.program_id(1)
    BS, HB = MOBA_BLOCK, MOBA_HEADS_PER_STEP
    D = ATTN_HEAD_DIM

    @pl.when(i == 0)
    def _():
        kl = lax.broadcasted_iota(jnp.int32, (BS, BS), 0)
        ql = lax.broadcasted_iota(jnp.int32, (BS, BS), 1)
        d = ql - kl
        b_own = _rel_bucket(d)
        b_prev = _rel_bucket(d + BS)
        for a in range(HB):
            h = hp * HB + a
            far = tbl_ref[h, REL_BUCKETS - 1]

            def lookup(bucket):
                val = jnp.zeros((BS, BS), F32)
                for b in range(REL_BUCKETS):
                    val = jnp.where(bucket == b, tbl_ref[h, b], val)
                return (val - far) * LOG2E

            bown_scr[a] = jnp.where(d >= 0, lookup(b_own), -jnp.inf)
            bprev_scr[a] = lookup(b_prev)

    def krows(a, j):
        return k_ref[a, pl.ds(pl.multiple_of(j * BS, BS), BS), :]

    jp = jnp.maximum(i - 1, 0)
    no_prev = jnp.where(i == 0, -jnp.inf, 0.0)
    n_far = jnp.maximum(i - 1, 0)
    nb = k_ref.shape[1] // BS

    def clampj(j):
        return jnp.clip(j, 0, nb - 1)

    def scores(j, s_slot):
        for a in range(HB):
            s_slot[a] = _dot(krows(a, clampj(j)), qt_ref[a, 0])

    def softmax(j, s_slot, p_slot, al_slot, mask_tail, bias_scr=None):
        for a in range(HB):
            pr = pen_ref[a, 0, pl.ds(clampj(j), 1), :]
            if mask_tail is True:
                pr = pr + jnp.where(j >= n_far, -jnp.inf, 0.0)
            elif mask_tail is not False:
                pr = pr + mask_tail
            s = s_slot[a]
            if bias_scr is not None:
                s = s + bias_scr[a]
            m_old = m_scr[a]
            m_new = jnp.maximum(m_old, jnp.max(s, axis=0, keepdims=True) + pr)
            p_slot[a] = jnp.exp2(s - (m_new - pr)).astype(BF16)
            al_slot[a] = jnp.exp2(m_old - m_new)
            m_scr[a] = m_new

    def accumulate(j, p_slot, al_slot, maybe_prev=False):
        jv = clampj(j)
        if maybe_prev:
            jv = jnp.where(j < 0, jp, jv)
        for a in range(HB):
            acc_scr[a] = al_slot[a] * acc_scr[a] + _dot(vt_ref[a, jv], p_slot[a])

    s_slots = (s0_scr, s1_scr, s2_scr, s3_scr)
    p_slots = (p0_scr, p1_scr, p2_scr, p3_scr)
    al_slots = (al0_scr, al1_scr, al2_scr, al3_scr)
    scores(i, s2_scr)
    scores(jp, s3_scr)
    scores(0, s0_scr)
    for a in range(HB):
        s = s2_scr[a] + bown_scr[a]
        m0 = jnp.max(s, axis=0, keepdims=True)
        m_scr[a] = m0
        p2_scr[a] = jnp.exp2(s - m0).astype(BF16)
    for a in range(HB):
        acc_scr[a] = _dot(vt_ref[a, i], p2_scr[a])
    softmax(jp, s3_scr, p3_scr, al3_scr, no_prev, bias_scr=bprev_scr)

    def quad_loop(g, carry):
        j = 4 * g
        last = []
        for a in range(HB):
            kq = k_ref[a, pl.ds(pl.multiple_of((j + 1) * BS, BS), 4 * BS), :]
            s4 = _dot(kq, qt_ref[a, 0])
            for t in range(3):
                s_slots[t + 1][a] = s4[t * BS:(t + 1) * BS, :]
            last.append(s4[3 * BS:4 * BS, :])
        for t in range(4):
            accumulate(j + t - 1, p_slots[(t - 1) % 4], al_slots[(t - 1) % 4], maybe_prev=(t == 0))
            softmax(j + t, s_slots[t], p_slots[t], al_slots[t], False)
            if t == 0:
                for a in range(HB):
                    s0_scr[a] = last[a]
        return carry

    n_quads = n_far // 4
    lax.fori_loop(0, n_quads, quad_loop, 0)

    def pair_loop(g, carry):
        j = 4 * n_quads + 2 * g
        scores(j + 1, s1_scr)
        accumulate(j - 1, p3_scr, al3_scr, maybe_prev=True)
        softmax(j, s0_scr, p2_scr, al2_scr, False)
        scores(j + 2, s0_scr)
        accumulate(j, p2_scr, al2_scr)
        softmax(j + 1, s1_scr, p3_scr, al3_scr, True)
        return carry

    n_pairs = (n_far - 4 * n_quads + 1) // 2
    lax.fori_loop(0, n_pairs, pair_loop, 0)
    accumulate(4 * n_quads + 2 * n_pairs - 1, p3_scr, al3_scr, maybe_prev=True)
    for a in range(HB):
        acc = acc_scr[a]
        o_ref[:, a * D:(a + 1) * D] = (acc[0:D, :] / acc[D:D + 1, :]).T.astype(o_ref.dtype)


def _moba(qt, kn, vt, pen, table_h):
    H, nb, D, BS = qt.shape
    S = nb * BS
    nbp = pen.shape[2]
    HB = MOBA_HEADS_PER_STEP
    return pl.pallas_call(
        _moba_kernel,
        grid=(H // HB, nb),
        in_specs=[
            pl.BlockSpec(memory_space=pltpu.SMEM),
            pl.BlockSpec((HB, 1, D, BS), lambda hp, i: (hp, i, 0, 0)),
            pl.BlockSpec((HB, S, D), lambda hp, i: (hp, 0, 0)),
            pl.BlockSpec((HB, nb, MOBA_VT_ROWS, BS), lambda hp, i: (hp, 0, 0, 0)),
            pl.BlockSpec((HB, 1, nbp, BS), lambda hp, i: (hp, i, 0, 0)),
        ],
        out_specs=pl.BlockSpec((BS, HB * D), lambda hp, i: (i, hp)),
        out_shape=jax.ShapeDtypeStruct((S, H * D), BF16),
        scratch_shapes=[
            pltpu.VMEM((HB, BS, BS), F32),
            pltpu.VMEM((HB, BS, BS), F32),
            pltpu.VMEM((HB, 1, BS), F32),
            pltpu.VMEM((HB, MOBA_VT_ROWS, BS), F32),
        ] + [pltpu.VMEM((HB, BS, BS), F32)] * 4 + [pltpu.VMEM((HB, BS, BS), BF16)] * 4
        + [pltpu.VMEM((HB, 1, BS), F32)] * 4,
        compiler_params=_cparams(("arbitrary", "arbitrary")),
        name="moba",
    )(table_h, qt, kn, vt, pen)


def _mix_kernel(ys_ref, ya_ref, ws_ref, wa_ref, gs_ref, ga_ref, bs_ref, ba_ref, o_ref):
    gs = _sigmoid(gs_ref[...].astype(F32) + bs_ref[...])
    ga = _sigmoid(ga_ref[...].astype(F32) + ba_ref[...])
    o_ref[...] = (gs * _dot(ys_ref[...], ws_ref[...])
                  + ga * _dot(ya_ref[...], wa_ref[...])).astype(o_ref.dtype)


def _mix(y_ssd, y_attn, w_ssm_out, w_attn_out, proj, b_gate, tm=1024, tn=512):
    S = y_ssd.shape[0]
    tm = min(tm, S)
    gs_col = COL_G // tn
    ga_col = (COL_G + D_MODEL) // tn
    nb = D_MODEL // tn
    return pl.pallas_call(
        _mix_kernel,
        grid=(S // tm, D_MODEL // tn),
        in_specs=[
            pl.BlockSpec((tm, D_INNER), lambda m, n: (m, 0)),
            pl.BlockSpec((tm, D_ATTN), lambda m, n: (m, 0)),
            pl.BlockSpec((D_INNER, tn), lambda m, n: (0, n)),
            pl.BlockSpec((D_ATTN, tn), lambda m, n: (0, n)),
            pl.BlockSpec((tm, tn), lambda m, n: (m, gs_col + n)),
            pl.BlockSpec((tm, tn), lambda m, n: (m, ga_col + n)),
            pl.BlockSpec((1, tn), lambda m, n: (0, n)),
            pl.BlockSpec((1, tn), lambda m, n: (0, nb + n)),
        ],
        out_specs=pl.BlockSpec((tm, tn), lambda m, n: (m, n)),
        out_shape=jax.ShapeDtypeStruct((S, D_MODEL), BF16),
        compiler_params=_cparams(("arbitrary", "arbitrary")),
        name="mix",
    )(y_ssd, y_attn, w_ssm_out, w_attn_out, proj, proj, b_gate, b_gate)


def _out_norm_kernel(x_ref, mix_ref, w_ref, nw_ref, x1_ref, h2_ref):
    x1 = x_ref[...] + _dot(mix_ref[...], w_ref[...])
    x1_ref[...] = x1
    ms = jnp.mean(x1 * x1, axis=-1, keepdims=True)
    h2_ref[...] = (x1 * lax.rsqrt(ms + EPS) * nw_ref[...]).astype(h2_ref.dtype)


def _out_norm(x, mix, w_out, ffn_norm_w, tm=512):
    S = x.shape[0]
    tm = min(tm, S)
    return pl.pallas_call(
        _out_norm_kernel,
        grid=(S // tm,),
        in_specs=[
            pl.BlockSpec((tm, D_MODEL), lambda m: (m, 0)),
            pl.BlockSpec((tm, D_MODEL), lambda m: (m, 0)),
            pl.BlockSpec((D_MODEL, D_MODEL), lambda m: (0, 0)),
            pl.BlockSpec((1, D_MODEL), lambda m: (0, 0)),
        ],
        out_specs=[
            pl.BlockSpec((tm, D_MODEL), lambda m: (m, 0)),
            pl.BlockSpec((tm, D_MODEL), lambda m: (m, 0)),
        ],
        out_shape=[
            jax.ShapeDtypeStruct((S, D_MODEL), F32),
            jax.ShapeDtypeStruct((S, D_MODEL), BF16),
        ],
        compiler_params=_cparams(("arbitrary",)),
        name="out_norm",
    )(x, mix, w_out, ffn_norm_w)


def _ffn_up_kernel(h_ref, wg32_ref, wu32_ref, cwg_ref, cwu_ref, cbg_ref, cbu_ref, o_ref,
                   tailg, tailu, padg, padu, wg_ref, wu_ref):
    tm = h_ref.shape[0]

    @pl.when(pl.program_id(1) == 0)
    def _():
        tailg[...] = jnp.zeros(tailg.shape, F32)
        tailu[...] = jnp.zeros(tailu.shape, F32)
        wg_ref[...] = wg32_ref[...].astype(BF16)
        wu_ref[...] = wu32_ref[...].astype(BF16)

    def conv(w_ref, cw_ref, cb_ref, tail, pad):
        pad[0:SUBLANES, :] = tail[...]
        pad[SUBLANES:SUBLANES + tm, :] = _dot(h_ref[...], w_ref[...])
        tail[...] = pad[tm:tm + SUBLANES, :]
        acc = cb_ref[...]
        for k in range(FFN_CONV):
            off = SUBLANES - (FFN_CONV - 1) + k
            acc = acc + cw_ref[k:k + 1, :] * pad[off:off + tm, :]
        return acc

    ug = conv(wg_ref, cwg_ref, cbg_ref, tailg, padg)
    uu = conv(wu_ref, cwu_ref, cbu_ref, tailu, padu)
    o_ref[...] = (_silu(ug) * uu).astype(o_ref.dtype)


def _ffn_up(h2, w_up, conv_w, conv_b, tm=1024, tn=512):
    S = h2.shape[0]
    tm = min(tm, S)
    nb = D_FF // tn
    return pl.pallas_call(
        _ffn_up_kernel,
        grid=(nb, S // tm),
        in_specs=[
            pl.BlockSpec((tm, D_MODEL), lambda n, m: (m, 0)),
            pl.BlockSpec((D_MODEL, tn), lambda n, m: (0, n)),
            pl.BlockSpec((D_MODEL, tn), lambda n, m: (0, nb + n)),
            pl.BlockSpec((FFN_CONV, tn), lambda n, m: (0, n)),
            pl.BlockSpec((FFN_CONV, tn), lambda n, m: (0, nb + n)),
            pl.BlockSpec((1, tn), lambda n, m: (0, n)),
            pl.BlockSpec((1, tn), lambda n, m: (0, nb + n)),
        ],
        out_specs=pl.BlockSpec((tm, tn), lambda n, m: (m, n)),
        out_shape=jax.ShapeDtypeStruct((S, D_FF), BF16),
        scratch_shapes=[
            pltpu.VMEM((SUBLANES, tn), F32),
            pltpu.VMEM((SUBLANES, tn), F32),
            pltpu.VMEM((tm + SUBLANES, tn), F32),
            pltpu.VMEM((tm + SUBLANES, tn), F32),
            pltpu.VMEM((D_MODEL, tn), BF16),
            pltpu.VMEM((D_MODEL, tn), BF16),
        ],
        compiler_params=_cparams(("arbitrary", "arbitrary")),
        name="ffn_up",
    )(h2, w_up, w_up, conv_w, conv_w, conv_b, conv_b)


def _ffn_down_kernel(a_ref, w_ref, x1_ref, o_ref):
    o_ref[...] = x1_ref[...] + _dot(a_ref[...], w_ref[...])


def _ffn_down(act, w_down, x1, tm=1024, tn=512):
    S = act.shape[0]
    tm = min(tm, S)
    return pl.pallas_call(
        _ffn_down_kernel,
        grid=(S // tm, D_MODEL // tn),
        in_specs=[
            pl.BlockSpec((tm, D_FF), lambda m, n: (m, 0)),
            pl.BlockSpec((D_FF, tn), lambda m, n: (0, n)),
            pl.BlockSpec((tm, tn), lambda m, n: (m, n)),
        ],
        out_specs=pl.BlockSpec((tm, tn), lambda m, n: (m, n)),
        out_shape=jax.ShapeDtypeStruct((S, D_MODEL), F32),
        compiler_params=_cparams(("arbitrary", "arbitrary")),
        name="ffn_down",
    )(act, w_down, x1)


def _pad_lanes(v):
    return jnp.pad(v, ((0, 0), (0, LANES - v.shape[1])))


def _layer(x, attn_norm_w, w_in, b_gate, ssm_conv_w, ssm_conv_b, ssm_dt_bias, ssm_a_log, ssm_d,
           ssm_norm_w, q_norm_w, k_norm_w, rel_bias, w_ssm_out, w_attn_out, w_out, ffn_norm_w,
           w_up, ffn_conv_w, ffn_conv_b, w_down):
    o_dt = D_INNER + D_XBC
    o_q = o_dt + SSM_HEADS
    wt = w_in.T
    wt_dt = jnp.pad(wt[o_dt:o_q], ((0, LANES - SSM_HEADS), (0, 0)))

    h, dt_raw = _norm_dt(x, attn_norm_w[None, :], wt_dt)
    proj = _in_proj(h, wt, o_dt)
    y_ssd = _ssd(proj, dt_raw, ssm_conv_w, ssm_conv_b[None, :], _pad_lanes(ssm_dt_bias[None, :]),
                 _pad_lanes(ssm_a_log[None, :]), jnp.repeat(ssm_d, SSM_HEAD_DIM)[None, :],
                 ssm_norm_w[None, :])
    qt, kn, vt, pen = _qk_prep(proj, q_norm_w[None, :], k_norm_w[None, :])
    y_attn = _moba(qt, kn, vt, pen, rel_bias.T)
    mix = _mix(y_ssd, y_attn, w_ssm_out.astype(BF16), w_attn_out.astype(BF16), proj, b_gate[None, :])
    x1, h2 = _out_norm(x, mix, w_out.astype(BF16), ffn_norm_w[None, :])
    act = _ffn_up(h2, w_up, ffn_conv_w, ffn_conv_b[None, :])
    return _ffn_down(act, w_down.astype(BF16), x1)


def kernel(x, attn_norm_w, w_in, b_gate, ssm_conv_w, ssm_conv_b, ssm_dt_bias, ssm_a_log, ssm_d,
           ssm_norm_w, q_norm_w, k_norm_w, rel_bias, w_ssm_out, w_attn_out, w_out, ffn_norm_w,
           w_up, ffn_conv_w, ffn_conv_b, w_down):
    assert x.shape[0] == 1 and attn_norm_w.shape[0] == 1
    out = _layer(x[0], attn_norm_w[0], w_in[0], b_gate[0], ssm_conv_w[0], ssm_conv_b[0],
                 ssm_dt_bias[0], ssm_a_log[0], ssm_d[0], ssm_norm_w[0], q_norm_w[0], k_norm_w[0],
                 rel_bias, w_ssm_out[0], w_attn_out[0], w_out[0], ffn_norm_w[0], w_up[0],
                 ffn_conv_w[0], ffn_conv_b[0], w_down[0])
    return out[None]
```

```python
import math

import jax
import jax.numpy as jnp
from jax import lax
from jax.experimental import pallas as pl
from jax.experimental.pallas import tpu as pltpu

F32 = jnp.float32
BF16 = jnp.bfloat16

D_MODEL = 2048
D_INNER = 4096
SSM_HEADS = 64
SSM_HEAD_DIM = 64
SSM_GROUPS = 8
SSM_HEADS_PER_GROUP = SSM_HEADS // SSM_GROUPS
SSM_STATE = 128
SSM_CONV = 4
SSM_CHUNK = 256
GROUP_CH = D_INNER // SSM_GROUPS
SSD_GROUPS_PER_STEP = 8
D_XBC = D_INNER + 2 * SSM_GROUPS * SSM_STATE
ATTN_HEADS = 16
ATTN_HEAD_DIM = 128
D_ATTN = ATTN_HEADS * ATTN_HEAD_DIM
MOBA_BLOCK = 256
MOBA_TOPK = 3
REL_BUCKETS = 32
REL_MAX_DIST = 128
D_FF = 5632
FFN_CONV = 3
EPS = 1e-6
LOG2E = 1.4426950408889634

LANES = 128
SUBLANES = 8
D_PROJ = D_XBC + 3 * D_ATTN + 2 * D_INNER
COL_Z = 0
COL_XBC = COL_Z + D_INNER
COL_Q = COL_XBC + D_XBC
COL_K = COL_Q + D_ATTN
COL_V = COL_K + D_ATTN
COL_G = COL_V + D_ATTN
VMEM_LIMIT = 56 * 1024 * 1024


def _cparams(sem):
    return pltpu.CompilerParams(dimension_semantics=sem, vmem_limit_bytes=VMEM_LIMIT)


def _split3(a):
    hi = a.astype(BF16)
    r = a - hi.astype(F32)
    mid = r.astype(BF16)
    lo = (r - mid.astype(F32)).astype(BF16)
    return hi, mid, lo


def _dot(a, b):
    return jnp.dot(a, b, preferred_element_type=F32)


def _dot_nt(a, b):
    return lax.dot_general(a, b, (((1,), (1,)), ((), ())), preferred_element_type=F32)


def _sigmoid(x):
    return 0.5 + 0.5 * jnp.tanh(0.5 * x)


def _silu(x):
    h = 0.5 * x
    return h + h * jnp.tanh(h)


def _norm_dt_kernel(x_ref, nw_ref, wdt_ref, h_ref, dt_ref):
    x = x_ref[...]
    ms = jnp.mean(x * x, axis=-1, keepdims=True)
    h = x * lax.rsqrt(ms + EPS) * nw_ref[...]
    h_ref[...] = h.astype(BF16)
    h_hi, h_mid, _ = _split3(h)
    w_hi, w_mid, _ = _split3(wdt_ref[...])
    dt_ref[...] = _dot_nt(h_hi, w_hi) + _dot_nt(h_hi, w_mid) + _dot_nt(h_mid, w_hi)


def _norm_dt(x, norm_w, wt_dt, tm=512):
    S = x.shape[0]
    tm = min(tm, S)
    return pl.pallas_call(
        _norm_dt_kernel,
        grid=(S // tm,),
        in_specs=[
            pl.BlockSpec((tm, D_MODEL), lambda m: (m, 0)),
            pl.BlockSpec((1, D_MODEL), lambda m: (0, 0)),
            pl.BlockSpec((LANES, D_MODEL), lambda m: (0, 0)),
        ],
        out_specs=[
            pl.BlockSpec((tm, D_MODEL), lambda m: (m, 0)),
            pl.BlockSpec((tm, LANES), lambda m: (m, 0)),
        ],
        out_shape=[
            jax.ShapeDtypeStruct((S, D_MODEL), BF16),
            jax.ShapeDtypeStruct((S, LANES), F32),
        ],
        compiler_params=_cparams(("arbitrary",)),
        name="norm_dt",
    )(x, norm_w, wt_dt)


def _in_proj_kernel(h_ref, w32_ref, o_ref, w_scr):
    @pl.when(pl.program_id(1) == 0)
    def _():
        w_scr[...] = w32_ref[...].astype(BF16)

    o_ref[...] = _dot_nt(h_ref[...], w_scr[...]).astype(o_ref.dtype)


def _in_proj(h, wt, dt_row0, tm=2048, tn=1024):
    S = h.shape[0]
    tm = min(tm, S)
    assert dt_row0 % tn == 0 and SSM_HEADS % SUBLANES == 0
    na = dt_row0 // tn

    def w_rows(n, m):
        return (pl.multiple_of(n * tn + jnp.where(n >= na, SSM_HEADS, 0), SSM_HEADS), 0)

    return pl.pallas_call(
        _in_proj_kernel,
        grid=(D_PROJ // tn, S // tm),
        in_specs=[
            pl.BlockSpec((tm, D_MODEL), lambda n, m: (m, 0)),
            pl.BlockSpec((pl.Element(tn), pl.Element(D_MODEL)), w_rows),
        ],
        out_specs=pl.BlockSpec((tm, tn), lambda n, m: (m, n)),
        out_shape=jax.ShapeDtypeStruct((S, D_PROJ), BF16),
        scratch_shapes=[pltpu.VMEM((tn, D_MODEL), BF16)],
        compiler_params=_cparams(("arbitrary", "arbitrary")),
        name="in_proj",
    )(h, wt)


def _conv_silu(raw_ref, tail_ref, pad_ref, w_ref, b_ref, g, taps):
    L = raw_ref.shape[0]
    pad_ref[0:SUBLANES, :] = tail_ref[g]
    pad_ref[SUBLANES:SUBLANES + L, :] = raw_ref[...].astype(F32)
    tail_ref[g] = pad_ref[L:L + SUBLANES, :]
    acc = b_ref[...]
    for k in range(taps):
        off = SUBLANES - (taps - 1) + k
        acc = acc + w_ref[k:k + 1, :] * pad_ref[off:off + L, :]
    return _silu(acc)


def _ssd_kernel(x_ref, b_ref, c_ref, z_ref, dtr_ref, cwx_ref, cwb_ref, cwc_ref,
                cbx_ref, cbb_ref, cbc_ref, dtb_ref, alog_ref, dsk_ref, nw_ref,
                o_ref,
                tailx, tailb, tailc, padx, padb, padc, acs_scr, acst_scr, lbt_scr, ht_scr, rhs_scr):
    c = pl.program_id(0)
    gp = pl.program_id(1)
    L = SSM_CHUNK
    QW = 4 * SSM_HEAD_DIM
    GPS = SSD_GROUPS_PER_STEP

    @pl.when(c == 0)
    def _():
        for gg in range(GPS):
            g0 = gp * GPS + gg
            tailx[g0] = jnp.zeros(tailx.shape[1:], F32)
            tailb[g0] = jnp.zeros(tailb.shape[1:], F32)
            tailc[g0] = jnp.zeros(tailc.shape[1:], F32)
            ht_scr[g0] = jnp.zeros(ht_scr.shape[1:], F32)
            rhs_scr[gg] = jnp.zeros(rhs_scr.shape[1:], BF16)

    @pl.when(gp == 0)
    def _():
        t = dtr_ref[...] + dtb_ref[...]
        dt = jnp.maximum(t, 0.0) + jnp.log(1.0 + jnp.exp(-jnp.abs(t)))
        a = -jnp.exp(alog_ref[...])
        da = dt * a
        ri = lax.broadcasted_iota(jnp.int32, (L, L), 0)
        ci = lax.broadcasted_iota(jnp.int32, (L, L), 1)
        tri = jnp.where(ri >= ci, 1.0, 0.0).astype(BF16)
        hi, mid, lo = _split3(da)
        a_cs = _dot(tri, hi) + _dot(tri, mid) + _dot(tri, lo)
        a2 = a_cs * LOG2E
        a2t = a2.T
        acs_scr[...] = a2
        acst_scr[...] = a2t
        lbt_scr[...] = a2t - jnp.log(dt.T) * LOG2E

    for gg in range(GPS):
        xs = slice(gg * GROUP_CH, (gg + 1) * GROUP_CH)
        ns = slice(gg * SSM_STATE, (gg + 1) * SSM_STATE)
        _ssd_group(gp * GPS + gg,
                   x_ref.at[:, xs], b_ref.at[:, ns], c_ref.at[:, ns], z_ref.at[:, xs],
                   cwx_ref.at[:, xs], cwb_ref.at[:, ns], cwc_ref.at[:, ns],
                   cbx_ref.at[:, xs], cbb_ref.at[:, ns], cbc_ref.at[:, ns],
                   dsk_ref.at[:, xs], nw_ref.at[:, xs], o_ref.at[:, xs],
                   tailx, tailb, tailc, padx.at[gg], padb.at[gg], padc.at[gg],
                   acs_scr, acst_scr, lbt_scr, ht_scr,
                   [rhs_scr.at[gg, q] for q in range(GROUP_CH // QW)])


def _ssd_group(g, x_ref, b_ref, c_ref, z_ref, cwx_ref, cwb_ref, cwc_ref, cbx_ref, cbb_ref, cbc_ref,
               dsk_ref, nw_ref, o_ref, tailx, tailb, tailc, padx, padb, padc,
               acs_scr, acst_scr, lbt_scr, ht_scr, rhs_refs):
    L = SSM_CHUNK
    QW = 4 * SSM_HEAD_DIM
    xa = _conv_silu(x_ref, tailx, padx, cwx_ref, cbx_ref, g, SSM_CONV)
    ba = _conv_silu(b_ref, tailb, padb, cwb_ref, cbb_ref, g, SSM_CONV)
    ca = _conv_silu(c_ref, tailc, padc, cwc_ref, cbc_ref, g, SSM_CONV)

    xb = xa.astype(BF16)
    bt = ba.T
    cb = _dot(ca.astype(BF16), bt.astype(BF16))
    a_cs = acs_scr[...]
    ri = lax.broadcasted_iota(jnp.int32, (L, L), 0)
    ci = lax.broadcasted_iota(jnp.int32, (L, L), 1)
    causal = ri >= ci
    lane_h = lax.broadcasted_iota(jnp.int32, (L, LANES), 1)
    lane_q = lax.broadcasted_iota(jnp.int32, (1, QW), 1)

    NH = QW // SSM_HEAD_DIM
    ys = []
    for q in range(GROUP_CH // QW):
        rhs = rhs_refs[q]
        xq = xb[:, q * QW:(q + 1) * QW]
        hq = ht_scr[g, :, q * QW:(q + 1) * QW]
        hqb = hq.astype(BF16)
        mhs, ces, bws = [], [], []
        cdq = jnp.zeros((1, QW), F32)
        for j in range(NH):
            h = g * SSM_HEADS_PER_GROUP + q * NH + j
            cols = slice(j * SSM_HEAD_DIM, (j + 1) * SSM_HEAD_DIM)
            rhs[j * L:(j + 1) * L, cols] = xq[:, cols]
            rhs[NH * L + j * SSM_STATE:NH * L + (j + 1) * SSM_STATE, cols] = hqb[:, cols]
            a_row = acst_scr[pl.ds(h, 1), :]
            b_row = lbt_scr[pl.ds(h, 1), :]
            a_col = jnp.sum(jnp.where(lane_h == h, a_cs, 0.0), axis=1, keepdims=True)
            dmat = jnp.where(causal, a_col - b_row, -jnp.inf)
            mhs.append((cb * jnp.exp2(dmat)).astype(BF16))
            ces.append((ca * jnp.exp2(a_col)).astype(BF16))
            a_last = a_row[:, L - 1:L]
            w_row = jnp.exp2(a_last - b_row)
            bws.append((bt * w_row).astype(BF16))
            lm = (lane_q >= j * SSM_HEAD_DIM) & (lane_q < (j + 1) * SSM_HEAD_DIM)
            cdq = jnp.where(lm, jnp.exp2(a_last), cdq)
        acc = _dot(jnp.concatenate(mhs + ces, axis=1), rhs[...])
        st = _dot(jnp.concatenate(bws, axis=1), rhs[0:NH * L, :])
        ht_scr[g, :, q * QW:(q + 1) * QW] = hq * cdq + st
        ys.append(acc + xa[:, q * QW:(q + 1) * QW] * dsk_ref[:, q * QW:(q + 1) * QW])

    z = z_ref[...].astype(F32)
    sz = _silu(z)
    ygs = [ys[q] * sz[:, q * QW:(q + 1) * QW] for q in range(len(ys))]
    ss = sum(jnp.sum(v * v, axis=1, keepdims=True) for v in ygs)
    scale = lax.rsqrt(ss * (1.0 / GROUP_CH) + EPS)
    for q in range(len(ys)):
        o_ref[:, q * QW:(q + 1) * QW] = (
            ygs[q] * scale * nw_ref[:, q * QW:(q + 1) * QW]).astype(o_ref.dtype)


def _ssd(proj, dt_raw, conv_w, conv_b, dt_bias, a_log, d_exp, norm_w):
    S = proj.shape[0]
    L, G, N = SSM_CHUNK, SSM_GROUPS, SSM_STATE
    GPS = SSD_GROUPS_PER_STEP
    XW, NW = GPS * GROUP_CH, GPS * N
    bcol = D_INNER // NW
    ccol = bcol + G // GPS
    pxcol = COL_XBC // XW
    pbcol = COL_XBC // NW + bcol
    pccol = pbcol + G // GPS
    zcol = COL_Z // XW
    return pl.pallas_call(
        _ssd_kernel,
        grid=(S // L, G // GPS),
        in_specs=[
            pl.BlockSpec((L, XW), lambda c, g: (c, pxcol + g)),
            pl.BlockSpec((L, NW), lambda c, g: (c, pbcol + g)),
            pl.BlockSpec((L, NW), lambda c, g: (c, pccol + g)),
            pl.BlockSpec((L, XW), lambda c, g: (c, zcol + g)),
            pl.BlockSpec((L, LANES), lambda c, g: (c, 0)),
            pl.BlockSpec((SSM_CONV, XW), lambda c, g: (0, g)),
            pl.BlockSpec((SSM_CONV, NW), lambda c, g: (0, bcol + g)),
            pl.BlockSpec((SSM_CONV, NW), lambda c, g: (0, ccol + g)),
            pl.BlockSpec((1, XW), lambda c, g: (0, g)),
            pl.BlockSpec((1, NW), lambda c, g: (0, bcol + g)),
            pl.BlockSpec((1, NW), lambda c, g: (0, ccol + g)),
            pl.BlockSpec((1, LANES), lambda c, g: (0, 0)),
            pl.BlockSpec((1, LANES), lambda c, g: (0, 0)),
            pl.BlockSpec((1, XW), lambda c, g: (0, g)),
            pl.BlockSpec((1, XW), lambda c, g: (0, g)),
        ],
        out_specs=pl.BlockSpec((L, XW), lambda c, g: (c, g)),
        out_shape=jax.ShapeDtypeStruct((S, D_INNER), BF16),
        scratch_shapes=[
            pltpu.VMEM((G, SUBLANES, GROUP_CH), F32),
            pltpu.VMEM((G, SUBLANES, N), F32),
            pltpu.VMEM((G, SUBLANES, N), F32),
            pltpu.VMEM((GPS, L + SUBLANES, GROUP_CH), F32),
            pltpu.VMEM((GPS, L + SUBLANES, N), F32),
            pltpu.VMEM((GPS, L + SUBLANES, N), F32),
            pltpu.VMEM((L, LANES), F32),
            pltpu.VMEM((LANES, L), F32),
            pltpu.VMEM((LANES, L), F32),
            pltpu.VMEM((G, N, GROUP_CH), F32),
            pltpu.VMEM((GPS, GROUP_CH // (4 * SSM_HEAD_DIM), 4 * (L + N), 4 * SSM_HEAD_DIM), BF16),
        ],
        compiler_params=_cparams(("arbitrary", "arbitrary")),
        name="ssd",
    )(proj, proj, proj, proj, dt_raw, conv_w, conv_w, conv_w, conv_b, conv_b, conv_b,
      dt_bias, a_log, d_exp, norm_w)


def _qk_prep_kernel(q_ref, k_ref, v_ref, qw_ref, kw_ref, qt_ref, kn_ref, vt_ref, pen_ref, kbar_scr):
    S = q_ref.shape[0]
    BS = MOBA_BLOCK
    nb = S // BS
    nbp = kbar_scr.shape[0]
    kbar_scr[...] = jnp.zeros(kbar_scr.shape, F32)
    eye = (lax.broadcasted_iota(jnp.int32, (ATTN_HEAD_DIM, ATTN_HEAD_DIM), 0)
           == lax.broadcasted_iota(jnp.int32, (ATTN_HEAD_DIM, ATTN_HEAD_DIM), 1)).astype(BF16)

    def kloop(b, carry):
        rows = pl.ds(pl.multiple_of(b * BS, BS), BS)
        k = k_ref[rows, :].astype(F32)
        kn = k * lax.rsqrt(jnp.mean(k * k, axis=-1, keepdims=True) + EPS) * kw_ref[...]
        kn_ref[0, rows, :] = kn.astype(BF16)
        kbar_scr[pl.ds(b, 1), :] = jnp.mean(kn, axis=0, keepdims=True)
        vt_ref[0, b, 0:ATTN_HEAD_DIM, :] = _dot_nt(eye, v_ref[rows, :]).astype(BF16)
        vt_ref[0, b, ATTN_HEAD_DIM:, :] = jnp.ones((MOBA_VT_ROWS - ATTN_HEAD_DIM, BS), BF16)
        return carry

    lax.fori_loop(0, nb, kloop, 0, unroll=4)
    kb_hi, kb_mid, _ = _split3(kbar_scr[...])

    def qloop(i, carry):
        rows = pl.ds(pl.multiple_of(i * BS, BS), BS)
        q = q_ref[rows, :].astype(F32)
        qn = q * lax.rsqrt(jnp.mean(q * q, axis=-1, keepdims=True) + EPS) * qw_ref[...]
        qnt = (qn * (ATTN_HEAD_DIM ** -0.5)).T
        qt_ref[0, i] = (qnt * LOG2E).astype(BF16)
        q_hi, q_mid, _ = _split3(qnt)
        gate = _dot(kb_hi, q_hi) + _dot(kb_mid, q_hi) + _dot(kb_hi, q_mid)
        blk = lax.broadcasted_iota(jnp.int32, gate.shape, 0)
        blk_f = blk.astype(F32)
        gate = jnp.where(blk < i, gate, -jnp.inf)
        pen = jnp.full(gate.shape, -jnp.inf, F32)
        for _ in range(MOBA_TOPK):
            mx = jnp.max(gate, axis=0, keepdims=True)
            cand = jnp.where((gate == mx) & (mx > -jnp.inf), blk_f, float(nbp))
            idx = jnp.min(cand, axis=0, keepdims=True)
            hit = blk_f == idx
            pen = jnp.where(hit, 0.0, pen)
            gate = jnp.where(hit, -jnp.inf, gate)
        pen_ref[0, i] = pen
        return carry

    lax.fori_loop(0, nb, qloop, 0, unroll=8)


def _qk_prep(proj, q_norm_w, k_norm_w):
    S = proj.shape[0]
    H, D, BS = ATTN_HEADS, ATTN_HEAD_DIM, MOBA_BLOCK
    nb = S // BS
    nbp = -(-nb // SUBLANES) * SUBLANES
    return pl.pallas_call(
        _qk_prep_kernel,
        grid=(H,),
        in_specs=[
            pl.BlockSpec((S, D), lambda h: (0, COL_Q // D + h)),
            pl.BlockSpec((S, D), lambda h: (0, COL_K // D + h)),
            pl.BlockSpec((S, D), lambda h: (0, COL_V // D + h)),
            pl.BlockSpec((1, D), lambda h: (0, 0)),
            pl.BlockSpec((1, D), lambda h: (0, 0)),
        ],
        out_specs=[
            pl.BlockSpec((1, nb, D, BS), lambda h: (h, 0, 0, 0)),
            pl.BlockSpec((1, S, D), lambda h: (h, 0, 0)),
            pl.BlockSpec((1, nb, MOBA_VT_ROWS, BS), lambda h: (h, 0, 0, 0)),
            pl.BlockSpec((1, nb, nbp, BS), lambda h: (h, 0, 0, 0)),
        ],
        out_shape=[
            jax.ShapeDtypeStruct((H, nb, D, BS), BF16),
            jax.ShapeDtypeStruct((H, S, D), BF16),
            jax.ShapeDtypeStruct((H, nb, MOBA_VT_ROWS, BS), BF16),
            jax.ShapeDtypeStruct((H, nb, nbp, BS), F32),
        ],
        scratch_shapes=[pltpu.VMEM((nbp, D), F32)],
        compiler_params=_cparams(("arbitrary",)),
        name="qk_prep",
    )(proj, proj, proj, q_norm_w, k_norm_w)


def _rel_bucket(dist):
    n = jnp.maximum(dist, 0)
    max_exact = REL_BUCKETS // 2
    nf = jnp.maximum(n, 1).astype(F32)
    large = max_exact + (jnp.log(nf / max_exact) / math.log(REL_MAX_DIST / max_exact)
                         * (REL_BUCKETS - max_exact)).astype(jnp.int32)
    large = jnp.minimum(large, REL_BUCKETS - 1)
    return jnp.where(n < max_exact, n, large)


MOBA_HEADS_PER_STEP = 4
MOBA_VT_ROWS = ATTN_HEAD_DIM + 2 * SUBLANES


def _moba_kernel(tbl_ref, qt_ref, k_ref, vt_ref, pen_ref, o_ref,
                 bown_scr, bprev_scr, m_scr, acc_scr, s0_scr, s1_scr, s2_scr, s3_scr,
                 p0_scr, p1_scr, p2_scr, p3_scr, al0_scr, al1_scr, al2_scr, al3_scr):
    hp = pl.program_id(0)
    i = pl.program_id(1)
    BS, HB = MOBA_BLOCK, MOBA_HEADS_PER_STEP
    D = ATTN_HEAD_DIM

    @pl.when(i == 0)
    def _():
        kl = lax.broadcasted_iota(jnp.int32, (BS, BS), 0)
        ql = lax.broadcasted_iota(jnp.int32, (BS, BS), 1)
        d = ql - kl
        b_own = _rel_bucket(d)
        b_prev = _rel_bucket(d + BS)
        for a in range(HB):
            h = hp * HB + a
            far = tbl_ref[h, REL_BUCKETS - 1]

            def lookup(bucket):
                val = jnp.zeros((BS, BS), F32)
                for b in range(REL_BUCKETS):
                    val = jnp.where(bucket == b, tbl_ref[h, b], val)
                return (val - far) * LOG2E

            bown_scr[a] = jnp.where(d >= 0, lookup(b_own), -jnp.inf)
            bprev_scr[a] = lookup(b_prev)

    def krows(a, j):
        return k_ref[a, pl.ds(pl.multiple_of(j * BS, BS), BS), :]

    jp = jnp.maximum(i - 1, 0)
    no_prev = jnp.where(i == 0, -jnp.inf, 0.0)
    n_far = jnp.maximum(i - 1, 0)
    nb = k_ref.shape[1] // BS

    def clampj(j):
        return jnp.clip(j, 0, nb - 1)

    def scores(j, s_slot):
        for a in range(HB):
            s_slot[a] = _dot(krows(a, clampj(j)), qt_ref[a, 0])

    def softmax(j, s_slot, p_slot, al_slot, mask_tail, bias_scr=None):
        for a in range(HB):
            pr = pen_ref[a, 0, pl.ds(clampj(j), 1), :]
            if mask_tail is True:
                pr = pr + jnp.where(j >= n_far, -jnp.inf, 0.0)
            elif mask_tail is not False:
                pr = pr + mask_tail
            def tile():
                s = s_slot[a]
                return s if bias_scr is None else s + bias_scr[a]

            m_old = m_scr[a]
            m_new = jnp.maximum(m_old, jnp.max(tile(), axis=0, keepdims=True) + pr)
            al_slot[a] = jnp.exp2(m_old - m_new)
            m_scr[a] = m_new
            p_slot[a] = jnp.exp2(tile() - (m_new - pr)).astype(BF16)

    def accumulate(j, p_slot, al_slot, maybe_prev=False):
        jv = clampj(j)
        if maybe_prev:
            jv = jnp.where(j < 0, jp, jv)
        for a in range(HB):
            acc_scr[a] = al_slot[a] * acc_scr[a] + _dot(vt_ref[a, jv], p_slot[a])

    s_slots = (s0_scr, s1_scr, s2_scr, s3_scr)
    p_slots = (p0_scr, p1_scr, p2_scr, p3_scr)
    al_slots = (al0_scr, al1_scr, al2_scr, al3_scr)
    scores(i, s2_scr)
    scores(jp, s3_scr)
    scores(0, s0_scr)
    for a in range(HB):
        s = s2_scr[a] + bown_scr[a]
        m0 = jnp.max(s, axis=0, keepdims=True)
        m_scr[a] = m0
        p2_scr[a] = jnp.exp2(s - m0).astype(BF16)
    for a in range(HB):
        acc_scr[a] = _dot(vt_ref[a, i], p2_scr[a])
    softmax(jp, s3_scr, p3_scr, al3_scr, no_prev, bias_scr=bprev_scr)

    def quad_loop(g, carry):
        j = 4 * g
        for a in range(HB):
            kq = k_ref[a, pl.ds(pl.multiple_of((j + 1) * BS, BS), 3 * BS), :]
            s3 = _dot(kq, qt_ref[a, 0])
            for t in range(3):
                s_slots[t + 1][a] = s3[t * BS:(t + 1) * BS, :]
        for t in range(4):
            accumulate(j + t - 1, p_slots[(t - 1) % 4], al_slots[(t - 1) % 4], maybe_prev=(t == 0))
            softmax(j + t, s_slots[t], p_slots[t], al_slots[t], False)
            if t == 0:
                scores(j + 4, s0_scr)
        return carry

    n_quads = n_far // 4
    lax.fori_loop(0, n_quads, quad_loop, 0)

    def pair_loop(g, carry):
        j = 4 * n_quads + 2 * g
        scores(j + 1, s1_scr)
        accumulate(j - 1, p3_scr, al3_scr, maybe_prev=True)
        softmax(j, s0_scr, p2_scr, al2_scr, False)
        scores(j + 2, s0_scr)
        accumulate(j, p2_scr, al2_scr)
        softmax(j + 1, s1_scr, p3_scr, al3_scr, True)
        return carry

    n_pairs = (n_far - 4 * n_quads + 1) // 2
    lax.fori_loop(0, n_pairs, pair_loop, 0)
    accumulate(4 * n_quads + 2 * n_pairs - 1, p3_scr, al3_scr, maybe_prev=True)
    for a in range(HB):
        acc = acc_scr[a]
        o_ref[:, a * D:(a + 1) * D] = (acc[0:D, :] / acc[D:D + 1, :]).T.astype(o_ref.dtype)


def _moba(qt, kn, vt, pen, table_h):
    H, nb, D, BS = qt.shape
    S = nb * BS
    nbp = pen.shape[2]
    HB = MOBA_HEADS_PER_STEP
    return pl.pallas_call(
        _moba_kernel,
        grid=(H // HB, nb),
        in_specs=[
            pl.BlockSpec(memory_space=pltpu.SMEM),
            pl.BlockSpec((HB, 1, D, BS), lambda hp, i: (hp, i, 0, 0)),
            pl.BlockSpec((HB, S, D), lambda hp, i: (hp, 0, 0)),
            pl.BlockSpec((HB, nb, MOBA_VT_ROWS, BS), lambda hp, i: (hp, 0, 0, 0)),
            pl.BlockSpec((HB, 1, nbp, BS), lambda hp, i: (hp, i, 0, 0)),
        ],
        out_specs=pl.BlockSpec((BS, HB * D), lambda hp, i: (i, hp)),
        out_shape=jax.ShapeDtypeStruct((S, H * D), BF16),
        scratch_shapes=[
            pltpu.VMEM((HB, BS, BS), F32),
            pltpu.VMEM((HB, BS, BS), F32),
            pltpu.VMEM((HB, 1, BS), F32),
            pltpu.VMEM((HB, MOBA_VT_ROWS, BS), F32),
        ] + [pltpu.VMEM((HB, BS, BS), F32)] * 4 + [pltpu.VMEM((HB, BS, BS), BF16)] * 4
        + [pltpu.VMEM((HB, 1, BS), F32)] * 4,
        compiler_params=_cparams(("arbitrary", "arbitrary")),
        name="moba",
    )(table_h, qt, kn, vt, pen)


def _mix_kernel(ys_ref, ya_ref, ws_ref, wa_ref, gs_ref, ga_ref, bs_ref, ba_ref, o_ref):
    gs = _sigmoid(gs_ref[...].astype(F32) + bs_ref[...])
    ga = _sigmoid(ga_ref[...].astype(F32) + ba_ref[...])
    o_ref[...] = (gs * _dot(ys_ref[...], ws_ref[...])
                  + ga * _dot(ya_ref[...], wa_ref[...])).astype(o_ref.dtype)


def _mix(y_ssd, y_attn, w_ssm_out, w_attn_out, proj, b_gate, tm=1024, tn=512):
    S = y_ssd.shape[0]
    tm = min(tm, S)
    gs_col = COL_G // tn
    ga_col = (COL_G + D_MODEL) // tn
    nb = D_MODEL // tn
    return pl.pallas_call(
        _mix_kernel,
        grid=(S // tm, D_MODEL // tn),
        in_specs=[
            pl.BlockSpec((tm, D_INNER), lambda m, n: (m, 0)),
            pl.BlockSpec((tm, D_ATTN), lambda m, n: (m, 0)),
            pl.BlockSpec((D_INNER, tn), lambda m, n: (0, n)),
            pl.BlockSpec((D_ATTN, tn), lambda m, n: (0, n)),
            pl.BlockSpec((tm, tn), lambda m, n: (m, gs_col + n)),
            pl.BlockSpec((tm, tn), lambda m, n: (m, ga_col + n)),
            pl.BlockSpec((1, tn), lambda m, n: (0, n)),
            pl.BlockSpec((1, tn), lambda m, n: (0, nb + n)),
        ],
        out_specs=pl.BlockSpec((tm, tn), lambda m, n: (m, n)),
        out_shape=jax.ShapeDtypeStruct((S, D_MODEL), BF16),
        compiler_params=_cparams(("arbitrary", "arbitrary")),
        name="mix",
    )(y_ssd, y_attn, w_ssm_out, w_attn_out, proj, proj, b_gate, b_gate)


def _out_norm_kernel(x_ref, mix_ref, w_ref, nw_ref, x1_ref, h2_ref):
    x1 = x_ref[...] + _dot(mix_ref[...], w_ref[...])
    x1_ref[...] = x1
    ms = jnp.mean(x1 * x1, axis=-1, keepdims=True)
    h2_ref[...] = (x1 * lax.rsqrt(ms + EPS) * nw_ref[...]).astype(h2_ref.dtype)


def _out_norm(x, mix, w_out, ffn_norm_w, tm=512):
    S = x.shape[0]
    tm = min(tm, S)
    return pl.pallas_call(
        _out_norm_kernel,
        grid=(S // tm,),
        in_specs=[
            pl.BlockSpec((tm, D_MODEL), lambda m: (m, 0)),
            pl.BlockSpec((tm, D_MODEL), lambda m: (m, 0)),
            pl.BlockSpec((D_MODEL, D_MODEL), lambda m: (0, 0)),
            pl.BlockSpec((1, D_MODEL), lambda m: (0, 0)),
        ],
        out_specs=[
            pl.BlockSpec((tm, D_MODEL), lambda m: (m, 0)),
            pl.BlockSpec((tm, D_MODEL), lambda m: (m, 0)),
        ],
        out_shape=[
            jax.ShapeDtypeStruct((S, D_MODEL), F32),
            jax.ShapeDtypeStruct((S, D_MODEL), BF16),
        ],
        compiler_params=_cparams(("arbitrary",)),
        name="out_norm",
    )(x, mix, w_out, ffn_norm_w)


def _ffn_up_kernel(h_ref, wg32_ref, wu32_ref, cwg_ref, cwu_ref, cbg_ref, cbu_ref, o_ref,
                   tailg, tailu, padg, padu, wg_ref, wu_ref):
    tm = h_ref.shape[0]

    @pl.when(pl.program_id(1) == 0)
    def _():
        tailg[...] = jnp.zeros(tailg.shape, F32)
        tailu[...] = jnp.zeros(tailu.shape, F32)
        wg_ref[...] = wg32_ref[...].astype(BF16)
        wu_ref[...] = wu32_ref[...].astype(BF16)

    def conv(w_ref, cw_ref, cb_ref, tail, pad):
        pad[0:SUBLANES, :] = tail[...]
        pad[SUBLANES:SUBLANES + tm, :] = _dot(h_ref[...], w_ref[...])
        tail[...] = pad[tm:tm + SUBLANES, :]
        acc = cb_ref[...]
        for k in range(FFN_CONV):
            off = SUBLANES - (FFN_CONV - 1) + k
            acc = acc + cw_ref[k:k + 1, :] * pad[off:off + tm, :]
        return acc

    ug = conv(wg_ref, cwg_ref, cbg_ref, tailg, padg)
    uu = conv(wu_ref, cwu_ref, cbu_ref, tailu, padu)
    o_ref[...] = (_silu(ug) * uu).astype(o_ref.dtype)


def _ffn_up(h2, w_up, conv_w, conv_b, tm=1024, tn=512):
    S = h2.shape[0]
    tm = min(tm, S)
    nb = D_FF // tn
    return pl.pallas_call(
        _ffn_up_kernel,
        grid=(nb, S // tm),
        in_specs=[
            pl.BlockSpec((tm, D_MODEL), lambda n, m: (m, 0)),
            pl.BlockSpec((D_MODEL, tn), lambda n, m: (0, n)),
            pl.BlockSpec((D_MODEL, tn), lambda n, m: (0, nb + n)),
            pl.BlockSpec((FFN_CONV, tn), lambda n, m: (0, n)),
            pl.BlockSpec((FFN_CONV, tn), lambda n, m: (0, nb + n)),
            pl.BlockSpec((1, tn), lambda n, m: (0, n)),
            pl.BlockSpec((1, tn), lambda n, m: (0, nb + n)),
        ],
        out_specs=pl.BlockSpec((tm, tn), lambda n, m: (m, n)),
        out_shape=jax.ShapeDtypeStruct((S, D_FF), BF16),
        scratch_shapes=[
            pltpu.VMEM((SUBLANES, tn), F32),
            pltpu.VMEM((SUBLANES, tn), F32),
            pltpu.VMEM((tm + SUBLANES, tn), F32),
            pltpu.VMEM((tm + SUBLANES, tn), F32),
            pltpu.VMEM((D_MODEL, tn), BF16),
            pltpu.VMEM((D_MODEL, tn), BF16),
        ],
        compiler_params=_cparams(("arbitrary", "arbitrary")),
        name="ffn_up",
    )(h2, w_up, w_up, conv_w, conv_w, conv_b, conv_b)


def _ffn_down_kernel(a_ref, w_ref, x1_ref, o_ref):
    o_ref[...] = x1_ref[...] + _dot(a_ref[...], w_ref[...])


def _ffn_down(act, w_down, x1, tm=1024, tn=512):
    S = act.shape[0]
    tm = min(tm, S)
    return pl.pallas_call(
        _ffn_down_kernel,
        grid=(S // tm, D_MODEL // tn),
        in_specs=[
            pl.BlockSpec((tm, D_FF), lambda m, n: (m, 0)),
            pl.BlockSpec((D_FF, tn), lambda m, n: (0, n)),
            pl.BlockSpec((tm, tn), lambda m, n: (m, n)),
        ],
        out_specs=pl.BlockSpec((tm, tn), lambda m, n: (m, n)),
        out_shape=jax.ShapeDtypeStruct((S, D_MODEL), F32),
        compiler_params=_cparams(("arbitrary", "arbitrary")),
        name="ffn_down",
    )(act, w_down, x1)


def _pad_lanes(v):
    return jnp.pad(v, ((0, 0), (0, LANES - v.shape[1])))


def _layer(x, attn_norm_w, w_in, b_gate, ssm_conv_w, ssm_conv_b, ssm_dt_bias, ssm_a_log, ssm_d,
           ssm_norm_w, q_norm_w, k_norm_w, rel_bias, w_ssm_out, w_attn_out, w_out, ffn_norm_w,
           w_up, ffn_conv_w, ffn_conv_b, w_down):
    o_dt = D_INNER + D_XBC
    o_q = o_dt + SSM_HEADS
    wt = w_in.T
    wt_dt = jnp.pad(wt[o_dt:o_q], ((0, LANES - SSM_HEADS), (0, 0)))

    h, dt_raw = _norm_dt(x, attn_norm_w[None, :], wt_dt)
    proj = _in_proj(h, wt, o_dt)
    y_ssd = _ssd(proj, dt_raw, ssm_conv_w, ssm_conv_b[None, :], _pad_lanes(ssm_dt_bias[None, :]),
                 _pad_lanes(ssm_a_log[None, :]), jnp.repeat(ssm_d, SSM_HEAD_DIM)[None, :],
                 ssm_norm_w[None, :])
    qt, kn, vt, pen = _qk_prep(proj, q_norm_w[None, :], k_norm_w[None, :])
    y_attn = _moba(qt, kn, vt, pen, rel_bias.T)
    mix = _mix(y_ssd, y_attn, w_ssm_out.astype(BF16), w_attn_out.astype(BF16), proj, b_gate[None, :])
    x1, h2 = _out_norm(x, mix, w_out.astype(BF16), ffn_norm_w[None, :])
    act = _ffn_up(h2, w_up, ffn_conv_w, ffn_conv_b[None, :])
    return _ffn_down(act, w_down.astype(BF16), x1)


def kernel(x, attn_norm_w, w_in, b_gate, ssm_conv_w, ssm_conv_b, ssm_dt_bias, ssm_a_log, ssm_d,
           ssm_norm_w, q_norm_w, k_norm_w, rel_bias, w_ssm_out, w_attn_out, w_out, ffn_norm_w,
           w_up, ffn_conv_w, ffn_conv_b, w_down):
    assert x.shape[0] == 1 and attn_norm_w.shape[0] == 1
    out = _layer(x[0], attn_norm_w[0], w_in[0], b_gate[0], ssm_conv_w[0], ssm_conv_b[0],
                 ssm_dt_bias[0], ssm_a_log[0], ssm_d[0], ssm_norm_w[0], q_norm_w[0], k_norm_w[0],
                 rel_bias, w_ssm_out[0], w_attn_out[0], w_out[0], ffn_norm_w[0], w_up[0],
                 ffn_conv_w[0], ffn_conv_b[0], w_down[0])
    return out[None]
```

```python
import math

import jax
import jax.numpy as jnp
from jax import lax
from jax.experimental import pallas as pl
from jax.experimental.pallas import tpu as pltpu

F32 = jnp.float32
BF16 = jnp.bfloat16

D_MODEL = 2048
D_INNER = 4096
SSM_HEADS = 64
SSM_HEAD_DIM = 64
SSM_GROUPS = 8
SSM_HEADS_PER_GROUP = SSM_HEADS // SSM_GROUPS
SSM_STATE = 128
SSM_CONV = 4
SSM_CHUNK = 256
GROUP_CH = D_INNER // SSM_GROUPS
SSD_GROUPS_PER_STEP = 8
D_XBC = D_INNER + 2 * SSM_GROUPS * SSM_STATE
ATTN_HEADS = 16
ATTN_HEAD_DIM = 128
D_ATTN = ATTN_HEADS * ATTN_HEAD_DIM
MOBA_BLOCK = 256
MOBA_TOPK = 3
REL_BUCKETS = 32
REL_MAX_DIST = 128
D_FF = 5632
FFN_CONV = 3
EPS = 1e-6
LOG2E = 1.4426950408889634

LANES = 128
SUBLANES = 8
D_PROJ = D_XBC + 3 * D_ATTN + 2 * D_INNER
COL_Z = 0
COL_XBC = COL_Z + D_INNER
COL_Q = COL_XBC + D_XBC
COL_K = COL_Q + D_ATTN
COL_V = COL_K + D_ATTN
COL_G = COL_V + D_ATTN
VMEM_LIMIT = 56 * 1024 * 1024


def _cparams(sem):
    return pltpu.CompilerParams(dimension_semantics=sem, vmem_limit_bytes=VMEM_LIMIT)


def _split3(a):
    hi = a.astype(BF16)
    r = a - hi.astype(F32)
    mid = r.astype(BF16)
    lo = (r - mid.astype(F32)).astype(BF16)
    return hi, mid, lo


def _dot(a, b):
    return jnp.dot(a, b, preferred_element_type=F32)


def _dot_nt(a, b):
    return lax.dot_general(a, b, (((1,), (1,)), ((), ())), preferred_element_type=F32)


def _sigmoid(x):
    return 0.5 + 0.5 * jnp.tanh(0.5 * x)


def _silu(x):
    h = 0.5 * x
    return h + h * jnp.tanh(h)


def _norm_dt_kernel(x_ref, nw_ref, wdt_ref, h_ref, dt_ref):
    x = x_ref[...]
    ms = jnp.mean(x * x, axis=-1, keepdims=True)
    h = x * lax.rsqrt(ms + EPS) * nw_ref[...]
    h_ref[...] = h.astype(BF16)
    h_hi, h_mid, _ = _split3(h)
    w_hi, w_mid, _ = _split3(wdt_ref[...])
    dt_ref[...] = _dot_nt(h_hi, w_hi) + _dot_nt(h_hi, w_mid) + _dot_nt(h_mid, w_hi)


def _norm_dt(x, norm_w, wt_dt, tm=512):
    S = x.shape[0]
    tm = min(tm, S)
    return pl.pallas_call(
        _norm_dt_kernel,
        grid=(S // tm,),
        in_specs=[
            pl.BlockSpec((tm, D_MODEL), lambda m: (m, 0)),
            pl.BlockSpec((1, D_MODEL), lambda m: (0, 0)),
            pl.BlockSpec((LANES, D_MODEL), lambda m: (0, 0)),
        ],
        out_specs=[
            pl.BlockSpec((tm, D_MODEL), lambda m: (m, 0)),
            pl.BlockSpec((tm, LANES), lambda m: (m, 0)),
        ],
        out_shape=[
            jax.ShapeDtypeStruct((S, D_MODEL), BF16),
            jax.ShapeDtypeStruct((S, LANES), F32),
        ],
        compiler_params=_cparams(("arbitrary",)),
        name="norm_dt",
    )(x, norm_w, wt_dt)


def _in_proj_kernel(h_ref, w32_ref, o_ref, w_scr):
    @pl.when(pl.program_id(1) == 0)
    def _():
        w_scr[...] = w32_ref[...].astype(BF16)

    o_ref[...] = _dot_nt(h_ref[...], w_scr[...]).astype(o_ref.dtype)


def _in_proj(h, wt, dt_row0, tm=2048, tn=1024):
    S = h.shape[0]
    tm = min(tm, S)
    assert dt_row0 % tn == 0 and SSM_HEADS % SUBLANES == 0
    na = dt_row0 // tn

    def w_rows(n, m):
        return (pl.multiple_of(n * tn + jnp.where(n >= na, SSM_HEADS, 0), SSM_HEADS), 0)

    return pl.pallas_call(
        _in_proj_kernel,
        grid=(D_PROJ // tn, S // tm),
        in_specs=[
            pl.BlockSpec((tm, D_MODEL), lambda n, m: (m, 0)),
            pl.BlockSpec((pl.Element(tn), pl.Element(D_MODEL)), w_rows),
        ],
        out_specs=pl.BlockSpec((tm, tn), lambda n, m: (m, n)),
        out_shape=jax.ShapeDtypeStruct((S, D_PROJ), BF16),
        scratch_shapes=[pltpu.VMEM((tn, D_MODEL), BF16)],
        compiler_params=_cparams(("arbitrary", "arbitrary")),
        name="in_proj",
    )(h, wt)


def _conv_silu(raw_ref, tail_ref, pad_ref, w_ref, b_ref, g, taps):
    L = raw_ref.shape[0]
    pad_ref[0:SUBLANES, :] = tail_ref[g]
    pad_ref[SUBLANES:SUBLANES + L, :] = raw_ref[...].astype(F32)
    tail_ref[g] = pad_ref[L:L + SUBLANES, :]
    acc = b_ref[...]
    for k in range(taps):
        off = SUBLANES - (taps - 1) + k
        acc = acc + w_ref[k:k + 1, :] * pad_ref[off:off + L, :]
    return _silu(acc)


def _ssd_kernel(x_ref, b_ref, c_ref, z_ref, dtr_ref, cwx_ref, cwb_ref, cwc_ref,
                cbx_ref, cbb_ref, cbc_ref, dtb_ref, alog_ref, dsk_ref, nw_ref,
                o_ref,
                tailx, tailb, tailc, padx, padb, padc, acs_scr, acst_scr, lbt_scr, ht_scr, rhs_scr):
    c = pl.program_id(0)
    gp = pl.program_id(1)
    L = SSM_CHUNK
    QW = 4 * SSM_HEAD_DIM
    GPS = SSD_GROUPS_PER_STEP

    @pl.when(c == 0)
    def _():
        for gg in range(GPS):
            g0 = gp * GPS + gg
            tailx[g0] = jnp.zeros(tailx.shape[1:], F32)
            tailb[g0] = jnp.zeros(tailb.shape[1:], F32)
            tailc[g0] = jnp.zeros(tailc.shape[1:], F32)
            ht_scr[g0] = jnp.zeros(ht_scr.shape[1:], F32)
            rhs_scr[gg] = jnp.zeros(rhs_scr.shape[1:], BF16)

    @pl.when(gp == 0)
    def _():
        t = dtr_ref[...] + dtb_ref[...]
        dt = jnp.maximum(t, 0.0) + jnp.log(1.0 + jnp.exp(-jnp.abs(t)))
        a = -jnp.exp(alog_ref[...])
        da = dt * a
        ri = lax.broadcasted_iota(jnp.int32, (L, L), 0)
        ci = lax.broadcasted_iota(jnp.int32, (L, L), 1)
        tri = jnp.where(ri >= ci, 1.0, 0.0).astype(BF16)
        hi, mid, lo = _split3(da)
        a_cs = _dot(tri, hi) + _dot(tri, mid) + _dot(tri, lo)
        a2 = a_cs * LOG2E
        a2t = a2.T
        acs_scr[...] = a2
        acst_scr[...] = a2t
        lbt_scr[...] = a2t - jnp.log(dt.T) * LOG2E

    for gg in range(GPS):
        xs = slice(gg * GROUP_CH, (gg + 1) * GROUP_CH)
        ns = slice(gg * SSM_STATE, (gg + 1) * SSM_STATE)
        _ssd_group(gp * GPS + gg,
                   x_ref.at[:, xs], b_ref.at[:, ns], c_ref.at[:, ns], z_ref.at[:, xs],
                   cwx_ref.at[:, xs], cwb_ref.at[:, ns], cwc_ref.at[:, ns],
                   cbx_ref.at[:, xs], cbb_ref.at[:, ns], cbc_ref.at[:, ns],
                   dsk_ref.at[:, xs], nw_ref.at[:, xs], o_ref.at[:, xs],
                   tailx, tailb, tailc, padx.at[gg], padb.at[gg], padc.at[gg],
                   acs_scr, acst_scr, lbt_scr, ht_scr,
                   [rhs_scr.at[gg, q] for q in range(GROUP_CH // QW)])


def _ssd_group(g, x_ref, b_ref, c_ref, z_ref, cwx_ref, cwb_ref, cwc_ref, cbx_ref, cbb_ref, cbc_ref,
               dsk_ref, nw_ref, o_ref, tailx, tailb, tailc, padx, padb, padc,
               acs_scr, acst_scr, lbt_scr, ht_scr, rhs_refs):
    L = SSM_CHUNK
    QW = 4 * SSM_HEAD_DIM
    xa = _conv_silu(x_ref, tailx, padx, cwx_ref, cbx_ref, g, SSM_CONV)
    ba = _conv_silu(b_ref, tailb, padb, cwb_ref, cbb_ref, g, SSM_CONV)
    ca = _conv_silu(c_ref, tailc, padc, cwc_ref, cbc_ref, g, SSM_CONV)

    xb = xa.astype(BF16)
    bt = ba.T
    cb = _dot(ca.astype(BF16), bt.astype(BF16))
    a_cs = acs_scr[...]
    ri = lax.broadcasted_iota(jnp.int32, (L, L), 0)
    ci = lax.broadcasted_iota(jnp.int32, (L, L), 1)
    causal = ri >= ci
    lane_h = lax.broadcasted_iota(jnp.int32, (L, LANES), 1)
    lane_q = lax.broadcasted_iota(jnp.int32, (1, QW), 1)

    NH = QW // SSM_HEAD_DIM
    ys = []
    for q in range(GROUP_CH // QW):
        rhs = rhs_refs[q]
        xq = xb[:, q * QW:(q + 1) * QW]
        hq = ht_scr[g, :, q * QW:(q + 1) * QW]
        hqb = hq.astype(BF16)
        mhs, ces, bws = [], [], []
        cdq = jnp.zeros((1, QW), F32)
        for j in range(NH):
            h = g * SSM_HEADS_PER_GROUP + q * NH + j
            cols = slice(j * SSM_HEAD_DIM, (j + 1) * SSM_HEAD_DIM)
            rhs[j * L:(j + 1) * L, cols] = xq[:, cols]
            rhs[NH * L + j * SSM_STATE:NH * L + (j + 1) * SSM_STATE, cols] = hqb[:, cols]
            a_row = acst_scr[pl.ds(h, 1), :]
            b_row = lbt_scr[pl.ds(h, 1), :]
            a_col = jnp.sum(jnp.where(lane_h == h, a_cs, 0.0), axis=1, keepdims=True)
            dmat = jnp.where(causal, a_col - b_row, -jnp.inf)
            mhs.append((cb * jnp.exp2(dmat)).astype(BF16))
            ces.append((ca * jnp.exp2(a_col)).astype(BF16))
            a_last = a_row[:, L - 1:L]
            w_row = jnp.exp2(a_last - b_row)
            bws.append((bt * w_row).astype(BF16))
            lm = (lane_q >= j * SSM_HEAD_DIM) & (lane_q < (j + 1) * SSM_HEAD_DIM)
            cdq = jnp.where(lm, jnp.exp2(a_last), cdq)
        acc = _dot(jnp.concatenate(mhs + ces, axis=1), rhs[...])
        st = _dot(jnp.concatenate(bws, axis=1), rhs[0:NH * L, :])
        ht_scr[g, :, q * QW:(q + 1) * QW] = hq * cdq + st
        ys.append(acc + xa[:, q * QW:(q + 1) * QW] * dsk_ref[:, q * QW:(q + 1) * QW])

    z = z_ref[...].astype(F32)
    sz = _silu(z)
    ygs = [ys[q] * sz[:, q * QW:(q + 1) * QW] for q in range(len(ys))]
    ss = sum(jnp.sum(v * v, axis=1, keepdims=True) for v in ygs)
    scale = lax.rsqrt(ss * (1.0 / GROUP_CH) + EPS)
    for q in range(len(ys)):
        o_ref[:, q * QW:(q + 1) * QW] = (
            ygs[q] * scale * nw_ref[:, q * QW:(q + 1) * QW]).astype(o_ref.dtype)


def _ssd(proj, dt_raw, conv_w, conv_b, dt_bias, a_log, d_exp, norm_w):
    S = proj.shape[0]
    L, G, N = SSM_CHUNK, SSM_GROUPS, SSM_STATE
    GPS = SSD_GROUPS_PER_STEP
    XW, NW = GPS * GROUP_CH, GPS * N
    bcol = D_INNER // NW
    ccol = bcol + G // GPS
    pxcol = COL_XBC // XW
    pbcol = COL_XBC // NW + bcol
    pccol = pbcol + G // GPS
    zcol = COL_Z // XW
    return pl.pallas_call(
        _ssd_kernel,
        grid=(S // L, G // GPS),
        in_specs=[
            pl.BlockSpec((L, XW), lambda c, g: (c, pxcol + g)),
            pl.BlockSpec((L, NW), lambda c, g: (c, pbcol + g)),
            pl.BlockSpec((L, NW), lambda c, g: (c, pccol + g)),
            pl.BlockSpec((L, XW), lambda c, g: (c, zcol + g)),
            pl.BlockSpec((L, LANES), lambda c, g: (c, 0)),
            pl.BlockSpec((SSM_CONV, XW), lambda c, g: (0, g)),
            pl.BlockSpec((SSM_CONV, NW), lambda c, g: (0, bcol + g)),
            pl.BlockSpec((SSM_CONV, NW), lambda c, g: (0, ccol + g)),
            pl.BlockSpec((1, XW), lambda c, g: (0, g)),
            pl.BlockSpec((1, NW), lambda c, g: (0, bcol + g)),
            pl.BlockSpec((1, NW), lambda c, g: (0, ccol + g)),
            pl.BlockSpec((1, LANES), lambda c, g: (0, 0)),
            pl.BlockSpec((1, LANES), lambda c, g: (0, 0)),
            pl.BlockSpec((1, XW), lambda c, g: (0, g)),
            pl.BlockSpec((1, XW), lambda c, g: (0, g)),
        ],
        out_specs=pl.BlockSpec((L, XW), lambda c, g: (c, g)),
        out_shape=jax.ShapeDtypeStruct((S, D_INNER), BF16),
        scratch_shapes=[
            pltpu.VMEM((G, SUBLANES, GROUP_CH), F32),
            pltpu.VMEM((G, SUBLANES, N), F32),
            pltpu.VMEM((G, SUBLANES, N), F32),
            pltpu.VMEM((GPS, L + SUBLANES, GROUP_CH), F32),
            pltpu.VMEM((GPS, L + SUBLANES, N), F32),
            pltpu.VMEM((GPS, L + SUBLANES, N), F32),
            pltpu.VMEM((L, LANES), F32),
            pltpu.VMEM((LANES, L), F32),
            pltpu.VMEM((LANES, L), F32),
            pltpu.VMEM((G, N, GROUP_CH), F32),
            pltpu.VMEM((GPS, GROUP_CH // (4 * SSM_HEAD_DIM), 4 * (L + N), 4 * SSM_HEAD_DIM), BF16),
        ],
        compiler_params=_cparams(("arbitrary", "arbitrary")),
        name="ssd",
    )(proj, proj, proj, proj, dt_raw, conv_w, conv_w, conv_w, conv_b, conv_b, conv_b,
      dt_bias, a_log, d_exp, norm_w)


def _qk_prep_kernel(q_ref, k_ref, v_ref, qw_ref, kw_ref, qt_ref, kn_ref, vt_ref, pen_ref, kbar_scr):
    S = q_ref.shape[0]
    BS = MOBA_BLOCK
    nb = S // BS
    nbp = kbar_scr.shape[0]
    kbar_scr[...] = jnp.zeros(kbar_scr.shape, F32)
    eye = (lax.broadcasted_iota(jnp.int32, (ATTN_HEAD_DIM, ATTN_HEAD_DIM), 0)
           == lax.broadcasted_iota(jnp.int32, (ATTN_HEAD_DIM, ATTN_HEAD_DIM), 1)).astype(BF16)

    def kloop(b, carry):
        rows = pl.ds(pl.multiple_of(b * BS, BS), BS)
        k = k_ref[rows, :].astype(F32)
        kn = k * lax.rsqrt(jnp.mean(k * k, axis=-1, keepdims=True) + EPS) * kw_ref[...]
        kn_ref[0, rows, :] = kn.astype(BF16)
        kbar_scr[pl.ds(b, 1), :] = jnp.mean(kn, axis=0, keepdims=True)
        vt_ref[0, b, 0:ATTN_HEAD_DIM, :] = _dot_nt(eye, v_ref[rows, :]).astype(BF16)
        vt_ref[0, b, ATTN_HEAD_DIM:, :] = jnp.ones((MOBA_VT_ROWS - ATTN_HEAD_DIM, BS), BF16)
        return carry

    lax.fori_loop(0, nb, kloop, 0, unroll=4)
    kb_hi, kb_mid, _ = _split3(kbar_scr[...])

    def qloop(i, carry):
        rows = pl.ds(pl.multiple_of(i * BS, BS), BS)
        q = q_ref[rows, :].astype(F32)
        qn = q * lax.rsqrt(jnp.mean(q * q, axis=-1, keepdims=True) + EPS) * qw_ref[...]
        qnt = (qn * (ATTN_HEAD_DIM ** -0.5)).T
        qt_ref[0, i] = (qnt * LOG2E).astype(BF16)
        q_hi, q_mid, _ = _split3(qnt)
        gate = _dot(kb_hi, q_hi) + _dot(kb_mid, q_hi) + _dot(kb_hi, q_mid)
        blk = lax.broadcasted_iota(jnp.int32, gate.shape, 0)
        blk_f = blk.astype(F32)
        gate = jnp.where(blk < i, gate, -jnp.inf)
        pen = jnp.full(gate.shape, -jnp.inf, F32)
        for _ in range(MOBA_TOPK):
            mx = jnp.max(gate, axis=0, keepdims=True)
            cand = jnp.where((gate == mx) & (mx > -jnp.inf), blk_f, float(nbp))
            idx = jnp.min(cand, axis=0, keepdims=True)
            hit = blk_f == idx
            pen = jnp.where(hit, 0.0, pen)
            gate = jnp.where(hit, -jnp.inf, gate)
        pen_ref[0, i] = pen
        return carry

    lax.fori_loop(0, nb, qloop, 0, unroll=8)


def _qk_prep(proj, q_norm_w, k_norm_w):
    S = proj.shape[0]
    H, D, BS = ATTN_HEADS, ATTN_HEAD_DIM, MOBA_BLOCK
    nb = S // BS
    assert nb % SUBLANES == 0
    nbp = nb
    return pl.pallas_call(
        _qk_prep_kernel,
        grid=(H,),
        in_specs=[
            pl.BlockSpec((S, D), lambda h: (0, COL_Q // D + h)),
            pl.BlockSpec((S, D), lambda h: (0, COL_K // D + h)),
            pl.BlockSpec((S, D), lambda h: (0, COL_V // D + h)),
            pl.BlockSpec((1, D), lambda h: (0, 0)),
            pl.BlockSpec((1, D), lambda h: (0, 0)),
        ],
        out_specs=[
            pl.BlockSpec((1, nb, D, BS), lambda h: (h, 0, 0, 0)),
            pl.BlockSpec((1, S, D), lambda h: (h, 0, 0)),
            pl.BlockSpec((1, nb, MOBA_VT_ROWS, BS), lambda h: (h, 0, 0, 0)),
            pl.BlockSpec((1, nb, nbp, BS), lambda h: (h, 0, 0, 0)),
        ],
        out_shape=[
            jax.ShapeDtypeStruct((H, nb, D, BS), BF16),
            jax.ShapeDtypeStruct((H, S, D), BF16),
            jax.ShapeDtypeStruct((H, nb, MOBA_VT_ROWS, BS), BF16),
            jax.ShapeDtypeStruct((H, nb, nbp, BS), F32),
        ],
        scratch_shapes=[pltpu.VMEM((nbp, D), F32)],
        compiler_params=_cparams(("arbitrary",)),
        name="qk_prep",
    )(proj, proj, proj, q_norm_w, k_norm_w)


def _rel_bucket(dist):
    n = jnp.maximum(dist, 0)
    max_exact = REL_BUCKETS // 2
    nf = jnp.maximum(n, 1).astype(F32)
    large = max_exact + (jnp.log(nf / max_exact) / math.log(REL_MAX_DIST / max_exact)
                         * (REL_BUCKETS - max_exact)).astype(jnp.int32)
    large = jnp.minimum(large, REL_BUCKETS - 1)
    return jnp.where(n < max_exact, n, large)


MOBA_HEADS_PER_STEP = 4
MOBA_VT_ROWS = ATTN_HEAD_DIM + 2 * SUBLANES


def _moba_kernel(tbl_ref, qt_ref, k_ref, vt_ref, pen_ref, o_ref,
                 bown_scr, bprev_scr, m_scr, acc_scr,
                 s0_scr, s1_scr, s2_scr, s3_scr, s4_scr, s5_scr, s6_scr, s7_scr,
                 p0_scr, p1_scr, p2_scr, p3_scr, p4_scr, p5_scr, p6_scr, p7_scr,
                 al0_scr, al1_scr, al2_scr, al3_scr, al4_scr, al5_scr, al6_scr, al7_scr):
    hp = pl.program_id(0)
    i = pl.program_id(1)
    BS, HB = MOBA_BLOCK, MOBA_HEADS_PER_STEP
    D = ATTN_HEAD_DIM

    @pl.when(i == 0)
    def _():
        kl = lax.broadcasted_iota(jnp.int32, (BS, BS), 0)
        ql = lax.broadcasted_iota(jnp.int32, (BS, BS), 1)
        d = ql - kl
        b_own = _rel_bucket(d)
        b_prev = _rel_bucket(d + BS)
        for a in range(HB):
            h = hp * HB + a
            far = tbl_ref[h, REL_BUCKETS - 1]

            def lookup(bucket):
                val = jnp.zeros((BS, BS), F32)
                for b in range(REL_BUCKETS):
                    val = jnp.where(bucket == b, tbl_ref[h, b], val)
                return (val - far) * LOG2E

            bown_scr[a] = jnp.where(d >= 0, lookup(b_own), -jnp.inf)
            bprev_scr[a] = lookup(b_prev)

    def krows(a, j):
        return k_ref[a, pl.ds(pl.multiple_of(j * BS, BS), BS), :]

    jp = jnp.maximum(i - 1, 0)
    no_prev = jnp.where(i == 0, -jnp.inf, 0.0)
    n_far = jnp.maximum(i - 1, 0)
    nb = k_ref.shape[1] // BS

    def clampj(j):
        return jnp.clip(j, 0, nb - 1)

    def scores(j, s_slot):
        for a in range(HB):
            s_slot[a] = _dot(krows(a, clampj(j)), qt_ref[a, 0])

    def softmax(j, s_slot, p_slot, al_slot, mask_tail, bias_scr=None):
        for a in range(HB):
            pr = pen_ref[a, 0, pl.ds(clampj(j), 1), :]
            if mask_tail is True:
                pr = pr + jnp.where(j >= n_far, -jnp.inf, 0.0)
            elif mask_tail is not False:
                pr = pr + mask_tail
            def tile():
                s = s_slot[a]
                return s if bias_scr is None else s + bias_scr[a]

            m_old = m_scr[a]
            m_new = jnp.maximum(m_old, jnp.max(tile(), axis=0, keepdims=True) + pr)
            al_slot[a] = jnp.exp2(m_old - m_new)
            m_scr[a] = m_new
            p_slot[a] = jnp.exp2(tile() - (m_new - pr)).astype(BF16)

    def accumulate(j, p_slot, al_slot, maybe_prev=False):
        jv = clampj(j)
        if maybe_prev:
            jv = jnp.where(j < 0, jp, jv)
        for a in range(HB):
            acc_scr[a] = al_slot[a] * acc_scr[a] + _dot(vt_ref[a, jv], p_slot[a])

    s_slots = (s0_scr, s1_scr, s2_scr, s3_scr)
    p_slots = (p0_scr, p1_scr, p2_scr, p3_scr)
    al_slots = (al0_scr, al1_scr, al2_scr, al3_scr)
    scores(i, s2_scr)
    scores(jp, s3_scr)
    scores(0, s0_scr)
    for a in range(HB):
        s = s2_scr[a] + bown_scr[a]
        m0 = jnp.max(s, axis=0, keepdims=True)
        m_scr[a] = m0
        p2_scr[a] = jnp.exp2(s - m0).astype(BF16)
    for a in range(HB):
        acc_scr[a] = _dot(vt_ref[a, i], p2_scr[a])
    softmax(jp, s3_scr, p3_scr, al3_scr, no_prev, bias_scr=bprev_scr)

    def run_full_trips(nbt, j0, trips, ss, ps, als):
        def body(g, carry):
            j = j0 + nbt * g
            for a in range(HB):
                kq = k_ref[a, pl.ds(pl.multiple_of((j + 1) * BS, BS), (nbt - 1) * BS), :]
                sm = _dot(kq, qt_ref[a, 0])
                for t in range(nbt - 1):
                    ss[t + 1][a] = sm[t * BS:(t + 1) * BS, :]
            for t in range(nbt):
                accumulate(j + t - 1, ps[(t - 1) % nbt], als[(t - 1) % nbt], maybe_prev=(t == 0))
                softmax(j + t, ss[t], ps[t], als[t], False)
                if t == 0:
                    scores(j + nbt, ss[0])
            return carry

        lax.fori_loop(0, trips, body, 0)

    n_octs = n_far // 8
    run_full_trips(8, 0, n_octs,
                   (s0_scr, s1_scr, s2_scr, s3_scr, s4_scr, s5_scr, s6_scr, s7_scr),
                   (p4_scr, p5_scr, p6_scr, p7_scr, p0_scr, p1_scr, p2_scr, p3_scr),
                   (al4_scr, al5_scr, al6_scr, al7_scr, al0_scr, al1_scr, al2_scr, al3_scr))
    n_quads = (n_far - 8 * n_octs) // 4
    run_full_trips(4, 8 * n_octs, n_quads, s_slots, p_slots, al_slots)
    done = 8 * n_octs + 4 * n_quads

    def pair_loop(g, carry):
        j = done + 2 * g
        scores(j + 1, s1_scr)
        accumulate(j - 1, p3_scr, al3_scr, maybe_prev=True)
        softmax(j, s0_scr, p2_scr, al2_scr, False)
        scores(j + 2, s0_scr)
        accumulate(j, p2_scr, al2_scr)
        softmax(j + 1, s1_scr, p3_scr, al3_scr, True)
        return carry

    n_pairs = (n_far - done + 1) // 2
    lax.fori_loop(0, n_pairs, pair_loop, 0)
    accumulate(done + 2 * n_pairs - 1, p3_scr, al3_scr, maybe_prev=True)
    for a in range(HB):
        acc = acc_scr[a]
        o_ref[:, a * D:(a + 1) * D] = (acc[0:D, :] / acc[D:D + 1, :]).T.astype(o_ref.dtype)


def _moba(qt, kn, vt, pen, table_h):
    H, nb, D, BS = qt.shape
    S = nb * BS
    nbp = pen.shape[2]
    HB = MOBA_HEADS_PER_STEP
    return pl.pallas_call(
        _moba_kernel,
        grid=(H // HB, nb),
        in_specs=[
            pl.BlockSpec(memory_space=pltpu.SMEM),
            pl.BlockSpec((HB, 1, D, BS), lambda hp, i: (hp, i, 0, 0)),
            pl.BlockSpec((HB, S, D), lambda hp, i: (hp, 0, 0)),
            pl.BlockSpec((HB, nb, MOBA_VT_ROWS, BS), lambda hp, i: (hp, 0, 0, 0)),
            pl.BlockSpec((HB, 1, nbp, BS), lambda hp, i: (hp, i, 0, 0)),
        ],
        out_specs=pl.BlockSpec((BS, HB * D), lambda hp, i: (i, hp)),
        out_shape=jax.ShapeDtypeStruct((S, H * D), BF16),
        scratch_shapes=[
            pltpu.VMEM((HB, BS, BS), F32),
            pltpu.VMEM((HB, BS, BS), F32),
            pltpu.VMEM((HB, 1, BS), F32),
            pltpu.VMEM((HB, MOBA_VT_ROWS, BS), F32),
        ] + [pltpu.VMEM((HB, BS, BS), F32)] * 8 + [pltpu.VMEM((HB, BS, BS), BF16)] * 8
        + [pltpu.VMEM((HB, 1, BS), F32)] * 8,
        compiler_params=_cparams(("arbitrary", "arbitrary")),
        name="moba",
    )(table_h, qt, kn, vt, pen)


def _mix_kernel(ys_ref, ya_ref, ws_ref, wa_ref, gs_ref, ga_ref, bs_ref, ba_ref, o_ref):
    gs = _sigmoid(gs_ref[...].astype(F32) + bs_ref[...])
    ga = _sigmoid(ga_ref[...].astype(F32) + ba_ref[...])
    o_ref[...] = (gs * _dot(ys_ref[...], ws_ref[...])
                  + ga * _dot(ya_ref[...], wa_ref[...])).astype(o_ref.dtype)


def _mix(y_ssd, y_attn, w_ssm_out, w_attn_out, proj, b_gate, tm=1024, tn=512):
    S = y_ssd.shape[0]
    tm = min(tm, S)
    gs_col = COL_G // tn
    ga_col = (COL_G + D_MODEL) // tn
    nb = D_MODEL // tn
    return pl.pallas_call(
        _mix_kernel,
        grid=(S // tm, D_MODEL // tn),
        in_specs=[
            pl.BlockSpec((tm, D_INNER), lambda m, n: (m, 0)),
            pl.BlockSpec((tm, D_ATTN), lambda m, n: (m, 0)),
            pl.BlockSpec((D_INNER, tn), lambda m, n: (0, n)),
            pl.BlockSpec((D_ATTN, tn), lambda m, n: (0, n)),
            pl.BlockSpec((tm, tn), lambda m, n: (m, gs_col + n)),
            pl.BlockSpec((tm, tn), lambda m, n: (m, ga_col + n)),
            pl.BlockSpec((1, tn), lambda m, n: (0, n)),
            pl.BlockSpec((1, tn), lambda m, n: (0, nb + n)),
        ],
        out_specs=pl.BlockSpec((tm, tn), lambda m, n: (m, n)),
        out_shape=jax.ShapeDtypeStruct((S, D_MODEL), BF16),
        compiler_params=_cparams(("arbitrary", "arbitrary")),
        name="mix",
    )(y_ssd, y_attn, w_ssm_out, w_attn_out, proj, proj, b_gate, b_gate)


def _out_norm_kernel(x_ref, mix_ref, w_ref, nw_ref, x1_ref, h2_ref):
    x1 = x_ref[...] + _dot(mix_ref[...], w_ref[...])
    x1_ref[...] = x1
    ms = jnp.mean(x1 * x1, axis=-1, keepdims=True)
    h2_ref[...] = (x1 * lax.rsqrt(ms + EPS) * nw_ref[...]).astype(h2_ref.dtype)


def _out_norm(x, mix, w_out, ffn_norm_w, tm=512):
    S = x.shape[0]
    tm = min(tm, S)
    return pl.pallas_call(
        _out_norm_kernel,
        grid=(S // tm,),
        in_specs=[
            pl.BlockSpec((tm, D_MODEL), lambda m: (m, 0)),
            pl.BlockSpec((tm, D_MODEL), lambda m: (m, 0)),
            pl.BlockSpec((D_MODEL, D_MODEL), lambda m: (0, 0)),
            pl.BlockSpec((1, D_MODEL), lambda m: (0, 0)),
        ],
        out_specs=[
            pl.BlockSpec((tm, D_MODEL), lambda m: (m, 0)),
            pl.BlockSpec((tm, D_MODEL), lambda m: (m, 0)),
        ],
        out_shape=[
            jax.ShapeDtypeStruct((S, D_MODEL), F32),
            jax.ShapeDtypeStruct((S, D_MODEL), BF16),
        ],
        compiler_params=_cparams(("arbitrary",)),
        name="out_norm",
    )(x, mix, w_out, ffn_norm_w)


def _ffn_up_kernel(h_ref, wg32_ref, wu32_ref, cwg_ref, cwu_ref, cbg_ref, cbu_ref, o_ref,
                   tailg, tailu, padg, padu, wg_ref, wu_ref):
    tm = h_ref.shape[0]

    @pl.when(pl.program_id(1) == 0)
    def _():
        tailg[...] = jnp.zeros(tailg.shape, F32)
        tailu[...] = jnp.zeros(tailu.shape, F32)
        wg_ref[...] = wg32_ref[...].astype(BF16)
        wu_ref[...] = wu32_ref[...].astype(BF16)

    def conv(w_ref, cw_ref, cb_ref, tail, pad):
        pad[0:SUBLANES, :] = tail[...]
        pad[SUBLANES:SUBLANES + tm, :] = _dot(h_ref[...], w_ref[...])
        tail[...] = pad[tm:tm + SUBLANES, :]
        acc = cb_ref[...]
        for k in range(FFN_CONV):
            off = SUBLANES - (FFN_CONV - 1) + k
            acc = acc + cw_ref[k:k + 1, :] * pad[off:off + tm, :]
        return acc

    ug = conv(wg_ref, cwg_ref, cbg_ref, tailg, padg)
    uu = conv(wu_ref, cwu_ref, cbu_ref, tailu, padu)
    o_ref[...] = (_silu(ug) * uu).astype(o_ref.dtype)


def _ffn_up(h2, w_up, conv_w, conv_b, tm=1024, tn=512):
    S = h2.shape[0]
    tm = min(tm, S)
    nb = D_FF // tn
    return pl.pallas_call(
        _ffn_up_kernel,
        grid=(nb, S // tm),
        in_specs=[
            pl.BlockSpec((tm, D_MODEL), lambda n, m: (m, 0)),
            pl.BlockSpec((D_MODEL, tn), lambda n, m: (0, n)),
            pl.BlockSpec((D_MODEL, tn), lambda n, m: (0, nb + n)),
            pl.BlockSpec((FFN_CONV, tn), lambda n, m: (0, n)),
            pl.BlockSpec((FFN_CONV, tn), lambda n, m: (0, nb + n)),
            pl.BlockSpec((1, tn), lambda n, m: (0, n)),
            pl.BlockSpec((1, tn), lambda n, m: (0, nb + n)),
        ],
        out_specs=pl.BlockSpec((tm, tn), lambda n, m: (m, n)),
        out_shape=jax.ShapeDtypeStruct((S, D_FF), BF16),
        scratch_shapes=[
            pltpu.VMEM((SUBLANES, tn), F32),
            pltpu.VMEM((SUBLANES, tn), F32),
            pltpu.VMEM((tm + SUBLANES, tn), F32),
            pltpu.VMEM((tm + SUBLANES, tn), F32),
            pltpu.VMEM((D_MODEL, tn), BF16),
            pltpu.VMEM((D_MODEL, tn), BF16),
        ],
        compiler_params=_cparams(("arbitrary", "arbitrary")),
        name="ffn_up",
    )(h2, w_up, w_up, conv_w, conv_w, conv_b, conv_b)


def _ffn_down_kernel(a_ref, w_ref, x1_ref, o_ref):
    o_ref[...] = x1_ref[...] + _dot(a_ref[...], w_ref[...])


def _ffn_down(act, w_down, x1, tm=1024, tn=512):
    S = act.shape[0]
    tm = min(tm, S)
    return pl.pallas_call(
        _ffn_down_kernel,
        grid=(S // tm, D_MODEL // tn),
        in_specs=[
            pl.BlockSpec((tm, D_FF), lambda m, n: (m, 0)),
            pl.BlockSpec((D_FF, tn), lambda m, n: (0, n)),
            pl.BlockSpec((tm, tn), lambda m, n: (m, n)),
        ],
        out_specs=pl.BlockSpec((tm, tn), lambda m, n: (m, n)),
        out_shape=jax.ShapeDtypeStruct((S, D_MODEL), F32),
        compiler_params=_cparams(("arbitrary", "arbitrary")),
        name="ffn_down",
    )(act, w_down, x1)


def _pad_lanes(v):
    return jnp.pad(v, ((0, 0), (0, LANES - v.shape[1])))


def _layer(x, attn_norm_w, w_in, b_gate, ssm_conv_w, ssm_conv_b, ssm_dt_bias, ssm_a_log, ssm_d,
           ssm_norm_w, q_norm_w, k_norm_w, rel_bias, w_ssm_out, w_attn_out, w_out, ffn_norm_w,
           w_up, ffn_conv_w, ffn_conv_b, w_down):
    o_dt = D_INNER + D_XBC
    o_q = o_dt + SSM_HEADS
    wt = w_in.T
    wt_dt = jnp.pad(wt[o_dt:o_q], ((0, LANES - SSM_HEADS), (0, 0)))

    h, dt_raw = _norm_dt(x, attn_norm_w[None, :], wt_dt)
    proj = _in_proj(h, wt, o_dt)
    y_ssd = _ssd(proj, dt_raw, ssm_conv_w, ssm_conv_b[None, :], _pad_lanes(ssm_dt_bias[None, :]),
                 _pad_lanes(ssm_a_log[None, :]), jnp.repeat(ssm_d, SSM_HEAD_DIM)[None, :],
                 ssm_norm_w[None, :])
    qt, kn, vt, pen = _qk_prep(proj, q_norm_w[None, :], k_norm_w[None, :])
    y_attn = _moba(qt, kn, vt, pen, rel_bias.T)
    mix = _mix(y_ssd, y_attn, w_ssm_out.astype(BF16), w_attn_out.astype(BF16), proj, b_gate[None, :])
    x1, h2 = _out_norm(x, mix, w_out.astype(BF16), ffn_norm_w[None, :])
    act = _ffn_up(h2, w_up, ffn_conv_w, ffn_conv_b[None, :])
    return _ffn_down(act, w_down.astype(BF16), x1)


def kernel(x, attn_norm_w, w_in, b_gate, ssm_conv_w, ssm_conv_b, ssm_dt_bias, ssm_a_log, ssm_d,
           ssm_norm_w, q_norm_w, k_norm_w, rel_bias, w_ssm_out, w_attn_out, w_out, ffn_norm_w,
           w_up, ffn_conv_w, ffn_conv_b, w_down):
    assert x.shape[0] == 1 and attn_norm_w.shape[0] == 1
    out = _layer(x[0], attn_norm_w[0], w_in[0], b_gate[0], ssm_conv_w[0], ssm_conv_b[0],
                 ssm_dt_bias[0], ssm_a_log[0], ssm_d[0], ssm_norm_w[0], q_norm_w[0], k_norm_w[0],
                 rel_bias, w_ssm_out[0], w_attn_out[0], w_out[0], ffn_norm_w[0], w_up[0],
                 ffn_conv_w[0], ffn_conv_b[0], w_down[0])
    return out[None]
```

```python
import math

import jax
import jax.numpy as jnp
from jax import lax
from jax.experimental import pallas as pl
from jax.experimental.pallas import tpu as pltpu

F32 = jnp.float32
BF16 = jnp.bfloat16

D_MODEL = 2048
D_INNER = 4096
SSM_HEADS = 64
SSM_HEAD_DIM = 64
SSM_GROUPS = 8
SSM_HEADS_PER_GROUP = SSM_HEADS // SSM_GROUPS
SSM_STATE = 128
SSM_CONV = 4
SSM_CHUNK = 256
GROUP_CH = D_INNER // SSM_GROUPS
SSD_GROUPS_PER_STEP = 8
D_XBC = D_INNER + 2 * SSM_GROUPS * SSM_STATE
ATTN_HEADS = 16
ATTN_HEAD_DIM = 128
D_ATTN = ATTN_HEADS * ATTN_HEAD_DIM
MOBA_BLOCK = 256
MOBA_TOPK = 3
REL_BUCKETS = 32
REL_MAX_DIST = 128
D_FF = 5632
FFN_CONV = 3
EPS = 1e-6
LOG2E = 1.4426950408889634

LANES = 128
SUBLANES = 8
D_PROJ = D_XBC + 3 * D_ATTN + 2 * D_INNER
COL_Z = 0
COL_XBC = COL_Z + D_INNER
COL_Q = COL_XBC + D_XBC
COL_K = COL_Q + D_ATTN
COL_V = COL_K + D_ATTN
COL_G = COL_V + D_ATTN
VMEM_LIMIT = 56 * 1024 * 1024


def _cparams(sem):
    return pltpu.CompilerParams(dimension_semantics=sem, vmem_limit_bytes=VMEM_LIMIT)


def _split3(a):
    hi = a.astype(BF16)
    r = a - hi.astype(F32)
    mid = r.astype(BF16)
    lo = (r - mid.astype(F32)).astype(BF16)
    return hi, mid, lo


def _dot(a, b):
    return jnp.dot(a, b, preferred_element_type=F32)


def _dot_nt(a, b):
    return lax.dot_general(a, b, (((1,), (1,)), ((), ())), preferred_element_type=F32)


def _sigmoid(x):
    return 0.5 + 0.5 * jnp.tanh(0.5 * x)


def _silu(x):
    h = 0.5 * x
    return h + h * jnp.tanh(h)


def _norm_dt_kernel(x_ref, nw_ref, wdt_ref, h_ref, dt_ref):
    x = x_ref[...]
    ms = jnp.mean(x * x, axis=-1, keepdims=True)
    h = x * lax.rsqrt(ms + EPS) * nw_ref[...]
    h_ref[...] = h.astype(BF16)
    h_hi, h_mid, _ = _split3(h)
    w_hi, w_mid, _ = _split3(wdt_ref[...])
    dt_ref[...] = _dot_nt(h_hi, w_hi) + _dot_nt(h_hi, w_mid) + _dot_nt(h_mid, w_hi)


def _norm_dt(x, norm_w, wt_dt, tm=512):
    S = x.shape[0]
    tm = min(tm, S)
    return pl.pallas_call(
        _norm_dt_kernel,
        grid=(S // tm,),
        in_specs=[
            pl.BlockSpec((tm, D_MODEL), lambda m: (m, 0)),
            pl.BlockSpec((1, D_MODEL), lambda m: (0, 0)),
            pl.BlockSpec((LANES, D_MODEL), lambda m: (0, 0)),
        ],
        out_specs=[
            pl.BlockSpec((tm, D_MODEL), lambda m: (m, 0)),
            pl.BlockSpec((tm, LANES), lambda m: (m, 0)),
        ],
        out_shape=[
            jax.ShapeDtypeStruct((S, D_MODEL), BF16),
            jax.ShapeDtypeStruct((S, LANES), F32),
        ],
        compiler_params=_cparams(("arbitrary",)),
        name="norm_dt",
    )(x, norm_w, wt_dt)


def _in_proj_kernel(h_ref, w32_ref, o_ref, w_scr):
    @pl.when(pl.program_id(1) == 0)
    def _():
        w_scr[...] = w32_ref[...].astype(BF16)

    o_ref[...] = _dot_nt(h_ref[...], w_scr[...]).astype(o_ref.dtype)


def _in_proj(h, wt, dt_row0, tm=2048, tn=1024):
    S = h.shape[0]
    tm = min(tm, S)
    assert dt_row0 % tn == 0 and SSM_HEADS % SUBLANES == 0
    na = dt_row0 // tn

    def w_rows(n, m):
        return (pl.multiple_of(n * tn + jnp.where(n >= na, SSM_HEADS, 0), SSM_HEADS), 0)

    return pl.pallas_call(
        _in_proj_kernel,
        grid=(D_PROJ // tn, S // tm),
        in_specs=[
            pl.BlockSpec((tm, D_MODEL), lambda n, m: (m, 0)),
            pl.BlockSpec((pl.Element(tn), pl.Element(D_MODEL)), w_rows),
        ],
        out_specs=pl.BlockSpec((tm, tn), lambda n, m: (m, n)),
        out_shape=jax.ShapeDtypeStruct((S, D_PROJ), BF16),
        scratch_shapes=[pltpu.VMEM((tn, D_MODEL), BF16)],
        compiler_params=_cparams(("arbitrary", "arbitrary")),
        name="in_proj",
    )(h, wt)


def _conv_silu(raw_ref, tail_ref, pad_ref, w_ref, b_ref, g, taps):
    L = raw_ref.shape[0]
    pad_ref[0:SUBLANES, :] = tail_ref[g]
    pad_ref[SUBLANES:SUBLANES + L, :] = raw_ref[...].astype(F32)
    tail_ref[g] = pad_ref[L:L + SUBLANES, :]
    acc = b_ref[...]
    for k in range(taps):
        off = SUBLANES - (taps - 1) + k
        acc = acc + w_ref[k:k + 1, :] * pad_ref[off:off + L, :]
    return _silu(acc)


def _ssd_kernel(x_ref, b_ref, c_ref, z_ref, dtr_ref, cwx_ref, cwb_ref, cwc_ref,
                cbx_ref, cbb_ref, cbc_ref, dtb_ref, alog_ref, dsk_ref, nw_ref,
                o_ref,
                tailx, tailb, tailc, padx, padb, padc, acs_scr, acst_scr, lbt_scr, ht_scr, rhs_scr):
    c = pl.program_id(0)
    gp = pl.program_id(1)
    L = SSM_CHUNK
    QW = 4 * SSM_HEAD_DIM
    GPS = SSD_GROUPS_PER_STEP

    @pl.when(c == 0)
    def _():
        for gg in range(GPS):
            g0 = gp * GPS + gg
            tailx[g0] = jnp.zeros(tailx.shape[1:], F32)
            tailb[g0] = jnp.zeros(tailb.shape[1:], F32)
            tailc[g0] = jnp.zeros(tailc.shape[1:], F32)
            ht_scr[g0] = jnp.zeros(ht_scr.shape[1:], F32)
            rhs_scr[gg] = jnp.zeros(rhs_scr.shape[1:], BF16)

    @pl.when(gp == 0)
    def _():
        t = dtr_ref[...] + dtb_ref[...]
        dt = jnp.maximum(t, 0.0) + jnp.log(1.0 + jnp.exp(-jnp.abs(t)))
        a = -jnp.exp(alog_ref[...])
        da = dt * a
        ri = lax.broadcasted_iota(jnp.int32, (L, L), 0)
        ci = lax.broadcasted_iota(jnp.int32, (L, L), 1)
        tri = jnp.where(ri >= ci, 1.0, 0.0).astype(BF16)
        hi, mid, lo = _split3(da)
        a_cs = _dot(tri, hi) + _dot(tri, mid) + _dot(tri, lo)
        a2 = a_cs * LOG2E
        a2t = a2.T
        acs_scr[...] = a2
        acst_scr[...] = a2t
        lbt_scr[...] = a2t - jnp.log(dt.T) * LOG2E

    for gg in range(GPS):
        xs = slice(gg * GROUP_CH, (gg + 1) * GROUP_CH)
        ns = slice(gg * SSM_STATE, (gg + 1) * SSM_STATE)
        _ssd_group(gp * GPS + gg,
                   x_ref.at[:, xs], b_ref.at[:, ns], c_ref.at[:, ns], z_ref.at[:, xs],
                   cwx_ref.at[:, xs], cwb_ref.at[:, ns], cwc_ref.at[:, ns],
                   cbx_ref.at[:, xs], cbb_ref.at[:, ns], cbc_ref.at[:, ns],
                   dsk_ref.at[:, xs], nw_ref.at[:, xs], o_ref.at[:, xs],
                   tailx, tailb, tailc, padx.at[gg], padb.at[gg], padc.at[gg],
                   acs_scr, acst_scr, lbt_scr, ht_scr,
                   [rhs_scr.at[gg, q] for q in range(GROUP_CH // QW)])


def _ssd_group(g, x_ref, b_ref, c_ref, z_ref, cwx_ref, cwb_ref, cwc_ref, cbx_ref, cbb_ref, cbc_ref,
               dsk_ref, nw_ref, o_ref, tailx, tailb, tailc, padx, padb, padc,
               acs_scr, acst_scr, lbt_scr, ht_scr, rhs_refs):
    L = SSM_CHUNK
    QW = 4 * SSM_HEAD_DIM
    xa = _conv_silu(x_ref, tailx, padx, cwx_ref, cbx_ref, g, SSM_CONV)
    ba = _conv_silu(b_ref, tailb, padb, cwb_ref, cbb_ref, g, SSM_CONV)
    ca = _conv_silu(c_ref, tailc, padc, cwc_ref, cbc_ref, g, SSM_CONV)

    xb = xa.astype(BF16)
    bt = ba.T
    cb = _dot(ca.astype(BF16), bt.astype(BF16))
    a_cs = acs_scr[...]
    ri = lax.broadcasted_iota(jnp.int32, (L, L), 0)
    ci = lax.broadcasted_iota(jnp.int32, (L, L), 1)
    causal = ri >= ci
    lane_h = lax.broadcasted_iota(jnp.int32, (L, LANES), 1)
    lane_q = lax.broadcasted_iota(jnp.int32, (1, QW), 1)

    NH = QW // SSM_HEAD_DIM
    ys = []
    for q in range(GROUP_CH // QW):
        rhs = rhs_refs[q]
        xq = xb[:, q * QW:(q + 1) * QW]
        hq = ht_scr[g, :, q * QW:(q + 1) * QW]
        hqb = hq.astype(BF16)
        mhs, ces, bws = [], [], []
        cdq = jnp.zeros((1, QW), F32)
        for j in range(NH):
            h = g * SSM_HEADS_PER_GROUP + q * NH + j
            cols = slice(j * SSM_HEAD_DIM, (j + 1) * SSM_HEAD_DIM)
            rhs[j * L:(j + 1) * L, cols] = xq[:, cols]
            rhs[NH * L + j * SSM_STATE:NH * L + (j + 1) * SSM_STATE, cols] = hqb[:, cols]
            a_row = acst_scr[pl.ds(h, 1), :]
            b_row = lbt_scr[pl.ds(h, 1), :]
            a_col = jnp.sum(jnp.where(lane_h == h, a_cs, 0.0), axis=1, keepdims=True)
            dmat = jnp.where(causal, a_col - b_row, -jnp.inf)
            mhs.append((cb * jnp.exp2(dmat)).astype(BF16))
            ces.append((ca * jnp.exp2(a_col)).astype(BF16))
            a_last = a_row[:, L - 1:L]
            w_row = jnp.exp2(a_last - b_row)
            bws.append((bt * w_row).astype(BF16))
            lm = (lane_q >= j * SSM_HEAD_DIM) & (lane_q < (j + 1) * SSM_HEAD_DIM)
            cdq = jnp.where(lm, jnp.exp2(a_last), cdq)
        acc = _dot(jnp.concatenate(mhs + ces, axis=1), rhs[...])
        st = _dot(jnp.concatenate(bws, axis=1), rhs[0:NH * L, :])
        ht_scr[g, :, q * QW:(q + 1) * QW] = hq * cdq + st
        ys.append(acc + xa[:, q * QW:(q + 1) * QW] * dsk_ref[:, q * QW:(q + 1) * QW])

    z = z_ref[...].astype(F32)
    sz = _silu(z)
    ygs = [ys[q] * sz[:, q * QW:(q + 1) * QW] for q in range(len(ys))]
    ss = sum(jnp.sum(v * v, axis=1, keepdims=True) for v in ygs)
    scale = lax.rsqrt(ss * (1.0 / GROUP_CH) + EPS)
    for q in range(len(ys)):
        o_ref[:, q * QW:(q + 1) * QW] = (
            ygs[q] * scale * nw_ref[:, q * QW:(q + 1) * QW]).astype(o_ref.dtype)


def _ssd(proj, dt_raw, conv_w, conv_b, dt_bias, a_log, d_exp, norm_w):
    S = proj.shape[0]
    L, G, N = SSM_CHUNK, SSM_GROUPS, SSM_STATE
    GPS = SSD_GROUPS_PER_STEP
    XW, NW = GPS * GROUP_CH, GPS * N
    bcol = D_INNER // NW
    ccol = bcol + G // GPS
    pxcol = COL_XBC // XW
    pbcol = COL_XBC // NW + bcol
    pccol = pbcol + G // GPS
    zcol = COL_Z // XW
    return pl.pallas_call(
        _ssd_kernel,
        grid=(S // L, G // GPS),
        in_specs=[
            pl.BlockSpec((L, XW), lambda c, g: (c, pxcol + g)),
            pl.BlockSpec((L, NW), lambda c, g: (c, pbcol + g)),
            pl.BlockSpec((L, NW), lambda c, g: (c, pccol + g)),
            pl.BlockSpec((L, XW), lambda c, g: (c, zcol + g)),
            pl.BlockSpec((L, LANES), lambda c, g: (c, 0)),
            pl.BlockSpec((SSM_CONV, XW), lambda c, g: (0, g)),
            pl.BlockSpec((SSM_CONV, NW), lambda c, g: (0, bcol + g)),
            pl.BlockSpec((SSM_CONV, NW), lambda c, g: (0, ccol + g)),
            pl.BlockSpec((1, XW), lambda c, g: (0, g)),
            pl.BlockSpec((1, NW), lambda c, g: (0, bcol + g)),
            pl.BlockSpec((1, NW), lambda c, g: (0, ccol + g)),
            pl.BlockSpec((1, LANES), lambda c, g: (0, 0)),
            pl.BlockSpec((1, LANES), lambda c, g: (0, 0)),
            pl.BlockSpec((1, XW), lambda c, g: (0, g)),
            pl.BlockSpec((1, XW), lambda c, g: (0, g)),
        ],
        out_specs=pl.BlockSpec((L, XW), lambda c, g: (c, g)),
        out_shape=jax.ShapeDtypeStruct((S, D_INNER), BF16),
        scratch_shapes=[
            pltpu.VMEM((G, SUBLANES, GROUP_CH), F32),
            pltpu.VMEM((G, SUBLANES, N), F32),
            pltpu.VMEM((G, SUBLANES, N), F32),
            pltpu.VMEM((GPS, L + SUBLANES, GROUP_CH), F32),
            pltpu.VMEM((GPS, L + SUBLANES, N), F32),
            pltpu.VMEM((GPS, L + SUBLANES, N), F32),
            pltpu.VMEM((L, LANES), F32),
            pltpu.VMEM((LANES, L), F32),
            pltpu.VMEM((LANES, L), F32),
            pltpu.VMEM((G, N, GROUP_CH), F32),
            pltpu.VMEM((GPS, GROUP_CH // (4 * SSM_HEAD_DIM), 4 * (L + N), 4 * SSM_HEAD_DIM), BF16),
        ],
        compiler_params=_cparams(("arbitrary", "arbitrary")),
        name="ssd",
    )(proj, proj, proj, proj, dt_raw, conv_w, conv_w, conv_w, conv_b, conv_b, conv_b,
      dt_bias, a_log, d_exp, norm_w)


def _qk_prep_kernel(q_ref, k_ref, v_ref, qw_ref, kw_ref, qt_ref, kn_ref, vt_ref, pen_ref, kbar_scr):
    S = q_ref.shape[0]
    BS = MOBA_BLOCK
    nb = S // BS
    nbp = kbar_scr.shape[0]
    kbar_scr[...] = jnp.zeros(kbar_scr.shape, F32)
    eye = (lax.broadcasted_iota(jnp.int32, (ATTN_HEAD_DIM, ATTN_HEAD_DIM), 0)
           == lax.broadcasted_iota(jnp.int32, (ATTN_HEAD_DIM, ATTN_HEAD_DIM), 1)).astype(BF16)

    def kloop(b, carry):
        rows = pl.ds(pl.multiple_of(b * BS, BS), BS)
        k = k_ref[rows, :].astype(F32)
        kn = k * lax.rsqrt(jnp.mean(k * k, axis=-1, keepdims=True) + EPS) * kw_ref[...]
        kn_ref[0, rows, :] = kn.astype(BF16)
        kbar_scr[pl.ds(b, 1), :] = jnp.mean(kn, axis=0, keepdims=True)
        vt_ref[0, b, 0:ATTN_HEAD_DIM, :] = _dot_nt(eye, v_ref[rows, :]).astype(BF16)
        vt_ref[0, b, ATTN_HEAD_DIM:, :] = jnp.ones((MOBA_VT_ROWS - ATTN_HEAD_DIM, BS), BF16)
        return carry

    lax.fori_loop(0, nb, kloop, 0, unroll=4)
    kb_hi, kb_mid, _ = _split3(kbar_scr[...])

    def qloop(i, carry):
        rows = pl.ds(pl.multiple_of(i * BS, BS), BS)
        q = q_ref[rows, :].astype(F32)
        qn = q * lax.rsqrt(jnp.mean(q * q, axis=-1, keepdims=True) + EPS) * qw_ref[...]
        qnt = (qn * (ATTN_HEAD_DIM ** -0.5)).T
        qt_ref[0, i] = (qnt * LOG2E).astype(BF16)
        q_hi, q_mid, _ = _split3(qnt)
        gate = _dot(kb_hi, q_hi) + _dot(kb_mid, q_hi) + _dot(kb_hi, q_mid)
        blk = lax.broadcasted_iota(jnp.int32, gate.shape, 0)
        blk_f = blk.astype(F32)
        gate = jnp.where(blk < i, gate, -jnp.inf)
        pen = jnp.full(gate.shape, -jnp.inf, F32)
        for _ in range(MOBA_TOPK):
            mx = jnp.max(gate, axis=0, keepdims=True)
            cand = jnp.where((gate == mx) & (mx > -jnp.inf), blk_f, float(nbp))
            idx = jnp.min(cand, axis=0, keepdims=True)
            hit = blk_f == idx
            pen = jnp.where(hit, 0.0, pen)
            gate = jnp.where(hit, -jnp.inf, gate)
        pen_ref[0, i] = pen
        return carry

    lax.fori_loop(0, nb, qloop, 0, unroll=8)


def _qk_prep(proj, q_norm_w, k_norm_w):
    S = proj.shape[0]
    H, D, BS = ATTN_HEADS, ATTN_HEAD_DIM, MOBA_BLOCK
    nb = S // BS
    nbp = -(-nb // SUBLANES) * SUBLANES
    return pl.pallas_call(
        _qk_prep_kernel,
        grid=(H,),
        in_specs=[
            pl.BlockSpec((S, D), lambda h: (0, COL_Q // D + h)),
            pl.BlockSpec((S, D), lambda h: (0, COL_K // D + h)),
            pl.BlockSpec((S, D), lambda h: (0, COL_V // D + h)),
            pl.BlockSpec((1, D), lambda h: (0, 0)),
            pl.BlockSpec((1, D), lambda h: (0, 0)),
        ],
        out_specs=[
            pl.BlockSpec((1, nb, D, BS), lambda h: (h, 0, 0, 0)),
            pl.BlockSpec((1, S, D), lambda h: (h, 0, 0)),
            pl.BlockSpec((1, nb, MOBA_VT_ROWS, BS), lambda h: (h, 0, 0, 0)),
            pl.BlockSpec((1, nb, nbp, BS), lambda h: (h, 0, 0, 0)),
        ],
        out_shape=[
            jax.ShapeDtypeStruct((H, nb, D, BS), BF16),
            jax.ShapeDtypeStruct((H, S, D), BF16),
            jax.ShapeDtypeStruct((H, nb, MOBA_VT_ROWS, BS), BF16),
            jax.ShapeDtypeStruct((H, nb, nbp, BS), F32),
        ],
        scratch_shapes=[pltpu.VMEM((nbp, D), F32)],
        compiler_params=_cparams(("arbitrary",)),
        name="qk_prep",
    )(proj, proj, proj, q_norm_w, k_norm_w)


def _rel_bucket(dist):
    n = jnp.maximum(dist, 0)
    max_exact = REL_BUCKETS // 2
    nf = jnp.maximum(n, 1).astype(F32)
    large = max_exact + (jnp.log(nf / max_exact) / math.log(REL_MAX_DIST / max_exact)
                         * (REL_BUCKETS - max_exact)).astype(jnp.int32)
    large = jnp.minimum(large, REL_BUCKETS - 1)
    return jnp.where(n < max_exact, n, large)


MOBA_HEADS_PER_STEP = 4
MOBA_VT_ROWS = ATTN_HEAD_DIM + 2 * SUBLANES


def _moba_kernel(tbl_ref, qt_ref, k_ref, vt_ref, pen_ref, o_ref,
                 bown_scr, bprev_scr, m_scr, acc_scr, s0_scr, s1_scr, s2_scr, s3_scr,
                 p0_scr, p1_scr, p2_scr, p3_scr, al0_scr, al1_scr, al2_scr, al3_scr):
    hp = pl.program_id(0)
    i = pl.program_id(1)
    BS, HB = MOBA_BLOCK, MOBA_HEADS_PER_STEP
    D = ATTN_HEAD_DIM

    @pl.when(i == 0)
    def _():
        kl = lax.broadcasted_iota(jnp.int32, (BS, BS), 0)
        ql = lax.broadcasted_iota(jnp.int32, (BS, BS), 1)
        d = ql - kl
        b_own = _rel_bucket(d)
        b_prev = _rel_bucket(d + BS)
        for a in range(HB):
            h = hp * HB + a
            far = tbl_ref[h, REL_BUCKETS - 1]

            def lookup(bucket):
                val = jnp.zeros((BS, BS), F32)
                for b in range(REL_BUCKETS):
                    val = jnp.where(bucket == b, tbl_ref[h, b], val)
                return (val - far) * LOG2E

            bown_scr[a] = jnp.where(d >= 0, lookup(b_own), -jnp.inf)
            bprev_scr[a] = lookup(b_prev)

    def krows(a, j):
        return k_ref[a, pl.ds(pl.multiple_of(j * BS, BS), BS), :]

    jp = jnp.maximum(i - 1, 0)
    no_prev = jnp.where(i == 0, -jnp.inf, 0.0)
    n_far = jnp.maximum(i - 1, 0)
    nb = k_ref.shape[1] // BS

    def clampj(j):
        return jnp.clip(j, 0, nb - 1)

    def scores(j, s_slot):
        for a in range(HB):
            s_slot[a] = _dot(krows(a, clampj(j)), qt_ref[a, 0])

    def softmax(j, s_slot, p_slot, al_slot, mask_tail, bias_scr=None):
        for a in range(HB):
            pr = pen_ref[a, 0, pl.ds(clampj(j), 1), :]
            if mask_tail is True:
                pr = pr + jnp.where(j >= n_far, -jnp.inf, 0.0)
            elif mask_tail is not False:
                pr = pr + mask_tail
            def tile():
                s = s_slot[a]
                return s if bias_scr is None else s + bias_scr[a]

            m_old = m_scr[a]
            m_new = jnp.maximum(m_old, jnp.max(tile(), axis=0, keepdims=True) + pr)
            al_slot[a] = jnp.exp2(m_old - m_new)
            m_scr[a] = m_new
            p_slot[a] = jnp.exp2(tile() - (m_new - pr)).astype(BF16)

    def accumulate(j, p_slot, al_slot, maybe_prev=False):
        jv = clampj(j)
        if maybe_prev:
            jv = jnp.where(j < 0, jp, jv)
        for a in range(HB):
            acc_scr[a] = al_slot[a] * acc_scr[a] + _dot(vt_ref[a, jv], p_slot[a])

    s_slots = (s0_scr, s1_scr, s2_scr, s3_scr)
    p_slots = (p0_scr, p1_scr, p2_scr, p3_scr)
    al_slots = (al0_scr, al1_scr, al2_scr, al3_scr)
    scores(i, s2_scr)
    scores(jp, s3_scr)
    scores(0, s0_scr)
    for a in range(HB):
        s = s2_scr[a] + bown_scr[a]
        m0 = jnp.max(s, axis=0, keepdims=True)
        m_scr[a] = m0
        p2_scr[a] = jnp.exp2(s - m0).astype(BF16)
    for a in range(HB):
        acc_scr[a] = _dot(vt_ref[a, i], p2_scr[a])
    softmax(jp, s3_scr, p3_scr, al3_scr, no_prev, bias_scr=bprev_scr)

    def quad_loop(g, carry):
        j = 4 * g
        for a in range(HB):
            kq = k_ref[a, pl.ds(pl.multiple_of((j + 1) * BS, BS), 3 * BS), :]
            s3 = _dot(kq, qt_ref[a, 0])
            for t in range(3):
                s_slots[t + 1][a] = s3[t * BS:(t + 1) * BS, :]
        for t in range(4):
            accumulate(j + t - 1, p_slots[(t - 1) % 4], al_slots[(t - 1) % 4], maybe_prev=(t == 0))
            softmax(j + t, s_slots[t], p_slots[t], al_slots[t], False)
            if t == 0:
                scores(j + 4, s0_scr)
        return carry

    n_quads = n_far // 4
    lax.fori_loop(0, n_quads, quad_loop, 0)

    def pair_loop(g, carry):
        j = 4 * n_quads + 2 * g
        scores(j + 1, s1_scr)
        accumulate(j - 1, p3_scr, al3_scr, maybe_prev=True)
        softmax(j, s0_scr, p2_scr, al2_scr, False)
        scores(j + 2, s0_scr)
        accumulate(j, p2_scr, al2_scr)
        softmax(j + 1, s1_scr, p3_scr, al3_scr, True)
        return carry

    n_pairs = (n_far - 4 * n_quads + 1) // 2
    lax.fori_loop(0, n_pairs, pair_loop, 0)
    accumulate(4 * n_quads + 2 * n_pairs - 1, p3_scr, al3_scr, maybe_prev=True)
    for a in range(HB):
        acc = acc_scr[a]
        o_ref[:, a * D:(a + 1) * D] = (acc[0:D, :] / acc[D:D + 1, :]).T.astype(o_ref.dtype)


def _moba(qt, kn, vt, pen, table_h):
    H, nb, D, BS = qt.shape
    S = nb * BS
    nbp = pen.shape[2]
    HB = MOBA_HEADS_PER_STEP
    return pl.pallas_call(
        _moba_kernel,
        grid=(H // HB, nb),
        in_specs=[
            pl.BlockSpec(memory_space=pltpu.SMEM),
            pl.BlockSpec((HB, 1, D, BS), lambda hp, i: (hp, i, 0, 0)),
            pl.BlockSpec((HB, S, D), lambda hp, i: (hp, 0, 0)),
            pl.BlockSpec((HB, nb, MOBA_VT_ROWS, BS), lambda hp, i: (hp, 0, 0, 0)),
            pl.BlockSpec((HB, 1, nbp, BS), lambda hp, i: (hp, i, 0, 0)),
        ],
        out_specs=pl.BlockSpec((BS, HB * D), lambda hp, i: (i, hp)),
        out_shape=jax.ShapeDtypeStruct((S, H * D), BF16),
        scratch_shapes=[
            pltpu.VMEM((HB, BS, BS), F32),
            pltpu.VMEM((HB, BS, BS), F32),
            pltpu.VMEM((HB, 1, BS), F32),
            pltpu.VMEM((HB, MOBA_VT_ROWS, BS), F32),
        ] + [pltpu.VMEM((HB, BS, BS), F32)] * 4 + [pltpu.VMEM((HB, BS, BS), BF16)] * 4
        + [pltpu.VMEM((HB, 1, BS), F32)] * 4,
        compiler_params=_cparams(("arbitrary", "arbitrary")),
        name="moba",
    )(table_h, qt, kn, vt, pen)


def _mix_kernel(ys_ref, ya_ref, ws_ref, wa_ref, gs_ref, ga_ref, bs_ref, ba_ref, o_ref):
    gs = _sigmoid(gs_ref[...].astype(F32) + bs_ref[...])
    ga = _sigmoid(ga_ref[...].astype(F32) + ba_ref[...])
    o_ref[...] = (gs * _dot(ys_ref[...], ws_ref[...])
                  + ga * _dot(ya_ref[...], wa_ref[...])).astype(o_ref.dtype)


def _mix(y_ssd, y_attn, w_ssm_out, w_attn_out, proj, b_gate, tm=1024, tn=512):
    S = y_ssd.shape[0]
    tm = min(tm, S)
    gs_col = COL_G // tn
    ga_col = (COL_G + D_MODEL) // tn
    nb = D_MODEL // tn
    return pl.pallas_call(
        _mix_kernel,
        grid=(S // tm, D_MODEL // tn),
        in_specs=[
            pl.BlockSpec((tm, D_INNER), lambda m, n: (m, 0)),
            pl.BlockSpec((tm, D_ATTN), lambda m, n: (m, 0)),
            pl.BlockSpec((D_INNER, tn), lambda m, n: (0, n)),
            pl.BlockSpec((D_ATTN, tn), lambda m, n: (0, n)),
            pl.BlockSpec((tm, tn), lambda m, n: (m, gs_col + n)),
            pl.BlockSpec((tm, tn), lambda m, n: (m, ga_col + n)),
            pl.BlockSpec((1, tn), lambda m, n: (0, n)),
            pl.BlockSpec((1, tn), lambda m, n: (0, nb + n)),
        ],
        out_specs=pl.BlockSpec((tm, tn), lambda m, n: (m, n)),
        out_shape=jax.ShapeDtypeStruct((S, D_MODEL), BF16),
        compiler_params=_cparams(("arbitrary", "arbitrary")),
        name="mix",
    )(y_ssd, y_attn, w_ssm_out, w_attn_out, proj, proj, b_gate, b_gate)


def _out_norm_kernel(x_ref, mix_ref, w_ref, nw_ref, x1_ref, h2_ref):
    x1 = x_ref[...] + _dot(mix_ref[...], w_ref[...])
    x1_ref[...] = x1
    ms = jnp.mean(x1 * x1, axis=-1, keepdims=True)
    h2_ref[...] = (x1 * lax.rsqrt(ms + EPS) * nw_ref[...]).astype(h2_ref.dtype)


def _out_norm(x, mix, w_out, ffn_norm_w, tm=512):
    S = x.shape[0]
    tm = min(tm, S)
    return pl.pallas_call(
        _out_norm_kernel,
        grid=(S // tm,),
        in_specs=[
            pl.BlockSpec((tm, D_MODEL), lambda m: (m, 0)),
            pl.BlockSpec((tm, D_MODEL), lambda m: (m, 0)),
            pl.BlockSpec((D_MODEL, D_MODEL), lambda m: (0, 0)),
            pl.BlockSpec((1, D_MODEL), lambda m: (0, 0)),
        ],
        out_specs=[
            pl.BlockSpec((tm, D_MODEL), lambda m: (m, 0)),
            pl.BlockSpec((tm, D_MODEL), lambda m: (m, 0)),
        ],
        out_shape=[
            jax.ShapeDtypeStruct((S, D_MODEL), F32),
            jax.ShapeDtypeStruct((S, D_MODEL), BF16),
        ],
        compiler_params=_cparams(("arbitrary",)),
        name="out_norm",
    )(x, mix, w_out, ffn_norm_w)


def _ffn_up_kernel(h_ref, wg32_ref, wu32_ref, cwg_ref, cwu_ref, cbg_ref, cbu_ref, o_ref,
                   tailg, tailu, padg, padu, wg_ref, wu_ref):
    tm = h_ref.shape[0]

    @pl.when(pl.program_id(1) == 0)
    def _():
        tailg[...] = jnp.zeros(tailg.shape, F32)
        tailu[...] = jnp.zeros(tailu.shape, F32)
        wg_ref[...] = wg32_ref[...].astype(BF16)
        wu_ref[...] = wu32_ref[...].astype(BF16)

    def conv(w_ref, cw_ref, cb_ref, tail, pad):
        pad[0:SUBLANES, :] = tail[...]
        pad[SUBLANES:SUBLANES + tm, :] = _dot(h_ref[...], w_ref[...])
        tail[...] = pad[tm:tm + SUBLANES, :]
        acc = cb_ref[...]
        for k in range(FFN_CONV):
            off = SUBLANES - (FFN_CONV - 1) + k
            acc = acc + cw_ref[k:k + 1, :] * pad[off:off + tm, :]
        return acc

    ug = conv(wg_ref, cwg_ref, cbg_ref, tailg, padg)
    uu = conv(wu_ref, cwu_ref, cbu_ref, tailu, padu)
    o_ref[...] = (_silu(ug) * uu).astype(o_ref.dtype)


def _ffn_up(h2, w_up, conv_w, conv_b, tm=1024, tn=512):
    S = h2.shape[0]
    tm = min(tm, S)
    nb = D_FF // tn
    return pl.pallas_call(
        _ffn_up_kernel,
        grid=(nb, S // tm),
        in_specs=[
            pl.BlockSpec((tm, D_MODEL), lambda n, m: (m, 0)),
            pl.BlockSpec((D_MODEL, tn), lambda n, m: (0, n)),
            pl.BlockSpec((D_MODEL, tn), lambda n, m: (0, nb + n)),
            pl.BlockSpec((FFN_CONV, tn), lambda n, m: (0, n)),
            pl.BlockSpec((FFN_CONV, tn), lambda n, m: (0, nb + n)),
            pl.BlockSpec((1, tn), lambda n, m: (0, n)),
            pl.BlockSpec((1, tn), lambda n, m: (0, nb + n)),
        ],
        out_specs=pl.BlockSpec((tm, tn), lambda n, m: (m, n)),
        out_shape=jax.ShapeDtypeStruct((S, D_FF), BF16),
        scratch_shapes=[
            pltpu.VMEM((SUBLANES, tn), F32),
            pltpu.VMEM((SUBLANES, tn), F32),
            pltpu.VMEM((tm + SUBLANES, tn), F32),
            pltpu.VMEM((tm + SUBLANES, tn), F32),
            pltpu.VMEM((D_MODEL, tn), BF16),
            pltpu.VMEM((D_MODEL, tn), BF16),
        ],
        compiler_params=_cparams(("arbitrary", "arbitrary")),
        name="ffn_up",
    )(h2, w_up, w_up, conv_w, conv_w, conv_b, conv_b)


def _ffn_down_kernel(a_ref, w_ref, x1_ref, o_ref):
    o_ref[...] = x1_ref[...] + _dot(a_ref[...], w_ref[...])


def _ffn_down(act, w_down, x1, tm=1024, tn=512):
    S = act.shape[0]
    tm = min(tm, S)
    return pl.pallas_call(
        _ffn_down_kernel,
        grid=(S // tm, D_MODEL // tn),
        in_specs=[
            pl.BlockSpec((tm, D_FF), lambda m, n: (m, 0)),
            pl.BlockSpec((D_FF, tn), lambda m, n: (0, n)),
            pl.BlockSpec((tm, tn), lambda m, n: (m, n)),
        ],
        out_specs=pl.BlockSpec((tm, tn), lambda m, n: (m, n)),
        out_shape=jax.ShapeDtypeStruct((S, D_MODEL), F32),
        compiler_params=_cparams(("arbitrary", "arbitrary")),
        name="ffn_down",
    )(act, w_down, x1)


def _pad_lanes(v):
    return jnp.pad(v, ((0, 0), (0, LANES - v.shape[1])))


def _layer(x, attn_norm_w, w_in, b_gate, ssm_conv_w, ssm_conv_b, ssm_dt_bias, ssm_a_log, ssm_d,
           ssm_norm_w, q_norm_w, k_norm_w, rel_bias, w_ssm_out, w_attn_out, w_out, ffn_norm_w,
           w_up, ffn_conv_w, ffn_conv_b, w_down):
    o_dt = D_INNER + D_XBC
    o_q = o_dt + SSM_HEADS
    wt = w_in.T
    wt_dt = jnp.pad(wt[o_dt:o_q], ((0, LANES - SSM_HEADS), (0, 0)))

    h, dt_raw = _norm_dt(x, attn_norm_w[None, :], wt_dt)
    proj = _in_proj(h, wt, o_dt)
    y_ssd = _ssd(proj, dt_raw, ssm_conv_w, ssm_conv_b[None, :], _pad_lanes(ssm_dt_bias[None, :]),
                 _pad_lanes(ssm_a_log[None, :]), jnp.repeat(ssm_d, SSM_HEAD_DIM)[None, :],
                 ssm_norm_w[None, :])
    qt, kn, vt, pen = _qk_prep(proj, q_norm_w[None, :], k_norm_w[None, :])
    y_attn = _moba(qt, kn, vt, pen, rel_bias.T)
    mix = _mix(y_ssd, y_attn, w_ssm_out.astype(BF16), w_attn_out.astype(BF16), proj, b_gate[None, :])
    x1, h2 = _out_norm(x, mix, w_out.astype(BF16), ffn_norm_w[None, :])
    act = _ffn_up(h2, w_up, ffn_conv_w, ffn_conv_b[None, :])
    return _ffn_down(act, w_down.astype(BF16), x1)


def kernel(x, attn_norm_w, w_in, b_gate, ssm_conv_w, ssm_conv_b, ssm_dt_bias, ssm_a_log, ssm_d,
           ssm_norm_w, q_norm_w, k_norm_w, rel_bias, w_ssm_out, w_attn_out, w_out, ffn_norm_w,
           w_up, ffn_conv_w, ffn_conv_b, w_down):
    assert x.shape[0] == 1 and attn_norm_w.shape[0] == 1
    out = _layer(x[0], attn_norm_w[0], w_in[0], b_gate[0], ssm_conv_w[0], ssm_conv_b[0],
                 ssm_dt_bias[0], ssm_a_log[0], ssm_d[0], ssm_norm_w[0], q_norm_w[0], k_norm_w[0],
                 rel_bias, w_ssm_out[0], w_attn_out[0], w_out[0], ffn_norm_w[0], w_up[0],
                 ffn_conv_w[0], ffn_conv_b[0], w_down[0])
    return out[None]
```

```python
import math

import jax
import jax.numpy as jnp
from jax import lax
from jax.experimental import pallas as pl
from jax.experimental.pallas import tpu as pltpu

F32 = jnp.float32
BF16 = jnp.bfloat16

D_MODEL = 2048
D_INNER = 4096
SSM_HEADS = 64
SSM_HEAD_DIM = 64
SSM_GROUPS = 8
SSM_HEADS_PER_GROUP = SSM_HEADS // SSM_GROUPS
SSM_STATE = 128
SSM_CONV = 4
SSM_CHUNK = 256
GROUP_CH = D_INNER // SSM_GROUPS
SSD_GROUPS_PER_STEP = 8
D_XBC = D_INNER + 2 * SSM_GROUPS * SSM_STATE
ATTN_HEADS = 16
ATTN_HEAD_DIM = 128
D_ATTN = ATTN_HEADS * ATTN_HEAD_DIM
MOBA_BLOCK = 256
MOBA_TOPK = 3
REL_BUCKETS = 32
REL_MAX_DIST = 128
D_FF = 5632
FFN_CONV = 3
EPS = 1e-6
LOG2E = 1.4426950408889634

LANES = 128
SUBLANES = 8
D_PROJ = D_XBC + 3 * D_ATTN + 2 * D_INNER
COL_Z = 0
COL_XBC = COL_Z + D_INNER
COL_Q = COL_XBC + D_XBC
COL_K = COL_Q + D_ATTN
COL_V = COL_K + D_ATTN
COL_G = COL_V + D_ATTN
VMEM_LIMIT = 56 * 1024 * 1024


def _cparams(sem):
    return pltpu.CompilerParams(dimension_semantics=sem, vmem_limit_bytes=VMEM_LIMIT)


def _split3(a):
    hi = a.astype(BF16)
    r = a - hi.astype(F32)
    mid = r.astype(BF16)
    lo = (r - mid.astype(F32)).astype(BF16)
    return hi, mid, lo


def _dot(a, b):
    return jnp.dot(a, b, preferred_element_type=F32)


def _dot_nt(a, b):
    return lax.dot_general(a, b, (((1,), (1,)), ((), ())), preferred_element_type=F32)


def _sigmoid(x):
    return 0.5 + 0.5 * jnp.tanh(0.5 * x)


def _silu(x):
    h = 0.5 * x
    return h + h * jnp.tanh(h)


def _norm_dt_kernel(x_ref, nw_ref, wdt_ref, h_ref, dt_ref):
    x = x_ref[...]
    ms = jnp.mean(x * x, axis=-1, keepdims=True)
    h = x * lax.rsqrt(ms + EPS) * nw_ref[...]
    h_ref[...] = h.astype(BF16)
    h_hi, h_mid, _ = _split3(h)
    w_hi, w_mid, _ = _split3(wdt_ref[...])
    dt_ref[...] = _dot_nt(h_hi, w_hi) + _dot_nt(h_hi, w_mid) + _dot_nt(h_mid, w_hi)


def _norm_dt(x, norm_w, wt_dt, tm=512):
    S = x.shape[0]
    tm = min(tm, S)
    return pl.pallas_call(
        _norm_dt_kernel,
        grid=(S // tm,),
        in_specs=[
            pl.BlockSpec((tm, D_MODEL), lambda m: (m, 0)),
            pl.BlockSpec((1, D_MODEL), lambda m: (0, 0)),
            pl.BlockSpec((LANES, D_MODEL), lambda m: (0, 0)),
        ],
        out_specs=[
            pl.BlockSpec((tm, D_MODEL), lambda m: (m, 0)),
            pl.BlockSpec((tm, LANES), lambda m: (m, 0)),
        ],
        out_shape=[
            jax.ShapeDtypeStruct((S, D_MODEL), BF16),
            jax.ShapeDtypeStruct((S, LANES), F32),
        ],
        compiler_params=_cparams(("arbitrary",)),
        name="norm_dt",
    )(x, norm_w, wt_dt)


def _in_proj_kernel(h_ref, w32_ref, o_ref, w_scr):
    @pl.when(pl.program_id(1) == 0)
    def _():
        w_scr[...] = w32_ref[...].astype(BF16)

    o_ref[...] = _dot_nt(h_ref[...], w_scr[...]).astype(o_ref.dtype)


def _in_proj(h, wt, dt_row0, tm=2048, tn=1024):
    S = h.shape[0]
    tm = min(tm, S)
    assert dt_row0 % tn == 0 and SSM_HEADS % SUBLANES == 0
    na = dt_row0 // tn

    def w_rows(n, m):
        return (pl.multiple_of(n * tn + jnp.where(n >= na, SSM_HEADS, 0), SSM_HEADS), 0)

    return pl.pallas_call(
        _in_proj_kernel,
        grid=(D_PROJ // tn, S // tm),
        in_specs=[
            pl.BlockSpec((tm, D_MODEL), lambda n, m: (m, 0)),
            pl.BlockSpec((pl.Element(tn), pl.Element(D_MODEL)), w_rows),
        ],
        out_specs=pl.BlockSpec((tm, tn), lambda n, m: (m, n)),
        out_shape=jax.ShapeDtypeStruct((S, D_PROJ), BF16),
        scratch_shapes=[pltpu.VMEM((tn, D_MODEL), BF16)],
        compiler_params=_cparams(("arbitrary", "arbitrary")),
        name="in_proj",
    )(h, wt)


def _conv_silu(raw_ref, tail_ref, pad_ref, w_ref, b_ref, g, taps, cols=slice(None)):
    L = raw_ref.shape[0]
    pad_ref[0:SUBLANES, cols] = tail_ref[g, :, cols]
    pad_ref[SUBLANES:SUBLANES + L, cols] = raw_ref[:, cols].astype(F32)
    tail_ref[g, :, cols] = pad_ref[L:L + SUBLANES, cols]
    acc = b_ref[:, cols]
    for k in range(taps):
        off = SUBLANES - (taps - 1) + k
        acc = acc + w_ref[k:k + 1, cols] * pad_ref[off:off + L, cols]
    return _silu(acc)


def _ssd_kernel(x_ref, b_ref, c_ref, z_ref, dtr_ref, cwx_ref, cwb_ref, cwc_ref,
                cbx_ref, cbb_ref, cbc_ref, dtb_ref, alog_ref, dsk_ref, nw_ref,
                o_ref,
                tailx, tailb, tailc, padx, padb, padc, acs_scr, acst_scr, lbt_scr, ht_scr, rhs_scr):
    c = pl.program_id(0)
    gp = pl.program_id(1)
    L = SSM_CHUNK
    QW = 4 * SSM_HEAD_DIM
    GPS = SSD_GROUPS_PER_STEP

    @pl.when(c == 0)
    def _():
        for gg in range(GPS):
            g0 = gp * GPS + gg
            tailx[g0] = jnp.zeros(tailx.shape[1:], F32)
            tailb[g0] = jnp.zeros(tailb.shape[1:], F32)
            tailc[g0] = jnp.zeros(tailc.shape[1:], F32)
            ht_scr[g0] = jnp.zeros(ht_scr.shape[1:], F32)
            rhs_scr[gg] = jnp.zeros(rhs_scr.shape[1:], BF16)

    @pl.when(gp == 0)
    def _():
        t = dtr_ref[...] + dtb_ref[...]
        dt = jnp.maximum(t, 0.0) + jnp.log(1.0 + jnp.exp(-jnp.abs(t)))
        a = -jnp.exp(alog_ref[...])
        da = dt * a
        ri = lax.broadcasted_iota(jnp.int32, (L, L), 0)
        ci = lax.broadcasted_iota(jnp.int32, (L, L), 1)
        tri = jnp.where(ri >= ci, 1.0, 0.0).astype(BF16)
        hi, mid, lo = _split3(da)
        a_cs = _dot(tri, hi) + _dot(tri, mid) + _dot(tri, lo)
        a2 = a_cs * LOG2E
        a2t = a2.T
        acs_scr[...] = a2
        acst_scr[...] = a2t
        lbt_scr[...] = a2t - jnp.log(dt.T) * LOG2E

    for gg in range(GPS):
        xs = slice(gg * GROUP_CH, (gg + 1) * GROUP_CH)
        ns = slice(gg * SSM_STATE, (gg + 1) * SSM_STATE)
        _ssd_group(gp * GPS + gg,
                   x_ref.at[:, xs], b_ref.at[:, ns], c_ref.at[:, ns], z_ref.at[:, xs],
                   cwx_ref.at[:, xs], cwb_ref.at[:, ns], cwc_ref.at[:, ns],
                   cbx_ref.at[:, xs], cbb_ref.at[:, ns], cbc_ref.at[:, ns],
                   dsk_ref.at[:, xs], nw_ref.at[:, xs], o_ref.at[:, xs],
                   tailx, tailb, tailc, padx.at[gg], padb.at[gg], padc.at[gg],
                   acs_scr, acst_scr, lbt_scr, ht_scr,
                   [rhs_scr.at[gg, q] for q in range(GROUP_CH // QW)])


def _ssd_group(g, x_ref, b_ref, c_ref, z_ref, cwx_ref, cwb_ref, cwc_ref, cbx_ref, cbb_ref, cbc_ref,
               dsk_ref, nw_ref, o_ref, tailx, tailb, tailc, padx, padb, padc,
               acs_scr, acst_scr, lbt_scr, ht_scr, rhs_refs):
    L = SSM_CHUNK
    QW = 4 * SSM_HEAD_DIM
    ba = _conv_silu(b_ref, tailb, padb, cwb_ref, cbb_ref, g, SSM_CONV)
    ca = _conv_silu(c_ref, tailc, padc, cwc_ref, cbc_ref, g, SSM_CONV)

    bt = ba.T
    cb = _dot(ca.astype(BF16), bt.astype(BF16))
    a_cs = acs_scr[...]
    ri = lax.broadcasted_iota(jnp.int32, (L, L), 0)
    ci = lax.broadcasted_iota(jnp.int32, (L, L), 1)
    causal = ri >= ci
    lane_h = lax.broadcasted_iota(jnp.int32, (L, LANES), 1)
    lane_q = lax.broadcasted_iota(jnp.int32, (1, QW), 1)

    NH = QW // SSM_HEAD_DIM
    ygs = []
    for q in range(GROUP_CH // QW):
        rhs = rhs_refs[q]
        qc = slice(q * QW, (q + 1) * QW)
        xa = _conv_silu(x_ref, tailx, padx, cwx_ref, cbx_ref, g, SSM_CONV, cols=qc)
        xq = xa.astype(BF16)
        hq = ht_scr[g, :, q * QW:(q + 1) * QW]
        hqb = hq.astype(BF16)
        mhs, ces, bws = [], [], []
        cdq = jnp.zeros((1, QW), F32)
        for j in range(NH):
            h = g * SSM_HEADS_PER_GROUP + q * NH + j
            cols = slice(j * SSM_HEAD_DIM, (j + 1) * SSM_HEAD_DIM)
            rhs[j * L:(j + 1) * L, cols] = xq[:, cols]
            rhs[NH * L + j * SSM_STATE:NH * L + (j + 1) * SSM_STATE, cols] = hqb[:, cols]
            a_row = acst_scr[pl.ds(h, 1), :]
            b_row = lbt_scr[pl.ds(h, 1), :]
            a_col = jnp.sum(jnp.where(lane_h == h, a_cs, 0.0), axis=1, keepdims=True)
            dmat = jnp.where(causal, a_col - b_row, -jnp.inf)
            mhs.append((cb * jnp.exp2(dmat)).astype(BF16))
            ces.append((ca * jnp.exp2(a_col)).astype(BF16))
            a_last = a_row[:, L - 1:L]
            w_row = jnp.exp2(a_last - b_row)
            bws.append((bt * w_row).astype(BF16))
            lm = (lane_q >= j * SSM_HEAD_DIM) & (lane_q < (j + 1) * SSM_HEAD_DIM)
            cdq = jnp.where(lm, jnp.exp2(a_last), cdq)
        acc = _dot(jnp.concatenate(mhs + ces, axis=1), rhs[...])
        st = _dot(jnp.concatenate(bws, axis=1), rhs[0:NH * L, :])
        ht_scr[g, :, q * QW:(q + 1) * QW] = hq * cdq + st
        y = acc + xa * dsk_ref[:, qc]
        ygs.append(y * _silu(z_ref[:, qc].astype(F32)))

    ss = sum(jnp.sum(v * v, axis=1, keepdims=True) for v in ygs)
    scale = lax.rsqrt(ss * (1.0 / GROUP_CH) + EPS)
    for q in range(len(ygs)):
        o_ref[:, q * QW:(q + 1) * QW] = (
            ygs[q] * scale * nw_ref[:, q * QW:(q + 1) * QW]).astype(o_ref.dtype)


def _ssd(proj, dt_raw, conv_w, conv_b, dt_bias, a_log, d_exp, norm_w):
    S = proj.shape[0]
    L, G, N = SSM_CHUNK, SSM_GROUPS, SSM_STATE
    GPS = SSD_GROUPS_PER_STEP
    XW, NW = GPS * GROUP_CH, GPS * N
    bcol = D_INNER // NW
    ccol = bcol + G // GPS
    pxcol = COL_XBC // XW
    pbcol = COL_XBC // NW + bcol
    pccol = pbcol + G // GPS
    zcol = COL_Z // XW
    return pl.pallas_call(
        _ssd_kernel,
        grid=(S // L, G // GPS),
        in_specs=[
            pl.BlockSpec((L, XW), lambda c, g: (c, pxcol + g)),
            pl.BlockSpec((L, NW), lambda c, g: (c, pbcol + g)),
            pl.BlockSpec((L, NW), lambda c, g: (c, pccol + g)),
            pl.BlockSpec((L, XW), lambda c, g: (c, zcol + g)),
            pl.BlockSpec((L, LANES), lambda c, g: (c, 0)),
            pl.BlockSpec((SSM_CONV, XW), lambda c, g: (0, g)),
            pl.BlockSpec((SSM_CONV, NW), lambda c, g: (0, bcol + g)),
            pl.BlockSpec((SSM_CONV, NW), lambda c, g: (0, ccol + g)),
            pl.BlockSpec((1, XW), lambda c, g: (0, g)),
            pl.BlockSpec((1, NW), lambda c, g: (0, bcol + g)),
            pl.BlockSpec((1, NW), lambda c, g: (0, ccol + g)),
            pl.BlockSpec((1, LANES), lambda c, g: (0, 0)),
            pl.BlockSpec((1, LANES), lambda c, g: (0, 0)),
            pl.BlockSpec((1, XW), lambda c, g: (0, g)),
            pl.BlockSpec((1, XW), lambda c, g: (0, g)),
        ],
        out_specs=pl.BlockSpec((L, XW), lambda c, g: (c, g)),
        out_shape=jax.ShapeDtypeStruct((S, D_INNER), BF16),
        scratch_shapes=[
            pltpu.VMEM((G, SUBLANES, GROUP_CH), F32),
            pltpu.VMEM((G, SUBLANES, N), F32),
            pltpu.VMEM((G, SUBLANES, N), F32),
            pltpu.VMEM((GPS, L + SUBLANES, GROUP_CH), F32),
            pltpu.VMEM((GPS, L + SUBLANES, N), F32),
            pltpu.VMEM((GPS, L + SUBLANES, N), F32),
            pltpu.VMEM((L, LANES), F32),
            pltpu.VMEM((LANES, L), F32),
            pltpu.VMEM((LANES, L), F32),
            pltpu.VMEM((G, N, GROUP_CH), F32),
            pltpu.VMEM((GPS, GROUP_CH // (4 * SSM_HEAD_DIM), 4 * (L + N), 4 * SSM_HEAD_DIM), BF16),
        ],
        compiler_params=_cparams(("arbitrary", "arbitrary")),
        name="ssd",
    )(proj, proj, proj, proj, dt_raw, conv_w, conv_w, conv_w, conv_b, conv_b, conv_b,
      dt_bias, a_log, d_exp, norm_w)


def _qk_prep_kernel(q_ref, k_ref, v_ref, qw_ref, kw_ref, qt_ref, kn_ref, vt_ref, pen_ref, kbar_scr):
    S = q_ref.shape[0]
    BS = MOBA_BLOCK
    nb = S // BS
    nbp = kbar_scr.shape[0]
    kbar_scr[...] = jnp.zeros(kbar_scr.shape, F32)
    eye = (lax.broadcasted_iota(jnp.int32, (ATTN_HEAD_DIM, ATTN_HEAD_DIM), 0)
           == lax.broadcasted_iota(jnp.int32, (ATTN_HEAD_DIM, ATTN_HEAD_DIM), 1)).astype(BF16)

    def kloop(b, carry):
        rows = pl.ds(pl.multiple_of(b * BS, BS), BS)
        k = k_ref[rows, :].astype(F32)
        kn = k * lax.rsqrt(jnp.mean(k * k, axis=-1, keepdims=True) + EPS) * kw_ref[...]
        kn_ref[0, rows, :] = kn.astype(BF16)
        kbar_scr[pl.ds(b, 1), :] = jnp.mean(kn, axis=0, keepdims=True)
        vt_ref[0, b, 0:ATTN_HEAD_DIM, :] = _dot_nt(eye, v_ref[rows, :]).astype(BF16)
        vt_ref[0, b, ATTN_HEAD_DIM:, :] = jnp.ones((MOBA_VT_ROWS - ATTN_HEAD_DIM, BS), BF16)
        return carry

    lax.fori_loop(0, nb, kloop, 0, unroll=4)
    kb_hi, kb_mid, _ = _split3(kbar_scr[...])

    def qloop(i, carry):
        rows = pl.ds(pl.multiple_of(i * BS, BS), BS)
        q = q_ref[rows, :].astype(F32)
        qn = q * lax.rsqrt(jnp.mean(q * q, axis=-1, keepdims=True) + EPS) * qw_ref[...]
        qnt = (qn * (ATTN_HEAD_DIM ** -0.5)).T
        qt_ref[0, i] = (qnt * LOG2E).astype(BF16)
        q_hi, q_mid, _ = _split3(qnt)
        gate = _dot(kb_hi, q_hi) + _dot(kb_mid, q_hi) + _dot(kb_hi, q_mid)
        blk = lax.broadcasted_iota(jnp.int32, gate.shape, 0)
        blk_f = blk.astype(F32)
        gate = jnp.where(blk < i, gate, -jnp.inf)
        pen = jnp.full(gate.shape, -jnp.inf, F32)
        for _ in range(MOBA_TOPK):
            mx = jnp.max(gate, axis=0, keepdims=True)
            cand = jnp.where((gate == mx) & (mx > -jnp.inf), blk_f, float(nbp))
            idx = jnp.min(cand, axis=0, keepdims=True)
            hit = blk_f == idx
            pen = jnp.where(hit, 0.0, pen)
            gate = jnp.where(hit, -jnp.inf, gate)
        pen_ref[0, i] = pen
        return carry

    lax.fori_loop(0, nb, qloop, 0, unroll=8)


def _qk_prep(proj, q_norm_w, k_norm_w):
    S = proj.shape[0]
    H, D, BS = ATTN_HEADS, ATTN_HEAD_DIM, MOBA_BLOCK
    nb = S // BS
    nbp = -(-nb // SUBLANES) * SUBLANES
    return pl.pallas_call(
        _qk_prep_kernel,
        grid=(H,),
        in_specs=[
            pl.BlockSpec((S, D), lambda h: (0, COL_Q // D + h)),
            pl.BlockSpec((S, D), lambda h: (0, COL_K // D + h)),
            pl.BlockSpec((S, D), lambda h: (0, COL_V // D + h)),
            pl.BlockSpec((1, D), lambda h: (0, 0)),
            pl.BlockSpec((1, D), lambda h: (0, 0)),
        ],
        out_specs=[
            pl.BlockSpec((1, nb, D, BS), lambda h: (h, 0, 0, 0)),
            pl.BlockSpec((1, S, D), lambda h: (h, 0, 0)),
            pl.BlockSpec((1, nb, MOBA_VT_ROWS, BS), lambda h: (h, 0, 0, 0)),
            pl.BlockSpec((1, nb, nbp, BS), lambda h: (h, 0, 0, 0)),
        ],
        out_shape=[
            jax.ShapeDtypeStruct((H, nb, D, BS), BF16),
            jax.ShapeDtypeStruct((H, S, D), BF16),
            jax.ShapeDtypeStruct((H, nb, MOBA_VT_ROWS, BS), BF16),
            jax.ShapeDtypeStruct((H, nb, nbp, BS), F32),
        ],
        scratch_shapes=[pltpu.VMEM((nbp, D), F32)],
        compiler_params=_cparams(("arbitrary",)),
        name="qk_prep",
    )(proj, proj, proj, q_norm_w, k_norm_w)


def _rel_bucket(dist):
    n = jnp.maximum(dist, 0)
    max_exact = REL_BUCKETS // 2
    nf = jnp.maximum(n, 1).astype(F32)
    large = max_exact + (jnp.log(nf / max_exact) / math.log(REL_MAX_DIST / max_exact)
                         * (REL_BUCKETS - max_exact)).astype(jnp.int32)
    large = jnp.minimum(large, REL_BUCKETS - 1)
    return jnp.where(n < max_exact, n, large)


MOBA_HEADS_PER_STEP = 4
MOBA_VT_ROWS = ATTN_HEAD_DIM + 2 * SUBLANES


def _moba_kernel(tbl_ref, qt_ref, k_ref, vt_ref, pen_ref, o_ref,
                 bown_scr, bprev_scr, m_scr, acc_scr, s0_scr, s1_scr, s2_scr, s3_scr,
                 p0_scr, p1_scr, p2_scr, p3_scr, al0_scr, al1_scr, al2_scr, al3_scr):
    hp = pl.program_id(0)
    i = pl.program_id(1)
    BS, HB = MOBA_BLOCK, MOBA_HEADS_PER_STEP
    D = ATTN_HEAD_DIM

    @pl.when(i == 0)
    def _():
        kl = lax.broadcasted_iota(jnp.int32, (BS, BS), 0)
        ql = lax.broadcasted_iota(jnp.int32, (BS, BS), 1)
        d = ql - kl
        b_own = _rel_bucket(d)
        b_prev = _rel_bucket(d + BS)
        for a in range(HB):
            h = hp * HB + a
            far = tbl_ref[h, REL_BUCKETS - 1]

            def lookup(bucket):
                val = jnp.zeros((BS, BS), F32)
                for b in range(REL_BUCKETS):
                    val = jnp.where(bucket == b, tbl_ref[h, b], val)
                return (val - far) * LOG2E

            bown_scr[a] = jnp.where(d >= 0, lookup(b_own), -jnp.inf)
            bprev_scr[a] = lookup(b_prev)

    def krows(a, j):
        return k_ref[a, pl.ds(pl.multiple_of(j * BS, BS), BS), :]

    jp = jnp.maximum(i - 1, 0)
    no_prev = jnp.where(i == 0, -jnp.inf, 0.0)
    n_far = jnp.maximum(i - 1, 0)
    nb = k_ref.shape[1] // BS

    def clampj(j):
        return jnp.clip(j, 0, nb - 1)

    def scores(j, s_slot):
        for a in range(HB):
            s_slot[a] = _dot(krows(a, clampj(j)), qt_ref[a, 0])

    def softmax(j, s_slot, p_slot, al_slot, mask_tail, bias_scr=None):
        for a in range(HB):
            pr = pen_ref[a, 0, pl.ds(clampj(j), 1), :]
            if mask_tail is True:
                pr = pr + jnp.where(j >= n_far, -jnp.inf, 0.0)
            elif mask_tail is not False:
                pr = pr + mask_tail
            def tile():
                s = s_slot[a]
                return s if bias_scr is None else s + bias_scr[a]

            m_old = m_scr[a]
            m_new = jnp.maximum(m_old, jnp.max(tile(), axis=0, keepdims=True) + pr)
            al_slot[a] = jnp.exp2(m_old - m_new)
            m_scr[a] = m_new
            p_slot[a] = jnp.exp2(tile() - (m_new - pr)).astype(BF16)

    def accumulate(j, p_slot, al_slot, maybe_prev=False):
        jv = clampj(j)
        if maybe_prev:
            jv = jnp.where(j < 0, jp, jv)
        for a in range(HB):
            acc_scr[a] = al_slot[a] * acc_scr[a] + _dot(vt_ref[a, jv], p_slot[a])

    s_slots = (s0_scr, s1_scr, s2_scr, s3_scr)
    p_slots = (p0_scr, p1_scr, p2_scr, p3_scr)
    al_slots = (al0_scr, al1_scr, al2_scr, al3_scr)
    scores(i, s2_scr)
    scores(jp, s3_scr)
    scores(0, s0_scr)
    for a in range(HB):
        s = s2_scr[a] + bown_scr[a]
        m0 = jnp.max(s, axis=0, keepdims=True)
        m_scr[a] = m0
        p2_scr[a] = jnp.exp2(s - m0).astype(BF16)
    for a in range(HB):
        acc_scr[a] = _dot(vt_ref[a, i], p2_scr[a])
    softmax(jp, s3_scr, p3_scr, al3_scr, no_prev, bias_scr=bprev_scr)

    def quad_loop(g, carry):
        j = 4 * g
        for a in range(HB):
            kq = k_ref[a, pl.ds(pl.multiple_of((j + 1) * BS, BS), 3 * BS), :]
            s3 = _dot(kq, qt_ref[a, 0])
            for t in range(3):
                s_slots[t + 1][a] = s3[t * BS:(t + 1) * BS, :]
        for t in range(4):
            accumulate(j + t - 1, p_slots[(t - 1) % 4], al_slots[(t - 1) % 4], maybe_prev=(t == 0))
            softmax(j + t, s_slots[t], p_slots[t], al_slots[t], False)
            if t == 0:
                scores(j + 4, s0_scr)
        return carry

    n_quads = n_far // 4
    lax.fori_loop(0, n_quads, quad_loop, 0)

    def pair_loop(g, carry):
        j = 4 * n_quads + 2 * g
        scores(j + 1, s1_scr)
        accumulate(j - 1, p3_scr, al3_scr, maybe_prev=True)
        softmax(j, s0_scr, p2_scr, al2_scr, False)
        scores(j + 2, s0_scr)
        accumulate(j, p2_scr, al2_scr)
        softmax(j + 1, s1_scr, p3_scr, al3_scr, True)
        return carry

    n_pairs = (n_far - 4 * n_quads + 1) // 2
    lax.fori_loop(0, n_pairs, pair_loop, 0)
    accumulate(4 * n_quads + 2 * n_pairs - 1, p3_scr, al3_scr, maybe_prev=True)
    for a in range(HB):
        acc = acc_scr[a]
        o_ref[:, a * D:(a + 1) * D] = (acc[0:D, :] / acc[D:D + 1, :]).T.astype(o_ref.dtype)


def _moba(qt, kn, vt, pen, table_h):
    H, nb, D, BS = qt.shape
    S = nb * BS
    nbp = pen.shape[2]
    HB = MOBA_HEADS_PER_STEP
    return pl.pallas_call(
        _moba_kernel,
        grid=(H // HB, nb),
        in_specs=[
            pl.BlockSpec(memory_space=pltpu.SMEM),
            pl.BlockSpec((HB, 1, D, BS), lambda hp, i: (hp, i, 0, 0)),
            pl.BlockSpec((HB, S, D), lambda hp, i: (hp, 0, 0)),
            pl.BlockSpec((HB, nb, MOBA_VT_ROWS, BS), lambda hp, i: (hp, 0, 0, 0)),
            pl.BlockSpec((HB, 1, nbp, BS), lambda hp, i: (hp, i, 0, 0)),
        ],
        out_specs=pl.BlockSpec((BS, HB * D), lambda hp, i: (i, hp)),
        out_shape=jax.ShapeDtypeStruct((S, H * D), BF16),
        scratch_shapes=[
            pltpu.VMEM((HB, BS, BS), F32),
            pltpu.VMEM((HB, BS, BS), F32),
            pltpu.VMEM((HB, 1, BS), F32),
            pltpu.VMEM((HB, MOBA_VT_ROWS, BS), F32),
        ] + [pltpu.VMEM((HB, BS, BS), F32)] * 4 + [pltpu.VMEM((HB, BS, BS), BF16)] * 4
        + [pltpu.VMEM((HB, 1, BS), F32)] * 4,
        compiler_params=_cparams(("arbitrary", "arbitrary")),
        name="moba",
    )(table_h, qt, kn, vt, pen)


def _mix_kernel(ys_ref, ya_ref, ws_ref, wa_ref, gs_ref, ga_ref, bs_ref, ba_ref, o_ref):
    gs = _sigmoid(gs_ref[...].astype(F32) + bs_ref[...])
    ga = _sigmoid(ga_ref[...].astype(F32) + ba_ref[...])
    o_ref[...] = (gs * _dot(ys_ref[...], ws_ref[...])
                  + ga * _dot(ya_ref[...], wa_ref[...])).astype(o_ref.dtype)


def _mix(y_ssd, y_attn, w_ssm_out, w_attn_out, proj, b_gate, tm=1024, tn=512):
    S = y_ssd.shape[0]
    tm = min(tm, S)
    gs_col = COL_G // tn
    ga_col = (COL_G + D_MODEL) // tn
    nb = D_MODEL // tn
    return pl.pallas_call(
        _mix_kernel,
        grid=(S // tm, D_MODEL // tn),
        in_specs=[
            pl.BlockSpec((tm, D_INNER), lambda m, n: (m, 0)),
            pl.BlockSpec((tm, D_ATTN), lambda m, n: (m, 0)),
            pl.BlockSpec((D_INNER, tn), lambda m, n: (0, n)),
            pl.BlockSpec((D_ATTN, tn), lambda m, n: (0, n)),
            pl.BlockSpec((tm, tn), lambda m, n: (m, gs_col + n)),
            pl.BlockSpec((tm, tn), lambda m, n: (m, ga_col + n)),
            pl.BlockSpec((1, tn), lambda m, n: (0, n)),
            pl.BlockSpec((1, tn), lambda m, n: (0, nb + n)),
        ],
        out_specs=pl.BlockSpec((tm, tn), lambda m, n: (m, n)),
        out_shape=jax.ShapeDtypeStruct((S, D_MODEL), BF16),
        compiler_params=_cparams(("arbitrary", "arbitrary")),
        name="mix",
    )(y_ssd, y_attn, w_ssm_out, w_attn_out, proj, proj, b_gate, b_gate)


def _out_norm_kernel(x_ref, mix_ref, w_ref, nw_ref, x1_ref, h2_ref):
    x1 = x_ref[...] + _dot(mix_ref[...], w_ref[...])
    x1_ref[...] = x1
    ms = jnp.mean(x1 * x1, axis=-1, keepdims=True)
    h2_ref[...] = (x1 * lax.rsqrt(ms + EPS) * nw_ref[...]).astype(h2_ref.dtype)


def _out_norm(x, mix, w_out, ffn_norm_w, tm=512):
    S = x.shape[0]
    tm = min(tm, S)
    return pl.pallas_call(
        _out_norm_kernel,
        grid=(S // tm,),
        in_specs=[
            pl.BlockSpec((tm, D_MODEL), lambda m: (m, 0)),
            pl.BlockSpec((tm, D_MODEL), lambda m: (m, 0)),
            pl.BlockSpec((D_MODEL, D_MODEL), lambda m: (0, 0)),
            pl.BlockSpec((1, D_MODEL), lambda m: (0, 0)),
        ],
        out_specs=[
            pl.BlockSpec((tm, D_MODEL), lambda m: (m, 0)),
            pl.BlockSpec((tm, D_MODEL), lambda m: (m, 0)),
        ],
        out_shape=[
            jax.ShapeDtypeStruct((S, D_MODEL), F32),
            jax.ShapeDtypeStruct((S, D_MODEL), BF16),
        ],
        compiler_params=_cparams(("arbitrary",)),
        name="out_norm",
    )(x, mix, w_out, ffn_norm_w)


def _ffn_up_kernel(h_ref, wg32_ref, wu32_ref, cwg_ref, cwu_ref, cbg_ref, cbu_ref, o_ref,
                   tailg, tailu, padg, padu, wg_ref, wu_ref):
    tm = h_ref.shape[0]

    @pl.when(pl.program_id(1) == 0)
    def _():
        tailg[...] = jnp.zeros(tailg.shape, F32)
        tailu[...] = jnp.zeros(tailu.shape, F32)
        wg_ref[...] = wg32_ref[...].astype(BF16)
        wu_ref[...] = wu32_ref[...].astype(BF16)

    def conv(w_ref, cw_ref, cb_ref, tail, pad):
        pad[0:SUBLANES, :] = tail[...]
        pad[SUBLANES:SUBLANES + tm, :] = _dot(h_ref[...], w_ref[...])
        tail[...] = pad[tm:tm + SUBLANES, :]
        acc = cb_ref[...]
        for k in range(FFN_CONV):
            off = SUBLANES - (FFN_CONV - 1) + k
            acc = acc + cw_ref[k:k + 1, :] * pad[off:off + tm, :]
        return acc

    ug = conv(wg_ref, cwg_ref, cbg_ref, tailg, padg)
    uu = conv(wu_ref, cwu_ref, cbu_ref, tailu, padu)
    o_ref[...] = (_silu(ug) * uu).astype(o_ref.dtype)


def _ffn_up(h2, w_up, conv_w, conv_b, tm=1024, tn=512):
    S = h2.shape[0]
    tm = min(tm, S)
    nb = D_FF // tn
    return pl.pallas_call(
        _ffn_up_kernel,
        grid=(nb, S // tm),
        in_specs=[
            pl.BlockSpec((tm, D_MODEL), lambda n, m: (m, 0)),
            pl.BlockSpec((D_MODEL, tn), lambda n, m: (0, n)),
            pl.BlockSpec((D_MODEL, tn), lambda n, m: (0, nb + n)),
            pl.BlockSpec((FFN_CONV, tn), lambda n, m: (0, n)),
            pl.BlockSpec((FFN_CONV, tn), lambda n, m: (0, nb + n)),
            pl.BlockSpec((1, tn), lambda n, m: (0, n)),
            pl.BlockSpec((1, tn), lambda n, m: (0, nb + n)),
        ],
        out_specs=pl.BlockSpec((tm, tn), lambda n, m: (m, n)),
        out_shape=jax.ShapeDtypeStruct((S, D_FF), BF16),
        scratch_shapes=[
            pltpu.VMEM((SUBLANES, tn), F32),
            pltpu.VMEM((SUBLANES, tn), F32),
            pltpu.VMEM((tm + SUBLANES, tn), F32),
            pltpu.VMEM((tm + SUBLANES, tn), F32),
            pltpu.VMEM((D_MODEL, tn), BF16),
            pltpu.VMEM((D_MODEL, tn), BF16),
        ],
        compiler_params=_cparams(("arbitrary", "arbitrary")),
        name="ffn_up",
    )(h2, w_up, w_up, conv_w, conv_w, conv_b, conv_b)


def _ffn_down_kernel(a_ref, w_ref, x1_ref, o_ref):
    o_ref[...] = x1_ref[...] + _dot(a_ref[...], w_ref[...])


def _ffn_down(act, w_down, x1, tm=1024, tn=512):
    S = act.shape[0]
    tm = min(tm, S)
    return pl.pallas_call(
        _ffn_down_kernel,
        grid=(S // tm, D_MODEL // tn),
        in_specs=[
            pl.BlockSpec((tm, D_FF), lambda m, n: (m, 0)),
            pl.BlockSpec((D_FF, tn), lambda m, n: (0, n)),
            pl.BlockSpec((tm, tn), lambda m, n: (m, n)),
        ],
        out_specs=pl.BlockSpec((tm, tn), lambda m, n: (m, n)),
        out_shape=jax.ShapeDtypeStruct((S, D_MODEL), F32),
        compiler_params=_cparams(("arbitrary", "arbitrary")),
        name="ffn_down",
    )(act, w_down, x1)


def _pad_lanes(v):
    return jnp.pad(v, ((0, 0), (0, LANES - v.shape[1])))


def _layer(x, attn_norm_w, w_in, b_gate, ssm_conv_w, ssm_conv_b, ssm_dt_bias, ssm_a_log, ssm_d,
           ssm_norm_w, q_norm_w, k_norm_w, rel_bias, w_ssm_out, w_attn_out, w_out, ffn_norm_w,
           w_up, ffn_conv_w, ffn_conv_b, w_down):
    o_dt = D_INNER + D_XBC
    o_q = o_dt + SSM_HEADS
    wt = w_in.T
    wt_dt = jnp.pad(wt[o_dt:o_q], ((0, LANES - SSM_HEADS), (0, 0)))

    h, dt_raw = _norm_dt(x, attn_norm_w[None, :], wt_dt)
    proj = _in_proj(h, wt, o_dt)
    y_ssd = _ssd(proj, dt_raw, ssm_conv_w, ssm_conv_b[None, :], _pad_lanes(ssm_dt_bias[None, :]),
                 _pad_lanes(ssm_a_log[None, :]), jnp.repeat(ssm_d, SSM_HEAD_DIM)[None, :],
                 ssm_norm_w[None, :])
    qt, kn, vt, pen = _qk_prep(proj, q_norm_w[None, :], k_norm_w[None, :])
    y_attn = _moba(qt, kn, vt, pen, rel_bias.T)
    mix = _mix(y_ssd, y_attn, w_ssm_out.astype(BF16), w_attn_out.astype(BF16), proj, b_gate[None, :])
    x1, h2 = _out_norm(x, mix, w_out.astype(BF16), ffn_norm_w[None, :])
    act = _ffn_up(h2, w_up, ffn_conv_w, ffn_conv_b[None, :])
    return _ffn_down(act, w_down.astype(BF16), x1)


def kernel(x, attn_norm_w, w_in, b_gate, ssm_conv_w, ssm_conv_b, ssm_dt_bias, ssm_a_log, ssm_d,
           ssm_norm_w, q_norm_w, k_norm_w, rel_bias, w_ssm_out, w_attn_out, w_out, ffn_norm_w,
           w_up, ffn_conv_w, ffn_conv_b, w_down):
    assert x.shape[0] == 1 and attn_norm_w.shape[0] == 1
    out = _layer(x[0], attn_norm_w[0], w_in[0], b_gate[0], ssm_conv_w[0], ssm_conv_b[0],
                 ssm_dt_bias[0], ssm_a_log[0], ssm_d[0], ssm_norm_w[0], q_norm_w[0], k_norm_w[0],
                 rel_bias, w_ssm_out[0], w_attn_out[0], w_out[0], ffn_norm_w[0], w_up[0],
                 ffn_conv_w[0], ffn_conv_b[0], w_down[0])
    return out[None]
```

```python
import math

import jax
import jax.numpy as jnp
from jax import lax
from jax.experimental import pallas as pl
from jax.experimental.pallas import tpu as pltpu

F32 = jnp.float32
BF16 = jnp.bfloat16

D_MODEL = 2048
D_INNER = 4096
SSM_HEADS = 64
SSM_HEAD_DIM = 64
SSM_GROUPS = 8
SSM_HEADS_PER_GROUP = SSM_HEADS // SSM_GROUPS
SSM_STATE = 128
SSM_CONV = 4
SSM_CHUNK = 256
GROUP_CH = D_INNER // SSM_GROUPS
SSD_GROUPS_PER_STEP = 8
D_XBC = D_INNER + 2 * SSM_GROUPS * SSM_STATE
ATTN_HEADS = 16
ATTN_HEAD_DIM = 128
D_ATTN = ATTN_HEADS * ATTN_HEAD_DIM
MOBA_BLOCK = 256
MOBA_TOPK = 3
REL_BUCKETS = 32
REL_MAX_DIST = 128
D_FF = 5632
FFN_CONV = 3
EPS = 1e-6
LOG2E = 1.4426950408889634

LANES = 128
SUBLANES = 8
D_PROJ = D_XBC + 3 * D_ATTN + 2 * D_INNER
COL_Z = 0
COL_XBC = COL_Z + D_INNER
COL_Q = COL_XBC + D_XBC
COL_K = COL_Q + D_ATTN
COL_V = COL_K + D_ATTN
COL_G = COL_V + D_ATTN
VMEM_LIMIT = 56 * 1024 * 1024


def _cparams(sem):
    return pltpu.CompilerParams(dimension_semantics=sem, vmem_limit_bytes=VMEM_LIMIT)


def _split3(a):
    hi = a.astype(BF16)
    r = a - hi.astype(F32)
    mid = r.astype(BF16)
    lo = (r - mid.astype(F32)).astype(BF16)
    return hi, mid, lo


def _dot(a, b):
    return jnp.dot(a, b, preferred_element_type=F32)


def _dot_nt(a, b):
    return lax.dot_general(a, b, (((1,), (1,)), ((), ())), preferred_element_type=F32)


def _sigmoid(x):
    return 0.5 + 0.5 * jnp.tanh(0.5 * x)


def _silu(x):
    h = 0.5 * x
    return h + h * jnp.tanh(h)


def _norm_dt_kernel(x_ref, nw_ref, wdt_ref, h_ref, dt_ref):
    x = x_ref[...]
    ms = jnp.mean(x * x, axis=-1, keepdims=True)
    h = x * lax.rsqrt(ms + EPS) * nw_ref[...]
    h_ref[...] = h.astype(BF16)
    h_hi, h_mid, _ = _split3(h)
    w_hi, w_mid, _ = _split3(wdt_ref[...])
    dt_ref[...] = _dot_nt(h_hi, w_hi) + _dot_nt(h_hi, w_mid) + _dot_nt(h_mid, w_hi)


def _norm_dt(x, norm_w, wt_dt, tm=512):
    S = x.shape[0]
    tm = min(tm, S)
    return pl.pallas_call(
        _norm_dt_kernel,
        grid=(S // tm,),
        in_specs=[
            pl.BlockSpec((tm, D_MODEL), lambda m: (m, 0)),
            pl.BlockSpec((1, D_MODEL), lambda m: (0, 0)),
            pl.BlockSpec((LANES, D_MODEL), lambda m: (0, 0)),
        ],
        out_specs=[
            pl.BlockSpec((tm, D_MODEL), lambda m: (m, 0)),
            pl.BlockSpec((tm, LANES), lambda m: (m, 0)),
        ],
        out_shape=[
            jax.ShapeDtypeStruct((S, D_MODEL), BF16),
            jax.ShapeDtypeStruct((S, LANES), F32),
        ],
        compiler_params=_cparams(("arbitrary",)),
        name="norm_dt",
    )(x, norm_w, wt_dt)


def _in_proj_kernel(h_ref, w32_ref, o_ref, w_scr):
    @pl.when(pl.program_id(1) == 0)
    def _():
        w_scr[...] = w32_ref[...].astype(BF16)

    o_ref[...] = _dot_nt(h_ref[...], w_scr[...]).astype(o_ref.dtype)


def _in_proj(h, wt, dt_row0, tm=2048, tn=1024):
    S = h.shape[0]
    tm = min(tm, S)
    assert dt_row0 % tn == 0 and SSM_HEADS % SUBLANES == 0
    na = dt_row0 // tn

    def w_rows(n, m):
        return (pl.multiple_of(n * tn + jnp.where(n >= na, SSM_HEADS, 0), SSM_HEADS), 0)

    return pl.pallas_call(
        _in_proj_kernel,
        grid=(D_PROJ // tn, S // tm),
        in_specs=[
            pl.BlockSpec((tm, D_MODEL), lambda n, m: (m, 0)),
            pl.BlockSpec((pl.Element(tn), pl.Element(D_MODEL)), w_rows),
        ],
        out_specs=pl.BlockSpec((tm, tn), lambda n, m: (m, n)),
        out_shape=jax.ShapeDtypeStruct((S, D_PROJ), BF16),
        scratch_shapes=[pltpu.VMEM((tn, D_MODEL), BF16)],
        compiler_params=_cparams(("arbitrary", "arbitrary")),
        name="in_proj",
    )(h, wt)


def _conv_silu(raw_ref, tail_ref, pad_ref, w_ref, b_ref, g, taps, cols=slice(None)):
    L = raw_ref.shape[0]
    pad_ref[0:SUBLANES, cols] = tail_ref[g, :, cols]
    pad_ref[SUBLANES:SUBLANES + L, cols] = raw_ref[:, cols].astype(F32)
    tail_ref[g, :, cols] = pad_ref[L:L + SUBLANES, cols]
    acc = b_ref[:, cols]
    for k in range(taps):
        off = SUBLANES - (taps - 1) + k
        acc = acc + w_ref[k:k + 1, cols] * pad_ref[off:off + L, cols]
    return _silu(acc)


def _ssd_kernel(x_ref, b_ref, c_ref, z_ref, dtr_ref, cwx_ref, cwb_ref, cwc_ref,
                cbx_ref, cbb_ref, cbc_ref, dtb_ref, alog_ref, dsk_ref, nw_ref,
                o_ref,
                tailx, tailb, tailc, padx, padb, padc, acs_scr, acst_scr, lbt_scr, ht_scr, rhs_scr):
    c = pl.program_id(0)
    gp = pl.program_id(1)
    L = SSM_CHUNK
    QW = 4 * SSM_HEAD_DIM
    GPS = SSD_GROUPS_PER_STEP

    @pl.when(c == 0)
    def _():
        for gg in range(GPS):
            g0 = gp * GPS + gg
            tailx[g0] = jnp.zeros(tailx.shape[1:], F32)
            tailb[g0] = jnp.zeros(tailb.shape[1:], F32)
            tailc[g0] = jnp.zeros(tailc.shape[1:], F32)
            ht_scr[g0] = jnp.zeros(ht_scr.shape[1:], F32)
            rhs_scr[gg] = jnp.zeros(rhs_scr.shape[1:], BF16)

    @pl.when(gp == 0)
    def _():
        t = dtr_ref[...] + dtb_ref[...]
        dt = jnp.maximum(t, 0.0) + jnp.log(1.0 + jnp.exp(-jnp.abs(t)))
        a = -jnp.exp(alog_ref[...])
        da = dt * a
        ri = lax.broadcasted_iota(jnp.int32, (L, L), 0)
        ci = lax.broadcasted_iota(jnp.int32, (L, L), 1)
        tri = jnp.where(ri >= ci, 1.0, 0.0).astype(BF16)
        hi, mid, lo = _split3(da)
        a_cs = _dot(tri, hi) + _dot(tri, mid) + _dot(tri, lo)
        a2 = a_cs * LOG2E
        a2t = a2.T
        acs_scr[...] = a2
        acst_scr[...] = a2t
        lbt_scr[...] = a2t - jnp.log(dt.T) * LOG2E

    for gg in range(GPS):
        xs = slice(gg * GROUP_CH, (gg + 1) * GROUP_CH)
        ns = slice(gg * SSM_STATE, (gg + 1) * SSM_STATE)
        _ssd_group(gp * GPS + gg,
                   x_ref.at[:, xs], b_ref.at[:, ns], c_ref.at[:, ns], z_ref.at[:, xs],
                   cwx_ref.at[:, xs], cwb_ref.at[:, ns], cwc_ref.at[:, ns],
                   cbx_ref.at[:, xs], cbb_ref.at[:, ns], cbc_ref.at[:, ns],
                   dsk_ref.at[:, xs], nw_ref.at[:, xs], o_ref.at[:, xs],
                   tailx, tailb, tailc, padx.at[gg], padb.at[gg], padc.at[gg],
                   acs_scr, acst_scr, lbt_scr, ht_scr,
                   [rhs_scr.at[gg, q] for q in range(GROUP_CH // QW)])


def _ssd_group(g, x_ref, b_ref, c_ref, z_ref, cwx_ref, cwb_ref, cwc_ref, cbx_ref, cbb_ref, cbc_ref,
               dsk_ref, nw_ref, o_ref, tailx, tailb, tailc, padx, padb, padc,
               acs_scr, acst_scr, lbt_scr, ht_scr, rhs_refs):
    L = SSM_CHUNK
    QW = 4 * SSM_HEAD_DIM
    ba = _conv_silu(b_ref, tailb, padb, cwb_ref, cbb_ref, g, SSM_CONV)
    ca = _conv_silu(c_ref, tailc, padc, cwc_ref, cbc_ref, g, SSM_CONV)

    bt = ba.T
    cb = _dot(ca.astype(BF16), bt.astype(BF16))
    a_cs = acs_scr[...]
    ri = lax.broadcasted_iota(jnp.int32, (L, L), 0)
    ci = lax.broadcasted_iota(jnp.int32, (L, L), 1)
    causal = ri >= ci
    lane_h = lax.broadcasted_iota(jnp.int32, (L, LANES), 1)
    lane_q = lax.broadcasted_iota(jnp.int32, (1, QW), 1)

    NH = QW // SSM_HEAD_DIM
    ygs = []
    for q in range(GROUP_CH // QW):
        rhs = rhs_refs[q]
        qc = slice(q * QW, (q + 1) * QW)
        xa = _conv_silu(x_ref, tailx, padx, cwx_ref, cbx_ref, g, SSM_CONV, cols=qc)
        xq = xa.astype(BF16)
        hq = ht_scr[g, :, q * QW:(q + 1) * QW]
        hqb = hq.astype(BF16)
        mhs, ces, bws = [], [], []
        cdq = jnp.zeros((1, QW), F32)
        for j in range(NH):
            h = g * SSM_HEADS_PER_GROUP + q * NH + j
            cols = slice(j * SSM_HEAD_DIM, (j + 1) * SSM_HEAD_DIM)
            rhs[j * L:(j + 1) * L, cols] = xq[:, cols]
            rhs[NH * L + j * SSM_STATE:NH * L + (j + 1) * SSM_STATE, cols] = hqb[:, cols]
            a_row = acst_scr[pl.ds(h, 1), :]
            b_row = lbt_scr[pl.ds(h, 1), :]
            a_col = jnp.sum(jnp.where(lane_h == h, a_cs, 0.0), axis=1, keepdims=True)
            dmat = jnp.where(causal, a_col - b_row, -jnp.inf)
            mhs.append((cb * jnp.exp2(dmat)).astype(BF16))
            ces.append((ca * jnp.exp2(a_col)).astype(BF16))
            a_last = a_row[:, L - 1:L]
            w_row = jnp.exp2(a_last - b_row)
            bws.append((bt * w_row).astype(BF16))
            lm = (lane_q >= j * SSM_HEAD_DIM) & (lane_q < (j + 1) * SSM_HEAD_DIM)
            cdq = jnp.where(lm, jnp.exp2(a_last), cdq)
        acc = _dot(jnp.concatenate(mhs + ces, axis=1), rhs[...])
        st = _dot(jnp.concatenate(bws, axis=1), rhs[0:NH * L, :])
        ht_scr[g, :, q * QW:(q + 1) * QW] = hq * cdq + st
        y = acc + xa * dsk_ref[:, qc]
        ygs.append(y * _silu(z_ref[:, qc].astype(F32)))

    ss = sum(jnp.sum(v * v, axis=1, keepdims=True) for v in ygs)
    scale = lax.rsqrt(ss * (1.0 / GROUP_CH) + EPS)
    for q in range(len(ygs)):
        o_ref[:, q * QW:(q + 1) * QW] = (
            ygs[q] * scale * nw_ref[:, q * QW:(q + 1) * QW]).astype(o_ref.dtype)


def _ssd(proj, dt_raw, conv_w, conv_b, dt_bias, a_log, d_exp, norm_w):
    S = proj.shape[0]
    L, G, N = SSM_CHUNK, SSM_GROUPS, SSM_STATE
    GPS = SSD_GROUPS_PER_STEP
    XW, NW = GPS * GROUP_CH, GPS * N
    bcol = D_INNER // NW
    ccol = bcol + G // GPS
    pxcol = COL_XBC // XW
    pbcol = COL_XBC // NW + bcol
    pccol = pbcol + G // GPS
    zcol = COL_Z // XW
    return pl.pallas_call(
        _ssd_kernel,
        grid=(S // L, G // GPS),
        in_specs=[
            pl.BlockSpec((L, XW), lambda c, g: (c, pxcol + g)),
            pl.BlockSpec((L, NW), lambda c, g: (c, pbcol + g)),
            pl.BlockSpec((L, NW), lambda c, g: (c, pccol + g)),
            pl.BlockSpec((L, XW), lambda c, g: (c, zcol + g)),
            pl.BlockSpec((L, LANES), lambda c, g: (c, 0)),
            pl.BlockSpec((SSM_CONV, XW), lambda c, g: (0, g)),
            pl.BlockSpec((SSM_CONV, NW), lambda c, g: (0, bcol + g)),
            pl.BlockSpec((SSM_CONV, NW), lambda c, g: (0, ccol + g)),
            pl.BlockSpec((1, XW), lambda c, g: (0, g)),
            pl.BlockSpec((1, NW), lambda c, g: (0, bcol + g)),
            pl.BlockSpec((1, NW), lambda c, g: (0, ccol + g)),
            pl.BlockSpec((1, LANES), lambda c, g: (0, 0)),
            pl.BlockSpec((1, LANES), lambda c, g: (0, 0)),
            pl.BlockSpec((1, XW), lambda c, g: (0, g)),
            pl.BlockSpec((1, XW), lambda c, g: (0, g)),
        ],
        out_specs=pl.BlockSpec((L, XW), lambda c, g: (c, g)),
        out_shape=jax.ShapeDtypeStruct((S, D_INNER), BF16),
        scratch_shapes=[
            pltpu.VMEM((G, SUBLANES, GROUP_CH), F32),
            pltpu.VMEM((G, SUBLANES, N), F32),
            pltpu.VMEM((G, SUBLANES, N), F32),
            pltpu.VMEM((GPS, L + SUBLANES, GROUP_CH), F32),
            pltpu.VMEM((GPS, L + SUBLANES, N), F32),
            pltpu.VMEM((GPS, L + SUBLANES, N), F32),
            pltpu.VMEM((L, LANES), F32),
            pltpu.VMEM((LANES, L), F32),
            pltpu.VMEM((LANES, L), F32),
            pltpu.VMEM((G, N, GROUP_CH), F32),
            pltpu.VMEM((GPS, GROUP_CH // (4 * SSM_HEAD_DIM), 4 * (L + N), 4 * SSM_HEAD_DIM), BF16),
        ],
        compiler_params=_cparams(("arbitrary", "arbitrary")),
        name="ssd",
    )(proj, proj, proj, proj, dt_raw, conv_w, conv_w, conv_w, conv_b, conv_b, conv_b,
      dt_bias, a_log, d_exp, norm_w)


def _qk_prep_kernel(q_ref, k_ref, v_ref, qw_ref, kw_ref, qt_ref, kn_ref, vt_ref, pen_ref, kbar_scr):
    S = q_ref.shape[0]
    BS = MOBA_BLOCK
    nb = S // BS
    nbp = kbar_scr.shape[0]
    kbar_scr[...] = jnp.zeros(kbar_scr.shape, F32)
    eye = (lax.broadcasted_iota(jnp.int32, (ATTN_HEAD_DIM, ATTN_HEAD_DIM), 0)
           == lax.broadcasted_iota(jnp.int32, (ATTN_HEAD_DIM, ATTN_HEAD_DIM), 1)).astype(BF16)

    def kloop(b, carry):
        rows = pl.ds(pl.multiple_of(b * BS, BS), BS)
        k = k_ref[rows, :].astype(F32)
        kn = k * lax.rsqrt(jnp.mean(k * k, axis=-1, keepdims=True) + EPS) * kw_ref[...]
        kn_ref[0, rows, :] = kn.astype(BF16)
        kbar_scr[pl.ds(b, 1), :] = jnp.mean(kn, axis=0, keepdims=True)
        vt_ref[0, b, 0:ATTN_HEAD_DIM, :] = _dot_nt(eye, v_ref[rows, :]).astype(BF16)
        vt_ref[0, b, ATTN_HEAD_DIM:, :] = jnp.ones((MOBA_VT_ROWS - ATTN_HEAD_DIM, BS), BF16)
        return carry

    lax.fori_loop(0, nb, kloop, 0, unroll=4)
    kb_hi, kb_mid, _ = _split3(kbar_scr[...])

    def qloop(i, carry):
        rows = pl.ds(pl.multiple_of(i * BS, BS), BS)
        q = q_ref[rows, :].astype(F32)
        qn = q * lax.rsqrt(jnp.mean(q * q, axis=-1, keepdims=True) + EPS) * qw_ref[...]
        qnt = (qn * (ATTN_HEAD_DIM ** -0.5)).T
        qt_ref[0, i] = (qnt * LOG2E).astype(BF16)
        q_hi, q_mid, _ = _split3(qnt)
        gate = _dot(kb_hi, q_hi) + _dot(kb_mid, q_hi) + _dot(kb_hi, q_mid)
        blk = lax.broadcasted_iota(jnp.int32, gate.shape, 0)
        blk_f = blk.astype(F32)
        gate = jnp.where(blk < i, gate, -jnp.inf)
        pen = jnp.full(gate.shape, -jnp.inf, F32)
        for _ in range(MOBA_TOPK):
            mx = jnp.max(gate, axis=0, keepdims=True)
            cand = jnp.where((gate == mx) & (mx > -jnp.inf), blk_f, float(nbp))
            idx = jnp.min(cand, axis=0, keepdims=True)
            hit = blk_f == idx
            pen = jnp.where(hit, 0.0, pen)
            gate = jnp.where(hit, -jnp.inf, gate)
        pen_ref[0, i] = pen
        return carry

    lax.fori_loop(0, nb, qloop, 0, unroll=8)


def _qk_prep(proj, q_norm_w, k_norm_w):
    S = proj.shape[0]
    H, D, BS = ATTN_HEADS, ATTN_HEAD_DIM, MOBA_BLOCK
    nb = S // BS
    nbp = -(-nb // SUBLANES) * SUBLANES
    return pl.pallas_call(
        _qk_prep_kernel,
        grid=(H,),
        in_specs=[
            pl.BlockSpec((S, D), lambda h: (0, COL_Q // D + h)),
            pl.BlockSpec((S, D), lambda h: (0, COL_K // D + h)),
            pl.BlockSpec((S, D), lambda h: (0, COL_V // D + h)),
            pl.BlockSpec((1, D), lambda h: (0, 0)),
            pl.BlockSpec((1, D), lambda h: (0, 0)),
        ],
        out_specs=[
            pl.BlockSpec((1, nb, D, BS), lambda h: (h, 0, 0, 0)),
            pl.BlockSpec((1, S, D), lambda h: (h, 0, 0)),
            pl.BlockSpec((1, nb, MOBA_VT_ROWS, BS), lambda h: (h, 0, 0, 0)),
            pl.BlockSpec((1, nb, nbp, BS), lambda h: (h, 0, 0, 0)),
        ],
        out_shape=[
            jax.ShapeDtypeStruct((H, nb, D, BS), BF16),
            jax.ShapeDtypeStruct((H, S, D), BF16),
            jax.ShapeDtypeStruct((H, nb, MOBA_VT_ROWS, BS), BF16),
            jax.ShapeDtypeStruct((H, nb, nbp, BS), F32),
        ],
        scratch_shapes=[pltpu.VMEM((nbp, D), F32)],
        compiler_params=_cparams(("arbitrary",)),
        name="qk_prep",
    )(proj, proj, proj, q_norm_w, k_norm_w)


def _rel_bucket(dist):
    n = jnp.maximum(dist, 0)
    max_exact = REL_BUCKETS // 2
    nf = jnp.maximum(n, 1).astype(F32)
    large = max_exact + (jnp.log(nf / max_exact) / math.log(REL_MAX_DIST / max_exact)
                         * (REL_BUCKETS - max_exact)).astype(jnp.int32)
    large = jnp.minimum(large, REL_BUCKETS - 1)
    return jnp.where(n < max_exact, n, large)


MOBA_HEADS_PER_STEP = 4
MOBA_VT_ROWS = ATTN_HEAD_DIM + 2 * SUBLANES


def _moba_kernel(tbl_ref, qt_ref, k_ref, vt_ref, pen_ref, o_ref,
                 bown_scr, bprev_scr, m_scr, acc_scr, s0_scr, s1_scr, s2_scr, s3_scr,
                 p0_scr, p1_scr, p2_scr, p3_scr, al0_scr, al1_scr, al2_scr, al3_scr):
    hp = pl.program_id(0)
    i = pl.program_id(1)
    BS, HB = MOBA_BLOCK, MOBA_HEADS_PER_STEP
    D = ATTN_HEAD_DIM

    @pl.when(i == 0)
    def _():
        kl = lax.broadcasted_iota(jnp.int32, (BS, BS), 0)
        ql = lax.broadcasted_iota(jnp.int32, (BS, BS), 1)
        d = ql - kl
        b_own = _rel_bucket(d)
        b_prev = _rel_bucket(d + BS)
        for a in range(HB):
            h = hp * HB + a
            far = tbl_ref[h, REL_BUCKETS - 1]

            def lookup(bucket):
                val = jnp.zeros((BS, BS), F32)
                for b in range(REL_BUCKETS):
                    val = jnp.where(bucket == b, tbl_ref[h, b], val)
                return (val - far) * LOG2E

            bown_scr[a] = jnp.where(d >= 0, lookup(b_own), -jnp.inf)
            bprev_scr[a] = lookup(b_prev)

    def krows(a, j):
        return k_ref[a, pl.ds(pl.multiple_of(j * BS, BS), BS), :]

    jp = jnp.maximum(i - 1, 0)
    no_prev = jnp.where(i == 0, -jnp.inf, 0.0)
    n_far = jnp.maximum(i - 1, 0)
    nb = k_ref.shape[1] // BS

    def clampj(j):
        return jnp.clip(j, 0, nb - 1)

    def scores(j, s_slot):
        for a in range(HB):
            s_slot[a] = _dot(krows(a, clampj(j)), qt_ref[a, 0])

    def softmax(j, s_slot, p_slot, al_slot, mask_tail, bias_scr=None):
        for a in range(HB):
            pr = pen_ref[a, 0, pl.ds(clampj(j), 1), :]
            if mask_tail is True:
                pr = pr + jnp.where(j >= n_far, -jnp.inf, 0.0)
            elif mask_tail is not False:
                pr = pr + mask_tail
            def tile():
                s = s_slot[a]
                return s if bias_scr is None else s + bias_scr[a]

            m_old = m_scr[a]
            m_new = jnp.maximum(m_old, jnp.max(tile(), axis=0, keepdims=True) + pr)
            al_slot[a] = jnp.exp2(m_old - m_new)
            m_scr[a] = m_new
            p_slot[a] = jnp.exp2(tile() - (m_new - pr)).astype(BF16)

    def accumulate(j, p_slot, al_slot, maybe_prev=False):
        jv = clampj(j)
        if maybe_prev:
            jv = jnp.where(j < 0, jp, jv)
        for a in range(HB):
            acc_scr[a] = al_slot[a] * acc_scr[a] + _dot(vt_ref[a, jv], p_slot[a])

    s_slots = (s0_scr, s1_scr, s2_scr, s3_scr)
    p_slots = (p0_scr, p1_scr, p2_scr, p3_scr)
    al_slots = (al0_scr, al1_scr, al2_scr, al3_scr)
    scores(i, s2_scr)
    scores(jp, s3_scr)
    scores(0, s0_scr)
    for a in range(HB):
        s = s2_scr[a] + bown_scr[a]
        m0 = jnp.max(s, axis=0, keepdims=True)
        m_scr[a] = m0
        p2_scr[a] = jnp.exp2(s - m0).astype(BF16)
    for a in range(HB):
        acc_scr[a] = _dot(vt_ref[a, i], p2_scr[a])
    softmax(jp, s3_scr, p3_scr, al3_scr, no_prev, bias_scr=bprev_scr)

    def quad_loop(g, carry):
        j = 4 * g
        for a in range(HB):
            kq = k_ref[a, pl.ds(pl.multiple_of((j + 1) * BS, BS), 3 * BS), :]
            s3 = _dot(kq, qt_ref[a, 0])
            for t in range(3):
                s_slots[t + 1][a] = s3[t * BS:(t + 1) * BS, :]
        for t in range(4):
            accumulate(j + t - 1, p_slots[(t - 1) % 4], al_slots[(t - 1) % 4], maybe_prev=(t == 0))
            softmax(j + t, s_slots[t], p_slots[t], al_slots[t], False)
            if t == 0:
                scores(j + 4, s0_scr)
        return carry

    n_quads = n_far // 4
    lax.fori_loop(0, n_quads, quad_loop, 0)

    def pair_loop(g, carry):
        j = 4 * n_quads + 2 * g
        scores(j + 1, s1_scr)
        accumulate(j - 1, p3_scr, al3_scr, maybe_prev=True)
        softmax(j, s0_scr, p2_scr, al2_scr, False)
        scores(j + 2, s0_scr)
        accumulate(j, p2_scr, al2_scr)
        softmax(j + 1, s1_scr, p3_scr, al3_scr, True)
        return carry

    n_pairs = (n_far - 4 * n_quads + 1) // 2
    lax.fori_loop(0, n_pairs, pair_loop, 0)
    accumulate(4 * n_quads + 2 * n_pairs - 1, p3_scr, al3_scr, maybe_prev=True)
    for a in range(HB):
        acc = acc_scr[a]
        inv_l = 1.0 / acc[D:D + 1, :]
        o_ref[:, a * D:(a + 1) * D] = (acc[0:D, :] * inv_l).T.astype(o_ref.dtype)


def _moba(qt, kn, vt, pen, table_h):
    H, nb, D, BS = qt.shape
    S = nb * BS
    nbp = pen.shape[2]
    HB = MOBA_HEADS_PER_STEP
    return pl.pallas_call(
        _moba_kernel,
        grid=(H // HB, nb),
        in_specs=[
            pl.BlockSpec(memory_space=pltpu.SMEM),
            pl.BlockSpec((HB, 1, D, BS), lambda hp, i: (hp, i, 0, 0)),
            pl.BlockSpec((HB, S, D), lambda hp, i: (hp, 0, 0)),
            pl.BlockSpec((HB, nb, MOBA_VT_ROWS, BS), lambda hp, i: (hp, 0, 0, 0)),
            pl.BlockSpec((HB, 1, nbp, BS), lambda hp, i: (hp, i, 0, 0)),
        ],
        out_specs=pl.BlockSpec((BS, HB * D), lambda hp, i: (i, hp)),
        out_shape=jax.ShapeDtypeStruct((S, H * D), BF16),
        scratch_shapes=[
            pltpu.VMEM((HB, BS, BS), F32),
            pltpu.VMEM((HB, BS, BS), F32),
            pltpu.VMEM((HB, 1, BS), F32),
            pltpu.VMEM((HB, MOBA_VT_ROWS, BS), F32),
        ] + [pltpu.VMEM((HB, BS, BS), F32)] * 4 + [pltpu.VMEM((HB, BS, BS), BF16)] * 4
        + [pltpu.VMEM((HB, 1, BS), F32)] * 4,
        compiler_params=_cparams(("arbitrary", "arbitrary")),
        name="moba",
    )(table_h, qt, kn, vt, pen)


def _mix_kernel(ys_ref, ya_ref, ws_ref, wa_ref, gs_ref, ga_ref, bs_ref, ba_ref, o_ref):
    gs = _sigmoid(gs_ref[...].astype(F32) + bs_ref[...])
    ga = _sigmoid(ga_ref[...].astype(F32) + ba_ref[...])
    o_ref[...] = (gs * _dot(ys_ref[...], ws_ref[...])
                  + ga * _dot(ya_ref[...], wa_ref[...])).astype(o_ref.dtype)


def _mix(y_ssd, y_attn, w_ssm_out, w_attn_out, proj, b_gate, tm=1024, tn=512):
    S = y_ssd.shape[0]
    tm = min(tm, S)
    gs_col = COL_G // tn
    ga_col = (COL_G + D_MODEL) // tn
    nb = D_MODEL // tn
    return pl.pallas_call(
        _mix_kernel,
        grid=(S // tm, D_MODEL // tn),
        in_specs=[
            pl.BlockSpec((tm, D_INNER), lambda m, n: (m, 0)),
            pl.BlockSpec((tm, D_ATTN), lambda m, n: (m, 0)),
            pl.BlockSpec((D_INNER, tn), lambda m, n: (0, n)),
            pl.BlockSpec((D_ATTN, tn), lambda m, n: (0, n)),
            pl.BlockSpec((tm, tn), lambda m, n: (m, gs_col + n)),
            pl.BlockSpec((tm, tn), lambda m, n: (m, ga_col + n)),
            pl.BlockSpec((1, tn), lambda m, n: (0, n)),
            pl.BlockSpec((1, tn), lambda m, n: (0, nb + n)),
        ],
        out_specs=pl.BlockSpec((tm, tn), lambda m, n: (m, n)),
        out_shape=jax.ShapeDtypeStruct((S, D_MODEL), BF16),
        compiler_params=_cparams(("arbitrary", "arbitrary")),
        name="mix",
    )(y_ssd, y_attn, w_ssm_out, w_attn_out, proj, proj, b_gate, b_gate)


def _out_norm_kernel(x_ref, mix_ref, w_ref, nw_ref, x1_ref, h2_ref):
    x1 = x_ref[...] + _dot(mix_ref[...], w_ref[...])
    x1_ref[...] = x1
    ms = jnp.mean(x1 * x1, axis=-1, keepdims=True)
    h2_ref[...] = (x1 * lax.rsqrt(ms + EPS) * nw_ref[...]).astype(h2_ref.dtype)


def _out_norm(x, mix, w_out, ffn_norm_w, tm=512):
    S = x.shape[0]
    tm = min(tm, S)
    return pl.pallas_call(
        _out_norm_kernel,
        grid=(S // tm,),
        in_specs=[
            pl.BlockSpec((tm, D_MODEL), lambda m: (m, 0)),
            pl.BlockSpec((tm, D_MODEL), lambda m: (m, 0)),
            pl.BlockSpec((D_MODEL, D_MODEL), lambda m: (0, 0)),
            pl.BlockSpec((1, D_MODEL), lambda m: (0, 0)),
        ],
        out_specs=[
            pl.BlockSpec((tm, D_MODEL), lambda m: (m, 0)),
            pl.BlockSpec((tm, D_MODEL), lambda m: (m, 0)),
        ],
        out_shape=[
            jax.ShapeDtypeStruct((S, D_MODEL), F32),
            jax.ShapeDtypeStruct((S, D_MODEL), BF16),
        ],
        compiler_params=_cparams(("arbitrary",)),
        name="out_norm",
    )(x, mix, w_out, ffn_norm_w)


def _ffn_up_kernel(h_ref, wg32_ref, wu32_ref, cwg_ref, cwu_ref, cbg_ref, cbu_ref, o_ref,
                   tailg, tailu, padg, padu, wg_ref, wu_ref):
    tm = h_ref.shape[0]

    @pl.when(pl.program_id(1) == 0)
    def _():
        tailg[...] = jnp.zeros(tailg.shape, F32)
        tailu[...] = jnp.zeros(tailu.shape, F32)
        wg_ref[...] = wg32_ref[...].astype(BF16)
        wu_ref[...] = wu32_ref[...].astype(BF16)

    def conv(w_ref, cw_ref, cb_ref, tail, pad):
        pad[0:SUBLANES, :] = tail[...]
        pad[SUBLANES:SUBLANES + tm, :] = _dot(h_ref[...], w_ref[...])
        tail[...] = pad[tm:tm + SUBLANES, :]
        acc = cb_ref[...]
        for k in range(FFN_CONV):
            off = SUBLANES - (FFN_CONV - 1) + k
            acc = acc + cw_ref[k:k + 1, :] * pad[off:off + tm, :]
        return acc

    ug = conv(wg_ref, cwg_ref, cbg_ref, tailg, padg)
    uu = conv(wu_ref, cwu_ref, cbu_ref, tailu, padu)
    o_ref[...] = (_silu(ug) * uu).astype(o_ref.dtype)


def _ffn_up(h2, w_up, conv_w, conv_b, tm=1024, tn=512):
    S = h2.shape[0]
    tm = min(tm, S)
    nb = D_FF // tn
    return pl.pallas_call(
        _ffn_up_kernel,
        grid=(nb, S // tm),
        in_specs=[
            pl.BlockSpec((tm, D_MODEL), lambda n, m: (m, 0)),
            pl.BlockSpec((D_MODEL, tn), lambda n, m: (0, n)),
            pl.BlockSpec((D_MODEL, tn), lambda n, m: (0, nb + n)),
            pl.BlockSpec((FFN_CONV, tn), lambda n, m: (0, n)),
            pl.BlockSpec((FFN_CONV, tn), lambda n, m: (0, nb + n)),
            pl.BlockSpec((1, tn), lambda n, m: (0, n)),
            pl.BlockSpec((1, tn), lambda n, m: (0, nb + n)),
        ],
        out_specs=pl.BlockSpec((tm, tn), lambda n, m: (m, n)),
        out_shape=jax.ShapeDtypeStruct((S, D_FF), BF16),
        scratch_shapes=[
            pltpu.VMEM((SUBLANES, tn), F32),
            pltpu.VMEM((SUBLANES, tn), F32),
            pltpu.VMEM((tm + SUBLANES, tn), F32),
            pltpu.VMEM((tm + SUBLANES, tn), F32),
            pltpu.VMEM((D_MODEL, tn), BF16),
            pltpu.VMEM((D_MODEL, tn), BF16),
        ],
        compiler_params=_cparams(("arbitrary", "arbitrary")),
        name="ffn_up",
    )(h2, w_up, w_up, conv_w, conv_w, conv_b, conv_b)


def _ffn_down_kernel(a_ref, w_ref, x1_ref, o_ref):
    o_ref[...] = x1_ref[...] + _dot(a_ref[...], w_ref[...])


def _ffn_down(act, w_down, x1, tm=1024, tn=512):
    S = act.shape[0]
    tm = min(tm, S)
    return pl.pallas_call(
        _ffn_down_kernel,
        grid=(S // tm, D_MODEL // tn),
        in_specs=[
            pl.BlockSpec((tm, D_FF), lambda m, n: (m, 0)),
            pl.BlockSpec((D_FF, tn), lambda m, n: (0, n)),
            pl.BlockSpec((tm, tn), lambda m, n: (m, n)),
        ],
        out_specs=pl.BlockSpec((tm, tn), lambda m, n: (m, n)),
        out_shape=jax.ShapeDtypeStruct((S, D_MODEL), F32),
        compiler_params=_cparams(("arbitrary", "arbitrary")),
        name="ffn_down",
    )(act, w_down, x1)


def _pad_lanes(v):
    return jnp.pad(v, ((0, 0), (0, LANES - v.shape[1])))


def _layer(x, attn_norm_w, w_in, b_gate, ssm_conv_w, ssm_conv_b, ssm_dt_bias, ssm_a_log, ssm_d,
           ssm_norm_w, q_norm_w, k_norm_w, rel_bias, w_ssm_out, w_attn_out, w_out, ffn_norm_w,
           w_up, ffn_conv_w, ffn_conv_b, w_down):
    o_dt = D_INNER + D_XBC
    o_q = o_dt + SSM_HEADS
    wt = w_in.T
    wt_dt = jnp.pad(wt[o_dt:o_q], ((0, LANES - SSM_HEADS), (0, 0)))

    h, dt_raw = _norm_dt(x, attn_norm_w[None, :], wt_dt)
    proj = _in_proj(h, wt, o_dt)
    y_ssd = _ssd(proj, dt_raw, ssm_conv_w, ssm_conv_b[None, :], _pad_lanes(ssm_dt_bias[None, :]),
                 _pad_lanes(ssm_a_log[None, :]), jnp.repeat(ssm_d, SSM_HEAD_DIM)[None, :],
                 ssm_norm_w[None, :])
    qt, kn, vt, pen = _qk_prep(proj, q_norm_w[None, :], k_norm_w[None, :])
    y_attn = _moba(qt, kn, vt, pen, rel_bias.T)
    mix = _mix(y_ssd, y_attn, w_ssm_out.astype(BF16), w_attn_out.astype(BF16), proj, b_gate[None, :])
    x1, h2 = _out_norm(x, mix, w_out.astype(BF16), ffn_norm_w[None, :])
    act = _ffn_up(h2, w_up, ffn_conv_w, ffn_conv_b[None, :])
    return _ffn_down(act, w_down.astype(BF16), x1)


def kernel(x, attn_norm_w, w_in, b_gate, ssm_conv_w, ssm_conv_b, ssm_dt_bias, ssm_a_log, ssm_d,
           ssm_norm_w, q_norm_w, k_norm_w, rel_bias, w_ssm_out, w_attn_out, w_out, ffn_norm_w,
           w_up, ffn_conv_w, ffn_conv_b, w_down):
    assert x.shape[0] == 1 and attn_norm_w.shape[0] == 1
    out = _layer(x[0], attn_norm_w[0], w_in[0], b_gate[0], ssm_conv_w[0], ssm_conv_b[0],
                 ssm_dt_bias[0], ssm_a_log[0], ssm_d[0], ssm_norm_w[0], q_norm_w[0], k_norm_w[0],
                 rel_bias, w_ssm_out[0], w_attn_out[0], w_out[0], ffn_norm_w[0], w_up[0],
                 ffn_conv_w[0], ffn_conv_b[0], w_down[0])
    return out[None]
```

```python
import math

import jax
import jax.numpy as jnp
from jax import lax
from jax.experimental import pallas as pl
from jax.experimental.pallas import tpu as pltpu

F32 = jnp.float32
BF16 = jnp.bfloat16

D_MODEL = 2048
D_INNER = 4096
SSM_HEADS = 64
SSM_HEAD_DIM = 64
SSM_GROUPS = 8
SSM_HEADS_PER_GROUP = SSM_HEADS // SSM_GROUPS
SSM_STATE = 128
SSM_CONV = 4
SSM_CHUNK = 256
GROUP_CH = D_INNER // SSM_GROUPS
SSD_GROUPS_PER_STEP = 8
D_XBC = D_INNER + 2 * SSM_GROUPS * SSM_STATE
ATTN_HEADS = 16
ATTN_HEAD_DIM = 128
D_ATTN = ATTN_HEADS * ATTN_HEAD_DIM
MOBA_BLOCK = 256
MOBA_TOPK = 3
REL_BUCKETS = 32
REL_MAX_DIST = 128
D_FF = 5632
FFN_CONV = 3
EPS = 1e-6
LOG2E = 1.4426950408889634

LANES = 128
SUBLANES = 8
D_PROJ = D_XBC + 3 * D_ATTN + 2 * D_INNER
COL_Z = 0
COL_XBC = COL_Z + D_INNER
COL_Q = COL_XBC + D_XBC
COL_K = COL_Q + D_ATTN
COL_V = COL_K + D_ATTN
COL_G = COL_V + D_ATTN
VMEM_LIMIT = 56 * 1024 * 1024


def _cparams(sem):
    return pltpu.CompilerParams(dimension_semantics=sem, vmem_limit_bytes=VMEM_LIMIT)


def _split3(a):
    hi = a.astype(BF16)
    r = a - hi.astype(F32)
    mid = r.astype(BF16)
    lo = (r - mid.astype(F32)).astype(BF16)
    return hi, mid, lo


def _dot(a, b):
    return jnp.dot(a, b, preferred_element_type=F32)


def _dot_nt(a, b):
    return lax.dot_general(a, b, (((1,), (1,)), ((), ())), preferred_element_type=F32)


def _sigmoid(x):
    return 0.5 + 0.5 * jnp.tanh(0.5 * x)


def _silu(x):
    h = 0.5 * x
    return h + h * jnp.tanh(h)


def _norm_dt_kernel(x_ref, nw_ref, wdt_ref, h_ref, dt_ref):
    x = x_ref[...]
    ms = jnp.mean(x * x, axis=-1, keepdims=True)
    h = x * lax.rsqrt(ms + EPS) * nw_ref[...]
    h_ref[...] = h.astype(BF16)
    h_hi, h_mid, _ = _split3(h)
    w_hi, w_mid, _ = _split3(wdt_ref[...])
    dt_ref[...] = _dot_nt(h_hi, w_hi) + _dot_nt(h_hi, w_mid) + _dot_nt(h_mid, w_hi)


def _norm_dt(x, norm_w, wt_dt, tm=512):
    S = x.shape[0]
    tm = min(tm, S)
    return pl.pallas_call(
        _norm_dt_kernel,
        grid=(S // tm,),
        in_specs=[
            pl.BlockSpec((tm, D_MODEL), lambda m: (m, 0)),
            pl.BlockSpec((1, D_MODEL), lambda m: (0, 0)),
            pl.BlockSpec((LANES, D_MODEL), lambda m: (0, 0)),
        ],
        out_specs=[
            pl.BlockSpec((tm, D_MODEL), lambda m: (m, 0)),
            pl.BlockSpec((tm, LANES), lambda m: (m, 0)),
        ],
        out_shape=[
            jax.ShapeDtypeStruct((S, D_MODEL), BF16),
            jax.ShapeDtypeStruct((S, LANES), F32),
        ],
        compiler_params=_cparams(("arbitrary",)),
        name="norm_dt",
    )(x, norm_w, wt_dt)


def _in_proj_kernel(h_ref, w32_ref, o_ref, w_scr):
    @pl.when(pl.program_id(1) == 0)
    def _():
        w_scr[...] = w32_ref[...].astype(BF16)

    o_ref[...] = _dot_nt(h_ref[...], w_scr[...]).astype(o_ref.dtype)


def _in_proj(h, wt, dt_row0, tm=2048, tn=1024):
    S = h.shape[0]
    tm = min(tm, S)
    assert dt_row0 % tn == 0 and SSM_HEADS % SUBLANES == 0
    na = dt_row0 // tn

    def w_rows(n, m):
        return (pl.multiple_of(n * tn + jnp.where(n >= na, SSM_HEADS, 0), SSM_HEADS), 0)

    return pl.pallas_call(
        _in_proj_kernel,
        grid=(D_PROJ // tn, S // tm),
        in_specs=[
            pl.BlockSpec((tm, D_MODEL), lambda n, m: (m, 0)),
            pl.BlockSpec((pl.Element(tn), pl.Element(D_MODEL)), w_rows),
        ],
        out_specs=pl.BlockSpec((tm, tn), lambda n, m: (m, n)),
        out_shape=jax.ShapeDtypeStruct((S, D_PROJ), BF16),
        scratch_shapes=[pltpu.VMEM((tn, D_MODEL), BF16)],
        compiler_params=_cparams(("arbitrary", "arbitrary")),
        name="in_proj",
    )(h, wt)


def _conv_silu(raw_ref, tail_ref, pad_ref, w_ref, b_ref, g, taps, cols=slice(None)):
    L = raw_ref.shape[0]
    pad_ref[0:SUBLANES, cols] = tail_ref[g, :, cols]
    pad_ref[SUBLANES:SUBLANES + L, cols] = raw_ref[:, cols].astype(F32)
    tail_ref[g, :, cols] = pad_ref[L:L + SUBLANES, cols]
    acc = b_ref[:, cols]
    for k in range(taps):
        off = SUBLANES - (taps - 1) + k
        acc = acc + w_ref[k:k + 1, cols] * pad_ref[off:off + L, cols]
    return _silu(acc)


def _ssd_kernel(x_ref, b_ref, c_ref, z_ref, dtr_ref, cwx_ref, cwb_ref, cwc_ref,
                cbx_ref, cbb_ref, cbc_ref, dtb_ref, alog_ref, dsk_ref, nw_ref,
                o_ref,
                tailx, tailb, tailc, padx, padb, padc, acs_scr, acst_scr, lbt_scr, ht_scr, rhs_scr):
    c = pl.program_id(0)
    gp = pl.program_id(1)
    L = SSM_CHUNK
    QW = 4 * SSM_HEAD_DIM
    GPS = SSD_GROUPS_PER_STEP

    @pl.when(c == 0)
    def _():
        for gg in range(GPS):
            g0 = gp * GPS + gg
            tailx[g0] = jnp.zeros(tailx.shape[1:], F32)
            tailb[g0] = jnp.zeros(tailb.shape[1:], F32)
            tailc[g0] = jnp.zeros(tailc.shape[1:], F32)
            ht_scr[g0] = jnp.zeros(ht_scr.shape[1:], F32)
            rhs_scr[gg] = jnp.zeros(rhs_scr.shape[1:], BF16)

    @pl.when(gp == 0)
    def _():
        t = dtr_ref[...] + dtb_ref[...]
        dt = jnp.maximum(t, 0.0) + jnp.log(1.0 + jnp.exp(-jnp.abs(t)))
        a = -jnp.exp(alog_ref[...])
        da = dt * a
        ri = lax.broadcasted_iota(jnp.int32, (L, L), 0)
        ci = lax.broadcasted_iota(jnp.int32, (L, L), 1)
        tri = jnp.where(ri >= ci, 1.0, 0.0).astype(BF16)
        hi, mid, lo = _split3(da)
        a_cs = _dot(tri, hi) + _dot(tri, mid) + _dot(tri, lo)
        a2 = a_cs * LOG2E
        a2t = a2.T
        acs_scr[...] = a2
        acst_scr[...] = a2t
        lbt_scr[...] = a2t - jnp.log(dt.T) * LOG2E

    for gg in range(GPS):
        xs = slice(gg * GROUP_CH, (gg + 1) * GROUP_CH)
        ns = slice(gg * SSM_STATE, (gg + 1) * SSM_STATE)
        _ssd_group(gp * GPS + gg,
                   x_ref.at[:, xs], b_ref.at[:, ns], c_ref.at[:, ns], z_ref.at[:, xs],
                   cwx_ref.at[:, xs], cwb_ref.at[:, ns], cwc_ref.at[:, ns],
                   cbx_ref.at[:, xs], cbb_ref.at[:, ns], cbc_ref.at[:, ns],
                   dsk_ref.at[:, xs], nw_ref.at[:, xs], o_ref.at[:, xs],
                   tailx, tailb, tailc, padx.at[gg], padb.at[gg], padc.at[gg],
                   acs_scr, acst_scr, lbt_scr, ht_scr,
                   [rhs_scr.at[gg, q] for q in range(GROUP_CH // QW)])


def _ssd_group(g, x_ref, b_ref, c_ref, z_ref, cwx_ref, cwb_ref, cwc_ref, cbx_ref, cbb_ref, cbc_ref,
               dsk_ref, nw_ref, o_ref, tailx, tailb, tailc, padx, padb, padc,
               acs_scr, acst_scr, lbt_scr, ht_scr, rhs_refs):
    L = SSM_CHUNK
    QW = 4 * SSM_HEAD_DIM
    ba = _conv_silu(b_ref, tailb, padb, cwb_ref, cbb_ref, g, SSM_CONV)
    ca = _conv_silu(c_ref, tailc, padc, cwc_ref, cbc_ref, g, SSM_CONV)

    bt = ba.T
    cb = _dot(ca.astype(BF16), bt.astype(BF16))
    a_cs = acs_scr[...]
    ri = lax.broadcasted_iota(jnp.int32, (L, L), 0)
    ci = lax.broadcasted_iota(jnp.int32, (L, L), 1)
    causal = ri >= ci
    lane_h = lax.broadcasted_iota(jnp.int32, (L, LANES), 1)
    lane_q = lax.broadcasted_iota(jnp.int32, (1, QW), 1)

    NH = QW // SSM_HEAD_DIM
    ygs = []
    for q in range(GROUP_CH // QW):
        rhs = rhs_refs[q]
        qc = slice(q * QW, (q + 1) * QW)
        xa = _conv_silu(x_ref, tailx, padx, cwx_ref, cbx_ref, g, SSM_CONV, cols=qc)
        xq = xa.astype(BF16)
        hq = ht_scr[g, :, q * QW:(q + 1) * QW]
        hqb = hq.astype(BF16)
        mhs, ces, bws = [], [], []
        cdq = jnp.zeros((1, QW), F32)
        for j in range(NH):
            h = g * SSM_HEADS_PER_GROUP + q * NH + j
            cols = slice(j * SSM_HEAD_DIM, (j + 1) * SSM_HEAD_DIM)
            rhs[j * L:(j + 1) * L, cols] = xq[:, cols]
            rhs[NH * L + j * SSM_STATE:NH * L + (j + 1) * SSM_STATE, cols] = hqb[:, cols]
            a_row = acst_scr[pl.ds(h, 1), :]
            b_row = lbt_scr[pl.ds(h, 1), :]
            a_col = jnp.sum(jnp.where(lane_h == h, a_cs, 0.0), axis=1, keepdims=True)
            dmat = jnp.where(causal, a_col - b_row, -jnp.inf)
            mhs.append((cb * jnp.exp2(dmat)).astype(BF16))
            ces.append((ca * jnp.exp2(a_col)).astype(BF16))
            a_last = a_row[:, L - 1:L]
            w_row = jnp.exp2(a_last - b_row)
            bws.append((bt * w_row).astype(BF16))
            lm = (lane_q >= j * SSM_HEAD_DIM) & (lane_q < (j + 1) * SSM_HEAD_DIM)
            cdq = jnp.where(lm, jnp.exp2(a_last), cdq)
        acc = _dot(jnp.concatenate(mhs + ces, axis=1), rhs[...])
        st = _dot(jnp.concatenate(bws, axis=1), rhs[0:NH * L, :])
        ht_scr[g, :, q * QW:(q + 1) * QW] = hq * cdq + st
        y = acc + xa * dsk_ref[:, qc]
        ygs.append(y * _silu(z_ref[:, qc].astype(F32)))

    ss = sum(jnp.sum(v * v, axis=1, keepdims=True) for v in ygs)
    scale = lax.rsqrt(ss * (1.0 / GROUP_CH) + EPS)
    for q in range(len(ygs)):
        o_ref[:, q * QW:(q + 1) * QW] = (
            ygs[q] * scale * nw_ref[:, q * QW:(q + 1) * QW]).astype(o_ref.dtype)


def _ssd(proj, dt_raw, conv_w, conv_b, dt_bias, a_log, d_exp, norm_w):
    S = proj.shape[0]
    L, G, N = SSM_CHUNK, SSM_GROUPS, SSM_STATE
    GPS = SSD_GROUPS_PER_STEP
    XW, NW = GPS * GROUP_CH, GPS * N
    bcol = D_INNER // NW
    ccol = bcol + G // GPS
    pxcol = COL_XBC // XW
    pbcol = COL_XBC // NW + bcol
    pccol = pbcol + G // GPS
    zcol = COL_Z // XW
    return pl.pallas_call(
        _ssd_kernel,
        grid=(S // L, G // GPS),
        in_specs=[
            pl.BlockSpec((L, XW), lambda c, g: (c, pxcol + g)),
            pl.BlockSpec((L, NW), lambda c, g: (c, pbcol + g)),
            pl.BlockSpec((L, NW), lambda c, g: (c, pccol + g)),
            pl.BlockSpec((L, XW), lambda c, g: (c, zcol + g)),
            pl.BlockSpec((L, LANES), lambda c, g: (c, 0)),
            pl.BlockSpec((SSM_CONV, XW), lambda c, g: (0, g)),
            pl.BlockSpec((SSM_CONV, NW), lambda c, g: (0, bcol + g)),
            pl.BlockSpec((SSM_CONV, NW), lambda c, g: (0, ccol + g)),
            pl.BlockSpec((1, XW), lambda c, g: (0, g)),
            pl.BlockSpec((1, NW), lambda c, g: (0, bcol + g)),
            pl.BlockSpec((1, NW), lambda c, g: (0, ccol + g)),
            pl.BlockSpec((1, LANES), lambda c, g: (0, 0)),
            pl.BlockSpec((1, LANES), lambda c, g: (0, 0)),
            pl.BlockSpec((1, XW), lambda c, g: (0, g)),
            pl.BlockSpec((1, XW), lambda c, g: (0, g)),
        ],
        out_specs=pl.BlockSpec((L, XW), lambda c, g: (c, g)),
        out_shape=jax.ShapeDtypeStruct((S, D_INNER), BF16),
        scratch_shapes=[
            pltpu.VMEM((G, SUBLANES, GROUP_CH), F32),
            pltpu.VMEM((G, SUBLANES, N), F32),
            pltpu.VMEM((G, SUBLANES, N), F32),
            pltpu.VMEM((GPS, L + SUBLANES, GROUP_CH), F32),
            pltpu.VMEM((GPS, L + SUBLANES, N), F32),
            pltpu.VMEM((GPS, L + SUBLANES, N), F32),
            pltpu.VMEM((L, LANES), F32),
            pltpu.VMEM((LANES, L), F32),
            pltpu.VMEM((LANES, L), F32),
            pltpu.VMEM((G, N, GROUP_CH), F32),
            pltpu.VMEM((GPS, GROUP_CH // (4 * SSM_HEAD_DIM), 4 * (L + N), 4 * SSM_HEAD_DIM), BF16),
        ],
        compiler_params=_cparams(("arbitrary", "arbitrary")),
        name="ssd",
    )(proj, proj, proj, proj, dt_raw, conv_w, conv_w, conv_w, conv_b, conv_b, conv_b,
      dt_bias, a_log, d_exp, norm_w)


def _qk_prep_kernel(q_ref, k_ref, v_ref, qw_ref, kw_ref, qt_ref, kn_ref, vt_ref, pen_ref, kbar_scr):
    S = q_ref.shape[0]
    BS = MOBA_BLOCK
    nb = S // BS
    nbp = kbar_scr.shape[0]
    kbar_scr[...] = jnp.zeros(kbar_scr.shape, F32)
    eye = (lax.broadcasted_iota(jnp.int32, (ATTN_HEAD_DIM, ATTN_HEAD_DIM), 0)
           == lax.broadcasted_iota(jnp.int32, (ATTN_HEAD_DIM, ATTN_HEAD_DIM), 1)).astype(BF16)

    def kloop(b, carry):
        rows = pl.ds(pl.multiple_of(b * BS, BS), BS)
        k = k_ref[rows, :].astype(F32)
        kn = k * lax.rsqrt(jnp.mean(k * k, axis=-1, keepdims=True) + EPS) * kw_ref[...]
        kn_ref[0, rows, :] = kn.astype(BF16)
        kbar_scr[pl.ds(b, 1), :] = jnp.mean(kn, axis=0, keepdims=True)
        vt_ref[0, b, 0:ATTN_HEAD_DIM, :] = _dot_nt(eye, v_ref[rows, :]).astype(BF16)
        vt_ref[0, b, ATTN_HEAD_DIM:, :] = jnp.ones((MOBA_VT_ROWS - ATTN_HEAD_DIM, BS), BF16)
        return carry

    lax.fori_loop(0, nb, kloop, 0, unroll=4)
    kb_hi, kb_mid, _ = _split3(kbar_scr[...])

    def qloop(i, carry):
        rows = pl.ds(pl.multiple_of(i * BS, BS), BS)
        q = q_ref[rows, :].astype(F32)
        qn = q * lax.rsqrt(jnp.mean(q * q, axis=-1, keepdims=True) + EPS) * qw_ref[...]
        qnt = (qn * (ATTN_HEAD_DIM ** -0.5)).T
        qt_ref[0, i] = (qnt * LOG2E).astype(BF16)
        q_hi, q_mid, _ = _split3(qnt)
        gate = _dot(kb_hi, q_hi) + _dot(kb_mid, q_hi) + _dot(kb_hi, q_mid)
        blk = lax.broadcasted_iota(jnp.int32, gate.shape, 0)
        blk_f = blk.astype(F32)
        gate = jnp.where(blk < i, gate, -jnp.inf)
        pen = jnp.full(gate.shape, -jnp.inf, F32)
        for _ in range(MOBA_TOPK):
            mx = jnp.max(gate, axis=0, keepdims=True)
            cand = jnp.where((gate == mx) & (mx > -jnp.inf), blk_f, float(nbp))
            idx = jnp.min(cand, axis=0, keepdims=True)
            hit = blk_f == idx
            pen = jnp.where(hit, 0.0, pen)
            gate = jnp.where(hit, -jnp.inf, gate)
        pen_ref[0, i] = pen
        return carry

    lax.fori_loop(0, nb, qloop, 0, unroll=8)


def _qk_prep(proj, q_norm_w, k_norm_w):
    S = proj.shape[0]
    H, D, BS = ATTN_HEADS, ATTN_HEAD_DIM, MOBA_BLOCK
    nb = S // BS
    nbp = -(-nb // SUBLANES) * SUBLANES
    return pl.pallas_call(
        _qk_prep_kernel,
        grid=(H,),
        in_specs=[
            pl.BlockSpec((S, D), lambda h: (0, COL_Q // D + h)),
            pl.BlockSpec((S, D), lambda h: (0, COL_K // D + h)),
            pl.BlockSpec((S, D), lambda h: (0, COL_V // D + h)),
            pl.BlockSpec((1, D), lambda h: (0, 0)),
            pl.BlockSpec((1, D), lambda h: (0, 0)),
        ],
        out_specs=[
            pl.BlockSpec((1, nb, D, BS), lambda h: (h, 0, 0, 0)),
            pl.BlockSpec((1, S, D), lambda h: (h, 0, 0)),
            pl.BlockSpec((1, nb, MOBA_VT_ROWS, BS), lambda h: (h, 0, 0, 0)),
            pl.BlockSpec((1, nb, nbp, BS), lambda h: (h, 0, 0, 0)),
        ],
        out_shape=[
            jax.ShapeDtypeStruct((H, nb, D, BS), BF16),
            jax.ShapeDtypeStruct((H, S, D), BF16),
            jax.ShapeDtypeStruct((H, nb, MOBA_VT_ROWS, BS), BF16),
            jax.ShapeDtypeStruct((H, nb, nbp, BS), F32),
        ],
        scratch_shapes=[pltpu.VMEM((nbp, D), F32)],
        compiler_params=_cparams(("arbitrary",)),
        name="qk_prep",
    )(proj, proj, proj, q_norm_w, k_norm_w)


def _rel_bucket(dist):
    n = jnp.maximum(dist, 0)
    max_exact = REL_BUCKETS // 2
    nf = jnp.maximum(n, 1).astype(F32)
    large = max_exact + (jnp.log(nf / max_exact) / math.log(REL_MAX_DIST / max_exact)
                         * (REL_BUCKETS - max_exact)).astype(jnp.int32)
    large = jnp.minimum(large, REL_BUCKETS - 1)
    return jnp.where(n < max_exact, n, large)


MOBA_HEADS_PER_STEP = 4
MOBA_VT_ROWS = ATTN_HEAD_DIM + 2 * SUBLANES


def _moba_kernel(tbl_ref, qt_ref, k_ref, vt_ref, pen_ref, o_ref,
                 bown_scr, bprev_scr, m_scr, acc_scr, s0_scr, s1_scr, s2_scr, s3_scr,
                 p0_scr, p1_scr, p2_scr, p3_scr, al0_scr, al1_scr, al2_scr, al3_scr):
    hp = pl.program_id(0)
    i = pl.program_id(1)
    BS, HB = MOBA_BLOCK, MOBA_HEADS_PER_STEP
    D = ATTN_HEAD_DIM

    @pl.when(i == 0)
    def _():
        kl = lax.broadcasted_iota(jnp.int32, (BS, BS), 0)
        ql = lax.broadcasted_iota(jnp.int32, (BS, BS), 1)
        d = ql - kl
        b_own = _rel_bucket(d)
        b_prev = _rel_bucket(d + BS)
        for a in range(HB):
            h = hp * HB + a
            far = tbl_ref[h, REL_BUCKETS - 1]

            def lookup(bucket):
                val = jnp.zeros((BS, BS), F32)
                for b in range(REL_BUCKETS):
                    val = jnp.where(bucket == b, tbl_ref[h, b], val)
                return (val - far) * LOG2E

            bown_scr[a] = jnp.where(d >= 0, lookup(b_own), -jnp.inf)
            bprev_scr[a] = lookup(b_prev)

    def krows(a, j):
        return k_ref[a, pl.ds(pl.multiple_of(j * BS, BS), BS), :]

    jp = jnp.maximum(i - 1, 0)
    no_prev = jnp.where(i == 0, -jnp.inf, 0.0)
    n_far = jnp.maximum(i - 1, 0)
    nb = k_ref.shape[1] // BS

    def clampj(j):
        return jnp.clip(j, 0, nb - 1)

    def scores(j, s_slot):
        for a in range(HB):
            s_slot[a] = _dot(krows(a, clampj(j)), qt_ref[a, 0])

    def softmax(j, s_slot, p_slot, al_slot, mask_tail, bias_scr=None):
        for a in range(HB):
            pr = pen_ref[a, 0, pl.ds(clampj(j), 1), :]
            if mask_tail is True:
                pr = pr + jnp.where(j >= n_far, -jnp.inf, 0.0)
            elif mask_tail is not False:
                pr = pr + mask_tail
            def tile():
                s = s_slot[a]
                return s if bias_scr is None else s + bias_scr[a]

            m_old = m_scr[a]
            m_new = jnp.maximum(m_old, jnp.max(tile(), axis=0, keepdims=True) + pr)
            al_slot[a] = jnp.exp2(m_old - m_new)
            m_scr[a] = m_new
            p_slot[a] = jnp.exp2(tile() - (m_new - pr)).astype(BF16)

    def accumulate(j, p_slot, al_slot, maybe_prev=False):
        jv = clampj(j)
        if maybe_prev:
            jv = jnp.where(j < 0, jp, jv)
        for a in range(HB):
            acc_scr[a] = al_slot[a] * acc_scr[a] + _dot(vt_ref[a, jv], p_slot[a])

    s_slots = (s0_scr, s1_scr, s2_scr, s3_scr)
    p_slots = (p0_scr, p1_scr, p2_scr, p3_scr)
    al_slots = (al0_scr, al1_scr, al2_scr, al3_scr)
    scores(i, s2_scr)
    scores(jp, s3_scr)
    scores(0, s0_scr)
    for a in range(HB):
        s = s2_scr[a] + bown_scr[a]
        m0 = jnp.max(s, axis=0, keepdims=True)
        m_scr[a] = m0
        p2_scr[a] = jnp.exp2(s - m0).astype(BF16)
    for a in range(HB):
        acc_scr[a] = _dot(vt_ref[a, i], p2_scr[a])
    softmax(jp, s3_scr, p3_scr, al3_scr, no_prev, bias_scr=bprev_scr)

    def quad_loop(g, carry):
        j = 4 * g
        for a in range(HB):
            kq = k_ref[a, pl.ds(pl.multiple_of((j + 1) * BS, BS), 3 * BS), :]
            s3 = _dot(kq, qt_ref[a, 0])
            for t in range(3):
                s_slots[t + 1][a] = s3[t * BS:(t + 1) * BS, :]
        for t in range(4):
            accumulate(j + t - 1, p_slots[(t - 1) % 4], al_slots[(t - 1) % 4], maybe_prev=(t == 0))
            softmax(j + t, s_slots[t], p_slots[t], al_slots[t], False)
            if t == 0:
                scores(j + 4, s0_scr)
        return carry

    n_quads = n_far // 4
    lax.fori_loop(0, n_quads, quad_loop, 0)

    def pair_loop(g, carry):
        j = 4 * n_quads + 2 * g
        scores(j + 1, s1_scr)
        accumulate(j - 1, p3_scr, al3_scr, maybe_prev=True)
        softmax(j, s0_scr, p2_scr, al2_scr, False)
        scores(j + 2, s0_scr)
        accumulate(j, p2_scr, al2_scr)
        softmax(j + 1, s1_scr, p3_scr, al3_scr, True)
        return carry

    n_pairs = (n_far - 4 * n_quads + 1) // 2
    lax.fori_loop(0, n_pairs, pair_loop, 0)
    accumulate(4 * n_quads + 2 * n_pairs - 1, p3_scr, al3_scr, maybe_prev=True)
    for a in range(HB):
        acc = acc_scr[a]
        inv_l = 1.0 / acc[D:D + 1, :]
        o_ref[:, a * D:(a + 1) * D] = (acc[0:D, :] * inv_l).T.astype(o_ref.dtype)


def _moba(qt, kn, vt, pen, table_h):
    H, nb, D, BS = qt.shape
    S = nb * BS
    nbp = pen.shape[2]
    HB = MOBA_HEADS_PER_STEP
    return pl.pallas_call(
        _moba_kernel,
        grid=(H // HB, nb),
        in_specs=[
            pl.BlockSpec(memory_space=pltpu.SMEM),
            pl.BlockSpec((HB, 1, D, BS), lambda hp, i: (hp, i, 0, 0)),
            pl.BlockSpec((HB, S, D), lambda hp, i: (hp, 0, 0)),
            pl.BlockSpec((HB, nb, MOBA_VT_ROWS, BS), lambda hp, i: (hp, 0, 0, 0)),
            pl.BlockSpec((HB, 1, nbp, BS), lambda hp, i: (hp, i, 0, 0)),
        ],
        out_specs=pl.BlockSpec((BS, HB * D), lambda hp, i: (i, hp)),
        out_shape=jax.ShapeDtypeStruct((S, H * D), BF16),
        scratch_shapes=[
            pltpu.VMEM((HB, BS, BS), F32),
            pltpu.VMEM((HB, BS, BS), F32),
            pltpu.VMEM((HB, 1, BS), F32),
            pltpu.VMEM((HB, MOBA_VT_ROWS, BS), F32),
        ] + [pltpu.VMEM((HB, BS, BS), F32)] * 4 + [pltpu.VMEM((HB, BS, BS), BF16)] * 4
        + [pltpu.VMEM((HB, 1, BS), F32)] * 4,
        compiler_params=_cparams(("arbitrary", "arbitrary")),
        name="moba",
    )(table_h, qt, kn, vt, pen)


def _mix_kernel(ys_ref, ya_ref, ws_ref, wa_ref, gs_ref, ga_ref, bs_ref, ba_ref, o_ref):
    ys = _dot(ys_ref[...], ws_ref[...])
    ya = _dot(ya_ref[...], wa_ref[...])
    o_ref[...] = (_sigmoid(gs_ref[...].astype(F32) + bs_ref[...]) * ys
                  + _sigmoid(ga_ref[...].astype(F32) + ba_ref[...]) * ya).astype(o_ref.dtype)


def _mix(y_ssd, y_attn, w_ssm_out, w_attn_out, proj, b_gate, tm=1024, tn=512):
    S = y_ssd.shape[0]
    tm = min(tm, S)
    gs_col = COL_G // tn
    ga_col = (COL_G + D_MODEL) // tn
    nb = D_MODEL // tn
    return pl.pallas_call(
        _mix_kernel,
        grid=(S // tm, D_MODEL // tn),
        in_specs=[
            pl.BlockSpec((tm, D_INNER), lambda m, n: (m, 0)),
            pl.BlockSpec((tm, D_ATTN), lambda m, n: (m, 0)),
            pl.BlockSpec((D_INNER, tn), lambda m, n: (0, n)),
            pl.BlockSpec((D_ATTN, tn), lambda m, n: (0, n)),
            pl.BlockSpec((tm, tn), lambda m, n: (m, gs_col + n)),
            pl.BlockSpec((tm, tn), lambda m, n: (m, ga_col + n)),
            pl.BlockSpec((1, tn), lambda m, n: (0, n)),
            pl.BlockSpec((1, tn), lambda m, n: (0, nb + n)),
        ],
        out_specs=pl.BlockSpec((tm, tn), lambda m, n: (m, n)),
        out_shape=jax.ShapeDtypeStruct((S, D_MODEL), BF16),
        compiler_params=_cparams(("arbitrary", "arbitrary")),
        name="mix",
    )(y_ssd, y_attn, w_ssm_out, w_attn_out, proj, proj, b_gate, b_gate)


def _out_norm_kernel(x_ref, mix_ref, w_ref, nw_ref, x1_ref, h2_ref):
    x1 = x_ref[...] + _dot(mix_ref[...], w_ref[...])
    x1_ref[...] = x1
    ms = jnp.mean(x1 * x1, axis=-1, keepdims=True)
    h2_ref[...] = (x1 * lax.rsqrt(ms + EPS) * nw_ref[...]).astype(h2_ref.dtype)


def _out_norm(x, mix, w_out, ffn_norm_w, tm=512):
    S = x.shape[0]
    tm = min(tm, S)
    return pl.pallas_call(
        _out_norm_kernel,
        grid=(S // tm,),
        in_specs=[
            pl.BlockSpec((tm, D_MODEL), lambda m: (m, 0)),
            pl.BlockSpec((tm, D_MODEL), lambda m: (m, 0)),
            pl.BlockSpec((D_MODEL, D_MODEL), lambda m: (0, 0)),
            pl.BlockSpec((1, D_MODEL), lambda m: (0, 0)),
        ],
        out_specs=[
            pl.BlockSpec((tm, D_MODEL), lambda m: (m, 0)),
            pl.BlockSpec((tm, D_MODEL), lambda m: (m, 0)),
        ],
        out_shape=[
            jax.ShapeDtypeStruct((S, D_MODEL), F32),
            jax.ShapeDtypeStruct((S, D_MODEL), BF16),
        ],
        compiler_params=_cparams(("arbitrary",)),
        name="out_norm",
    )(x, mix, w_out, ffn_norm_w)


def _ffn_up_kernel(h_ref, wg32_ref, wu32_ref, cwg_ref, cwu_ref, cbg_ref, cbu_ref, o_ref,
                   tailg, tailu, padg, padu, wg_ref, wu_ref):
    tm = h_ref.shape[0]

    @pl.when(pl.program_id(1) == 0)
    def _():
        tailg[...] = jnp.zeros(tailg.shape, F32)
        tailu[...] = jnp.zeros(tailu.shape, F32)
        wg_ref[...] = wg32_ref[...].astype(BF16)
        wu_ref[...] = wu32_ref[...].astype(BF16)

    def conv(w_ref, cw_ref, cb_ref, tail, pad):
        pad[0:SUBLANES, :] = tail[...]
        pad[SUBLANES:SUBLANES + tm, :] = _dot(h_ref[...], w_ref[...])
        tail[...] = pad[tm:tm + SUBLANES, :]
        acc = cb_ref[...]
        for k in range(FFN_CONV):
            off = SUBLANES - (FFN_CONV - 1) + k
            acc = acc + cw_ref[k:k + 1, :] * pad[off:off + tm, :]
        return acc

    ug = conv(wg_ref, cwg_ref, cbg_ref, tailg, padg)
    uu = conv(wu_ref, cwu_ref, cbu_ref, tailu, padu)
    o_ref[...] = (_silu(ug) * uu).astype(o_ref.dtype)


def _ffn_up(h2, w_up, conv_w, conv_b, tm=1024, tn=512):
    S = h2.shape[0]
    tm = min(tm, S)
    nb = D_FF // tn
    return pl.pallas_call(
        _ffn_up_kernel,
        grid=(nb, S // tm),
        in_specs=[
            pl.BlockSpec((tm, D_MODEL), lambda n, m: (m, 0)),
            pl.BlockSpec((D_MODEL, tn), lambda n, m: (0, n)),
            pl.BlockSpec((D_MODEL, tn), lambda n, m: (0, nb + n)),
            pl.BlockSpec((FFN_CONV, tn), lambda n, m: (0, n)),
            pl.BlockSpec((FFN_CONV, tn), lambda n, m: (0, nb + n)),
            pl.BlockSpec((1, tn), lambda n, m: (0, n)),
            pl.BlockSpec((1, tn), lambda n, m: (0, nb + n)),
        ],
        out_specs=pl.BlockSpec((tm, tn), lambda n, m: (m, n)),
        out_shape=jax.ShapeDtypeStruct((S, D_FF), BF16),
        scratch_shapes=[
            pltpu.VMEM((SUBLANES, tn), F32),
            pltpu.VMEM((SUBLANES, tn), F32),
            pltpu.VMEM((tm + SUBLANES, tn), F32),
            pltpu.VMEM((tm + SUBLANES, tn), F32),
            pltpu.VMEM((D_MODEL, tn), BF16),
            pltpu.VMEM((D_MODEL, tn), BF16),
        ],
        compiler_params=_cparams(("arbitrary", "arbitrary")),
        name="ffn_up",
    )(h2, w_up, w_up, conv_w, conv_w, conv_b, conv_b)


def _ffn_down_kernel(a_ref, w_ref, x1_ref, o_ref):
    o_ref[...] = x1_ref[...] + _dot(a_ref[...], w_ref[...])


def _ffn_down(act, w_down, x1, tm=1024, tn=512):
    S = act.shape[0]
    tm = min(tm, S)
    return pl.pallas_call(
        _ffn_down_kernel,
        grid=(S // tm, D_MODEL // tn),
        in_specs=[
            pl.BlockSpec((tm, D_FF), lambda m, n: (m, 0)),
            pl.BlockSpec((D_FF, tn), lambda m, n: (0, n)),
            pl.BlockSpec((tm, tn), lambda m, n: (m, n)),
        ],
        out_specs=pl.BlockSpec((tm, tn), lambda m, n: (m, n)),
        out_shape=jax.ShapeDtypeStruct((S, D_MODEL), F32),
        compiler_params=_cparams(("arbitrary", "arbitrary")),
        name="ffn_down",
    )(act, w_down, x1)


def _pad_lanes(v):
    return jnp.pad(v, ((0, 0), (0, LANES - v.shape[1])))


def _layer(x, attn_norm_w, w_in, b_gate, ssm_conv_w, ssm_conv_b, ssm_dt_bias, ssm_a_log, ssm_d,
           ssm_norm_w, q_norm_w, k_norm_w, rel_bias, w_ssm_out, w_attn_out, w_out, ffn_norm_w,
           w_up, ffn_conv_w, ffn_conv_b, w_down):
    o_dt = D_INNER + D_XBC
    o_q = o_dt + SSM_HEADS
    wt = w_in.T
    wt_dt = jnp.pad(wt[o_dt:o_q], ((0, LANES - SSM_HEADS), (0, 0)))

    h, dt_raw = _norm_dt(x, attn_norm_w[None, :], wt_dt)
    proj = _in_proj(h, wt, o_dt)
    y_ssd = _ssd(proj, dt_raw, ssm_conv_w, ssm_conv_b[None, :], _pad_lanes(ssm_dt_bias[None, :]),
                 _pad_lanes(ssm_a_log[None, :]), jnp.repeat(ssm_d, SSM_HEAD_DIM)[None, :],
                 ssm_norm_w[None, :])
    qt, kn, vt, pen = _qk_prep(proj, q_norm_w[None, :], k_norm_w[None, :])
    y_attn = _moba(qt, kn, vt, pen, rel_bias.T)
    mix = _mix(y_ssd, y_attn, w_ssm_out.astype(BF16), w_attn_out.astype(BF16), proj, b_gate[None, :])
    x1, h2 = _out_norm(x, mix, w_out.astype(BF16), ffn_norm_w[None, :])
    act = _ffn_up(h2, w_up, ffn_conv_w, ffn_conv_b[None, :])
    return _ffn_down(act, w_down.astype(BF16), x1)


def kernel(x, attn_norm_w, w_in, b_gate, ssm_conv_w, ssm_conv_b, ssm_dt_bias, ssm_a_log, ssm_d,
           ssm_norm_w, q_norm_w, k_norm_w, rel_bias, w_ssm_out, w_attn_out, w_out, ffn_norm_w,
           w_up, ffn_conv_w, ffn_conv_b, w_down):
    assert x.shape[0] == 1 and attn_norm_w.shape[0] == 1
    out = _layer(x[0], attn_norm_w[0], w_in[0], b_gate[0], ssm_conv_w[0], ssm_conv_b[0],
                 ssm_dt_bias[0], ssm_a_log[0], ssm_d[0], ssm_norm_w[0], q_norm_w[0], k_norm_w[0],
                 rel_bias, w_ssm_out[0], w_attn_out[0], w_out[0], ffn_norm_w[0], w_up[0],
                 ffn_conv_w[0], ffn_conv_b[0], w_down[0])
    return out[None]
```
